```python
import jax, jax.numpy as jnp
from jax import lax
import numpy as np

D_MODEL = 1024
BATCH = 8
SEQ = 2048
DEPTH = 1
DEC_BATCH = 128
DEC_SEQ = 1
PAST_LEN = 16384
PAGE_SIZE = 128

MLSTM_HEADS = 4
HEAD_DIM = 128
MLSTM_W = MLSTM_HEADS * HEAD_DIM
CONV_GROUPS = 4
CONV_CH = D_MODEL - MLSTM_W
CONV_WIDTH = 3
D_FF = 4 * D_MODEL
PLE_DIM = 256
CHUNK = 64
EPS = 1e-6
M_INIT = -1e30
N_IN = 4 * MLSTM_W + 2 * MLSTM_HEADS + 3 * CONV_CH

kernel_name = 'hymba_mlstm_shortconv_decoder_step'


def rmsnorm(x, g):
    xf = x.astype(jnp.float32)
    y = xf * lax.rsqrt(jnp.mean(xf * xf, axis=-1, keepdims=True) + EPS)
    return (y * g.astype(jnp.float32)).astype(x.dtype)


def mlstm_chunkwise(q, k, v, ig, lf, C0, n0, m0):
    B, S, H, Dk = q.shape
    Dv = v.shape[-1]
    L = CHUNK if S % CHUNK == 0 else S
    nc = S // L

    def to_chunks(a):
        return jnp.moveaxis(a.reshape((B, nc, L) + a.shape[2:]), 1, 0)

    causal = jnp.tril(jnp.ones((L, L), dtype=bool))

    def step(carry, xs):
        C, n, m = carry
        qc, kc, vc, ic, fc = xs
        b = jnp.cumsum(fc, axis=1)
        dlog = b[:, :, None, :] - b[:, None, :, :] + ic[:, None, :, :]
        dlog = jnp.where(causal[None, :, :, None], dlog, -jnp.inf)
        m_inter = b + m[:, None, :]
        m_t = jnp.maximum(m_inter, jnp.max(dlog, axis=2))
        dmat = jnp.exp(dlog - m_t[:, :, None, :])
        scores = jnp.einsum('bthd,bshd->btsh', qc, kc) * dmat
        decay = jnp.exp(m_inter - m_t)
        num = (jnp.einsum('btsh,bshv->bthv', scores, vc)
               + decay[..., None] * jnp.einsum('bthk,bhkv->bthv', qc, C))
        den = jnp.sum(scores, axis=2) + decay * jnp.einsum('bthk,bhk->bth', qc, n)
        h = num / jnp.maximum(jnp.abs(den), jnp.exp(-m_t))[..., None]
        m_new = m_t[:, -1]
        w = jnp.exp(b[:, -1:, :] - b + ic - m_new[:, None, :])
        cdec = jnp.exp(b[:, -1] + m - m_new)
        C_new = cdec[..., None, None] * C + jnp.einsum('bsh,bshk,bshv->bhkv', w, kc, vc)
        n_new = cdec[..., None] * n + jnp.einsum('bsh,bshk->bhk', w, kc)
        return (C_new, n_new, m_new), h

    xs = (to_chunks(q), to_chunks(k), to_chunks(v), to_chunks(ig), to_chunks(lf))
    (C, n, m), hs = lax.scan(step, (C0, n0, m0), xs)
    h = jnp.moveaxis(hs, 0, 1).reshape(B, S, H, Dv)
    return h, C, n, m


def layer(x, p, conv_buf, C0, n0, m0, norm_mix, w_in, b_gate_i, b_gate_f, mh_norm,
          conv_w, w_out, norm_mlp, w_up, w_down, norm_ple, w_ple_gate, w_ple_proj):
    B, S, _ = x.shape
    f32 = jnp.float32
    h = rmsnorm(x, norm_mix)
    z = h @ w_in
    sizes = [MLSTM_W, MLSTM_W, MLSTM_W, MLSTM_W, MLSTM_HEADS, MLSTM_HEADS, CONV_CH, CONV_CH, CONV_CH]
    offs = []
    acc = 0
    for s in sizes[:-1]:
        acc += s
        offs.append(acc)
    q, k, v, og, ig, fg, gb, gc, u = jnp.split(z, offs, axis=-1)
    q = q.astype(f32).reshape(B, S, MLSTM_HEADS, HEAD_DIM) * (HEAD_DIM ** -0.5)
    k = k.astype(f32).reshape(B, S, MLSTM_HEADS, HEAD_DIM)
    v = v.astype(f32).reshape(B, S, MLSTM_HEADS, HEAD_DIM)
    ig = ig.astype(f32) + b_gate_i.astype(f32)
    lf = jax.nn.log_sigmoid(fg.astype(f32) + b_gate_f.astype(f32))
    hm, C, n, m = mlstm_chunkwise(q, k, v, ig, lf, C0.astype(f32), n0.astype(f32), m0.astype(f32))
    hm = hm * lax.rsqrt(jnp.mean(hm * hm, axis=-1, keepdims=True) + EPS)
    hm = hm * mh_norm.astype(f32).reshape(MLSTM_HEADS, HEAD_DIM)
    hm = hm.reshape(B, S, MLSTM_W) * jax.nn.sigmoid(og.astype(f32))
    cu = gc * u
    full = jnp.concatenate([conv_buf.astype(cu.dtype), cu], axis=1)
    yc = conv_w[0] * full[:, 0:S]
    for j in range(1, CONV_WIDTH):
        yc = yc + conv_w[j] * full[:, j:j + S]
    new_buf = full[:, S:]
    yc = gb * yc
    mix = jnp.concatenate([hm.astype(x.dtype), yc.astype(x.dtype)], axis=-1) @ w_out
    x = x + mix
    hf = jax.nn.relu(rmsnorm(x, norm_mlp) @ w_up)
    x = x + (hf * hf) @ w_down
    gate = jax.nn.sigmoid((rmsnorm(x, norm_ple) @ w_ple_gate).astype(f32))
    x = x + (gate * (p @ w_ple_proj).astype(f32)).astype(x.dtype)
    return x, new_buf, C, n, m


def setup_inputs(seed: int = 0) -> dict:
    key = jax.random.key(seed)
    ks = jax.random.split(key, 24)
    nrm = jax.random.normal
    f32 = jnp.float32
    d = {}
    d['x_prompt'] = nrm(ks[0], (BATCH, SEQ, D_MODEL), f32)
    d['x_sample'] = nrm(ks[1], (DEC_BATCH, DEC_SEQ, D_MODEL), f32)
    d['state_mlstm_C'] = 0.5 * nrm(ks[2], (DEPTH, DEC_BATCH, MLSTM_HEADS, HEAD_DIM, HEAD_DIM), f32)
    d['state_mlstm_n'] = 0.5 * nrm(ks[3], (DEPTH, DEC_BATCH, MLSTM_HEADS, HEAD_DIM), f32)
    d['state_mlstm_m'] = jax.random.uniform(ks[4], (DEPTH, DEC_BATCH, MLSTM_HEADS), f32, 0.0, 4.0)
    d['state_conv'] = nrm(ks[5], (DEPTH, DEC_BATCH, CONV_WIDTH - 1, CONV_CH), f32)
    d['p_prompt'] = nrm(ks[6], (DEPTH, BATCH, SEQ, PLE_DIM), f32)
    d['p_sample'] = nrm(ks[7], (DEPTH, DEC_BATCH, DEC_SEQ, PLE_DIM), f32)
    d['norm_mix'] = 1.0 + 0.02 * nrm(ks[8], (DEPTH, D_MODEL), f32)
    d['w_in'] = nrm(ks[9], (DEPTH, D_MODEL, N_IN), f32) * D_MODEL ** -0.5
    d['b_gate_i'] = 0.1 * nrm(ks[10], (DEPTH, MLSTM_HEADS), f32)
    d['b_gate_f'] = jnp.linspace(3.0, 6.0, MLSTM_HEADS, dtype=f32)[None, :] + 0.01 * nrm(ks[11], (DEPTH, MLSTM_HEADS), f32)
    d['mh_norm'] = 1.0 + 0.02 * nrm(ks[12], (DEPTH, MLSTM_W), f32)
    d['conv_w'] = nrm(ks[13], (DEPTH, CONV_WIDTH, CONV_CH), f32) * CONV_WIDTH ** -0.5
    d['w_out'] = nrm(ks[14], (DEPTH, MLSTM_W + CONV_CH, D_MODEL), f32) * (MLSTM_W + CONV_CH) ** -0.5
    d['norm_mlp'] = 1.0 + 0.02 * nrm(ks[15], (DEPTH, D_MODEL), f32)
    d['w_up'] = nrm(ks[16], (DEPTH, D_MODEL, D_FF), f32) * D_MODEL ** -0.5
    d['w_down'] = nrm(ks[17], (DEPTH, D_FF, D_MODEL), f32) * D_FF ** -0.5
    d['norm_ple'] = 1.0 + 0.02 * nrm(ks[18], (DEPTH, D_MODEL), f32)
    d['w_ple_gate'] = nrm(ks[19], (DEPTH, D_MODEL, D_MODEL), f32) * D_MODEL ** -0.5
    d['w_ple_proj'] = nrm(ks[20], (DEPTH, PLE_DIM, D_MODEL), f32) * PLE_DIM ** -0.5
    d['norm_final'] = 1.0 + 0.02 * nrm(ks[21], (D_MODEL,), f32)
    return d


def reference(x_prompt, x_sample, state_mlstm_C, state_mlstm_n, state_mlstm_m, state_conv,
              p_prompt, p_sample, norm_mix, w_in, b_gate_i, b_gate_f, mh_norm, conv_w, w_out,
              norm_mlp, w_up, w_down, norm_ple, w_ple_gate, w_ple_proj, norm_final):
    f32 = jnp.float32
    Bp = x_prompt.shape[0]
    xp = x_prompt
    xs = x_sample
    pC, pn, pm, pconv = [], [], [], []
    sC, sn, sm, sconv = [], [], [], []
    for i in range(DEPTH):
        params = (norm_mix[i], w_in[i], b_gate_i[i], b_gate_f[i], mh_norm[i], conv_w[i], w_out[i],
                  norm_mlp[i], w_up[i], w_down[i], norm_ple[i], w_ple_gate[i], w_ple_proj[i])
        conv0 = jnp.zeros((Bp, CONV_WIDTH - 1, CONV_CH), xp.dtype)
        C0 = jnp.zeros((Bp, MLSTM_HEADS, HEAD_DIM, HEAD_DIM), f32)
        n0 = jnp.zeros((Bp, MLSTM_HEADS, HEAD_DIM), f32)
        m0 = jnp.full((Bp, MLSTM_HEADS), M_INIT, f32)
        xp, b1, c1, n1, m1 = layer(xp, p_prompt[i], conv0, C0, n0, m0, *params)
        pC.append(c1); pn.append(n1); pm.append(m1); pconv.append(b1)
        xs, b2, c2, n2, m2 = layer(xs, p_sample[i], state_conv[i], state_mlstm_C[i],
                                   state_mlstm_n[i], state_mlstm_m[i], *params)
        sC.append(c2); sn.append(n2); sm.append(m2); sconv.append(b2)
    y_prompt = rmsnorm(xp, norm_final)
    y_sample = rmsnorm(xs, norm_final)
    return (y_prompt, y_sample,
            jnp.stack(pC), jnp.stack(pn), jnp.stack(pm), jnp.stack(pconv),
            jnp.stack(sC), jnp.stack(sn), jnp.stack(sm), jnp.stack(sconv))
```

```python
import functools
import math

import jax
import jax.numpy as jnp
from jax import lax
from jax.experimental import pallas as pl
from jax.experimental.pallas import tpu as pltpu

F32 = jnp.float32
BF16 = jnp.bfloat16

D_MODEL = 1024
HEADS = 4
HEAD_DIM = 128
MLSTM_W = HEADS * HEAD_DIM
CONV_CH = D_MODEL - MLSTM_W
CONV_WIDTH = 3
D_FF = 4 * D_MODEL
PLE_DIM = 256
EPS = 1e-6
M_INIT = -1e30
Q_SCALE = HEAD_DIM ** -0.5
LOG2E = math.log2(math.e)

N_MAIN = 4 * MLSTM_W + 3 * CONV_CH
GATE_ROWS = 32
LANES = 128
SUBLANES = 8
VMEM_LIMIT_BYTES = 60000 * 1024

SEQ_TILE = 256
FF_CHUNK = 1024
SAMPLE_TILE = 8


def _dot(a, b):
    return jnp.dot(a, b, preferred_element_type=F32)


def _dot_nt(a, b):
    return lax.dot_general(a, b, (((1,), (1,)), ((), ())), preferred_element_type=F32)


def _dot_tn(a, b):
    return lax.dot_general(a, b, (((0,), (0,)), ((), ())), preferred_element_type=F32)


def _rms(x, g):
    y = x * lax.rsqrt(jnp.mean(x * x, axis=-1, keepdims=True) + EPS)
    return y * g


def _log_sigmoid(x):
    return jnp.minimum(x, 0.0) - jnp.log1p(jnp.exp(-jnp.abs(x)))


def _head(c, h):
    return slice(c * MLSTM_W + h * HEAD_DIM, c * MLSTM_W + (h + 1) * HEAD_DIM)


def _dense_tail(x, mix_bf16, p, wout_ref, nmlp_ref, wup_ref, wdown_ref, nple_ref, wpg_ref, wpp_ref, nfin_ref):
    x1 = x + _dot(mix_bf16, wout_ref[...])
    xn = _rms(x1, nmlp_ref[...]).astype(BF16)
    acc = jnp.zeros_like(x1)
    for c in range(D_FF // FF_CHUNK):
        cols = slice(c * FF_CHUNK, (c + 1) * FF_CHUNK)
        hf = jnp.maximum(_dot(xn, wup_ref[:, cols]), 0.0)
        acc = acc + _dot((hf * hf).astype(BF16), wdown_ref[cols, :])
    x2 = x1 + acc
    gate = jax.nn.sigmoid(_dot(_rms(x2, nple_ref[...]).astype(BF16), wpg_ref[...]))
    x3 = x2 + gate * _dot(p.astype(BF16), wpp_ref[...])
    return _rms(x3, nfin_ref[...])


def _cummax_lanes(a):
    n = a.shape[1]
    lane = lax.broadcasted_iota(jnp.int32, a.shape, 1)
    d = 1
    while d < n:
        shifted = pltpu.roll(a, d, axis=1)
        a = jnp.maximum(a, jnp.where(lane >= d, shifted, -jnp.inf))
        d *= 2
    return a


def _prompt_kernel(x_ref, p_ref, wmain_ref, wgt_ref, bi_ref, bf_ref, nmix_ref, mh_ref, cw_ref,
                   wout_ref, nmlp_ref, wup_ref, wdown_ref, nple_ref, wpg_ref, wpp_ref, nfin_ref, tri_ref,
                   y_ref, cout_ref, mout_ref, convout_ref,
                   caug, m_scr, cu_buf):
    t = SEQ_TILE
    j = pl.program_id(1)

    @pl.when(j == 0)
    def _():
        caug[...] = jnp.zeros_like(caug)
        m_scr[...] = jnp.full(m_scr.shape, M_INIT, F32)
        cu_buf[0:SUBLANES, :] = jnp.zeros((SUBLANES, CONV_CH), F32)

    x = x_ref[0]
    hb = _rms(x, nmix_ref[...]).astype(BF16)
    z = _dot(hb, wmain_ref[...])
    gt = _dot_nt(wgt_ref[...], hb)

    ig = gt[0:SUBLANES] + bi_ref[...]
    lf = _log_sigmoid(gt[16:16 + SUBLANES] + bf_ref[...])
    hi = lf.astype(BF16)
    r1 = lf - hi.astype(F32)
    mid = r1.astype(BF16)
    lo = (r1 - mid.astype(F32)).astype(BF16)
    parts = _dot(jnp.concatenate([hi, mid, lo, jnp.zeros_like(lo)], axis=0), tri_ref[...])
    b = parts[0:8] + parts[8:16] + parts[16:24]
    a = ig - b
    m_prev = jnp.concatenate([m_scr[...]] * (t // LANES), axis=1)
    g = jnp.maximum(_cummax_lanes(a), m_prev)
    m_t = b + g
    b_last = b[:, t - 1:t]
    m_new = m_t[:, t - 1:t]
    decay = jnp.exp(m_prev - g)
    e_neg_m = jnp.exp(-m_t)
    w_state = jnp.exp(a + (b_last - m_new))
    c_dec = jnp.exp(b_last + m_prev[:, 0:1] - m_new)
    m_scr[...] = jnp.broadcast_to(m_new, m_scr.shape)

    a2 = a * LOG2E
    rows = jnp.concatenate(
        [g * (-LOG2E), decay, e_neg_m, w_state, jnp.zeros((LANES - 4 * SUBLANES, t), F32)], axis=0)
    cols = rows.T

    row_id = lax.broadcasted_iota(jnp.int32, (t, t), 0)
    col_id = lax.broadcasted_iota(jnp.int32, (t, t), 1)
    causal = col_id <= row_id
    lane_id = lax.broadcasted_iota(jnp.int32, (t, HEAD_DIM), 1)

    heads_out = []
    for h in range(HEADS):
        qs = (z[:, _head(0, h)] * Q_SCALE).astype(BF16)
        kb = z[:, _head(1, h)].astype(BF16)
        v = z[:, _head(2, h)]
        c_col = cols[:, h:h + 1]
        dec = cols[:, SUBLANES + h:SUBLANES + h + 1]
        enm = cols[:, 2 * SUBLANES + h:2 * SUBLANES + h + 1]
        wst = cols[:, 3 * SUBLANES + h:3 * SUBLANES + h + 1]

        s = _dot_nt(qs, kb)
        dmat = jnp.exp2(jnp.where(causal, c_col + a2[h:h + 1, :], -jnp.inf))
        pm = s * dmat
        row_sum = jnp.sum(pm, axis=1, keepdims=True)
        intra = _dot(pm.astype(BF16), v.astype(BF16))
        c_state = caug[h]
        inter = _dot(qs, c_state.astype(BF16))
        num = intra + dec * inter[:, 0:HEAD_DIM]
        den = row_sum + dec * inter[:, HEAD_DIM:HEAD_DIM + 1]
        hh = num / jnp.maximum(jnp.abs(den), enm)
        hn = hh * lax.rsqrt(jnp.mean(hh * hh, axis=-1, keepdims=True) + EPS)
        hn = hn * mh_ref[:, _head(0, h)]
        heads_out.append(hn * jax.nn.sigmoid(z[:, _head(3, h)]))

        vw = jnp.concatenate([v * wst, jnp.where(lane_id == 0, wst, 0.0)], axis=1).astype(BF16)
        caug[h] = c_dec[h:h + 1, :] * c_state + _dot_tn(kb, vw)

    base = 4 * MLSTM_W
    gb = z[:, base:base + CONV_CH]
    cu = z[:, base + CONV_CH:base + 2 * CONV_CH] * z[:, base + 2 * CONV_CH:base + 3 * CONV_CH]
    cu_buf[SUBLANES:SUBLANES + t, :] = cu
    conv = (cw_ref[0:1, :] * cu_buf[SUBLANES - 2:SUBLANES - 2 + t, :]
            + cw_ref[1:2, :] * cu_buf[SUBLANES - 1:SUBLANES - 1 + t, :]
            + cw_ref[2:3, :] * cu)
    yc = gb * conv
    cu_buf[0:SUBLANES, :] = cu_buf[t:t + SUBLANES, :]

    mix = jnp.concatenate(heads_out + [yc], axis=1).astype(BF16)
    y_ref[0] = _dense_tail(x, mix, p_ref[0], wout_ref, nmlp_ref, wup_ref, wdown_ref,
                           nple_ref, wpg_ref, wpp_ref, nfin_ref)

    @pl.when(j == pl.num_programs(1) - 1)
    def _():
        cout_ref[0] = caug[...]
        mout_ref[0] = m_scr[...]
        convout_ref[0] = cu[t - (CONV_WIDTH - 1):t, :]


def _resident(shape):
    return pl.BlockSpec(shape, lambda *_: (0,) * len(shape), pipeline_mode=pl.Buffered(1))


def _prompt_call(x, p, wts):
    bsz, seq, _ = x.shape
    t = SEQ_TILE
    assert seq % t == 0 and t % LANES == 0
    tri = (jnp.arange(t)[:, None] <= jnp.arange(t)[None, :]).astype(BF16)
    consts = [wts["w_main"], wts["w_gt"], wts["bi8"], wts["bf8"], wts["norm_mix"], wts["mh_norm"], wts["conv_w"],
              wts["w_out"], wts["norm_mlp"], wts["w_up"], wts["w_down"], wts["norm_ple"], wts["w_pg"],
              wts["w_pp"], wts["norm_final"], tri]
    in_specs = [pl.BlockSpec((1, t, D_MODEL), lambda b, j: (b, j, 0)),
                pl.BlockSpec((1, t, PLE_DIM), lambda b, j: (b, j, 0))]
    in_specs += [_resident(c.shape) for c in consts]
    out_shape = (jax.ShapeDtypeStruct((bsz, seq, D_MODEL), F32),
                 jax.ShapeDtypeStruct((bsz, HEADS, HEAD_DIM, 2 * HEAD_DIM), F32),
                 jax.ShapeDtypeStruct((bsz, SUBLANES, LANES), F32),
                 jax.ShapeDtypeStruct((bsz, CONV_WIDTH - 1, CONV_CH), F32))
    out_specs = (pl.BlockSpec((1, t, D_MODEL), lambda b, j: (b, j, 0)),
                 pl.BlockSpec((1, HEADS, HEAD_DIM, 2 * HEAD_DIM), lambda b, j: (b, 0, 0, 0)),
                 pl.BlockSpec((1, SUBLANES, LANES), lambda b, j: (b, 0, 0)),
                 pl.BlockSpec((1, CONV_WIDTH - 1, CONV_CH), lambda b, j: (b, 0, 0)))
    scratch = [pltpu.VMEM((HEADS, HEAD_DIM, 2 * HEAD_DIM), F32),
               pltpu.VMEM((SUBLANES, LANES), F32),
               pltpu.VMEM((t + SUBLANES, CONV_CH), F32)]
    return pl.pallas_call(
        _prompt_kernel,
        grid=(bsz, seq // t),
        in_specs=in_specs,
        out_specs=out_specs,
        out_shape=out_shape,
        scratch_shapes=scratch,
        compiler_params=pltpu.CompilerParams(
            dimension_semantics=("arbitrary", "arbitrary"),
            vmem_limit_bytes=VMEM_LIMIT_BYTES),
        name="prompt_layer",
    )(x, p, *consts)


def _sample_inproj_kernel(x_ref, wmain_ref, wgc_ref, nmix_ref, z_ref, g_ref):
    hb = _rms(x_ref[...], nmix_ref[...]).astype(BF16)
    z_ref[...] = _dot(hb, wmain_ref[...])
    g_ref[...] = _dot(hb, wgc_ref[...])


def _sample_state_kernel(zq_ref, zk_ref, zv_ref, zo_ref, g_ref, c0_ref, n0_ref, m0_ref, bi_ref, bf_ref, mh_ref,
                         hm_ref, cnew_ref, nnew_ref, mnew_ref,
                         inter_scr, wv_scr, cd_scr):
    tb = SAMPLE_TILE
    g = g_ref[...]
    ig = g[:, 0:HEADS] + bi_ref[...]
    lf = _log_sigmoid(g[:, HEADS:2 * HEADS] + bf_ref[...])
    m_inter = lf + m0_ref[...]
    m_new = jnp.maximum(m_inter, ig)
    w_in = jnp.exp(ig - m_new)
    c_dec = jnp.exp(m_inter - m_new)
    e_neg_m = jnp.exp(-m_new)
    mnew_ref[...] = m_new

    qs, ks, vs, scores, qns = [], [], [], [], []
    for h in range(HEADS):
        hs = _head(0, h)
        q = zq_ref[:, hs] * Q_SCALE
        k = zk_ref[:, hs]
        v = zv_ref[:, hs]
        n0 = n0_ref[:, hs]
        wi = w_in[:, h:h + 1]
        cd = c_dec[:, h:h + 1]
        scores.append(jnp.sum(q * k, axis=1, keepdims=True) * wi)
        qns.append(jnp.sum(q * n0, axis=1, keepdims=True))
        wv_scr[:, hs] = wi * v
        cd_scr[:, hs] = jnp.broadcast_to(cd, (tb, HEAD_DIM))
        nnew_ref[:, hs] = cd * n0 + wi * k
        qs.append(q)
        ks.append(k)
        vs.append(v)

    eye = (lax.broadcasted_iota(jnp.int32, (HEAD_DIM, HEAD_DIM), 0)
           == lax.broadcasted_iota(jnp.int32, (HEAD_DIM, HEAD_DIM), 1))

    for i in range(tb):
        row = slice(i, i + 1)
        for h in range(HEADS):
            hs = _head(0, h)
            c0 = c0_ref[i, h]
            q_rows = jnp.broadcast_to(zq_ref[row, hs] * Q_SCALE, (SUBLANES, HEAD_DIM)).astype(BF16)
            inter_scr[row, hs] = _dot(q_rows, c0.astype(BF16))[0:1, :]
            k_diag = jnp.where(eye, jnp.broadcast_to(zk_ref[row, hs], (HEAD_DIM, HEAD_DIM)), 0.0).astype(BF16)
            v_rows = jnp.broadcast_to(wv_scr[row, hs], (HEAD_DIM, HEAD_DIM)).astype(BF16)
            cd = jnp.broadcast_to(cd_scr[row, hs], (HEAD_DIM, HEAD_DIM))
            cnew_ref[i, h] = cd * c0 + _dot(k_diag, v_rows)

    for h in range(HEADS):
        hs = _head(0, h)
        cd = c_dec[:, h:h + 1]
        num = scores[h] * vs[h] + cd * inter_scr[:, hs]
        den = scores[h] + cd * qns[h]
        hh = num / jnp.maximum(jnp.abs(den), e_neg_m[:, h:h + 1])
        hn = hh * lax.rsqrt(jnp.mean(hh * hh, axis=-1, keepdims=True) + EPS)
        hm_ref[:, hs] = hn * mh_ref[:, hs] * jax.nn.sigmoid(zo_ref[:, hs])


def _sample_tail_kernel(x_ref, p_ref, hm_ref, z_ref, sc_ref, cw_ref,
                        wout_ref, nmlp_ref, wup_ref, wdown_ref, nple_ref, wpg_ref, wpp_ref, nfin_ref,
                        y_ref, buf_ref):
    base = 4 * MLSTM_W
    gb = z_ref[:, base:base + CONV_CH]
    cu = z_ref[:, base + CONV_CH:base + 2 * CONV_CH] * z_ref[:, base + 2 * CONV_CH:base + 3 * CONV_CH]
    old0 = sc_ref[:, 0:CONV_CH]
    old1 = sc_ref[:, CONV_CH:2 * CONV_CH]
    yc = gb * (cw_ref[0:1, :] * old0 + cw_ref[1:2, :] * old1 + cw_ref[2:3, :] * cu)
    buf_ref[:, 0:CONV_CH] = old1
    buf_ref[:, CONV_CH:2 * CONV_CH] = cu
    mix = jnp.concatenate([hm_ref[...], yc], axis=1).astype(BF16)
    y_ref[...] = _dense_tail(x_ref[...], mix, p_ref[...], wout_ref, nmlp_ref, wup_ref, wdown_ref,
                             nple_ref, wpg_ref, wpp_ref, nfin_ref)


def _sample_calls(xs, ps, c0, n0, m0, sconv, wts):
    nb = xs.shape[0]
    tb = SAMPLE_TILE
    assert nb % tb == 0
    z, g = pl.pallas_call(
        _sample_inproj_kernel,
        out_shape=(jax.ShapeDtypeStruct((nb, N_MAIN), F32), jax.ShapeDtypeStruct((nb, LANES), F32)),
        compiler_params=pltpu.CompilerParams(vmem_limit_bytes=VMEM_LIMIT_BYTES),
        name="sample_inproj",
    )(xs, wts["w_main"], wts["w_gc"], wts["norm_mix"])

    def zcol(c):
        return pl.BlockSpec((tb, MLSTM_W), lambda i, c=c: (i, c))

    row_w = pl.BlockSpec((tb, MLSTM_W), lambda i: (i, 0))
    row_h = pl.BlockSpec((tb, HEADS), lambda i: (i, 0))
    state = pl.BlockSpec((tb, HEADS, HEAD_DIM, HEAD_DIM), lambda i: (i, 0, 0, 0))
    hm, c_new, n_new, m_new = pl.pallas_call(
        _sample_state_kernel,
        grid=(nb // tb,),
        in_specs=[zcol(0), zcol(1), zcol(2), zcol(3),
                  pl.BlockSpec((tb, LANES), lambda i: (i, 0)),
                  state, row_w, row_h,
                  pl.BlockSpec((1, HEADS), lambda i: (0, 0)),
                  pl.BlockSpec((1, HEADS), lambda i: (0, 0)),
                  pl.BlockSpec((1, MLSTM_W), lambda i: (0, 0))],
        out_specs=(row_w, state, row_w, row_h),
        out_shape=(jax.ShapeDtypeStruct((nb, MLSTM_W), F32),
                   jax.ShapeDtypeStruct((nb, HEADS, HEAD_DIM, HEAD_DIM), F32),
                   jax.ShapeDtypeStruct((nb, MLSTM_W), F32),
                   jax.ShapeDtypeStruct((nb, HEADS), F32)),
        scratch_shapes=[pltpu.VMEM((tb, MLSTM_W), F32)] * 3,
        compiler_params=pltpu.CompilerParams(dimension_semantics=("arbitrary",)),
        name="sample_state",
    )(z, z, z, z, g, c0, n0, m0, wts["b_i"], wts["b_f"], wts["mh_norm"])

    ys, new_buf = pl.pallas_call(
        _sample_tail_kernel,
        out_shape=(jax.ShapeDtypeStruct((nb, D_MODEL), F32),
                   jax.ShapeDtypeStruct((nb, (CONV_WIDTH - 1) * CONV_CH), F32)),
        compiler_params=pltpu.CompilerParams(vmem_limit_bytes=VMEM_LIMIT_BYTES),
        name="sample_tail",
    )(xs, ps, hm, z, sconv, wts["conv_w"], wts["w_out"], wts["norm_mlp"], wts["w_up"], wts["w_down"],
      wts["norm_ple"], wts["w_pg"], wts["w_pp"], wts["norm_final"])
    return ys, c_new, n_new, m_new, new_buf


def _prepare_weights(norm_mix, w_in, b_gate_i, b_gate_f, mh_norm, conv_w, w_out, norm_mlp, w_up, w_down,
                     norm_ple, w_ple_gate, w_ple_proj, norm_final):
    g0 = 4 * MLSTM_W
    w_i = w_in[:, g0:g0 + HEADS]
    w_f = w_in[:, g0 + HEADS:g0 + 2 * HEADS]
    w_gt = jnp.zeros((GATE_ROWS, D_MODEL), F32)
    w_gt = w_gt.at[0:HEADS].set(w_i.T).at[16:16 + HEADS].set(w_f.T)
    w_gc = jnp.zeros((D_MODEL, LANES), F32).at[:, 0:HEADS].set(w_i).at[:, HEADS:2 * HEADS].set(w_f)
    pad = jnp.zeros((SUBLANES - HEADS, 1), F32)
    return dict(
        w_main=jnp.concatenate([w_in[:, :g0], w_in[:, g0 + 2 * HEADS:]], axis=1).astype(BF16),
        w_gt=w_gt.astype(BF16),
        w_gc=w_gc.astype(BF16),
        bi8=jnp.concatenate([b_gate_i.reshape(HEADS, 1), pad], axis=0),
        bf8=jnp.concatenate([b_gate_f.reshape(HEADS, 1), pad], axis=0),
        b_i=b_gate_i.reshape(1, HEADS),
        b_f=b_gate_f.reshape(1, HEADS),
        norm_mix=norm_mix.reshape(1, D_MODEL),
        mh_norm=mh_norm.reshape(1, MLSTM_W),
        conv_w=conv_w,
        w_out=w_out.astype(BF16),
        norm_mlp=norm_mlp.reshape(1, D_MODEL),
        w_up=w_up.astype(BF16),
        w_down=w_down.astype(BF16),
        norm_ple=norm_ple.reshape(1, D_MODEL),
        w_pg=w_ple_gate.astype(BF16),
        w_pp=w_ple_proj.astype(BF16),
        norm_final=norm_final.reshape(1, D_MODEL),
    )


def kernel(x_prompt, x_sample, state_mlstm_C, state_mlstm_n, state_mlstm_m, state_conv, p_prompt, p_sample,
           norm_mix, w_in, b_gate_i, b_gate_f, mh_norm, conv_w, w_out, norm_mlp, w_up, w_down, norm_ple,
           w_ple_gate, w_ple_proj, norm_final):
    assert norm_mix.shape[0] == 1, "single-layer trunk"
    wts = _prepare_weights(norm_mix[0], w_in[0], b_gate_i[0], b_gate_f[0], mh_norm[0], conv_w[0], w_out[0],
                           norm_mlp[0], w_up[0], w_down[0], norm_ple[0], w_ple_gate[0], w_ple_proj[0], norm_final)

    y_prompt, c_aug, m_rows, conv_tail = _prompt_call(x_prompt, p_prompt[0], wts)
    prompt_c = c_aug[:, :, :, 0:HEAD_DIM]
    prompt_n = c_aug[:, :, :, HEAD_DIM]
    prompt_m = m_rows[:, 0:HEADS, 0]

    nb = x_sample.shape[0]
    ys, c_new, n_new, m_new, new_buf = _sample_calls(
        x_sample.reshape(nb, D_MODEL), p_sample[0].reshape(nb, PLE_DIM), state_mlstm_C[0],
        state_mlstm_n[0].reshape(nb, MLSTM_W), state_mlstm_m[0],
        state_conv[0].reshape(nb, (CONV_WIDTH - 1) * CONV_CH), wts)

    return (y_prompt, ys.reshape(nb, 1, D_MODEL),
            prompt_c[None], prompt_n[None], prompt_m[None], conv_tail[None],
            c_new[None], n_new.reshape(1, nb, HEADS, HEAD_DIM), m_new[None],
            new_buf.reshape(1, nb, CONV_WIDTH - 1, CONV_CH))
```

```python
import functools
import math

import jax
import jax.numpy as jnp
from jax import lax
from jax.experimental import pallas as pl
from jax.experimental.pallas import tpu as pltpu

F32 = jnp.float32
BF16 = jnp.bfloat16

D_MODEL = 1024
HEADS = 4
HEAD_DIM = 128
MLSTM_W = HEADS * HEAD_DIM
CONV_CH = D_MODEL - MLSTM_W
CONV_WIDTH = 3
D_FF = 4 * D_MODEL
PLE_DIM = 256
EPS = 1e-6
M_INIT = -1e30
Q_SCALE = HEAD_DIM ** -0.5
LOG2E = math.log2(math.e)

HEAD_COLS = 4 * HEAD_DIM
N_MAIN = HEADS * HEAD_COLS + 3 * CONV_CH
GATE_ROWS = 32
LANES = 128
SUBLANES = 8
VMEM_LIMIT_BYTES = 60000 * 1024

SEQ_TILE = 256
FF_CHUNK = 512
SAMPLE_TILE = 8


def _dot(a, b):
    return jnp.dot(a, b, preferred_element_type=F32)


def _dot_nt(a, b):
    return lax.dot_general(a, b, (((1,), (1,)), ((), ())), preferred_element_type=F32)


def _dot_tn(a, b):
    return lax.dot_general(a, b, (((0,), (0,)), ((), ())), preferred_element_type=F32)


def _rms(x, g):
    y = x * lax.rsqrt(jnp.mean(x * x, axis=-1, keepdims=True) + EPS)
    return y * g


def _log_sigmoid(x):
    return jnp.minimum(x, 0.0) - jnp.log1p(jnp.exp(-jnp.abs(x)))


def _mlp_chunk(xn_bf16, c, wup_ref, wdown_ref):
    cols = slice(c * FF_CHUNK, (c + 1) * FF_CHUNK)
    hf = jnp.maximum(_dot(xn_bf16, wup_ref[:, cols]), 0.0)
    return _dot((hf * hf).astype(BF16), wdown_ref[cols, :])


def _ple_and_final(x2, p, nple_ref, wpg_ref, wpp_ref, nfin_ref):
    gate = jax.nn.sigmoid(_dot(_rms(x2, nple_ref[...]).astype(BF16), wpg_ref[...]))
    x3 = x2 + gate * _dot(p.astype(BF16), wpp_ref[...])
    return _rms(x3, nfin_ref[...])


def _cummax_lanes(a):
    n = a.shape[1]
    lane = lax.broadcasted_iota(jnp.int32, a.shape, 1)
    d = 1
    while d < n:
        shifted = pltpu.roll(a, d, axis=1)
        a = jnp.maximum(a, jnp.where(lane >= d, shifted, -jnp.inf))
        d *= 2
    return a


def _prompt_kernel(tiles_per_seq, num_tiles,
                   x_ref, xlag_ref, plag_ref, wmain_ref, wgt_ref, bi_ref, bf_ref, nmix_ref, mh_ref, cw_ref,
                   wout_ref, nmlp_ref, wup_ref, wdown_ref, nple_ref, wpg_ref, wpp_ref, nfin_ref, tri_ref,
                   y_ref, cout_ref, mout_ref, convout_ref,
                   caug, m_scr, cu_buf, mix_scr):
    t = SEQ_TILE
    s_id = pl.program_id(0)
    is_real = s_id < num_tiles
    j = lax.rem(jnp.minimum(s_id, num_tiles - 1), tiles_per_seq)

    @pl.when(s_id == 0)
    def _():
        mix_scr[...] = jnp.zeros_like(mix_scr)

    @pl.when(j == 0)
    def _():
        caug[...] = jnp.zeros_like(caug)
        m_scr[...] = jnp.full(m_scr.shape, M_INIT, F32)
        cu_buf[0:SUBLANES, :] = jnp.zeros((SUBLANES, CONV_CH), F32)

    hb = _rms(x_ref[0], nmix_ref[...]).astype(BF16)
    gt = _dot_nt(wgt_ref[...], hb)
    ig = gt[0:SUBLANES] + bi_ref[...]
    lf = _log_sigmoid(gt[16:16 + SUBLANES] + bf_ref[...])
    hi = lf.astype(BF16)
    r1 = lf - hi.astype(F32)
    mid = r1.astype(BF16)
    lo = (r1 - mid.astype(F32)).astype(BF16)
    parts = _dot(jnp.concatenate([hi, mid, lo, jnp.zeros_like(lo)], axis=0), tri_ref[...])
    b = parts[0:8] + parts[8:16] + parts[16:24]
    a = ig - b
    m_prev = jnp.concatenate([m_scr[...]] * (t // LANES), axis=1)
    g = jnp.maximum(_cummax_lanes(a), m_prev)
    m_t = b + g
    b_last = b[:, t - 1:t]
    m_new = m_t[:, t - 1:t]
    decay = jnp.exp(m_prev - g)
    e_neg_m = jnp.exp(-m_t)
    w_state = jnp.exp(a + (b_last - m_new))
    c_dec = jnp.exp(b_last + m_prev[:, 0:1] - m_new)
    m_scr[...] = jnp.broadcast_to(m_new, m_scr.shape)
    a2 = a * LOG2E
    rows = jnp.concatenate(
        [g * (-LOG2E), decay, e_neg_m, w_state, jnp.zeros((LANES - 4 * SUBLANES, t), F32)], axis=0)
    cols = rows.T

    x1 = xlag_ref[0] + _dot(mix_scr[...], wout_ref[...])
    xn = _rms(x1, nmlp_ref[...]).astype(BF16)

    zc = _dot(hb, wmain_ref[:, HEADS * HEAD_COLS:N_MAIN])
    cu = zc[:, CONV_CH:2 * CONV_CH] * zc[:, 2 * CONV_CH:3 * CONV_CH]
    cu_buf[SUBLANES:SUBLANES + t, :] = cu
    conv = (cw_ref[0:1, :] * cu_buf[SUBLANES - 2:SUBLANES - 2 + t, :]
            + cw_ref[1:2, :] * cu_buf[SUBLANES - 1:SUBLANES - 1 + t, :]
            + cw_ref[2:3, :] * cu)
    mix_scr[:, MLSTM_W:D_MODEL] = (zc[:, 0:CONV_CH] * conv).astype(BF16)
    cu_buf[0:SUBLANES, :] = cu_buf[t:t + SUBLANES, :]

    row_id = lax.broadcasted_iota(jnp.int32, (t, t), 0)
    col_id = lax.broadcasted_iota(jnp.int32, (t, t), 1)
    causal = col_id <= row_id
    lane_id = lax.broadcasted_iota(jnp.int32, (t, HEAD_DIM), 1)

    acc = jnp.zeros((t, D_MODEL), F32)
    chunks_per_head = D_FF // FF_CHUNK // HEADS
    next_chunk = 0
    for h in range(HEADS):
        zh = _dot(hb, wmain_ref[:, h * HEAD_COLS:(h + 1) * HEAD_COLS])
        qs = (zh[:, 0:HEAD_DIM] * Q_SCALE).astype(BF16)
        kb = zh[:, HEAD_DIM:2 * HEAD_DIM].astype(BF16)
        v = zh[:, 2 * HEAD_DIM:3 * HEAD_DIM]
        c_col = cols[:, h:h + 1]
        dec = cols[:, SUBLANES + h:SUBLANES + h + 1]
        enm = cols[:, 2 * SUBLANES + h:2 * SUBLANES + h + 1]
        wst = cols[:, 3 * SUBLANES + h:3 * SUBLANES + h + 1]
        s = _dot_nt(qs, kb)

        for _ in range(chunks_per_head // 2):
            acc = acc + _mlp_chunk(xn, next_chunk, wup_ref, wdown_ref)
            next_chunk += 1

        dmat = jnp.exp2(jnp.where(causal, c_col + a2[h:h + 1, :], -jnp.inf))
        pm = s * dmat
        row_sum = jnp.sum(pm, axis=1, keepdims=True)
        intra = _dot(pm.astype(BF16), v.astype(BF16))
        c_state = caug[h]
        inter = _dot(qs, c_state.astype(BF16))
        vw = jnp.concatenate([v * wst, jnp.where(lane_id == 0, wst, 0.0)], axis=1).astype(BF16)
        caug[h] = c_dec[h:h + 1, :] * c_state + _dot_tn(kb, vw)

        for _ in range(chunks_per_head - chunks_per_head // 2):
            acc = acc + _mlp_chunk(xn, next_chunk, wup_ref, wdown_ref)
            next_chunk += 1

        num = intra + dec * inter[:, 0:HEAD_DIM]
        den = row_sum + dec * inter[:, HEAD_DIM:HEAD_DIM + 1]
        hh = num / jnp.maximum(jnp.abs(den), enm)
        hn = hh * lax.rsqrt(jnp.mean(hh * hh, axis=-1, keepdims=True) + EPS)
        hn = hn * mh_ref[:, h * HEAD_DIM:(h + 1) * HEAD_DIM]
        mix_scr[:, h * HEAD_DIM:(h + 1) * HEAD_DIM] = (
            hn * jax.nn.sigmoid(zh[:, 3 * HEAD_DIM:4 * HEAD_DIM])).astype(BF16)

    y_ref[0] = _ple_and_final(x1 + acc, plag_ref[0], nple_ref, wpg_ref, wpp_ref, nfin_ref)

    @pl.when(jnp.logical_and(is_real, j == tiles_per_seq - 1))
    def _():
        cout_ref[0] = caug[...]
        mout_ref[0] = m_scr[...]
        convout_ref[0] = cu[t - (CONV_WIDTH - 1):t, :]


def _resident(shape):
    return pl.BlockSpec(shape, lambda *_: (0,) * len(shape), pipeline_mode=pl.Buffered(1))


def _prompt_call(x, p, wts):
    bsz, seq, _ = x.shape
    t = SEQ_TILE
    assert seq % t == 0 and t % LANES == 0
    tri = (jnp.arange(t)[:, None] <= jnp.arange(t)[None, :]).astype(BF16)
    consts = [wts["w_main"], wts["w_gt"], wts["bi8"], wts["bf8"], wts["norm_mix"], wts["mh_norm"], wts["conv_w"],
              wts["w_out"], wts["norm_mlp"], wts["w_up"], wts["w_down"], wts["norm_ple"], wts["w_pg"],
              wts["w_pp"], wts["norm_final"], tri]
    nj = seq // t
    nt = bsz * nj

    def cur(s):
        c = jnp.minimum(s, nt - 1)
        return c // nj, c % nj

    def lag(s):
        c = jnp.maximum(s - 1, 0)
        return c // nj, c % nj

    in_specs = [pl.BlockSpec((1, t, D_MODEL), lambda s: (*cur(s), 0)),
                pl.BlockSpec((1, t, D_MODEL), lambda s: (*lag(s), 0)),
                pl.BlockSpec((1, t, PLE_DIM), lambda s: (*lag(s), 0))]
    in_specs += [_resident(c.shape) for c in consts]
    out_shape = (jax.ShapeDtypeStruct((bsz, seq, D_MODEL), F32),
                 jax.ShapeDtypeStruct((bsz, HEADS, HEAD_DIM, 2 * HEAD_DIM), F32),
                 jax.ShapeDtypeStruct((bsz, SUBLANES, LANES), F32),
                 jax.ShapeDtypeStruct((bsz, CONV_WIDTH - 1, CONV_CH), F32))
    out_specs = (pl.BlockSpec((1, t, D_MODEL), lambda s: (*lag(s), 0)),
                 pl.BlockSpec((1, HEADS, HEAD_DIM, 2 * HEAD_DIM), lambda s: (cur(s)[0], 0, 0, 0)),
                 pl.BlockSpec((1, SUBLANES, LANES), lambda s: (cur(s)[0], 0, 0)),
                 pl.BlockSpec((1, CONV_WIDTH - 1, CONV_CH), lambda s: (cur(s)[0], 0, 0)))
    scratch = [pltpu.VMEM((HEADS, HEAD_DIM, 2 * HEAD_DIM), F32),
               pltpu.VMEM((SUBLANES, LANES), F32),
               pltpu.VMEM((t + SUBLANES, CONV_CH), F32),
               pltpu.VMEM((t, D_MODEL), BF16)]
    return pl.pallas_call(
        functools.partial(_prompt_kernel, nj, nt),
        grid=(nt + 1,),
        in_specs=in_specs,
        out_specs=out_specs,
        out_shape=out_shape,
        scratch_shapes=scratch,
        compiler_params=pltpu.CompilerParams(
            dimension_semantics=("arbitrary",),
            vmem_limit_bytes=VMEM_LIMIT_BYTES),
        name="prompt_layer",
    )(x, x, p, *consts)


def _sample_inproj_kernel(x_ref, wmain_ref, wgc_ref, nmix_ref, z_ref, g_ref):
    hb = _rms(x_ref[...], nmix_ref[...]).astype(BF16)
    z_ref[...] = _dot(hb, wmain_ref[...])
    g_ref[...] = _dot(hb, wgc_ref[...])


def _sample_state_kernel(z0_ref, z1_ref, z2_ref, z3_ref, g_ref, c0_ref, n0_ref, m0_ref, bi_ref, bf_ref, mh_ref,
                         hm_ref, cnew_ref, nnew_ref, mnew_ref,
                         inter_scr, wv_scr, cd_scr):
    tb = SAMPLE_TILE
    zh_refs = (z0_ref, z1_ref, z2_ref, z3_ref)
    q_cols = slice(0, HEAD_DIM)
    k_cols = slice(HEAD_DIM, 2 * HEAD_DIM)
    v_cols = slice(2 * HEAD_DIM, 3 * HEAD_DIM)
    o_cols = slice(3 * HEAD_DIM, 4 * HEAD_DIM)
    g = g_ref[...]
    ig = g[:, 0:HEADS] + bi_ref[...]
    lf = _log_sigmoid(g[:, HEADS:2 * HEADS] + bf_ref[...])
    m_inter = lf + m0_ref[...]
    m_new = jnp.maximum(m_inter, ig)
    w_in = jnp.exp(ig - m_new)
    c_dec = jnp.exp(m_inter - m_new)
    e_neg_m = jnp.exp(-m_new)
    mnew_ref[...] = m_new

    vs, scores, qns = [], [], []
    for h in range(HEADS):
        hs = slice(h * HEAD_DIM, (h + 1) * HEAD_DIM)
        q = zh_refs[h][:, q_cols] * Q_SCALE
        k = zh_refs[h][:, k_cols]
        v = zh_refs[h][:, v_cols]
        n0 = n0_ref[:, hs]
        wi = w_in[:, h:h + 1]
        cd = c_dec[:, h:h + 1]
        scores.append(jnp.sum(q * k, axis=1, keepdims=True) * wi)
        qns.append(jnp.sum(q * n0, axis=1, keepdims=True))
        wv_scr[:, hs] = wi * v
        cd_scr[:, hs] = jnp.broadcast_to(cd, (tb, HEAD_DIM))
        nnew_ref[:, hs] = cd * n0 + wi * k
        vs.append(v)

    eye = (lax.broadcasted_iota(jnp.int32, (HEAD_DIM, HEAD_DIM), 0)
           == lax.broadcasted_iota(jnp.int32, (HEAD_DIM, HEAD_DIM), 1))

    for i in range(tb):
        row = slice(i, i + 1)
        for h in range(HEADS):
            hs = slice(h * HEAD_DIM, (h + 1) * HEAD_DIM)
            c0 = c0_ref[i, h]
            q_rows = jnp.broadcast_to(zh_refs[h][row, q_cols] * Q_SCALE, (SUBLANES, HEAD_DIM)).astype(BF16)
            inter_scr[row, hs] = _dot(q_rows, c0.astype(BF16))[0:1, :]
            k_diag = jnp.where(eye, jnp.broadcast_to(zh_refs[h][row, k_cols], (HEAD_DIM, HEAD_DIM)),
                               0.0).astype(BF16)
            v_rows = jnp.broadcast_to(wv_scr[row, hs], (HEAD_DIM, HEAD_DIM)).astype(BF16)
            cd = jnp.broadcast_to(cd_scr[row, hs], (HEAD_DIM, HEAD_DIM))
            cnew_ref[i, h] = cd * c0 + _dot(k_diag, v_rows)

    for h in range(HEADS):
        hs = slice(h * HEAD_DIM, (h + 1) * HEAD_DIM)
        cd = c_dec[:, h:h + 1]
        num = scores[h] * vs[h] + cd * inter_scr[:, hs]
        den = scores[h] + cd * qns[h]
        hh = num / jnp.maximum(jnp.abs(den), e_neg_m[:, h:h + 1])
        hn = hh * lax.rsqrt(jnp.mean(hh * hh, axis=-1, keepdims=True) + EPS)
        hm_ref[:, hs] = hn * mh_ref[:, hs] * jax.nn.sigmoid(zh_refs[h][:, o_cols])


def _sample_tail_kernel(x_ref, p_ref, hm_ref, z_ref, sc_ref, cw_ref,
                        wout_ref, nmlp_ref, wup_ref, wdown_ref, nple_ref, wpg_ref, wpp_ref, nfin_ref,
                        y_ref, buf_ref):
    base = HEADS * HEAD_COLS
    gb = z_ref[:, base:base + CONV_CH]
    cu = z_ref[:, base + CONV_CH:base + 2 * CONV_CH] * z_ref[:, base + 2 * CONV_CH:base + 3 * CONV_CH]
    old0 = sc_ref[:, 0:CONV_CH]
    old1 = sc_ref[:, CONV_CH:2 * CONV_CH]
    yc = gb * (cw_ref[0:1, :] * old0 + cw_ref[1:2, :] * old1 + cw_ref[2:3, :] * cu)
    buf_ref[:, 0:CONV_CH] = old1
    buf_ref[:, CONV_CH:2 * CONV_CH] = cu
    mix = jnp.concatenate([hm_ref[...], yc], axis=1).astype(BF16)
    x1 = x_ref[...] + _dot(mix, wout_ref[...])
    xn = _rms(x1, nmlp_ref[...]).astype(BF16)
    acc = jnp.zeros_like(x1)
    for c in range(D_FF // FF_CHUNK):
        acc = acc + _mlp_chunk(xn, c, wup_ref, wdown_ref)
    y_ref[...] = _ple_and_final(x1 + acc, p_ref[...], nple_ref, wpg_ref, wpp_ref, nfin_ref)


def _sample_calls(xs, ps, c0, n0, m0, sconv, wts):
    nb = xs.shape[0]
    tb = SAMPLE_TILE
    assert nb % tb == 0
    z, g = pl.pallas_call(
        _sample_inproj_kernel,
        out_shape=(jax.ShapeDtypeStruct((nb, N_MAIN), F32), jax.ShapeDtypeStruct((nb, LANES), F32)),
        compiler_params=pltpu.CompilerParams(vmem_limit_bytes=VMEM_LIMIT_BYTES),
        name="sample_inproj",
    )(xs, wts["w_main"], wts["w_gc"], wts["norm_mix"])

    def zhead(h):
        return pl.BlockSpec((tb, HEAD_COLS), lambda i, h=h: (i, h))

    row_w = pl.BlockSpec((tb, MLSTM_W), lambda i: (i, 0))
    row_h = pl.BlockSpec((tb, HEADS), lambda i: (i, 0))
    state = pl.BlockSpec((tb, HEADS, HEAD_DIM, HEAD_DIM), lambda i: (i, 0, 0, 0))
    hm, c_new, n_new, m_new = pl.pallas_call(
        _sample_state_kernel,
        grid=(nb // tb,),
        in_specs=[zhead(0), zhead(1), zhead(2), zhead(3),
                  pl.BlockSpec((tb, LANES), lambda i: (i, 0)),
                  state, row_w, row_h,
                  pl.BlockSpec((1, HEADS), lambda i: (0, 0)),
                  pl.BlockSpec((1, HEADS), lambda i: (0, 0)),
                  pl.BlockSpec((1, MLSTM_W), lambda i: (0, 0))],
        out_specs=(row_w, state, row_w, row_h),
        out_shape=(jax.ShapeDtypeStruct((nb, MLSTM_W), F32),
                   jax.ShapeDtypeStruct((nb, HEADS, HEAD_DIM, HEAD_DIM), F32),
                   jax.ShapeDtypeStruct((nb, MLSTM_W), F32),
                   jax.ShapeDtypeStruct((nb, HEADS), F32)),
        scratch_shapes=[pltpu.VMEM((tb, MLSTM_W), F32)] * 3,
        compiler_params=pltpu.CompilerParams(dimension_semantics=("arbitrary",)),
        name="sample_state",
    )(z, z, z, z, g, c0, n0, m0, wts["b_i"], wts["b_f"], wts["mh_norm"])

    ys, new_buf = pl.pallas_call(
        _sample_tail_kernel,
        out_shape=(jax.ShapeDtypeStruct((nb, D_MODEL), F32),
                   jax.ShapeDtypeStruct((nb, (CONV_WIDTH - 1) * CONV_CH), F32)),
        compiler_params=pltpu.CompilerParams(vmem_limit_bytes=VMEM_LIMIT_BYTES),
        name="sample_tail",
    )(xs, ps, hm, z, sconv, wts["conv_w"], wts["w_out"], wts["norm_mlp"], wts["w_up"], wts["w_down"],
      wts["norm_ple"], wts["w_pg"], wts["w_pp"], wts["norm_final"])
    return ys, c_new, n_new, m_new, new_buf


def _prepare_weights(norm_mix, w_in, b_gate_i, b_gate_f, mh_norm, conv_w, w_out, norm_mlp, w_up, w_down,
                     norm_ple, w_ple_gate, w_ple_proj, norm_final):
    g0 = 4 * MLSTM_W
    w_i = w_in[:, g0:g0 + HEADS]
    w_f = w_in[:, g0 + HEADS:g0 + 2 * HEADS]
    w_gt = jnp.zeros((GATE_ROWS, D_MODEL), F32)
    w_gt = w_gt.at[0:HEADS].set(w_i.T).at[16:16 + HEADS].set(w_f.T)
    w_gc = jnp.zeros((D_MODEL, LANES), F32).at[:, 0:HEADS].set(w_i).at[:, HEADS:2 * HEADS].set(w_f)
    pad = jnp.zeros((SUBLANES - HEADS, 1), F32)
    per_head = w_in[:, :g0].reshape(D_MODEL, 4, HEADS, HEAD_DIM).transpose(0, 2, 1, 3).reshape(D_MODEL, g0)
    return dict(
        w_main=jnp.concatenate([per_head, w_in[:, g0 + 2 * HEADS:]], axis=1).astype(BF16),
        w_gt=w_gt.astype(BF16),
        w_gc=w_gc.astype(BF16),
        bi8=jnp.concatenate([b_gate_i.reshape(HEADS, 1), pad], axis=0),
        bf8=jnp.concatenate([b_gate_f.reshape(HEADS, 1), pad], axis=0),
        b_i=b_gate_i.reshape(1, HEADS),
        b_f=b_gate_f.reshape(1, HEADS),
        norm_mix=norm_mix.reshape(1, D_MODEL),
        mh_norm=mh_norm.reshape(1, MLSTM_W),
        conv_w=conv_w,
        w_out=w_out.astype(BF16),
        norm_mlp=norm_mlp.reshape(1, D_MODEL),
        w_up=w_up.astype(BF16),
        w_down=w_down.astype(BF16),
        norm_ple=norm_ple.reshape(1, D_MODEL),
        w_pg=w_ple_gate.astype(BF16),
        w_pp=w_ple_proj.astype(BF16),
        norm_final=norm_final.reshape(1, D_MODEL),
    )


def kernel(x_prompt, x_sample, state_mlstm_C, state_mlstm_n, state_mlstm_m, state_conv, p_prompt, p_sample,
           norm_mix, w_in, b_gate_i, b_gate_f, mh_norm, conv_w, w_out, norm_mlp, w_up, w_down, norm_ple,
           w_ple_gate, w_ple_proj, norm_final):
    assert norm_mix.shape[0] == 1, "single-layer trunk"
    wts = _prepare_weights(norm_mix[0], w_in[0], b_gate_i[0], b_gate_f[0], mh_norm[0], conv_w[0], w_out[0],
                           norm_mlp[0], w_up[0], w_down[0], norm_ple[0], w_ple_gate[0], w_ple_proj[0], norm_final)

    y_prompt, c_aug, m_rows, conv_tail = _prompt_call(x_prompt, p_prompt[0], wts)
    prompt_c = c_aug[:, :, :, 0:HEAD_DIM]
    prompt_n = c_aug[:, :, :, HEAD_DIM]
    prompt_m = m_rows[:, 0:HEADS, 0]

    nb = x_sample.shape[0]
    ys, c_new, n_new, m_new, new_buf = _sample_calls(
        x_sample.reshape(nb, D_MODEL), p_sample[0].reshape(nb, PLE_DIM), state_mlstm_C[0],
        state_mlstm_n[0].reshape(nb, MLSTM_W), state_mlstm_m[0],
        state_conv[0].reshape(nb, (CONV_WIDTH - 1) * CONV_CH), wts)

    return (y_prompt, ys.reshape(nb, 1, D_MODEL),
            prompt_c[None], prompt_n[None], prompt_m[None], conv_tail[None],
            c_new[None], n_new.reshape(1, nb, HEADS, HEAD_DIM), m_new[None],
            new_buf.reshape(1, nb, CONV_WIDTH - 1, CONV_CH))
```

```python
import functools
import math

import jax
import jax.numpy as jnp
from jax import lax
from jax.experimental import pallas as pl
from jax.experimental.pallas import tpu as pltpu

F32 = jnp.float32
BF16 = jnp.bfloat16

D_MODEL = 1024
HEADS = 4
HEAD_DIM = 128
MLSTM_W = HEADS * HEAD_DIM
CONV_CH = D_MODEL - MLSTM_W
CONV_WIDTH = 3
D_FF = 4 * D_MODEL
PLE_DIM = 256
EPS = 1e-6
M_INIT = -1e30
Q_SCALE = HEAD_DIM ** -0.5
LOG2E = math.log2(math.e)

GATE_ROWS = 32
LANES = 128
SUBLANES = 8
VMEM_LIMIT_BYTES = 60000 * 1024

SEQ_TILE = 256
FF_CHUNK = 512
SAMPLE_TILE = 8


def _dot(a, b):
    return jnp.dot(a, b, preferred_element_type=F32)


def _dot_nt(a, b):
    return lax.dot_general(a, b, (((1,), (1,)), ((), ())), preferred_element_type=F32)


def _dot_tn(a, b):
    return lax.dot_general(a, b, (((0,), (0,)), ((), ())), preferred_element_type=F32)


def _rms(x, g):
    y = x * lax.rsqrt(jnp.mean(x * x, axis=-1, keepdims=True) + EPS)
    return y * g


def _log_sigmoid(x):
    return jnp.minimum(x, 0.0) - jnp.log1p(jnp.exp(-jnp.abs(x)))


def _mlp_chunk(xn_bf16, c, wup_ref, wdown_ref):
    cols = slice(c * FF_CHUNK, (c + 1) * FF_CHUNK)
    hf = jnp.maximum(_dot(xn_bf16, wup_ref[:, cols]), 0.0)
    return _dot((hf * hf).astype(BF16), wdown_ref[cols, :])


def _ple_and_final(x2, p, nple_ref, wpg_ref, wpp_ref, nfin_ref):
    gate = jax.nn.sigmoid(_dot(_rms(x2, nple_ref[...]).astype(BF16), wpg_ref[...]))
    x3 = x2 + gate * _dot(p.astype(BF16), wpp_ref[...])
    return _rms(x3, nfin_ref[...])


def _cummax_lanes(a):
    n = a.shape[1]
    lane = lax.broadcasted_iota(jnp.int32, a.shape, 1)
    d = 1
    while d < n:
        shifted = pltpu.roll(a, d, axis=1)
        a = jnp.maximum(a, jnp.where(lane >= d, shifted, -jnp.inf))
        d *= 2
    return a


def _prompt_kernel(tiles_per_seq, num_tiles,
                   x_ref, xlag_ref, plag_ref, wheads_ref, wconv_ref, wgt_ref, bi_ref, bf_ref, nmix_ref, mh_ref, cw_ref,
                   wout_ref, nmlp_ref, wup_ref, wdown_ref, nple_ref, wpg_ref, wpp_ref, nfin_ref, tri_ref,
                   y_ref, cout_ref, mout_ref, convout_ref,
                   caug, m_scr, cu_buf, mix_scr):
    t = SEQ_TILE
    s_id = pl.program_id(0)
    is_real = s_id < num_tiles
    j = lax.rem(jnp.minimum(s_id, num_tiles - 1), tiles_per_seq)

    @pl.when(s_id == 0)
    def _():
        mix_scr[...] = jnp.zeros_like(mix_scr)

    @pl.when(j == 0)
    def _():
        caug[...] = jnp.zeros_like(caug)
        m_scr[...] = jnp.full(m_scr.shape, M_INIT, F32)
        cu_buf[0:SUBLANES, :] = jnp.zeros((SUBLANES, CONV_CH), F32)

    hb = _rms(x_ref[0], nmix_ref[...]).astype(BF16)
    gt = _dot_nt(wgt_ref[...], hb)
    ig = gt[0:SUBLANES] + bi_ref[...]
    lf = _log_sigmoid(gt[16:16 + SUBLANES] + bf_ref[...])
    hi = lf.astype(BF16)
    r1 = lf - hi.astype(F32)
    mid = r1.astype(BF16)
    lo = (r1 - mid.astype(F32)).astype(BF16)
    parts = _dot(jnp.concatenate([hi, mid, lo, jnp.zeros_like(lo)], axis=0), tri_ref[...])
    b = parts[0:8] + parts[8:16] + parts[16:24]
    a = ig - b
    m_prev = jnp.concatenate([m_scr[...]] * (t // LANES), axis=1)
    g = jnp.maximum(_cummax_lanes(a), m_prev)
    m_t = b + g
    b_last = b[:, t - 1:t]
    m_new = m_t[:, t - 1:t]
    decay = jnp.exp(m_prev - g)
    e_neg_m = jnp.exp(-m_t)
    w_state = jnp.exp(a + (b_last - m_new))
    c_dec = jnp.exp(b_last + m_prev[:, 0:1] - m_new)
    m_scr[...] = jnp.broadcast_to(m_new, m_scr.shape)
    a2 = a * LOG2E
    rows = jnp.concatenate(
        [g * (-LOG2E), decay, e_neg_m, w_state, jnp.zeros((LANES - 4 * SUBLANES, t), F32)], axis=0)
    cols = rows.T

    x1 = xlag_ref[0] + _dot(mix_scr[...], wout_ref[...])
    xn = _rms(x1, nmlp_ref[...]).astype(BF16)

    zc = _dot(hb, wconv_ref[...])
    cu = zc[:, CONV_CH:2 * CONV_CH] * zc[:, 2 * CONV_CH:3 * CONV_CH]
    cu_buf[SUBLANES:SUBLANES + t, :] = cu
    conv = (cw_ref[0:1, :] * cu_buf[SUBLANES - 2:SUBLANES - 2 + t, :]
            + cw_ref[1:2, :] * cu_buf[SUBLANES - 1:SUBLANES - 1 + t, :]
            + cw_ref[2:3, :] * cu)
    mix_scr[:, MLSTM_W:D_MODEL] = (zc[:, 0:CONV_CH] * conv).astype(BF16)
    cu_buf[0:SUBLANES, :] = cu_buf[t:t + SUBLANES, :]

    row_id = lax.broadcasted_iota(jnp.int32, (t, t), 0)
    col_id = lax.broadcasted_iota(jnp.int32, (t, t), 1)
    causal = col_id <= row_id
    lane_id = lax.broadcasted_iota(jnp.int32, (t, HEAD_DIM), 1)

    acc = jnp.zeros((t, D_MODEL), F32)
    chunks_per_head = D_FF // FF_CHUNK // HEADS
    next_chunk = 0
    for h in range(HEADS):
        if h % 2 == 0:
            pair = slice(h * HEAD_DIM, (h + 2) * HEAD_DIM)
            zq2, zk2, zv2, zo2 = (
                _dot(hb, wheads_ref[:, grp * MLSTM_W + pair.start:grp * MLSTM_W + pair.stop])
                for grp in range(4))
        half = slice((h % 2) * HEAD_DIM, (h % 2 + 1) * HEAD_DIM)
        qs = (zq2[:, half] * Q_SCALE).astype(BF16)
        kb = zk2[:, half].astype(BF16)
        v = zv2[:, half]
        c_col = cols[:, h:h + 1]
        dec = cols[:, SUBLANES + h:SUBLANES + h + 1]
        enm = cols[:, 2 * SUBLANES + h:2 * SUBLANES + h + 1]
        wst = cols[:, 3 * SUBLANES + h:3 * SUBLANES + h + 1]
        s = _dot_nt(qs, kb)

        for _ in range(chunks_per_head // 2):
            acc = acc + _mlp_chunk(xn, next_chunk, wup_ref, wdown_ref)
            next_chunk += 1

        dmat = jnp.exp2(jnp.where(causal, c_col + a2[h:h + 1, :], -jnp.inf))
        pm = s * dmat
        row_sum = jnp.sum(pm, axis=1, keepdims=True)
        intra = _dot(pm.astype(BF16), v.astype(BF16))
        c_state = caug[h]
        inter = _dot(qs, c_state.astype(BF16))
        vw = jnp.concatenate([v * wst, jnp.where(lane_id == 0, wst, 0.0)], axis=1).astype(BF16)
        caug[h] = c_dec[h:h + 1, :] * c_state + _dot_tn(kb, vw)

        for _ in range(chunks_per_head - chunks_per_head // 2):
            acc = acc + _mlp_chunk(xn, next_chunk, wup_ref, wdown_ref)
            next_chunk += 1

        num = intra + dec * inter[:, 0:HEAD_DIM]
        den = row_sum + dec * inter[:, HEAD_DIM:HEAD_DIM + 1]
        hh = num / jnp.maximum(jnp.abs(den), enm)
        hn = hh * lax.rsqrt(jnp.mean(hh * hh, axis=-1, keepdims=True) + EPS)
        hn = hn * mh_ref[:, h * HEAD_DIM:(h + 1) * HEAD_DIM]
        mix_scr[:, h * HEAD_DIM:(h + 1) * HEAD_DIM] = (
            hn * jax.nn.sigmoid(zo2[:, half])).astype(BF16)

    y_ref[0] = _ple_and_final(x1 + acc, plag_ref[0], nple_ref, wpg_ref, wpp_ref, nfin_ref)

    @pl.when(jnp.logical_and(is_real, j == tiles_per_seq - 1))
    def _():
        cout_ref[0] = caug[...]
        mout_ref[0] = m_scr[...]
        convout_ref[0] = cu[t - (CONV_WIDTH - 1):t, :]


def _resident(shape):
    return pl.BlockSpec(shape, lambda *_: (0,) * len(shape), pipeline_mode=pl.Buffered(1))


def _prompt_call(x, p, wts):
    bsz, seq, _ = x.shape
    t = SEQ_TILE
    assert seq % t == 0 and t % LANES == 0
    tri = (jnp.arange(t)[:, None] <= jnp.arange(t)[None, :]).astype(BF16)
    consts = [wts["w_heads"], wts["w_conv"], wts["w_gt"], wts["bi8"], wts["bf8"], wts["norm_mix"], wts["mh_norm"], wts["conv_w"],
              wts["w_out"], wts["norm_mlp"], wts["w_up"], wts["w_down"], wts["norm_ple"], wts["w_pg"],
              wts["w_pp"], wts["norm_final"], tri]
    nj = seq // t
    nt = bsz * nj

    def cur(s):
        c = jnp.minimum(s, nt - 1)
        return c // nj, c % nj

    def lag(s):
        c = jnp.maximum(s - 1, 0)
        return c // nj, c % nj

    in_specs = [pl.BlockSpec((1, t, D_MODEL), lambda s: (*cur(s), 0)),
                pl.BlockSpec((1, t, D_MODEL), lambda s: (*lag(s), 0)),
                pl.BlockSpec((1, t, PLE_DIM), lambda s: (*lag(s), 0))]
    in_specs += [_resident(c.shape) for c in consts]
    out_shape = (jax.ShapeDtypeStruct((bsz, seq, D_MODEL), F32),
                 jax.ShapeDtypeStruct((bsz, HEADS, HEAD_DIM, 2 * HEAD_DIM), F32),
                 jax.ShapeDtypeStruct((bsz, SUBLANES, LANES), F32),
                 jax.ShapeDtypeStruct((bsz, CONV_WIDTH - 1, CONV_CH), F32))
    out_specs = (pl.BlockSpec((1, t, D_MODEL), lambda s: (*lag(s), 0)),
                 pl.BlockSpec((1, HEADS, HEAD_DIM, 2 * HEAD_DIM), lambda s: (cur(s)[0], 0, 0, 0)),
                 pl.BlockSpec((1, SUBLANES, LANES), lambda s: (cur(s)[0], 0, 0)),
                 pl.BlockSpec((1, CONV_WIDTH - 1, CONV_CH), lambda s: (cur(s)[0], 0, 0)))
    scratch = [pltpu.VMEM((HEADS, HEAD_DIM, 2 * HEAD_DIM), F32),
               pltpu.VMEM((SUBLANES, LANES), F32),
               pltpu.VMEM((t + SUBLANES, CONV_CH), F32),
               pltpu.VMEM((t, D_MODEL), BF16)]
    return pl.pallas_call(
        functools.partial(_prompt_kernel, nj, nt),
        grid=(nt + 1,),
        in_specs=in_specs,
        out_specs=out_specs,
        out_shape=out_shape,
        scratch_shapes=scratch,
        compiler_params=pltpu.CompilerParams(
            dimension_semantics=("arbitrary",),
            vmem_limit_bytes=VMEM_LIMIT_BYTES),
        name="prompt_layer",
    )(x, x, p, *consts)


def _sample_inproj_kernel(x_ref, wheads_ref, wconv_ref, wgc_ref, nmix_ref, zh_ref, zc_ref, g_ref):
    hb = _rms(x_ref[...], nmix_ref[...]).astype(BF16)
    zh_ref[...] = _dot(hb, wheads_ref[...])
    zc_ref[...] = _dot(hb, wconv_ref[...])
    g_ref[...] = _dot(hb, wgc_ref[...])


def _sample_state_kernel(zq_ref, zk_ref, zv_ref, zo_ref, g_ref, c0_ref, n0_ref, m0_ref, bi_ref, bf_ref, mh_ref,
                         hm_ref, cnew_ref, nnew_ref, mnew_ref,
                         inter_scr, wv_scr, cd_scr):
    tb = SAMPLE_TILE
    g = g_ref[...]
    ig = g[:, 0:HEADS] + bi_ref[...]
    lf = _log_sigmoid(g[:, HEADS:2 * HEADS] + bf_ref[...])
    m_inter = lf + m0_ref[...]
    m_new = jnp.maximum(m_inter, ig)
    w_in = jnp.exp(ig - m_new)
    c_dec = jnp.exp(m_inter - m_new)
    e_neg_m = jnp.exp(-m_new)
    mnew_ref[...] = m_new

    vs, scores, qns = [], [], []
    for h in range(HEADS):
        hs = slice(h * HEAD_DIM, (h + 1) * HEAD_DIM)
        q = zq_ref[:, hs] * Q_SCALE
        k = zk_ref[:, hs]
        v = zv_ref[:, hs]
        n0 = n0_ref[:, hs]
        wi = w_in[:, h:h + 1]
        cd = c_dec[:, h:h + 1]
        scores.append(jnp.sum(q * k, axis=1, keepdims=True) * wi)
        qns.append(jnp.sum(q * n0, axis=1, keepdims=True))
        wv_scr[:, hs] = wi * v
        cd_scr[:, hs] = jnp.broadcast_to(cd, (tb, HEAD_DIM))
        nnew_ref[:, hs] = cd * n0 + wi * k
        vs.append(v)

    eye = (lax.broadcasted_iota(jnp.int32, (HEAD_DIM, HEAD_DIM), 0)
           == lax.broadcasted_iota(jnp.int32, (HEAD_DIM, HEAD_DIM), 1))

    for i in range(tb):
        row = slice(i, i + 1)
        for h in range(HEADS):
            hs = slice(h * HEAD_DIM, (h + 1) * HEAD_DIM)
            c0 = c0_ref[i, h]
            q_rows = jnp.broadcast_to(zq_ref[row, hs] * Q_SCALE, (SUBLANES, HEAD_DIM)).astype(BF16)
            inter_scr[row, hs] = _dot(q_rows, c0.astype(BF16))[0:1, :]
            k_diag = jnp.where(eye, jnp.broadcast_to(zk_ref[row, hs], (HEAD_DIM, HEAD_DIM)), 0.0).astype(BF16)
            v_rows = jnp.broadcast_to(wv_scr[row, hs], (HEAD_DIM, HEAD_DIM)).astype(BF16)
            cd = jnp.broadcast_to(cd_scr[row, hs], (HEAD_DIM, HEAD_DIM))
            cnew_ref[i, h] = cd * c0 + _dot(k_diag, v_rows)

    for h in range(HEADS):
        hs = slice(h * HEAD_DIM, (h + 1) * HEAD_DIM)
        cd = c_dec[:, h:h + 1]
        num = scores[h] * vs[h] + cd * inter_scr[:, hs]
        den = scores[h] + cd * qns[h]
        hh = num / jnp.maximum(jnp.abs(den), e_neg_m[:, h:h + 1])
        hn = hh * lax.rsqrt(jnp.mean(hh * hh, axis=-1, keepdims=True) + EPS)
        hm_ref[:, hs] = hn * mh_ref[:, hs] * jax.nn.sigmoid(zo_ref[:, hs])


def _sample_tail_kernel(x_ref, p_ref, hm_ref, zc_ref, sc_ref, cw_ref,
                        wout_ref, nmlp_ref, wup_ref, wdown_ref, nple_ref, wpg_ref, wpp_ref, nfin_ref,
                        y_ref, buf_ref):
    gb = zc_ref[:, 0:CONV_CH]
    cu = zc_ref[:, CONV_CH:2 * CONV_CH] * zc_ref[:, 2 * CONV_CH:3 * CONV_CH]
    old0 = sc_ref[:, 0:CONV_CH]
    old1 = sc_ref[:, CONV_CH:2 * CONV_CH]
    yc = gb * (cw_ref[0:1, :] * old0 + cw_ref[1:2, :] * old1 + cw_ref[2:3, :] * cu)
    buf_ref[:, 0:CONV_CH] = old1
    buf_ref[:, CONV_CH:2 * CONV_CH] = cu
    mix = jnp.concatenate([hm_ref[...], yc], axis=1).astype(BF16)
    x1 = x_ref[...] + _dot(mix, wout_ref[...])
    xn = _rms(x1, nmlp_ref[...]).astype(BF16)
    acc = jnp.zeros_like(x1)
    for c in range(D_FF // FF_CHUNK):
        acc = acc + _mlp_chunk(xn, c, wup_ref, wdown_ref)
    y_ref[...] = _ple_and_final(x1 + acc, p_ref[...], nple_ref, wpg_ref, wpp_ref, nfin_ref)


def _sample_calls(xs, ps, c0, n0, m0, sconv, wts):
    nb = xs.shape[0]
    tb = SAMPLE_TILE
    assert nb % tb == 0
    zh, zc, g = pl.pallas_call(
        _sample_inproj_kernel,
        out_shape=(jax.ShapeDtypeStruct((nb, 4 * MLSTM_W), F32),
                   jax.ShapeDtypeStruct((nb, 3 * CONV_CH), F32),
                   jax.ShapeDtypeStruct((nb, LANES), F32)),
        compiler_params=pltpu.CompilerParams(vmem_limit_bytes=VMEM_LIMIT_BYTES),
        name="sample_inproj",
    )(xs, wts["w_heads"], wts["w_conv"], wts["w_gc"], wts["norm_mix"])

    def zgroup(grp):
        return pl.BlockSpec((tb, MLSTM_W), lambda i, grp=grp: (i, grp))

    row_w = pl.BlockSpec((tb, MLSTM_W), lambda i: (i, 0))
    row_h = pl.BlockSpec((tb, HEADS), lambda i: (i, 0))
    state = pl.BlockSpec((tb, HEADS, HEAD_DIM, HEAD_DIM), lambda i: (i, 0, 0, 0))
    hm, c_new, n_new, m_new = pl.pallas_call(
        _sample_state_kernel,
        grid=(nb // tb,),
        in_specs=[zgroup(0), zgroup(1), zgroup(2), zgroup(3),
                  pl.BlockSpec((tb, LANES), lambda i: (i, 0)),
                  state, row_w, row_h,
                  pl.BlockSpec((1, HEADS), lambda i: (0, 0)),
                  pl.BlockSpec((1, HEADS), lambda i: (0, 0)),
                  pl.BlockSpec((1, MLSTM_W), lambda i: (0, 0))],
        out_specs=(row_w, state, row_w, row_h),
        out_shape=(jax.ShapeDtypeStruct((nb, MLSTM_W), F32),
                   jax.ShapeDtypeStruct((nb, HEADS, HEAD_DIM, HEAD_DIM), F32),
                   jax.ShapeDtypeStruct((nb, MLSTM_W), F32),
                   jax.ShapeDtypeStruct((nb, HEADS), F32)),
        scratch_shapes=[pltpu.VMEM((tb, MLSTM_W), F32)] * 3,
        compiler_params=pltpu.CompilerParams(dimension_semantics=("arbitrary",)),
        name="sample_state",
    )(zh, zh, zh, zh, g, c0, n0, m0, wts["b_i"], wts["b_f"], wts["mh_norm"])

    ys, new_buf = pl.pallas_call(
        _sample_tail_kernel,
        out_shape=(jax.ShapeDtypeStruct((nb, D_MODEL), F32),
                   jax.ShapeDtypeStruct((nb, (CONV_WIDTH - 1) * CONV_CH), F32)),
        compiler_params=pltpu.CompilerParams(vmem_limit_bytes=VMEM_LIMIT_BYTES),
        name="sample_tail",
    )(xs, ps, hm, zc, sconv, wts["conv_w"], wts["w_out"], wts["norm_mlp"], wts["w_up"], wts["w_down"],
      wts["norm_ple"], wts["w_pg"], wts["w_pp"], wts["norm_final"])
    return ys, c_new, n_new, m_new, new_buf


def _prepare_weights(norm_mix, w_in, b_gate_i, b_gate_f, mh_norm, conv_w, w_out, norm_mlp, w_up, w_down,
                     norm_ple, w_ple_gate, w_ple_proj, norm_final):
    g0 = 4 * MLSTM_W
    w_i = w_in[:, g0:g0 + HEADS]
    w_f = w_in[:, g0 + HEADS:g0 + 2 * HEADS]
    w_gt = jnp.zeros((GATE_ROWS, D_MODEL), F32)
    w_gt = w_gt.at[0:HEADS].set(w_i.T).at[16:16 + HEADS].set(w_f.T)
    w_gc = jnp.zeros((D_MODEL, LANES), F32).at[:, 0:HEADS].set(w_i).at[:, HEADS:2 * HEADS].set(w_f)
    pad = jnp.zeros((SUBLANES - HEADS, 1), F32)
    return dict(
        w_heads=w_in[:, :g0].astype(BF16),
        w_conv=w_in[:, g0 + 2 * HEADS:].astype(BF16),
        w_gt=w_gt.astype(BF16),
        w_gc=w_gc.astype(BF16),
        bi8=jnp.concatenate([b_gate_i.reshape(HEADS, 1), pad], axis=0),
        bf8=jnp.concatenate([b_gate_f.reshape(HEADS, 1), pad], axis=0),
        b_i=b_gate_i.reshape(1, HEADS),
        b_f=b_gate_f.reshape(1, HEADS),
        norm_mix=norm_mix.reshape(1, D_MODEL),
        mh_norm=mh_norm.reshape(1, MLSTM_W),
        conv_w=conv_w,
        w_out=w_out.astype(BF16),
        norm_mlp=norm_mlp.reshape(1, D_MODEL),
        w_up=w_up.astype(BF16),
        w_down=w_down.astype(BF16),
        norm_ple=norm_ple.reshape(1, D_MODEL),
        w_pg=w_ple_gate.astype(BF16),
        w_pp=w_ple_proj.astype(BF16),
        norm_final=norm_final.reshape(1, D_MODEL),
    )


def kernel(x_prompt, x_sample, state_mlstm_C, state_mlstm_n, state_mlstm_m, state_conv, p_prompt, p_sample,
           norm_mix, w_in, b_gate_i, b_gate_f, mh_norm, conv_w, w_out, norm_mlp, w_up, w_down, norm_ple,
           w_ple_gate, w_ple_proj, norm_final):
    assert norm_mix.shape[0] == 1, "single-layer trunk"
    wts = _prepare_weights(norm_mix[0], w_in[0], b_gate_i[0], b_gate_f[0], mh_norm[0], conv_w[0], w_out[0],
                           norm_mlp[0], w_up[0], w_down[0], norm_ple[0], w_ple_gate[0], w_ple_proj[0], norm_final)

    y_prompt, c_aug, m_rows, conv_tail = _prompt_call(x_prompt, p_prompt[0], wts)
    prompt_c = c_aug[:, :, :, 0:HEAD_DIM]
    prompt_n = c_aug[:, :, :, HEAD_DIM]
    prompt_m = m_rows[:, 0:HEADS, 0]

    nb = x_sample.shape[0]
    ys, c_new, n_new, m_new, new_buf = _sample_calls(
        x_sample.reshape(nb, D_MODEL), p_sample[0].reshape(nb, PLE_DIM), state_mlstm_C[0],
        state_mlstm_n[0].reshape(nb, MLSTM_W), state_mlstm_m[0],
        state_conv[0].reshape(nb, (CONV_WIDTH - 1) * CONV_CH), wts)

    return (y_prompt, ys.reshape(nb, 1, D_MODEL),
            prompt_c[None], prompt_n[None], prompt_m[None], conv_tail[None],
            c_new[None], n_new.reshape(1, nb, HEADS, HEAD_DIM), m_new[None],
            new_buf.reshape(1, nb, CONV_WIDTH - 1, CONV_CH))
```

```python
import functools
import math

import jax
import jax.numpy as jnp
from jax import lax
from jax.experimental import pallas as pl
from jax.experimental.pallas import tpu as pltpu

F32 = jnp.float32
BF16 = jnp.bfloat16

D_MODEL = 1024
HEADS = 4
HEAD_DIM = 128
MLSTM_W = HEADS * HEAD_DIM
CONV_CH = D_MODEL - MLSTM_W
CONV_WIDTH = 3
D_FF = 4 * D_MODEL
PLE_DIM = 256
EPS = 1e-6
M_INIT = -1e30
Q_SCALE = HEAD_DIM ** -0.5
LOG2E = math.log2(math.e)

GATE_ROWS = 32
LANES = 128
SUBLANES = 8
VMEM_LIMIT_BYTES = 60000 * 1024

SEQ_TILE = 512
CHUNK = 256
FF_CHUNK = 512
SAMPLE_TILE = 8


def _dot(a, b):
    return jnp.dot(a, b, preferred_element_type=F32)


def _dot_nt(a, b):
    return lax.dot_general(a, b, (((1,), (1,)), ((), ())), preferred_element_type=F32)


def _dot_tn(a, b):
    return lax.dot_general(a, b, (((0,), (0,)), ((), ())), preferred_element_type=F32)


def _rms(x, g):
    y = x * lax.rsqrt(jnp.mean(x * x, axis=-1, keepdims=True) + EPS)
    return y * g


def _log_sigmoid(x):
    return jnp.minimum(x, 0.0) - jnp.log1p(jnp.exp(-jnp.abs(x)))


def _mlp_chunk(xn_bf16, c, wup_ref, wdown_ref):
    cols = slice(c * FF_CHUNK, (c + 1) * FF_CHUNK)
    hf = jnp.maximum(_dot(xn_bf16, wup_ref[:, cols]), 0.0)
    return _dot((hf * hf).astype(BF16), wdown_ref[cols, :])


def _ple_and_final(x2, p, nple_ref, wpg_ref, wpp_ref, nfin_ref):
    gate = jax.nn.sigmoid(_dot(_rms(x2, nple_ref[...]).astype(BF16), wpg_ref[...]))
    x3 = x2 + gate * _dot(p.astype(BF16), wpp_ref[...])
    return _rms(x3, nfin_ref[...])


def _cummax_lanes(a):
    n = a.shape[1]
    lane = lax.broadcasted_iota(jnp.int32, a.shape, 1)
    d = 1
    while d < n:
        shifted = pltpu.roll(a, d, axis=1)
        a = jnp.maximum(a, jnp.where(lane >= d, shifted, -jnp.inf))
        d *= 2
    return a


def _prompt_kernel(tiles_per_seq, num_tiles,
                   x_ref, xlag_ref, plag_ref, wheads_ref, wconv_ref, wgt_ref, bi_ref, bf_ref, nmix_ref, mh_ref, cw_ref,
                   wout_ref, nmlp_ref, wup_ref, wdown_ref, nple_ref, wpg_ref, wpp_ref, nfin_ref, tri_ref,
                   y_ref, cout_ref, mout_ref, convout_ref,
                   caug, m_scr, cu_buf, mix_scr):
    t = SEQ_TILE
    lc = CHUNK
    s_id = pl.program_id(0)
    is_real = s_id < num_tiles
    j = lax.rem(jnp.minimum(s_id, num_tiles - 1), tiles_per_seq)

    @pl.when(s_id == 0)
    def _():
        mix_scr[...] = jnp.zeros_like(mix_scr)

    @pl.when(j == 0)
    def _():
        caug[...] = jnp.zeros_like(caug)
        m_scr[...] = jnp.full(m_scr.shape, M_INIT, F32)
        cu_buf[0:SUBLANES, :] = jnp.zeros((SUBLANES, CONV_CH), F32)

    x1 = xlag_ref[0] + _dot(mix_scr[...], wout_ref[...])
    xn = _rms(x1, nmlp_ref[...]).astype(BF16)

    hb = _rms(x_ref[0], nmix_ref[...]).astype(BF16)
    gt = _dot_nt(wgt_ref[...], hb)

    row_id = lax.broadcasted_iota(jnp.int32, (lc, lc), 0)
    col_id = lax.broadcasted_iota(jnp.int32, (lc, lc), 1)
    causal = col_id <= row_id
    lane_id = lax.broadcasted_iota(jnp.int32, (lc, HEAD_DIM), 1)

    acc = jnp.zeros((t, D_MODEL), F32)
    n_phases = (t // lc) * HEADS
    mlp_per_phase = D_FF // FF_CHUNK // n_phases
    mlp_first = (mlp_per_phase + 1) // 2
    next_mlp = 0
    m_carry = m_scr[...]
    for c in range(t // lc):
        tok = slice(c * lc, (c + 1) * lc)
        hb_c = hb[tok, :]
        ig = gt[0:SUBLANES, tok] + bi_ref[...]
        lf = _log_sigmoid(gt[16:16 + SUBLANES, tok] + bf_ref[...])
        hi = lf.astype(BF16)
        r1 = lf - hi.astype(F32)
        mid = r1.astype(BF16)
        lo = (r1 - mid.astype(F32)).astype(BF16)
        parts = _dot(jnp.concatenate([hi, mid, lo, jnp.zeros_like(lo)], axis=0), tri_ref[...])
        b = parts[0:8] + parts[8:16] + parts[16:24]
        a = ig - b
        m_prev = jnp.concatenate([m_carry] * (lc // LANES), axis=1)
        g = jnp.maximum(_cummax_lanes(a), m_prev)
        m_t = b + g
        b_last = b[:, lc - 1:lc]
        m_new = m_t[:, lc - 1:lc]
        decay = jnp.exp(m_prev - g)
        e_neg_m = jnp.exp(-m_t)
        w_state = jnp.exp(a + (b_last - m_new))
        c_dec = jnp.exp(b_last + m_prev[:, 0:1] - m_new)
        m_carry = jnp.broadcast_to(m_new, m_carry.shape)
        a2 = a * LOG2E
        rows = jnp.concatenate(
            [g * (-LOG2E), decay, e_neg_m, w_state, jnp.zeros((LANES - 4 * SUBLANES, lc), F32)], axis=0)
        cols = rows.T

        for h in range(HEADS):
            if h % 2 == 0:
                pair = slice(h * HEAD_DIM, (h + 2) * HEAD_DIM)
                zq2, zk2, zv2, zo2 = (
                    _dot(hb_c, wheads_ref[:, grp * MLSTM_W + pair.start:grp * MLSTM_W + pair.stop])
                    for grp in range(4))
            half = slice((h % 2) * HEAD_DIM, (h % 2 + 1) * HEAD_DIM)
            qs = (zq2[:, half] * Q_SCALE).astype(BF16)
            kb = zk2[:, half].astype(BF16)
            v = zv2[:, half]
            c_col = cols[:, h:h + 1]
            dec = cols[:, SUBLANES + h:SUBLANES + h + 1]
            enm = cols[:, 2 * SUBLANES + h:2 * SUBLANES + h + 1]
            wst = cols[:, 3 * SUBLANES + h:3 * SUBLANES + h + 1]
            s = _dot_nt(qs, kb)

            for _ in range(mlp_first):
                acc = acc + _mlp_chunk(xn, next_mlp, wup_ref, wdown_ref)
                next_mlp += 1

            dmat = jnp.exp2(jnp.where(causal, c_col + a2[h:h + 1, :], -jnp.inf))
            pm = s * dmat
            row_sum = jnp.sum(pm, axis=1, keepdims=True)
            intra = _dot(pm.astype(BF16), v.astype(BF16))
            c_state = caug[h]
            inter = _dot(qs, c_state.astype(BF16))
            vw = jnp.concatenate([v * wst, jnp.where(lane_id == 0, wst, 0.0)], axis=1).astype(BF16)
            caug[h] = c_dec[h:h + 1, :] * c_state + _dot_tn(kb, vw)

            for _ in range(mlp_per_phase - mlp_first):
                acc = acc + _mlp_chunk(xn, next_mlp, wup_ref, wdown_ref)
                next_mlp += 1

            num = intra + dec * inter[:, 0:HEAD_DIM]
            den = row_sum + dec * inter[:, HEAD_DIM:HEAD_DIM + 1]
            hh = num / jnp.maximum(jnp.abs(den), enm)
            hn = hh * lax.rsqrt(jnp.mean(hh * hh, axis=-1, keepdims=True) + EPS)
            hn = hn * mh_ref[:, h * HEAD_DIM:(h + 1) * HEAD_DIM]
            mix_scr[tok, h * HEAD_DIM:(h + 1) * HEAD_DIM] = (
                hn * jax.nn.sigmoid(zo2[:, half])).astype(BF16)
    m_scr[...] = m_carry
    assert next_mlp == D_FF // FF_CHUNK

    y_ref[0] = _ple_and_final(x1 + acc, plag_ref[0], nple_ref, wpg_ref, wpp_ref, nfin_ref)

    zc = _dot(hb, wconv_ref[...])
    cu = zc[:, CONV_CH:2 * CONV_CH] * zc[:, 2 * CONV_CH:3 * CONV_CH]
    cu_buf[SUBLANES:SUBLANES + t, :] = cu
    conv = (cw_ref[0:1, :] * cu_buf[SUBLANES - 2:SUBLANES - 2 + t, :]
            + cw_ref[1:2, :] * cu_buf[SUBLANES - 1:SUBLANES - 1 + t, :]
            + cw_ref[2:3, :] * cu)
    mix_scr[:, MLSTM_W:D_MODEL] = (zc[:, 0:CONV_CH] * conv).astype(BF16)
    cu_buf[0:SUBLANES, :] = cu_buf[t:t + SUBLANES, :]

    @pl.when(jnp.logical_and(is_real, j == tiles_per_seq - 1))
    def _():
        cout_ref[0] = caug[...]
        mout_ref[0] = m_scr[...]
        convout_ref[0] = cu[t - (CONV_WIDTH - 1):t, :]


def _resident(shape):
    return pl.BlockSpec(shape, lambda *_: (0,) * len(shape), pipeline_mode=pl.Buffered(1))


def _prompt_call(x, p, wts):
    bsz, seq, _ = x.shape
    t = SEQ_TILE
    assert seq % t == 0 and t % CHUNK == 0 and CHUNK % LANES == 0
    tri = (jnp.arange(CHUNK)[:, None] <= jnp.arange(CHUNK)[None, :]).astype(BF16)
    consts = [wts["w_heads"], wts["w_conv"], wts["w_gt"], wts["bi8"], wts["bf8"], wts["norm_mix"], wts["mh_norm"], wts["conv_w"],
              wts["w_out"], wts["norm_mlp"], wts["w_up"], wts["w_down"], wts["norm_ple"], wts["w_pg"],
              wts["w_pp"], wts["norm_final"], tri]
    nj = seq // t
    nt = bsz * nj

    def cur(s):
        c = jnp.minimum(s, nt - 1)
        return c // nj, c % nj

    def lag(s):
        c = jnp.maximum(s - 1, 0)
        return c // nj, c % nj

    in_specs = [pl.BlockSpec((1, t, D_MODEL), lambda s: (*cur(s), 0)),
                pl.BlockSpec((1, t, D_MODEL), lambda s: (*lag(s), 0)),
                pl.BlockSpec((1, t, PLE_DIM), lambda s: (*lag(s), 0))]
    in_specs += [_resident(c.shape) for c in consts]
    out_shape = (jax.ShapeDtypeStruct((bsz, seq, D_MODEL), F32),
                 jax.ShapeDtypeStruct((bsz, HEADS, HEAD_DIM, 2 * HEAD_DIM), F32),
                 jax.ShapeDtypeStruct((bsz, SUBLANES, LANES), F32),
                 jax.ShapeDtypeStruct((bsz, CONV_WIDTH - 1, CONV_CH), F32))
    out_specs = (pl.BlockSpec((1, t, D_MODEL), lambda s: (*lag(s), 0)),
                 pl.BlockSpec((1, HEADS, HEAD_DIM, 2 * HEAD_DIM), lambda s: (cur(s)[0], 0, 0, 0)),
                 pl.BlockSpec((1, SUBLANES, LANES), lambda s: (cur(s)[0], 0, 0)),
                 pl.BlockSpec((1, CONV_WIDTH - 1, CONV_CH), lambda s: (cur(s)[0], 0, 0)))
    scratch = [pltpu.VMEM((HEADS, HEAD_DIM, 2 * HEAD_DIM), F32),
               pltpu.VMEM((SUBLANES, LANES), F32),
               pltpu.VMEM((t + SUBLANES, CONV_CH), F32),
               pltpu.VMEM((t, D_MODEL), BF16)]
    return pl.pallas_call(
        functools.partial(_prompt_kernel, nj, nt),
        grid=(nt + 1,),
        in_specs=in_specs,
        out_specs=out_specs,
        out_shape=out_shape,
        scratch_shapes=scratch,
        compiler_params=pltpu.CompilerParams(
            dimension_semantics=("arbitrary",),
            vmem_limit_bytes=VMEM_LIMIT_BYTES),
        name="prompt_layer",
    )(x, x, p, *consts)


def _sample_inproj_kernel(x_ref, wheads_ref, wconv_ref, wgc_ref, nmix_ref, zh_ref, zc_ref, g_ref):
    hb = _rms(x_ref[...], nmix_ref[...]).astype(BF16)
    zh_ref[...] = _dot(hb, wheads_ref[...])
    zc_ref[...] = _dot(hb, wconv_ref[...])
    g_ref[...] = _dot(hb, wgc_ref[...])


def _sample_state_kernel(zq_ref, zk_ref, zv_ref, zo_ref, g_ref, c0_ref, n0_ref, m0_ref, bi_ref, bf_ref, mh_ref,
                         hm_ref, cnew_ref, nnew_ref, mnew_ref,
                         inter_scr, wv_scr, cd_scr):
    tb = SAMPLE_TILE
    g = g_ref[...]
    ig = g[:, 0:HEADS] + bi_ref[...]
    lf = _log_sigmoid(g[:, HEADS:2 * HEADS] + bf_ref[...])
    m_inter = lf + m0_ref[...]
    m_new = jnp.maximum(m_inter, ig)
    w_in = jnp.exp(ig - m_new)
    c_dec = jnp.exp(m_inter - m_new)
    e_neg_m = jnp.exp(-m_new)
    mnew_ref[...] = m_new

    vs, scores, qns = [], [], []
    for h in range(HEADS):
        hs = slice(h * HEAD_DIM, (h + 1) * HEAD_DIM)
        q = zq_ref[:, hs] * Q_SCALE
        k = zk_ref[:, hs]
        v = zv_ref[:, hs]
        n0 = n0_ref[:, hs]
        wi = w_in[:, h:h + 1]
        cd = c_dec[:, h:h + 1]
        scores.append(jnp.sum(q * k, axis=1, keepdims=True) * wi)
        qns.append(jnp.sum(q * n0, axis=1, keepdims=True))
        wv_scr[:, hs] = wi * v
        cd_scr[:, hs] = jnp.broadcast_to(cd, (tb, HEAD_DIM))
        nnew_ref[:, hs] = cd * n0 + wi * k
        vs.append(v)

    eye = (lax.broadcasted_iota(jnp.int32, (HEAD_DIM, HEAD_DIM), 0)
           == lax.broadcasted_iota(jnp.int32, (HEAD_DIM, HEAD_DIM), 1))

    for i in range(tb):
        row = slice(i, i + 1)
        for h in range(HEADS):
            hs = slice(h * HEAD_DIM, (h + 1) * HEAD_DIM)
            c0 = c0_ref[i, h]
            q_rows = jnp.broadcast_to(zq_ref[row, hs] * Q_SCALE, (SUBLANES, HEAD_DIM)).astype(BF16)
            inter_scr[row, hs] = _dot(q_rows, c0.astype(BF16))[0:1, :]
            k_diag = jnp.where(eye, jnp.broadcast_to(zk_ref[row, hs], (HEAD_DIM, HEAD_DIM)), 0.0).astype(BF16)
            v_rows = jnp.broadcast_to(wv_scr[row, hs], (HEAD_DIM, HEAD_DIM)).astype(BF16)
            cd = jnp.broadcast_to(cd_scr[row, hs], (HEAD_DIM, HEAD_DIM))
            cnew_ref[i, h] = cd * c0 + _dot(k_diag, v_rows)

    for h in range(HEADS):
        hs = slice(h * HEAD_DIM, (h + 1) * HEAD_DIM)
        cd = c_dec[:, h:h + 1]
        num = scores[h] * vs[h] + cd * inter_scr[:, hs]
        den = scores[h] + cd * qns[h]
        hh = num / jnp.maximum(jnp.abs(den), e_neg_m[:, h:h + 1])
        hn = hh * lax.rsqrt(jnp.mean(hh * hh, axis=-1, keepdims=True) + EPS)
        hm_ref[:, hs] = hn * mh_ref[:, hs] * jax.nn.sigmoid(zo_ref[:, hs])


def _sample_tail_kernel(x_ref, p_ref, hm_ref, zc_ref, sc_ref, cw_ref,
                        wout_ref, nmlp_ref, wup_ref, wdown_ref, nple_ref, wpg_ref, wpp_ref, nfin_ref,
                        y_ref, buf_ref):
    gb = zc_ref[:, 0:CONV_CH]
    cu = zc_ref[:, CONV_CH:2 * CONV_CH] * zc_ref[:, 2 * CONV_CH:3 * CONV_CH]
    old0 = sc_ref[:, 0:CONV_CH]
    old1 = sc_ref[:, CONV_CH:2 * CONV_CH]
    yc = gb * (cw_ref[0:1, :] * old0 + cw_ref[1:2, :] * old1 + cw_ref[2:3, :] * cu)
    buf_ref[:, 0:CONV_CH] = old1
    buf_ref[:, CONV_CH:2 * CONV_CH] = cu
    mix = jnp.concatenate([hm_ref[...], yc], axis=1).astype(BF16)
    x1 = x_ref[...] + _dot(mix, wout_ref[...])
    xn = _rms(x1, nmlp_ref[...]).astype(BF16)
    acc = jnp.zeros_like(x1)
    for c in range(D_FF // FF_CHUNK):
        acc = acc + _mlp_chunk(xn, c, wup_ref, wdown_ref)
    y_ref[...] = _ple_and_final(x1 + acc, p_ref[...], nple_ref, wpg_ref, wpp_ref, nfin_ref)


def _sample_calls(xs, ps, c0, n0, m0, sconv, wts):
    nb = xs.shape[0]
    tb = SAMPLE_TILE
    assert nb % tb == 0
    zh, zc, g = pl.pallas_call(
        _sample_inproj_kernel,
        out_shape=(jax.ShapeDtypeStruct((nb, 4 * MLSTM_W), F32),
                   jax.ShapeDtypeStruct((nb, 3 * CONV_CH), F32),
                   jax.ShapeDtypeStruct((nb, LANES), F32)),
        compiler_params=pltpu.CompilerParams(vmem_limit_bytes=VMEM_LIMIT_BYTES),
        name="sample_inproj",
    )(xs, wts["w_heads"], wts["w_conv"], wts["w_gc"], wts["norm_mix"])

    def zgroup(grp):
        return pl.BlockSpec((tb, MLSTM_W), lambda i, grp=grp: (i, grp))

    row_w = pl.BlockSpec((tb, MLSTM_W), lambda i: (i, 0))
    row_h = pl.BlockSpec((tb, HEADS), lambda i: (i, 0))
    state = pl.BlockSpec((tb, HEADS, HEAD_DIM, HEAD_DIM), lambda i: (i, 0, 0, 0))
    hm, c_new, n_new, m_new = pl.pallas_call(
        _sample_state_kernel,
        grid=(nb // tb,),
        in_specs=[zgroup(0), zgroup(1), zgroup(2), zgroup(3),
                  pl.BlockSpec((tb, LANES), lambda i: (i, 0)),
                  state, row_w, row_h,
                  pl.BlockSpec((1, HEADS), lambda i: (0, 0)),
                  pl.BlockSpec((1, HEADS), lambda i: (0, 0)),
                  pl.BlockSpec((1, MLSTM_W), lambda i: (0, 0))],
        out_specs=(row_w, state, row_w, row_h),
        out_shape=(jax.ShapeDtypeStruct((nb, MLSTM_W), F32),
                   jax.ShapeDtypeStruct((nb, HEADS, HEAD_DIM, HEAD_DIM), F32),
                   jax.ShapeDtypeStruct((nb, MLSTM_W), F32),
                   jax.ShapeDtypeStruct((nb, HEADS), F32)),
        scratch_shapes=[pltpu.VMEM((tb, MLSTM_W), F32)] * 3,
        compiler_params=pltpu.CompilerParams(dimension_semantics=("arbitrary",)),
        name="sample_state",
    )(zh, zh, zh, zh, g, c0, n0, m0, wts["b_i"], wts["b_f"], wts["mh_norm"])

    ys, new_buf = pl.pallas_call(
        _sample_tail_kernel,
        out_shape=(jax.ShapeDtypeStruct((nb, D_MODEL), F32),
                   jax.ShapeDtypeStruct((nb, (CONV_WIDTH - 1) * CONV_CH), F32)),
        compiler_params=pltpu.CompilerParams(vmem_limit_bytes=VMEM_LIMIT_BYTES),
        name="sample_tail",
    )(xs, ps, hm, zc, sconv, wts["conv_w"], wts["w_out"], wts["norm_mlp"], wts["w_up"], wts["w_down"],
      wts["norm_ple"], wts["w_pg"], wts["w_pp"], wts["norm_final"])
    return ys, c_new, n_new, m_new, new_buf


def _prepare_weights(norm_mix, w_in, b_gate_i, b_gate_f, mh_norm, conv_w, w_out, norm_mlp, w_up, w_down,
                     norm_ple, w_ple_gate, w_ple_proj, norm_final):
    g0 = 4 * MLSTM_W
    w_i = w_in[:, g0:g0 + HEADS]
    w_f = w_in[:, g0 + HEADS:g0 + 2 * HEADS]
    w_gt = jnp.zeros((GATE_ROWS, D_MODEL), F32)
    w_gt = w_gt.at[0:HEADS].set(w_i.T).at[16:16 + HEADS].set(w_f.T)
    w_gc = jnp.zeros((D_MODEL, LANES), F32).at[:, 0:HEADS].set(w_i).at[:, HEADS:2 * HEADS].set(w_f)
    pad = jnp.zeros((SUBLANES - HEADS, 1), F32)
    return dict(
        w_heads=w_in[:, :g0].astype(BF16),
        w_conv=w_in[:, g0 + 2 * HEADS:].astype(BF16),
        w_gt=w_gt.astype(BF16),
        w_gc=w_gc.astype(BF16),
        bi8=jnp.concatenate([b_gate_i.reshape(HEADS, 1), pad], axis=0),
        bf8=jnp.concatenate([b_gate_f.reshape(HEADS, 1), pad], axis=0),
        b_i=b_gate_i.reshape(1, HEADS),
        b_f=b_gate_f.reshape(1, HEADS),
        norm_mix=norm_mix.reshape(1, D_MODEL),
        mh_norm=mh_norm.reshape(1, MLSTM_W),
        conv_w=conv_w,
        w_out=w_out.astype(BF16),
        norm_mlp=norm_mlp.reshape(1, D_MODEL),
        w_up=w_up.astype(BF16),
        w_down=w_down.astype(BF16),
        norm_ple=norm_ple.reshape(1, D_MODEL),
        w_pg=w_ple_gate.astype(BF16),
        w_pp=w_ple_proj.astype(BF16),
        norm_final=norm_final.reshape(1, D_MODEL),
    )


def kernel(x_prompt, x_sample, state_mlstm_C, state_mlstm_n, state_mlstm_m, state_conv, p_prompt, p_sample,
           norm_mix, w_in, b_gate_i, b_gate_f, mh_norm, conv_w, w_out, norm_mlp, w_up, w_down, norm_ple,
           w_ple_gate, w_ple_proj, norm_final):
    assert norm_mix.shape[0] == 1, "single-layer trunk"
    wts = _prepare_weights(norm_mix[0], w_in[0], b_gate_i[0], b_gate_f[0], mh_norm[0], conv_w[0], w_out[0],
                           norm_mlp[0], w_up[0], w_down[0], norm_ple[0], w_ple_gate[0], w_ple_proj[0], norm_final)

    y_prompt, c_aug, m_rows, conv_tail = _prompt_call(x_prompt, p_prompt[0], wts)
    prompt_c = c_aug[:, :, :, 0:HEAD_DIM]
    prompt_n = c_aug[:, :, :, HEAD_DIM]
    prompt_m = m_rows[:, 0:HEADS, 0]

    nb = x_sample.shape[0]
    ys, c_new, n_new, m_new, new_buf = _sample_calls(
        x_sample.reshape(nb, D_MODEL), p_sample[0].reshape(nb, PLE_DIM), state_mlstm_C[0],
        state_mlstm_n[0].reshape(nb, MLSTM_W), state_mlstm_m[0],
        state_conv[0].reshape(nb, (CONV_WIDTH - 1) * CONV_CH), wts)

    return (y_prompt, ys.reshape(nb, 1, D_MODEL),
            prompt_c[None], prompt_n[None], prompt_m[None], conv_tail[None],
            c_new[None], n_new.reshape(1, nb, HEADS, HEAD_DIM), m_new[None],
            new_buf.reshape(1, nb, CONV_WIDTH - 1, CONV_CH))
```

```python
import functools
import math

import jax
import jax.numpy as jnp
from jax import lax
from jax.experimental import pallas as pl
from jax.experimental.pallas import tpu as pltpu

F32 = jnp.float32
BF16 = jnp.bfloat16

D_MODEL = 1024
HEADS = 4
HEAD_DIM = 128
MLSTM_W = HEADS * HEAD_DIM
CONV_CH = D_MODEL - MLSTM_W
CONV_WIDTH = 3
D_FF = 4 * D_MODEL
PLE_DIM = 256
EPS = 1e-6
M_INIT = -1e30
Q_SCALE = HEAD_DIM ** -0.5
LOG2E = math.log2(math.e)

GATE_ROWS = 16
PREP_STEPS = 8
LANES = 128
SUBLANES = 8
VMEM_LIMIT_BYTES = 60000 * 1024

SEQ_TILE = 512
CHUNK = 256
FF_CHUNK = 512
SAMPLE_TILE = 8


def _dot(a, b):
    return jnp.dot(a, b, preferred_element_type=F32)


def _dot_nt(a, b):
    return lax.dot_general(a, b, (((1,), (1,)), ((), ())), preferred_element_type=F32)


def _dot_tn(a, b):
    return lax.dot_general(a, b, (((0,), (0,)), ((), ())), preferred_element_type=F32)


def _rms(x, g):
    y = x * lax.rsqrt(jnp.mean(x * x, axis=-1, keepdims=True) + EPS)
    return y * g


def _log_sigmoid(x):
    return jnp.minimum(x, 0.0) - jnp.log1p(jnp.exp(-jnp.abs(x)))


def _mlp_chunk(xn_bf16, c, wup_ref, wdown_ref):
    cols = slice(c * FF_CHUNK, (c + 1) * FF_CHUNK)
    hf = jnp.maximum(_dot(xn_bf16, wup_ref[:, cols]), 0.0)
    return _dot((hf * hf).astype(BF16), wdown_ref[cols, :])


def _ple_and_final(x2, p, nple_ref, wpg_ref, wpp_ref, nfin_ref):
    gate = jax.nn.sigmoid(_dot(_rms(x2, nple_ref[...]).astype(BF16), wpg_ref[...]))
    x3 = x2 + gate * _dot(p.astype(BF16), wpp_ref[...])
    return _rms(x3, nfin_ref[...])


def _cummax_lanes(a):
    n = a.shape[1]
    lane = lax.broadcasted_iota(jnp.int32, a.shape, 1)
    d = 1
    while d < n:
        shifted = pltpu.roll(a, d, axis=1)
        a = jnp.maximum(a, jnp.where(lane >= d, shifted, -jnp.inf))
        d *= 2
    return a


def _prompt_kernel(tiles_per_seq, num_tiles,
                   x_ref, xlag_ref, plag_ref, wheads_ref, wconv_ref, wgt_ref, bi_ref, bf_ref, nmix_ref, mh_ref, cw_ref,
                   wout_ref, nmlp_ref, wup_ref, wdown_ref, nple_ref, wpg_ref, wpp_ref, nfin_ref, tri_ref,
                   y_ref, cout_ref, mout_ref, convout_ref,
                   caug, m_scr, cu_buf, mix_scr):
    t = SEQ_TILE
    lc = CHUNK
    s_id = pl.program_id(0)
    is_real = s_id < num_tiles
    j = lax.rem(jnp.minimum(s_id, num_tiles - 1), tiles_per_seq)

    @pl.when(s_id == 0)
    def _():
        mix_scr[...] = jnp.zeros_like(mix_scr)

    @pl.when(j == 0)
    def _():
        caug[...] = jnp.zeros_like(caug)
        m_scr[...] = jnp.full(m_scr.shape, M_INIT, F32)
        cu_buf[0:SUBLANES, :] = jnp.zeros((SUBLANES, CONV_CH), F32)

    x1 = xlag_ref[0] + _dot(mix_scr[...], wout_ref[...])
    xn = _rms(x1, nmlp_ref[...]).astype(BF16)

    hb = _rms(x_ref[0], nmix_ref[...]).astype(BF16)
    gt = _dot_nt(wgt_ref[...], hb)

    row_id = lax.broadcasted_iota(jnp.int32, (lc, lc), 0)
    col_id = lax.broadcasted_iota(jnp.int32, (lc, lc), 1)
    causal = col_id <= row_id
    lane_id = lax.broadcasted_iota(jnp.int32, (lc, HEAD_DIM), 1)

    acc = jnp.zeros((t, D_MODEL), F32)
    n_phases = (t // lc) * HEADS
    mlp_per_phase = D_FF // FF_CHUNK // n_phases
    mlp_first = (mlp_per_phase + 1) // 2
    next_mlp = 0
    m_carry = m_scr[...]
    for c in range(t // lc):
        tok = slice(c * lc, (c + 1) * lc)
        hb_c = hb[tok, :]
        g8 = gt[0:SUBLANES, tok]
        ig = g8 + bi_ref[...]
        lf = _log_sigmoid(pltpu.roll(g8, HEADS, axis=0) + bf_ref[...])
        hi = lf.astype(BF16)
        r1 = lf - hi.astype(F32)
        mid = r1.astype(BF16)
        lo = (r1 - mid.astype(F32)).astype(BF16)
        parts = _dot(jnp.concatenate([hi, mid, lo, jnp.zeros_like(lo)], axis=0), tri_ref[...])
        b = parts[0:8] + parts[8:16] + parts[16:24]
        a = ig - b
        m_prev = jnp.concatenate([m_carry] * (lc // LANES), axis=1)
        g = jnp.maximum(_cummax_lanes(a), m_prev)
        m_t = b + g
        b_last = b[:, lc - 1:lc]
        m_new = m_t[:, lc - 1:lc]
        decay = jnp.exp(m_prev - g)
        e_neg_m = jnp.exp(-m_t)
        w_state = jnp.exp(a + (b_last - m_new))
        c_dec = jnp.exp(b_last + m_prev[:, 0:1] - m_new)
        m_carry = jnp.broadcast_to(m_new, m_carry.shape)
        a2 = a * LOG2E
        rows = jnp.concatenate(
            [g * (-LOG2E), decay, e_neg_m, w_state, jnp.zeros((LANES - 4 * SUBLANES, lc), F32)], axis=0)
        cols = rows.T

        for h in range(HEADS):
            if h % 2 == 0:
                pair = slice(h * HEAD_DIM, (h + 2) * HEAD_DIM)
                zq2, zk2, zv2, zo2 = (
                    _dot(hb_c, wheads_ref[:, grp * MLSTM_W + pair.start:grp * MLSTM_W + pair.stop])
                    for grp in range(4))
            half = slice((h % 2) * HEAD_DIM, (h % 2 + 1) * HEAD_DIM)
            qs = (zq2[:, half] * Q_SCALE).astype(BF16)
            kb = zk2[:, half].astype(BF16)
            v = zv2[:, half]
            c_col = cols[:, h:h + 1]
            dec = cols[:, SUBLANES + h:SUBLANES + h + 1]
            enm = cols[:, 2 * SUBLANES + h:2 * SUBLANES + h + 1]
            wst = cols[:, 3 * SUBLANES + h:3 * SUBLANES + h + 1]
            s = _dot_nt(qs, kb)

            for _ in range(mlp_first):
                acc = acc + _mlp_chunk(xn, next_mlp, wup_ref, wdown_ref)
                next_mlp += 1

            dmat = jnp.exp2(jnp.where(causal, c_col + a2[h:h + 1, :], -jnp.inf))
            pm = s * dmat
            row_sum = jnp.sum(pm, axis=1, keepdims=True)
            intra = _dot(pm.astype(BF16), v.astype(BF16))
            c_state = caug[h]
            inter = _dot(qs, c_state.astype(BF16))
            vw = jnp.concatenate([v * wst, jnp.where(lane_id == 0, wst, 0.0)], axis=1).astype(BF16)
            caug[h] = c_dec[h:h + 1, :] * c_state + _dot_tn(kb, vw)

            for _ in range(mlp_per_phase - mlp_first):
                acc = acc + _mlp_chunk(xn, next_mlp, wup_ref, wdown_ref)
                next_mlp += 1

            num = intra + dec * inter[:, 0:HEAD_DIM]
            den = row_sum + dec * inter[:, HEAD_DIM:HEAD_DIM + 1]
            hh = num / jnp.maximum(jnp.abs(den), enm)
            hn = hh * lax.rsqrt(jnp.mean(hh * hh, axis=-1, keepdims=True) + EPS)
            hn = hn * mh_ref[:, h * HEAD_DIM:(h + 1) * HEAD_DIM]
            mix_scr[tok, h * HEAD_DIM:(h + 1) * HEAD_DIM] = (
                hn * jax.nn.sigmoid(zo2[:, half])).astype(BF16)
    m_scr[...] = m_carry
    assert next_mlp == D_FF // FF_CHUNK

    y_ref[0] = _ple_and_final(x1 + acc, plag_ref[0], nple_ref, wpg_ref, wpp_ref, nfin_ref)

    zc = _dot(hb, wconv_ref[...])
    cu = zc[:, CONV_CH:2 * CONV_CH] * zc[:, 2 * CONV_CH:3 * CONV_CH]
    cu_buf[SUBLANES:SUBLANES + t, :] = cu
    conv = (cw_ref[0:1, :] * cu_buf[SUBLANES - 2:SUBLANES - 2 + t, :]
            + cw_ref[1:2, :] * cu_buf[SUBLANES - 1:SUBLANES - 1 + t, :]
            + cw_ref[2:3, :] * cu)
    mix_scr[:, MLSTM_W:D_MODEL] = (zc[:, 0:CONV_CH] * conv).astype(BF16)
    cu_buf[0:SUBLANES, :] = cu_buf[t:t + SUBLANES, :]

    @pl.when(jnp.logical_and(is_real, j == tiles_per_seq - 1))
    def _():
        cout_ref[0] = caug[...]
        mout_ref[0] = m_scr[...]
        convout_ref[0] = cu[t - (CONV_WIDTH - 1):t, :]


def _resident(shape):
    return pl.BlockSpec(shape, lambda *_: (0,) * len(shape), pipeline_mode=pl.Buffered(1))


def _prompt_call(x, p, wts):
    bsz, seq, _ = x.shape
    t = SEQ_TILE
    assert seq % t == 0 and t % CHUNK == 0 and CHUNK % LANES == 0
    tri = (jnp.arange(CHUNK)[:, None] <= jnp.arange(CHUNK)[None, :]).astype(BF16)
    consts = [wts["w_heads"], wts["w_conv"], wts["w_gt"], wts["bi8"], wts["bf8"], wts["norm_mix"], wts["mh_norm"], wts["conv_w"],
              wts["w_out"], wts["norm_mlp"], wts["w_up"], wts["w_down"], wts["norm_ple"], wts["w_pg"],
              wts["w_pp"], wts["norm_final"], tri]
    nj = seq // t
    nt = bsz * nj

    def cur(s):
        c = jnp.minimum(s, nt - 1)
        return c // nj, c % nj

    def lag(s):
        c = jnp.maximum(s - 1, 0)
        return c // nj, c % nj

    in_specs = [pl.BlockSpec((1, t, D_MODEL), lambda s: (*cur(s), 0)),
                pl.BlockSpec((1, t, D_MODEL), lambda s: (*lag(s), 0)),
                pl.BlockSpec((1, t, PLE_DIM), lambda s: (*lag(s), 0))]
    in_specs += [_resident(c.shape) for c in consts]
    out_shape = (jax.ShapeDtypeStruct((bsz, seq, D_MODEL), F32),
                 jax.ShapeDtypeStruct((bsz, HEADS, HEAD_DIM, 2 * HEAD_DIM), F32),
                 jax.ShapeDtypeStruct((bsz, SUBLANES, LANES), F32),
                 jax.ShapeDtypeStruct((bsz, CONV_WIDTH - 1, CONV_CH), F32))
    out_specs = (pl.BlockSpec((1, t, D_MODEL), lambda s: (*lag(s), 0)),
                 pl.BlockSpec((1, HEADS, HEAD_DIM, 2 * HEAD_DIM), lambda s: (cur(s)[0], 0, 0, 0)),
                 pl.BlockSpec((1, SUBLANES, LANES), lambda s: (cur(s)[0], 0, 0)),
                 pl.BlockSpec((1, CONV_WIDTH - 1, CONV_CH), lambda s: (cur(s)[0], 0, 0)))
    scratch = [pltpu.VMEM((HEADS, HEAD_DIM, 2 * HEAD_DIM), F32),
               pltpu.VMEM((SUBLANES, LANES), F32),
               pltpu.VMEM((t + SUBLANES, CONV_CH), F32),
               pltpu.VMEM((t, D_MODEL), BF16)]
    return pl.pallas_call(
        functools.partial(_prompt_kernel, nj, nt),
        grid=(nt + 1,),
        in_specs=in_specs,
        out_specs=out_specs,
        out_shape=out_shape,
        scratch_shapes=scratch,
        compiler_params=pltpu.CompilerParams(
            dimension_semantics=("arbitrary",),
            vmem_limit_bytes=VMEM_LIMIT_BYTES),
        name="prompt_layer",
    )(x, x, p, *consts)


def _sample_inproj_kernel(x_ref, wheads_ref, wconv_ref, wgc_ref, nmix_ref, zh_ref, zc_ref, g_ref):
    hb = _rms(x_ref[...], nmix_ref[...]).astype(BF16)
    zh_ref[...] = _dot(hb, wheads_ref[...])
    zc_ref[...] = _dot(hb, wconv_ref[...])
    g_ref[...] = _dot(hb, wgc_ref[...])


def _sample_state_kernel(zq_ref, zk_ref, zv_ref, zo_ref, g_ref, c0_ref, n0_ref, m0_ref, bi_ref, bf_ref, mh_ref,
                         hm_ref, cnew_ref, nnew_ref, mnew_ref,
                         inter_scr, wv_scr, cd_scr):
    tb = SAMPLE_TILE
    g = g_ref[...]
    ig = g[:, 0:HEADS] + bi_ref[...]
    lf = _log_sigmoid(g[:, HEADS:2 * HEADS] + bf_ref[...])
    m_inter = lf + m0_ref[...]
    m_new = jnp.maximum(m_inter, ig)
    w_in = jnp.exp(ig - m_new)
    c_dec = jnp.exp(m_inter - m_new)
    e_neg_m = jnp.exp(-m_new)
    mnew_ref[...] = m_new

    vs, scores, qns = [], [], []
    for h in range(HEADS):
        hs = slice(h * HEAD_DIM, (h + 1) * HEAD_DIM)
        q = zq_ref[:, hs] * Q_SCALE
        k = zk_ref[:, hs]
        v = zv_ref[:, hs]
        n0 = n0_ref[:, hs]
        wi = w_in[:, h:h + 1]
        cd = c_dec[:, h:h + 1]
        scores.append(jnp.sum(q * k, axis=1, keepdims=True) * wi)
        qns.append(jnp.sum(q * n0, axis=1, keepdims=True))
        wv_scr[:, hs] = wi * v
        cd_scr[:, hs] = jnp.broadcast_to(cd, (tb, HEAD_DIM))
        nnew_ref[:, hs] = cd * n0 + wi * k
        vs.append(v)

    eye = (lax.broadcasted_iota(jnp.int32, (HEAD_DIM, HEAD_DIM), 0)
           == lax.broadcasted_iota(jnp.int32, (HEAD_DIM, HEAD_DIM), 1))

    for i in range(tb):
        row = slice(i, i + 1)
        for h in range(HEADS):
            hs = slice(h * HEAD_DIM, (h + 1) * HEAD_DIM)
            c0 = c0_ref[i, h]
            q_rows = jnp.broadcast_to(zq_ref[row, hs] * Q_SCALE, (SUBLANES, HEAD_DIM)).astype(BF16)
            inter_scr[row, hs] = _dot(q_rows, c0.astype(BF16))[0:1, :]
            k_diag = jnp.where(eye, jnp.broadcast_to(zk_ref[row, hs], (HEAD_DIM, HEAD_DIM)), 0.0).astype(BF16)
            v_rows = jnp.broadcast_to(wv_scr[row, hs], (HEAD_DIM, HEAD_DIM)).astype(BF16)
            cd = jnp.broadcast_to(cd_scr[row, hs], (HEAD_DIM, HEAD_DIM))
            cnew_ref[i, h] = cd * c0 + _dot(k_diag, v_rows)

    for h in range(HEADS):
        hs = slice(h * HEAD_DIM, (h + 1) * HEAD_DIM)
        cd = c_dec[:, h:h + 1]
        num = scores[h] * vs[h] + cd * inter_scr[:, hs]
        den = scores[h] + cd * qns[h]
        hh = num / jnp.maximum(jnp.abs(den), e_neg_m[:, h:h + 1])
        hn = hh * lax.rsqrt(jnp.mean(hh * hh, axis=-1, keepdims=True) + EPS)
        hm_ref[:, hs] = hn * mh_ref[:, hs] * jax.nn.sigmoid(zo_ref[:, hs])


def _sample_tail_kernel(x_ref, p_ref, hm_ref, zc_ref, sc_ref, cw_ref,
                        wout_ref, nmlp_ref, wup_ref, wdown_ref, nple_ref, wpg_ref, wpp_ref, nfin_ref,
                        y_ref, buf_ref):
    gb = zc_ref[:, 0:CONV_CH]
    cu = zc_ref[:, CONV_CH:2 * CONV_CH] * zc_ref[:, 2 * CONV_CH:3 * CONV_CH]
    old0 = sc_ref[:, 0:CONV_CH]
    old1 = sc_ref[:, CONV_CH:2 * CONV_CH]
    yc = gb * (cw_ref[0:1, :] * old0 + cw_ref[1:2, :] * old1 + cw_ref[2:3, :] * cu)
    buf_ref[:, 0:CONV_CH] = old1
    buf_ref[:, CONV_CH:2 * CONV_CH] = cu
    mix = jnp.concatenate([hm_ref[...], yc], axis=1).astype(BF16)
    x1 = x_ref[...] + _dot(mix, wout_ref[...])
    xn = _rms(x1, nmlp_ref[...]).astype(BF16)
    acc = jnp.zeros_like(x1)
    for c in range(D_FF // FF_CHUNK):
        acc = acc + _mlp_chunk(xn, c, wup_ref, wdown_ref)
    y_ref[...] = _ple_and_final(x1 + acc, p_ref[...], nple_ref, wpg_ref, wpp_ref, nfin_ref)


def _sample_calls(xs, ps, c0, n0, m0, sconv, wts):
    nb = xs.shape[0]
    tb = SAMPLE_TILE
    assert nb % tb == 0
    zh, zc, g = pl.pallas_call(
        _sample_inproj_kernel,
        out_shape=(jax.ShapeDtypeStruct((nb, 4 * MLSTM_W), F32),
                   jax.ShapeDtypeStruct((nb, 3 * CONV_CH), F32),
                   jax.ShapeDtypeStruct((nb, LANES), F32)),
        compiler_params=pltpu.CompilerParams(vmem_limit_bytes=VMEM_LIMIT_BYTES),
        name="sample_inproj",
    )(xs, wts["w_heads"], wts["w_conv"], wts["w_gc"], wts["norm_mix"])

    def zgroup(grp):
        return pl.BlockSpec((tb, MLSTM_W), lambda i, grp=grp: (i, grp))

    row_w = pl.BlockSpec((tb, MLSTM_W), lambda i: (i, 0))
    row_h = pl.BlockSpec((tb, HEADS), lambda i: (i, 0))
    state = pl.BlockSpec((tb, HEADS, HEAD_DIM, HEAD_DIM), lambda i: (i, 0, 0, 0))
    hm, c_new, n_new, m_new = pl.pallas_call(
        _sample_state_kernel,
        grid=(nb // tb,),
        in_specs=[zgroup(0), zgroup(1), zgroup(2), zgroup(3),
                  pl.BlockSpec((tb, LANES), lambda i: (i, 0)),
                  state, row_w, row_h,
                  pl.BlockSpec((1, HEADS), lambda i: (0, 0)),
                  pl.BlockSpec((1, HEADS), lambda i: (0, 0)),
                  pl.BlockSpec((1, MLSTM_W), lambda i: (0, 0))],
        out_specs=(row_w, state, row_w, row_h),
        out_shape=(jax.ShapeDtypeStruct((nb, MLSTM_W), F32),
                   jax.ShapeDtypeStruct((nb, HEADS, HEAD_DIM, HEAD_DIM), F32),
                   jax.ShapeDtypeStruct((nb, MLSTM_W), F32),
                   jax.ShapeDtypeStruct((nb, HEADS), F32)),
        scratch_shapes=[pltpu.VMEM((tb, MLSTM_W), F32)] * 3,
        compiler_params=pltpu.CompilerParams(dimension_semantics=("arbitrary",)),
        name="sample_state",
    )(zh, zh, zh, zh, g, c0, n0, m0, wts["b_i"], wts["b_f"], wts["mh_norm"])

    ys, new_buf = pl.pallas_call(
        _sample_tail_kernel,
        out_shape=(jax.ShapeDtypeStruct((nb, D_MODEL), F32),
                   jax.ShapeDtypeStruct((nb, (CONV_WIDTH - 1) * CONV_CH), F32)),
        compiler_params=pltpu.CompilerParams(vmem_limit_bytes=VMEM_LIMIT_BYTES),
        name="sample_tail",
    )(xs, ps, hm, zc, sconv, wts["conv_w"], wts["w_out"], wts["norm_mlp"], wts["w_up"], wts["w_down"],
      wts["norm_ple"], wts["w_pg"], wts["w_pp"], wts["norm_final"])
    return ys, c_new, n_new, m_new, new_buf


def _weight_prep_kernel(win_ref, wout_ref, wup_ref, wdown_ref, wpg_ref, wpp_ref,
                        heads_ref, conv_ref, gt_ref, gc_ref, out_ref, up_ref, down_ref, pg_ref, pp_ref):
    g0 = 4 * MLSTM_W
    heads_ref[...] = win_ref[:, 0:g0].astype(BF16)
    rest = win_ref[:, g0:]
    conv_ref[...] = rest[:, 2 * HEADS:].astype(BF16)
    lane = lax.broadcasted_iota(jnp.int32, (rest.shape[0], LANES), 1)
    gates = jnp.where(lane < 2 * HEADS, rest[:, 0:LANES], 0.0)
    gc_ref[...] = gates.astype(BF16)
    gt_ref[...] = gates.T[0:GATE_ROWS, :].astype(BF16)
    out_ref[...] = wout_ref[...].astype(BF16)
    up_ref[...] = wup_ref[...].astype(BF16)
    down_ref[...] = wdown_ref[...].astype(BF16)
    pg_ref[...] = wpg_ref[...].astype(BF16)
    pp_ref[...] = wpp_ref[...].astype(BF16)


def _prepare_weights(norm_mix, w_in, b_gate_i, b_gate_f, mh_norm, conv_w, w_out, norm_mlp, w_up, w_down,
                     norm_ple, w_ple_gate, w_ple_proj, norm_final):
    srcs = (w_in, w_out, w_up, w_down, w_ple_gate, w_ple_proj)
    n_in = w_in.shape[1]
    outs = ((D_MODEL, 4 * MLSTM_W), (D_MODEL, 3 * CONV_CH), (GATE_ROWS, D_MODEL), (D_MODEL, LANES),
            w_out.shape, w_up.shape, w_down.shape, w_ple_gate.shape, w_ple_proj.shape)
    steps = PREP_STEPS

    def rows(shape):
        return pl.BlockSpec((shape[0] // steps, shape[1]), lambda i: (i, 0))

    out_specs = [rows(o) for o in outs]
    out_specs[2] = pl.BlockSpec((GATE_ROWS, D_MODEL // steps), lambda i: (0, i))
    assert n_in == 4 * MLSTM_W + 2 * HEADS + 3 * CONV_CH
    w_heads, w_conv, w_gt, w_gc, w_out_b, w_up_b, w_down_b, w_pg_b, w_pp_b = pl.pallas_call(
        _weight_prep_kernel,
        grid=(steps,),
        in_specs=[rows(a.shape) for a in srcs],
        out_specs=out_specs,
        out_shape=[jax.ShapeDtypeStruct(o, BF16) for o in outs],
        compiler_params=pltpu.CompilerParams(dimension_semantics=("arbitrary",),
                                             vmem_limit_bytes=VMEM_LIMIT_BYTES),
        name="weight_prep",
    )(*srcs)
    pad = jnp.zeros((SUBLANES - HEADS, 1), F32)
    return dict(
        w_heads=w_heads,
        w_conv=w_conv,
        w_gt=w_gt,
        w_gc=w_gc,
        bi8=jnp.concatenate([b_gate_i.reshape(HEADS, 1), pad], axis=0),
        bf8=jnp.concatenate([b_gate_f.reshape(HEADS, 1), pad], axis=0),
        b_i=b_gate_i.reshape(1, HEADS),
        b_f=b_gate_f.reshape(1, HEADS),
        norm_mix=norm_mix.reshape(1, D_MODEL),
        mh_norm=mh_norm.reshape(1, MLSTM_W),
        conv_w=conv_w,
        w_out=w_out_b,
        norm_mlp=norm_mlp.reshape(1, D_MODEL),
        w_up=w_up_b,
        w_down=w_down_b,
        norm_ple=norm_ple.reshape(1, D_MODEL),
        w_pg=w_pg_b,
        w_pp=w_pp_b,
        norm_final=norm_final.reshape(1, D_MODEL),
    )


def kernel(x_prompt, x_sample, state_mlstm_C, state_mlstm_n, state_mlstm_m, state_conv, p_prompt, p_sample,
           norm_mix, w_in, b_gate_i, b_gate_f, mh_norm, conv_w, w_out, norm_mlp, w_up, w_down, norm_ple,
           w_ple_gate, w_ple_proj, norm_final):
    assert norm_mix.shape[0] == 1, "single-layer trunk"
    wts = _prepare_weights(norm_mix[0], w_in[0], b_gate_i[0], b_gate_f[0], mh_norm[0], conv_w[0], w_out[0],
                           norm_mlp[0], w_up[0], w_down[0], norm_ple[0], w_ple_gate[0], w_ple_proj[0], norm_final)

    y_prompt, c_aug, m_rows, conv_tail = _prompt_call(x_prompt, p_prompt[0], wts)
    prompt_c = c_aug[:, :, :, 0:HEAD_DIM]
    prompt_n = c_aug[:, :, :, HEAD_DIM]
    prompt_m = m_rows[:, 0:HEADS, 0]

    nb = x_sample.shape[0]
    ys, c_new, n_new, m_new, new_buf = _sample_calls(
        x_sample.reshape(nb, D_MODEL), p_sample[0].reshape(nb, PLE_DIM), state_mlstm_C[0],
        state_mlstm_n[0].reshape(nb, MLSTM_W), state_mlstm_m[0],
        state_conv[0].reshape(nb, (CONV_WIDTH - 1) * CONV_CH), wts)

    return (y_prompt, ys.reshape(nb, 1, D_MODEL),
            prompt_c[None], prompt_n[None], prompt_m[None], conv_tail[None],
            c_new[None], n_new.reshape(1, nb, HEADS, HEAD_DIM), m_new[None],
            new_buf.reshape(1, nb, CONV_WIDTH - 1, CONV_CH))
```

```python
import functools
import math

import jax
import jax.numpy as jnp
from jax import lax
from jax.experimental import pallas as pl
from jax.experimental.pallas import tpu as pltpu

F32 = jnp.float32
BF16 = jnp.bfloat16

D_MODEL = 1024
HEADS = 4
HEAD_DIM = 128
MLSTM_W = HEADS * HEAD_DIM
CONV_CH = D_MODEL - MLSTM_W
CONV_WIDTH = 3
D_FF = 4 * D_MODEL
PLE_DIM = 256
EPS = 1e-6
M_INIT = -1e30
Q_SCALE = HEAD_DIM ** -0.5
LOG2E = math.log2(math.e)

GATE_ROWS = 16
PREP_STEPS = 8
LANES = 128
SUBLANES = 8
VMEM_LIMIT_BYTES = 60000 * 1024

SEQ_TILE = 512
CHUNK = 256
FF_CHUNK = 512
SAMPLE_TILE = 16


def _dot(a, b):
    return jnp.dot(a, b, preferred_element_type=F32)


def _dot_nt(a, b):
    return lax.dot_general(a, b, (((1,), (1,)), ((), ())), preferred_element_type=F32)


def _dot_tn(a, b):
    return lax.dot_general(a, b, (((0,), (0,)), ((), ())), preferred_element_type=F32)


def _rms(x, g):
    y = x * lax.rsqrt(jnp.mean(x * x, axis=-1, keepdims=True) + EPS)
    return y * g


def _log_sigmoid(x):
    return jnp.minimum(x, 0.0) - jnp.log1p(jnp.exp(-jnp.abs(x)))


def _mlp_chunk(xn_bf16, c, wup_ref, wdown_ref):
    cols = slice(c * FF_CHUNK, (c + 1) * FF_CHUNK)
    hf = jnp.maximum(_dot(xn_bf16, wup_ref[:, cols]), 0.0)
    return _dot((hf * hf).astype(BF16), wdown_ref[cols, :])


def _ple_and_final(x2, p, nple_ref, wpg_ref, wpp_ref, nfin_ref):
    gate = jax.nn.sigmoid(_dot(_rms(x2, nple_ref[...]).astype(BF16), wpg_ref[...]))
    x3 = x2 + gate * _dot(p.astype(BF16), wpp_ref[...])
    return _rms(x3, nfin_ref[...])


def _cummax_lanes(a):
    n = a.shape[1]
    lane = lax.broadcasted_iota(jnp.int32, a.shape, 1)
    d = 1
    while d < n:
        shifted = pltpu.roll(a, d, axis=1)
        a = jnp.maximum(a, jnp.where(lane >= d, shifted, -jnp.inf))
        d *= 2
    return a


def _prompt_kernel(tiles_per_seq, num_tiles,
                   x_ref, plag_ref, wheads_ref, wconv_ref, wgt_ref, bi_ref, bf_ref, nmix_ref, mh_ref, cw_ref,
                   wout_ref, nmlp_ref, wup_ref, wdown_ref, nple_ref, wpg_ref, wpp_ref, nfin_ref, tri_ref,
                   y_ref, cout_ref, mout_ref, convout_ref,
                   caug, m_scr, cu_buf, mix_scr, xres_scr):
    t = SEQ_TILE
    lc = CHUNK
    s_id = pl.program_id(0)
    is_real = s_id < num_tiles
    j = lax.rem(jnp.minimum(s_id, num_tiles - 1), tiles_per_seq)

    @pl.when(s_id == 0)
    def _():
        mix_scr[...] = jnp.zeros_like(mix_scr)
        xres_scr[...] = jnp.zeros_like(xres_scr)

    @pl.when(j == 0)
    def _():
        caug[...] = jnp.zeros_like(caug)
        m_scr[...] = jnp.full(m_scr.shape, M_INIT, F32)
        cu_buf[0:SUBLANES, :] = jnp.zeros((SUBLANES, CONV_CH), F32)

    y_ref[0] = xres_scr[...] + _dot(mix_scr[...], wout_ref[...])
    xn = _rms(y_ref[0], nmlp_ref[...]).astype(BF16)

    x = x_ref[0]
    hb = _rms(x, nmix_ref[...]).astype(BF16)
    xres_scr[...] = x
    gt = _dot_nt(wgt_ref[...], hb)

    row_id = lax.broadcasted_iota(jnp.int32, (lc, lc), 0)
    col_id = lax.broadcasted_iota(jnp.int32, (lc, lc), 1)
    causal = col_id <= row_id
    lane_id = lax.broadcasted_iota(jnp.int32, (lc, HEAD_DIM), 1)

    acc = jnp.zeros((t, D_MODEL), F32)
    n_phases = (t // lc) * HEADS
    mlp_per_phase = D_FF // FF_CHUNK // n_phases
    mlp_first = (mlp_per_phase + 1) // 2
    next_mlp = 0
    m_carry = m_scr[...]
    for c in range(t // lc):
        tok = slice(c * lc, (c + 1) * lc)
        hb_c = hb[tok, :]
        g8 = gt[0:SUBLANES, tok]
        ig = g8 + bi_ref[...]
        lf = _log_sigmoid(pltpu.roll(g8, HEADS, axis=0) + bf_ref[...])
        hi = lf.astype(BF16)
        r1 = lf - hi.astype(F32)
        mid = r1.astype(BF16)
        lo = (r1 - mid.astype(F32)).astype(BF16)
        parts = _dot(jnp.concatenate([hi, mid, lo, jnp.zeros_like(lo)], axis=0), tri_ref[...])
        b = parts[0:8] + parts[8:16] + parts[16:24]
        a = ig - b
        m_prev = jnp.concatenate([m_carry] * (lc // LANES), axis=1)
        g = jnp.maximum(_cummax_lanes(a), m_prev)
        m_t = b + g
        b_last = b[:, lc - 1:lc]
        m_new = m_t[:, lc - 1:lc]
        decay = jnp.exp(m_prev - g)
        e_neg_m = jnp.exp(-m_t)
        w_state = jnp.exp(a + (b_last - m_new))
        c_dec = jnp.exp(b_last + m_prev[:, 0:1] - m_new)
        m_carry = jnp.broadcast_to(m_new, m_carry.shape)
        a2 = a * LOG2E
        rows = jnp.concatenate(
            [g * (-LOG2E), decay, e_neg_m, w_state, jnp.zeros((LANES - 4 * SUBLANES, lc), F32)], axis=0)
        cols = rows.T

        for h in range(HEADS):
            if h % 2 == 0:
                pair = slice(h * HEAD_DIM, (h + 2) * HEAD_DIM)
                zq2, zk2, zv2, zo2 = (
                    _dot(hb_c, wheads_ref[:, grp * MLSTM_W + pair.start:grp * MLSTM_W + pair.stop])
                    for grp in range(4))
            half = slice((h % 2) * HEAD_DIM, (h % 2 + 1) * HEAD_DIM)
            qs = (zq2[:, half] * Q_SCALE).astype(BF16)
            kb = zk2[:, half].astype(BF16)
            v = zv2[:, half]
            c_col = cols[:, h:h + 1]
            dec = cols[:, SUBLANES + h:SUBLANES + h + 1]
            enm = cols[:, 2 * SUBLANES + h:2 * SUBLANES + h + 1]
            wst = cols[:, 3 * SUBLANES + h:3 * SUBLANES + h + 1]
            s = _dot_nt(qs, kb)

            for _ in range(mlp_first):
                acc = acc + _mlp_chunk(xn, next_mlp, wup_ref, wdown_ref)
                next_mlp += 1

            dmat = jnp.exp2(jnp.where(causal, c_col + a2[h:h + 1, :], -jnp.inf))
            pm = s * dmat
            row_sum = jnp.sum(pm, axis=1, keepdims=True)
            intra = _dot(pm.astype(BF16), v.astype(BF16))
            c_state = caug[h]
            inter = _dot(qs, c_state.astype(BF16))
            vw = jnp.concatenate([v * wst, jnp.where(lane_id == 0, wst, 0.0)], axis=1).astype(BF16)
            caug[h] = c_dec[h:h + 1, :] * c_state + _dot_tn(kb, vw)

            for _ in range(mlp_per_phase - mlp_first):
                acc = acc + _mlp_chunk(xn, next_mlp, wup_ref, wdown_ref)
                next_mlp += 1

            num = intra + dec * inter[:, 0:HEAD_DIM]
            den = row_sum + dec * inter[:, HEAD_DIM:HEAD_DIM + 1]
            hh = num / jnp.maximum(jnp.abs(den), enm)
            hn = hh * lax.rsqrt(jnp.mean(hh * hh, axis=-1, keepdims=True) + EPS)
            hn = hn * mh_ref[:, h * HEAD_DIM:(h + 1) * HEAD_DIM]
            mix_scr[tok, h * HEAD_DIM:(h + 1) * HEAD_DIM] = (
                hn * jax.nn.sigmoid(zo2[:, half])).astype(BF16)
    m_scr[...] = m_carry
    assert next_mlp == D_FF // FF_CHUNK

    y_ref[0] = _ple_and_final(y_ref[0] + acc, plag_ref[0], nple_ref, wpg_ref, wpp_ref, nfin_ref)

    zc = _dot(hb, wconv_ref[...])
    cu = zc[:, CONV_CH:2 * CONV_CH] * zc[:, 2 * CONV_CH:3 * CONV_CH]
    cu_buf[SUBLANES:SUBLANES + t, :] = cu
    conv = (cw_ref[0:1, :] * cu_buf[SUBLANES - 2:SUBLANES - 2 + t, :]
            + cw_ref[1:2, :] * cu_buf[SUBLANES - 1:SUBLANES - 1 + t, :]
            + cw_ref[2:3, :] * cu)
    mix_scr[:, MLSTM_W:D_MODEL] = (zc[:, 0:CONV_CH] * conv).astype(BF16)
    cu_buf[0:SUBLANES, :] = cu_buf[t:t + SUBLANES, :]

    @pl.when(jnp.logical_and(is_real, j == tiles_per_seq - 1))
    def _():
        cout_ref[0] = caug[...]
        mout_ref[0] = m_scr[...]
        convout_ref[0] = cu[t - (CONV_WIDTH - 1):t, :]


def _resident(shape):
    return pl.BlockSpec(shape, lambda *_: (0,) * len(shape), pipeline_mode=pl.Buffered(1))


def _prompt_call(x, p, wts):
    bsz, seq, _ = x.shape
    t = SEQ_TILE
    assert seq % t == 0 and t % CHUNK == 0 and CHUNK % LANES == 0
    tri = (jnp.arange(CHUNK)[:, None] <= jnp.arange(CHUNK)[None, :]).astype(BF16)
    consts = [wts["w_heads"], wts["w_conv"], wts["w_gt"], wts["bi8"], wts["bf8"], wts["norm_mix"], wts["mh_norm"], wts["conv_w"],
              wts["w_out"], wts["norm_mlp"], wts["w_up"], wts["w_down"], wts["norm_ple"], wts["w_pg"],
              wts["w_pp"], wts["norm_final"], tri]
    nj = seq // t
    nt = bsz * nj

    def cur(s):
        c = jnp.minimum(s, nt - 1)
        return c // nj, c % nj

    def lag(s):
        c = jnp.maximum(s - 1, 0)
        return c // nj, c % nj

    in_specs = [pl.BlockSpec((1, t, D_MODEL), lambda s: (*cur(s), 0)),
                pl.BlockSpec((1, t, PLE_DIM), lambda s: (*lag(s), 0))]
    in_specs += [_resident(c.shape) for c in consts]
    out_shape = (jax.ShapeDtypeStruct((bsz, seq, D_MODEL), F32),
                 jax.ShapeDtypeStruct((bsz, HEADS, HEAD_DIM, 2 * HEAD_DIM), F32),
                 jax.ShapeDtypeStruct((bsz, SUBLANES, LANES), F32),
                 jax.ShapeDtypeStruct((bsz, CONV_WIDTH - 1, CONV_CH), F32))
    out_specs = (pl.BlockSpec((1, t, D_MODEL), lambda s: (*lag(s), 0)),
                 pl.BlockSpec((1, HEADS, HEAD_DIM, 2 * HEAD_DIM), lambda s: (cur(s)[0], 0, 0, 0)),
                 pl.BlockSpec((1, SUBLANES, LANES), lambda s: (cur(s)[0], 0, 0)),
                 pl.BlockSpec((1, CONV_WIDTH - 1, CONV_CH), lambda s: (cur(s)[0], 0, 0)))
    scratch = [pltpu.VMEM((HEADS, HEAD_DIM, 2 * HEAD_DIM), F32),
               pltpu.VMEM((SUBLANES, LANES), F32),
               pltpu.VMEM((t + SUBLANES, CONV_CH), F32),
               pltpu.VMEM((t, D_MODEL), BF16),
               pltpu.VMEM((t, D_MODEL), F32)]
    return pl.pallas_call(
        functools.partial(_prompt_kernel, nj, nt),
        grid=(nt + 1,),
        in_specs=in_specs,
        out_specs=out_specs,
        out_shape=out_shape,
        scratch_shapes=scratch,
        compiler_params=pltpu.CompilerParams(
            dimension_semantics=("arbitrary",),
            vmem_limit_bytes=VMEM_LIMIT_BYTES),
        name="prompt_layer",
    )(x, p, *consts)


def _sample_inproj_kernel(x_ref, wheads_ref, wconv_ref, wgc_ref, nmix_ref, zh_ref, zc_ref, g_ref):
    hb = _rms(x_ref[...], nmix_ref[...]).astype(BF16)
    zh_ref[...] = _dot(hb, wheads_ref[...])
    zc_ref[...] = _dot(hb, wconv_ref[...])
    g_ref[...] = _dot(hb, wgc_ref[...])


def _sample_state_kernel(zq_ref, zk_ref, zv_ref, zo_ref, g_ref, c0_ref, n0_ref, m0_ref, bi_ref, bf_ref, mh_ref,
                         hm_ref, cnew_ref, nnew_ref, mnew_ref,
                         inter_scr, wv_scr, cd_scr):
    tb = SAMPLE_TILE
    g = g_ref[...]
    ig = g[:, 0:HEADS] + bi_ref[...]
    lf = _log_sigmoid(g[:, HEADS:2 * HEADS] + bf_ref[...])
    m_inter = lf + m0_ref[...]
    m_new = jnp.maximum(m_inter, ig)
    w_in = jnp.exp(ig - m_new)
    c_dec = jnp.exp(m_inter - m_new)
    e_neg_m = jnp.exp(-m_new)
    mnew_ref[...] = m_new

    vs, scores, qns = [], [], []
    for h in range(HEADS):
        hs = slice(h * HEAD_DIM, (h + 1) * HEAD_DIM)
        q = zq_ref[:, hs] * Q_SCALE
        k = zk_ref[:, hs]
        v = zv_ref[:, hs]
        n0 = n0_ref[:, hs]
        wi = w_in[:, h:h + 1]
        cd = c_dec[:, h:h + 1]
        scores.append(jnp.sum(q * k, axis=1, keepdims=True) * wi)
        qns.append(jnp.sum(q * n0, axis=1, keepdims=True))
        wv_scr[:, hs] = wi * v
        cd_scr[:, hs] = jnp.broadcast_to(cd, (tb, HEAD_DIM))
        nnew_ref[:, hs] = cd * n0 + wi * k
        vs.append(v)

    eye = (lax.broadcasted_iota(jnp.int32, (HEAD_DIM, HEAD_DIM), 0)
           == lax.broadcasted_iota(jnp.int32, (HEAD_DIM, HEAD_DIM), 1))

    for i in range(tb):
        row = slice(i, i + 1)
        for h in range(HEADS):
            hs = slice(h * HEAD_DIM, (h + 1) * HEAD_DIM)
            c0 = c0_ref[i, h]
            q_rows = jnp.broadcast_to(zq_ref[row, hs] * Q_SCALE, (SUBLANES, HEAD_DIM)).astype(BF16)
            inter_scr[row, hs] = _dot(q_rows, c0.astype(BF16))[0:1, :]
            k_diag = jnp.where(eye, jnp.broadcast_to(zk_ref[row, hs], (HEAD_DIM, HEAD_DIM)), 0.0).astype(BF16)
            v_rows = jnp.broadcast_to(wv_scr[row, hs], (HEAD_DIM, HEAD_DIM)).astype(BF16)
            cd = jnp.broadcast_to(cd_scr[row, hs], (HEAD_DIM, HEAD_DIM))
            cnew_ref[i, h] = cd * c0 + _dot(k_diag, v_rows)

    for h in range(HEADS):
        hs = slice(h * HEAD_DIM, (h + 1) * HEAD_DIM)
        cd = c_dec[:, h:h + 1]
        num = scores[h] * vs[h] + cd * inter_scr[:, hs]
        den = scores[h] + cd * qns[h]
        hh = num / jnp.maximum(jnp.abs(den), e_neg_m[:, h:h + 1])
        hn = hh * lax.rsqrt(jnp.mean(hh * hh, axis=-1, keepdims=True) + EPS)
        hm_ref[:, hs] = hn * mh_ref[:, hs] * jax.nn.sigmoid(zo_ref[:, hs])


def _sample_tail_kernel(x_ref, p_ref, hm_ref, zc_ref, sc_ref, cw_ref,
                        wout_ref, nmlp_ref, wup_ref, wdown_ref, nple_ref, wpg_ref, wpp_ref, nfin_ref,
                        y_ref, buf_ref):
    gb = zc_ref[:, 0:CONV_CH]
    cu = zc_ref[:, CONV_CH:2 * CONV_CH] * zc_ref[:, 2 * CONV_CH:3 * CONV_CH]
    old0 = sc_ref[:, 0:CONV_CH]
    old1 = sc_ref[:, CONV_CH:2 * CONV_CH]
    yc = gb * (cw_ref[0:1, :] * old0 + cw_ref[1:2, :] * old1 + cw_ref[2:3, :] * cu)
    buf_ref[:, 0:CONV_CH] = old1
    buf_ref[:, CONV_CH:2 * CONV_CH] = cu
    mix = jnp.concatenate([hm_ref[...], yc], axis=1).astype(BF16)
    x1 = x_ref[...] + _dot(mix, wout_ref[...])
    xn = _rms(x1, nmlp_ref[...]).astype(BF16)
    acc = jnp.zeros_like(x1)
    for c in range(D_FF // FF_CHUNK):
        acc = acc + _mlp_chunk(xn, c, wup_ref, wdown_ref)
    y_ref[...] = _ple_and_final(x1 + acc, p_ref[...], nple_ref, wpg_ref, wpp_ref, nfin_ref)


def _sample_calls(xs, ps, c0, n0, m0, sconv, wts):
    nb = xs.shape[0]
    tb = SAMPLE_TILE
    assert nb % tb == 0
    zh, zc, g = pl.pallas_call(
        _sample_inproj_kernel,
        out_shape=(jax.ShapeDtypeStruct((nb, 4 * MLSTM_W), F32),
                   jax.ShapeDtypeStruct((nb, 3 * CONV_CH), F32),
                   jax.ShapeDtypeStruct((nb, LANES), F32)),
        compiler_params=pltpu.CompilerParams(vmem_limit_bytes=VMEM_LIMIT_BYTES),
        name="sample_inproj",
    )(xs, wts["w_heads"], wts["w_conv"], wts["w_gc"], wts["norm_mix"])

    def zgroup(grp):
        return pl.BlockSpec((tb, MLSTM_W), lambda i, grp=grp: (i, grp))

    row_w = pl.BlockSpec((tb, MLSTM_W), lambda i: (i, 0))
    row_h = pl.BlockSpec((tb, HEADS), lambda i: (i, 0))
    state = pl.BlockSpec((tb, HEADS, HEAD_DIM, HEAD_DIM), lambda i: (i, 0, 0, 0))
    hm, c_new, n_new, m_new = pl.pallas_call(
        _sample_state_kernel,
        grid=(nb // tb,),
        in_specs=[zgroup(0), zgroup(1), zgroup(2), zgroup(3),
                  pl.BlockSpec((tb, LANES), lambda i: (i, 0)),
                  state, row_w, row_h,
                  pl.BlockSpec((1, HEADS), lambda i: (0, 0)),
                  pl.BlockSpec((1, HEADS), lambda i: (0, 0)),
                  pl.BlockSpec((1, MLSTM_W), lambda i: (0, 0))],
        out_specs=(row_w, state, row_w, row_h),
        out_shape=(jax.ShapeDtypeStruct((nb, MLSTM_W), F32),
                   jax.ShapeDtypeStruct((nb, HEADS, HEAD_DIM, HEAD_DIM), F32),
                   jax.ShapeDtypeStruct((nb, MLSTM_W), F32),
                   jax.ShapeDtypeStruct((nb, HEADS), F32)),
        scratch_shapes=[pltpu.VMEM((tb, MLSTM_W), F32)] * 3,
        compiler_params=pltpu.CompilerParams(dimension_semantics=("arbitrary",)),
        name="sample_state",
    )(zh, zh, zh, zh, g, c0, n0, m0, wts["b_i"], wts["b_f"], wts["mh_norm"])

    ys, new_buf = pl.pallas_call(
        _sample_tail_kernel,
        out_shape=(jax.ShapeDtypeStruct((nb, D_MODEL), F32),
                   jax.ShapeDtypeStruct((nb, (CONV_WIDTH - 1) * CONV_CH), F32)),
        compiler_params=pltpu.CompilerParams(vmem_limit_bytes=VMEM_LIMIT_BYTES),
        name="sample_tail",
    )(xs, ps, hm, zc, sconv, wts["conv_w"], wts["w_out"], wts["norm_mlp"], wts["w_up"], wts["w_down"],
      wts["norm_ple"], wts["w_pg"], wts["w_pp"], wts["norm_final"])
    return ys, c_new, n_new, m_new, new_buf


def _weight_prep_kernel(win_ref, wout_ref, wup_ref, wdown_ref, wpg_ref, wpp_ref,
                        heads_ref, conv_ref, gt_ref, gc_ref, out_ref, up_ref, down_ref, pg_ref, pp_ref):
    g0 = 4 * MLSTM_W
    heads_ref[...] = win_ref[:, 0:g0]
    rest = win_ref[:, g0:]
    conv_ref[...] = rest[:, 2 * HEADS:]
    lane = lax.broadcasted_iota(jnp.int32, (rest.shape[0], LANES), 1)
    gates = jnp.where(lane < 2 * HEADS, rest[:, 0:LANES].astype(F32), 0.0)
    gc_ref[...] = gates.astype(BF16)
    gt_ref[...] = gates.T[0:GATE_ROWS, :].astype(BF16)
    out_ref[...] = wout_ref[...].astype(BF16)
    up_ref[...] = wup_ref[...].astype(BF16)
    down_ref[...] = wdown_ref[...].astype(BF16)
    pg_ref[...] = wpg_ref[...].astype(BF16)
    pp_ref[...] = wpp_ref[...].astype(BF16)


def _prepare_weights(norm_mix, w_in, b_gate_i, b_gate_f, mh_norm, conv_w, w_out, norm_mlp, w_up, w_down,
                     norm_ple, w_ple_gate, w_ple_proj, norm_final):
    srcs = (w_in.astype(BF16), w_out, w_up, w_down, w_ple_gate, w_ple_proj)
    n_in = w_in.shape[1]
    outs = ((D_MODEL, 4 * MLSTM_W), (D_MODEL, 3 * CONV_CH), (GATE_ROWS, D_MODEL), (D_MODEL, LANES),
            w_out.shape, w_up.shape, w_down.shape, w_ple_gate.shape, w_ple_proj.shape)
    steps = PREP_STEPS

    def rows(shape):
        return pl.BlockSpec((shape[0] // steps, shape[1]), lambda i: (i, 0))

    out_specs = [rows(o) for o in outs]
    out_specs[2] = pl.BlockSpec((GATE_ROWS, D_MODEL // steps), lambda i: (0, i))
    assert n_in == 4 * MLSTM_W + 2 * HEADS + 3 * CONV_CH
    w_heads, w_conv, w_gt, w_gc, w_out_b, w_up_b, w_down_b, w_pg_b, w_pp_b = pl.pallas_call(
        _weight_prep_kernel,
        grid=(steps,),
        in_specs=[rows(a.shape) for a in srcs],
        out_specs=out_specs,
        out_shape=[jax.ShapeDtypeStruct(o, BF16) for o in outs],
        compiler_params=pltpu.CompilerParams(dimension_semantics=("arbitrary",),
                                             vmem_limit_bytes=VMEM_LIMIT_BYTES),
        name="weight_prep",
    )(*srcs)
    pad = jnp.zeros((SUBLANES - HEADS, 1), F32)
    return dict(
        w_heads=w_heads,
        w_conv=w_conv,
        w_gt=w_gt,
        w_gc=w_gc,
        bi8=jnp.concatenate([b_gate_i.reshape(HEADS, 1), pad], axis=0),
        bf8=jnp.concatenate([b_gate_f.reshape(HEADS, 1), pad], axis=0),
        b_i=b_gate_i.reshape(1, HEADS),
        b_f=b_gate_f.reshape(1, HEADS),
        norm_mix=norm_mix.reshape(1, D_MODEL),
        mh_norm=mh_norm.reshape(1, MLSTM_W),
        conv_w=conv_w,
        w_out=w_out_b,
        norm_mlp=norm_mlp.reshape(1, D_MODEL),
        w_up=w_up_b,
        w_down=w_down_b,
        norm_ple=norm_ple.reshape(1, D_MODEL),
        w_pg=w_pg_b,
        w_pp=w_pp_b,
        norm_final=norm_final.reshape(1, D_MODEL),
    )


def kernel(x_prompt, x_sample, state_mlstm_C, state_mlstm_n, state_mlstm_m, state_conv, p_prompt, p_sample,
           norm_mix, w_in, b_gate_i, b_gate_f, mh_norm, conv_w, w_out, norm_mlp, w_up, w_down, norm_ple,
           w_ple_gate, w_ple_proj, norm_final):
    assert norm_mix.shape[0] == 1, "single-layer trunk"
    wts = _prepare_weights(norm_mix[0], w_in[0], b_gate_i[0], b_gate_f[0], mh_norm[0], conv_w[0], w_out[0],
                           norm_mlp[0], w_up[0], w_down[0], norm_ple[0], w_ple_gate[0], w_ple_proj[0], norm_final)

    y_prompt, c_aug, m_rows, conv_tail = _prompt_call(x_prompt, p_prompt[0], wts)
    prompt_c = c_aug[:, :, :, 0:HEAD_DIM]
    prompt_n = c_aug[:, :, :, HEAD_DIM]
    prompt_m = m_rows[:, 0:HEADS, 0]

    nb = x_sample.shape[0]
    ys, c_new, n_new, m_new, new_buf = _sample_calls(
        x_sample.reshape(nb, D_MODEL), p_sample[0].reshape(nb, PLE_DIM), state_mlstm_C[0],
        state_mlstm_n[0].reshape(nb, MLSTM_W), state_mlstm_m[0],
        state_conv[0].reshape(nb, (CONV_WIDTH - 1) * CONV_CH), wts)

    return (y_prompt, ys.reshape(nb, 1, D_MODEL),
            prompt_c[None], prompt_n[None], prompt_m[None], conv_tail[None],
            c_new[None], n_new.reshape(1, nb, HEADS, HEAD_DIM), m_new[None],
            new_buf.reshape(1, nb, CONV_WIDTH - 1, CONV_CH))
```

```python
import functools
import math

import jax
import jax.numpy as jnp
from jax import lax
from jax.experimental import pallas as pl
from jax.experimental.pallas import tpu as pltpu

F32 = jnp.float32
BF16 = jnp.bfloat16

D_MODEL = 1024
HEADS = 4
HEAD_DIM = 128
MLSTM_W = HEADS * HEAD_DIM
CONV_CH = D_MODEL - MLSTM_W
CONV_WIDTH = 3
D_FF = 4 * D_MODEL
PLE_DIM = 256
EPS = 1e-6
M_INIT = -1e30
Q_SCALE = HEAD_DIM ** -0.5
LOG2E = math.log2(math.e)

GATE_ROWS = 16
PREP_STEPS = 8
LANES = 128
SUBLANES = 8
VMEM_LIMIT_BYTES = 60000 * 1024

SEQ_TILE = 512
CHUNK = 256
FF_CHUNK = 512
SAMPLE_TILE = 16


def _dot(a, b):
    return jnp.dot(a, b, preferred_element_type=F32)


def _dot_nt(a, b):
    return lax.dot_general(a, b, (((1,), (1,)), ((), ())), preferred_element_type=F32)


def _dot_tn(a, b):
    return lax.dot_general(a, b, (((0,), (0,)), ((), ())), preferred_element_type=F32)


def _rms(x, g):
    y = x * lax.rsqrt(jnp.mean(x * x, axis=-1, keepdims=True) + EPS)
    return y * g


def _log_sigmoid(x):
    return jnp.minimum(x, 0.0) - jnp.log1p(jnp.exp(-jnp.abs(x)))


def _mlp_chunk(xn_bf16, c, wup_ref, wdown_ref):
    cols = slice(c * FF_CHUNK, (c + 1) * FF_CHUNK)
    hf = jnp.maximum(_dot(xn_bf16, wup_ref[:, cols]), 0.0)
    return _dot((hf * hf).astype(BF16), wdown_ref[cols, :])


def _ple_and_final(x2, p, nple_ref, wpg_ref, wpp_ref, nfin_ref):
    gate = jax.nn.sigmoid(_dot(_rms(x2, nple_ref[...]).astype(BF16), wpg_ref[...]))
    x3 = x2 + gate * _dot(p.astype(BF16), wpp_ref[...])
    return _rms(x3, nfin_ref[...])


def _cummax_lanes(a):
    n = a.shape[1]
    lane = lax.broadcasted_iota(jnp.int32, a.shape, 1)
    d = 1
    while d < n:
        shifted = pltpu.roll(a, d, axis=1)
        a = jnp.maximum(a, jnp.where(lane >= d, shifted, -jnp.inf))
        d *= 2
    return a


def _prompt_kernel(tiles_per_seq, num_tiles,
                   x_ref, plag_ref, xs_ref, ps_ref, mixs_ref,
                   wheads_ref, wconv_ref, wgt_ref, bi_ref, bf_ref, nmix_ref, mh_ref, cw_ref,
                   wout_ref, nmlp_ref, wup_ref, wdown_ref, nple_ref, wpg_ref, wpp_ref, nfin_ref, tri_ref,
                   y_ref, ys_ref, cout_ref, nout_ref, mout_ref, convout_ref,
                   caug, m_scr, cu_buf, mix_scr, xres_scr):
    t = SEQ_TILE
    lc = CHUNK
    ns = xs_ref.shape[0]
    s_id = pl.program_id(0)
    is_real = s_id < num_tiles
    j = lax.rem(jnp.minimum(s_id, num_tiles - 1), tiles_per_seq)

    @pl.when(s_id == 0)
    def _():
        mix_scr[0:ns, :] = mixs_ref[...]
        mix_scr[ns:t, :] = jnp.zeros((t - ns, D_MODEL), BF16)
        xres_scr[0:ns, :] = xs_ref[...]
        xres_scr[ns:t, :] = jnp.zeros((t - ns, D_MODEL), F32)

    @pl.when(j == 0)
    def _():
        caug[...] = jnp.zeros_like(caug)
        m_scr[...] = jnp.full(m_scr.shape, M_INIT, F32)
        cu_buf[0:SUBLANES, :] = jnp.zeros((SUBLANES, CONV_CH), F32)

    y_ref[0] = xres_scr[...] + _dot(mix_scr[...], wout_ref[...])
    xn = _rms(y_ref[0], nmlp_ref[...]).astype(BF16)

    x = x_ref[0]
    hb = _rms(x, nmix_ref[...]).astype(BF16)
    xres_scr[...] = x
    gt = _dot_nt(wgt_ref[...], hb)

    row_id = lax.broadcasted_iota(jnp.int32, (lc, lc), 0)
    col_id = lax.broadcasted_iota(jnp.int32, (lc, lc), 1)
    causal = col_id <= row_id
    lane_id = lax.broadcasted_iota(jnp.int32, (lc, HEAD_DIM), 1)

    acc = jnp.zeros((t, D_MODEL), F32)
    n_phases = (t // lc) * HEADS
    mlp_per_phase = D_FF // FF_CHUNK // n_phases
    mlp_first = (mlp_per_phase + 1) // 2
    next_mlp = 0
    m_carry = m_scr[...]
    for c in range(t // lc):
        tok = slice(c * lc, (c + 1) * lc)
        hb_c = hb[tok, :]
        g8 = gt[0:SUBLANES, tok]
        ig = g8 + bi_ref[...]
        lf = _log_sigmoid(pltpu.roll(g8, HEADS, axis=0) + bf_ref[...])
        hi = lf.astype(BF16)
        r1 = lf - hi.astype(F32)
        mid = r1.astype(BF16)
        lo = (r1 - mid.astype(F32)).astype(BF16)
        parts = _dot(jnp.concatenate([hi, mid, lo, jnp.zeros_like(lo)], axis=0), tri_ref[...])
        b = parts[0:8] + parts[8:16] + parts[16:24]
        a = ig - b
        m_prev = jnp.concatenate([m_carry] * (lc // LANES), axis=1)
        g = jnp.maximum(_cummax_lanes(a), m_prev)
        m_t = b + g
        b_last = b[:, lc - 1:lc]
        m_new = m_t[:, lc - 1:lc]
        decay = jnp.exp(m_prev - g)
        e_neg_m = jnp.exp(-m_t)
        w_state = jnp.exp(a + (b_last - m_new))
        c_dec = jnp.exp(b_last + m_prev[:, 0:1] - m_new)
        m_carry = jnp.broadcast_to(m_new, m_carry.shape)
        a2 = a * LOG2E
        rows = jnp.concatenate(
            [g * (-LOG2E), decay, e_neg_m, w_state, jnp.zeros((LANES - 4 * SUBLANES, lc), F32)], axis=0)
        cols = rows.T

        for h in range(HEADS):
            if h % 2 == 0:
                pair = slice(h * HEAD_DIM, (h + 2) * HEAD_DIM)
                zq2, zk2, zv2, zo2 = (
                    _dot(hb_c, wheads_ref[:, grp * MLSTM_W + pair.start:grp * MLSTM_W + pair.stop])
                    for grp in range(4))
            half = slice((h % 2) * HEAD_DIM, (h % 2 + 1) * HEAD_DIM)
            qs = (zq2[:, half] * Q_SCALE).astype(BF16)
            kb = zk2[:, half].astype(BF16)
            v = zv2[:, half]
            c_col = cols[:, h:h + 1]
            dec = cols[:, SUBLANES + h:SUBLANES + h + 1]
            enm = cols[:, 2 * SUBLANES + h:2 * SUBLANES + h + 1]
            wst = cols[:, 3 * SUBLANES + h:3 * SUBLANES + h + 1]
            s = _dot_nt(qs, kb)

            for _ in range(mlp_first):
                acc = acc + _mlp_chunk(xn, next_mlp, wup_ref, wdown_ref)
                next_mlp += 1

            dmat = jnp.exp2(jnp.where(causal, c_col + a2[h:h + 1, :], -jnp.inf))
            pm = s * dmat
            row_sum = jnp.sum(pm, axis=1, keepdims=True)
            intra = _dot(pm.astype(BF16), v.astype(BF16))
            c_state = caug[h]
            inter = _dot(qs, c_state.astype(BF16))
            vw = jnp.concatenate([v * wst, jnp.where(lane_id == 0, wst, 0.0)], axis=1).astype(BF16)
            caug[h] = c_dec[h:h + 1, :] * c_state + _dot_tn(kb, vw)

            for _ in range(mlp_per_phase - mlp_first):
                acc = acc + _mlp_chunk(xn, next_mlp, wup_ref, wdown_ref)
                next_mlp += 1

            num = intra + dec * inter[:, 0:HEAD_DIM]
            den = row_sum + dec * inter[:, HEAD_DIM:HEAD_DIM + 1]
            hh = num / jnp.maximum(jnp.abs(den), enm)
            hn = hh * lax.rsqrt(jnp.mean(hh * hh, axis=-1, keepdims=True) + EPS)
            hn = hn * mh_ref[:, h * HEAD_DIM:(h + 1) * HEAD_DIM]
            mix_scr[tok, h * HEAD_DIM:(h + 1) * HEAD_DIM] = (
                hn * jax.nn.sigmoid(zo2[:, half])).astype(BF16)
    m_scr[...] = m_carry
    assert next_mlp == D_FF // FF_CHUNK

    p_sample = jnp.concatenate([ps_ref[...], jnp.zeros((t - ns, PLE_DIM), F32)], axis=0)
    p_tail = jnp.where(s_id == 0, p_sample, plag_ref[0])
    y_ref[0] = _ple_and_final(y_ref[0] + acc, p_tail, nple_ref, wpg_ref, wpp_ref, nfin_ref)

    @pl.when(s_id == 0)
    def _():
        ys_ref[...] = y_ref[0, 0:ns, :]

    zc = _dot(hb, wconv_ref[...])
    cu = zc[:, CONV_CH:2 * CONV_CH] * zc[:, 2 * CONV_CH:3 * CONV_CH]
    cu_buf[SUBLANES:SUBLANES + t, :] = cu
    conv = (cw_ref[0:1, :] * cu_buf[SUBLANES - 2:SUBLANES - 2 + t, :]
            + cw_ref[1:2, :] * cu_buf[SUBLANES - 1:SUBLANES - 1 + t, :]
            + cw_ref[2:3, :] * cu)
    mix_scr[:, MLSTM_W:D_MODEL] = (zc[:, 0:CONV_CH] * conv).astype(BF16)
    cu_buf[0:SUBLANES, :] = cu_buf[t:t + SUBLANES, :]

    @pl.when(jnp.logical_and(is_real, j == tiles_per_seq - 1))
    def _():
        for h in range(HEADS):
            cout_ref[0, h] = caug[h, :, 0:HEAD_DIM]
            nout_ref[0, h:h + 1, :] = caug[h, :, HEAD_DIM:2 * HEAD_DIM].T[0:1, :]
        mout_ref[0] = m_scr[...]
        convout_ref[0] = cu[t - (CONV_WIDTH - 1):t, :]


def _resident(shape):
    return pl.BlockSpec(shape, lambda *_: (0,) * len(shape), pipeline_mode=pl.Buffered(1))


def _prompt_call(x, p, xs, ps, mix_s, wts):
    bsz, seq, _ = x.shape
    t = SEQ_TILE
    assert seq % t == 0 and t % CHUNK == 0 and CHUNK % LANES == 0 and xs.shape[0] <= t
    tri = (jnp.arange(CHUNK)[:, None] <= jnp.arange(CHUNK)[None, :]).astype(BF16)
    consts = [wts["w_heads"], wts["w_conv"], wts["w_gt"], wts["bi8"], wts["bf8"], wts["norm_mix"], wts["mh_norm"], wts["conv_w"],
              wts["w_out"], wts["norm_mlp"], wts["w_up"], wts["w_down"], wts["norm_ple"], wts["w_pg"],
              wts["w_pp"], wts["norm_final"], tri]
    nj = seq // t
    nt = bsz * nj

    def cur(s):
        c = jnp.minimum(s, nt - 1)
        return c // nj, c % nj

    def lag(s):
        c = jnp.maximum(s - 1, 0)
        return c // nj, c % nj

    in_specs = [pl.BlockSpec((1, t, D_MODEL), lambda s: (*cur(s), 0)),
                pl.BlockSpec((1, t, PLE_DIM), lambda s: (*lag(s), 0))]
    in_specs += [_resident(c.shape) for c in (xs, ps, mix_s, *consts)]
    out_shape = (jax.ShapeDtypeStruct((bsz, seq, D_MODEL), F32),
                 jax.ShapeDtypeStruct(xs.shape, F32),
                 jax.ShapeDtypeStruct((bsz, HEADS, HEAD_DIM, HEAD_DIM), F32),
                 jax.ShapeDtypeStruct((bsz, HEADS, HEAD_DIM), F32),
                 jax.ShapeDtypeStruct((bsz, SUBLANES, LANES), F32),
                 jax.ShapeDtypeStruct((bsz, CONV_WIDTH - 1, CONV_CH), F32))
    out_specs = (pl.BlockSpec((1, t, D_MODEL), lambda s: (*lag(s), 0)),
                 pl.BlockSpec(xs.shape, lambda s: (0, 0)),
                 pl.BlockSpec((1, HEADS, HEAD_DIM, HEAD_DIM), lambda s: (cur(s)[0], 0, 0, 0)),
                 pl.BlockSpec((1, HEADS, HEAD_DIM), lambda s: (cur(s)[0], 0, 0)),
                 pl.BlockSpec((1, SUBLANES, LANES), lambda s: (cur(s)[0], 0, 0)),
                 pl.BlockSpec((1, CONV_WIDTH - 1, CONV_CH), lambda s: (cur(s)[0], 0, 0)))
    scratch = [pltpu.VMEM((HEADS, HEAD_DIM, 2 * HEAD_DIM), F32),
               pltpu.VMEM((SUBLANES, LANES), F32),
               pltpu.VMEM((t + SUBLANES, CONV_CH), F32),
               pltpu.VMEM((t, D_MODEL), BF16),
               pltpu.VMEM((t, D_MODEL), F32)]
    return pl.pallas_call(
        functools.partial(_prompt_kernel, nj, nt),
        grid=(nt + 1,),
        in_specs=in_specs,
        out_specs=out_specs,
        out_shape=out_shape,
        scratch_shapes=scratch,
        compiler_params=pltpu.CompilerParams(
            dimension_semantics=("arbitrary",),
            vmem_limit_bytes=VMEM_LIMIT_BYTES),
        name="prompt_layer",
    )(x, p, xs, ps, mix_s, *consts)


def _sample_inproj_kernel(x_ref, wheads_ref, wconv_ref, wgc_ref, nmix_ref, zh_ref, zc_ref, g_ref):
    hb = _rms(x_ref[...], nmix_ref[...]).astype(BF16)
    zh_ref[...] = _dot(hb, wheads_ref[...])
    zc_ref[...] = _dot(hb, wconv_ref[...])
    g_ref[...] = _dot(hb, wgc_ref[...])


def _sample_state_kernel(zq_ref, zk_ref, zv_ref, zo_ref, zb_ref, zc_ref, zu_ref, g_ref, c0_ref, n0_ref, m0_ref,
                         sc_ref, bi_ref, bf_ref, mh_ref, cw_ref,
                         mix_ref, cnew_ref, nnew_ref, mnew_ref, buf_ref,
                         inter_scr, wv_scr, cd_scr):
    tb = SAMPLE_TILE
    cu = zc_ref[...] * zu_ref[...]
    old0 = sc_ref[:, 0:CONV_CH]
    old1 = sc_ref[:, CONV_CH:2 * CONV_CH]
    conv = cw_ref[0:1, :] * old0 + cw_ref[1:2, :] * old1 + cw_ref[2:3, :] * cu
    mix_ref[:, MLSTM_W:D_MODEL] = (zb_ref[...] * conv).astype(BF16)
    buf_ref[:, 0:CONV_CH] = old1
    buf_ref[:, CONV_CH:2 * CONV_CH] = cu

    g = g_ref[...]
    ig = g[:, 0:HEADS] + bi_ref[...]
    lf = _log_sigmoid(g[:, HEADS:2 * HEADS] + bf_ref[...])
    m_inter = lf + m0_ref[...]
    m_new = jnp.maximum(m_inter, ig)
    w_in = jnp.exp(ig - m_new)
    c_dec = jnp.exp(m_inter - m_new)
    e_neg_m = jnp.exp(-m_new)
    mnew_ref[...] = m_new

    vs, scores, qns = [], [], []
    for h in range(HEADS):
        hs = slice(h * HEAD_DIM, (h + 1) * HEAD_DIM)
        q = zq_ref[:, hs] * Q_SCALE
        k = zk_ref[:, hs]
        v = zv_ref[:, hs]
        n0 = n0_ref[:, hs]
        wi = w_in[:, h:h + 1]
        cd = c_dec[:, h:h + 1]
        scores.append(jnp.sum(q * k, axis=1, keepdims=True) * wi)
        qns.append(jnp.sum(q * n0, axis=1, keepdims=True))
        wv_scr[:, hs] = wi * v
        cd_scr[:, hs] = jnp.broadcast_to(cd, (tb, HEAD_DIM))
        nnew_ref[:, hs] = cd * n0 + wi * k
        vs.append(v)

    eye = (lax.broadcasted_iota(jnp.int32, (HEAD_DIM, HEAD_DIM), 0)
           == lax.broadcasted_iota(jnp.int32, (HEAD_DIM, HEAD_DIM), 1))

    for i in range(tb):
        row = slice(i, i + 1)
        for h in range(HEADS):
            hs = slice(h * HEAD_DIM, (h + 1) * HEAD_DIM)
            c0 = c0_ref[i, h]
            q_rows = jnp.broadcast_to(zq_ref[row, hs] * Q_SCALE, (SUBLANES, HEAD_DIM)).astype(BF16)
            inter_scr[row, hs] = _dot(q_rows, c0.astype(BF16))[0:1, :]
            k_diag = jnp.where(eye, jnp.broadcast_to(zk_ref[row, hs], (HEAD_DIM, HEAD_DIM)), 0.0).astype(BF16)
            v_rows = jnp.broadcast_to(wv_scr[row, hs], (HEAD_DIM, HEAD_DIM)).astype(BF16)
            cd = jnp.broadcast_to(cd_scr[row, hs], (HEAD_DIM, HEAD_DIM))
            cnew_ref[i, h] = cd * c0 + _dot(k_diag, v_rows)

    for h in range(HEADS):
        hs = slice(h * HEAD_DIM, (h + 1) * HEAD_DIM)
        cd = c_dec[:, h:h + 1]
        num = scores[h] * vs[h] + cd * inter_scr[:, hs]
        den = scores[h] + cd * qns[h]
        hh = num / jnp.maximum(jnp.abs(den), e_neg_m[:, h:h + 1])
        hn = hh * lax.rsqrt(jnp.mean(hh * hh, axis=-1, keepdims=True) + EPS)
        mix_ref[:, hs] = (hn * mh_ref[:, hs] * jax.nn.sigmoid(zo_ref[:, hs])).astype(BF16)


def _sample_calls(xs, c0, n0, m0, sconv, wts):
    nb = xs.shape[0]
    tb = SAMPLE_TILE
    assert nb % tb == 0
    zh, zc, g = pl.pallas_call(
        _sample_inproj_kernel,
        out_shape=(jax.ShapeDtypeStruct((nb, 4 * MLSTM_W), F32),
                   jax.ShapeDtypeStruct((nb, 3 * CONV_CH), F32),
                   jax.ShapeDtypeStruct((nb, LANES), F32)),
        compiler_params=pltpu.CompilerParams(vmem_limit_bytes=VMEM_LIMIT_BYTES),
        name="sample_inproj",
    )(xs, wts["w_heads"], wts["w_conv"], wts["w_gc"], wts["norm_mix"])

    def zgroup(grp):
        return pl.BlockSpec((tb, MLSTM_W), lambda i, grp=grp: (i, grp))

    def zconv(grp):
        return pl.BlockSpec((tb, CONV_CH), lambda i, grp=grp: (i, grp))

    row_w = pl.BlockSpec((tb, MLSTM_W), lambda i: (i, 0))
    row_d = pl.BlockSpec((tb, D_MODEL), lambda i: (i, 0))
    row_h = pl.BlockSpec((tb, HEADS), lambda i: (i, 0))
    state = pl.BlockSpec((tb, HEADS, HEAD_DIM, HEAD_DIM), lambda i: (i, 0, 0, 0))

    def whole(shape):
        return pl.BlockSpec(shape, lambda i: (0,) * len(shape))

    mix, c_new, n_new, m_new, new_buf = pl.pallas_call(
        _sample_state_kernel,
        grid=(nb // tb,),
        in_specs=[zgroup(0), zgroup(1), zgroup(2), zgroup(3), zconv(0), zconv(1), zconv(2),
                  pl.BlockSpec((tb, LANES), lambda i: (i, 0)),
                  state, row_w, row_h, row_d,
                  whole((1, HEADS)), whole((1, HEADS)), whole((1, MLSTM_W)), whole((CONV_WIDTH, CONV_CH))],
        out_specs=(row_d, state, row_w, row_h, row_d),
        out_shape=(jax.ShapeDtypeStruct((nb, D_MODEL), BF16),
                   jax.ShapeDtypeStruct((nb, HEADS, HEAD_DIM, HEAD_DIM), F32),
                   jax.ShapeDtypeStruct((nb, MLSTM_W), F32),
                   jax.ShapeDtypeStruct((nb, HEADS), F32),
                   jax.ShapeDtypeStruct((nb, (CONV_WIDTH - 1) * CONV_CH), F32)),
        scratch_shapes=[pltpu.VMEM((tb, MLSTM_W), F32)] * 3,
        compiler_params=pltpu.CompilerParams(dimension_semantics=("arbitrary",)),
        name="sample_state",
    )(zh, zh, zh, zh, zc, zc, zc, g, c0, n0, m0, sconv, wts["b_i"], wts["b_f"], wts["mh_norm"], wts["conv_w"])
    return mix, c_new, n_new, m_new, new_buf


def _weight_prep_kernel(win_ref, wout_ref, wup_ref, wdown_ref, wpg_ref, wpp_ref,
                        heads_ref, conv_ref, gt_ref, gc_ref, out_ref, up_ref, down_ref, pg_ref, pp_ref):
    g0 = 4 * MLSTM_W
    heads_ref[...] = win_ref[:, 0:g0]
    rest = win_ref[:, g0:]
    conv_ref[...] = rest[:, 2 * HEADS:]
    lane = lax.broadcasted_iota(jnp.int32, (rest.shape[0], LANES), 1)
    gates = jnp.where(lane < 2 * HEADS, rest[:, 0:LANES].astype(F32), 0.0)
    gc_ref[...] = gates.astype(BF16)
    gt_ref[...] = gates.T[0:GATE_ROWS, :].astype(BF16)
    out_ref[...] = wout_ref[...].astype(BF16)
    up_ref[...] = wup_ref[...].astype(BF16)
    down_ref[...] = wdown_ref[...].astype(BF16)
    pg_ref[...] = wpg_ref[...].astype(BF16)
    pp_ref[...] = wpp_ref[...].astype(BF16)


def _prepare_weights(norm_mix, w_in, b_gate_i, b_gate_f, mh_norm, conv_w, w_out, norm_mlp, w_up, w_down,
                     norm_ple, w_ple_gate, w_ple_proj, norm_final):
    srcs = (w_in.astype(BF16), w_out, w_up, w_down, w_ple_gate, w_ple_proj)
    n_in = w_in.shape[1]
    outs = ((D_MODEL, 4 * MLSTM_W), (D_MODEL, 3 * CONV_CH), (GATE_ROWS, D_MODEL), (D_MODEL, LANES),
            w_out.shape, w_up.shape, w_down.shape, w_ple_gate.shape, w_ple_proj.shape)
    steps = PREP_STEPS

    def rows(shape):
        return pl.BlockSpec((shape[0] // steps, shape[1]), lambda i: (i, 0))

    out_specs = [rows(o) for o in outs]
    out_specs[2] = pl.BlockSpec((GATE_ROWS, D_MODEL // steps), lambda i: (0, i))
    assert n_in == 4 * MLSTM_W + 2 * HEADS + 3 * CONV_CH
    w_heads, w_conv, w_gt, w_gc, w_out_b, w_up_b, w_down_b, w_pg_b, w_pp_b = pl.pallas_call(
        _weight_prep_kernel,
        grid=(steps,),
        in_specs=[rows(a.shape) for a in srcs],
        out_specs=out_specs,
        out_shape=[jax.ShapeDtypeStruct(o, BF16) for o in outs],
        compiler_params=pltpu.CompilerParams(dimension_semantics=("arbitrary",),
                                             vmem_limit_bytes=VMEM_LIMIT_BYTES),
        name="weight_prep",
    )(*srcs)
    pad = jnp.zeros((SUBLANES - HEADS, 1), F32)
    return dict(
        w_heads=w_heads,
        w_conv=w_conv,
        w_gt=w_gt,
        w_gc=w_gc,
        bi8=jnp.concatenate([b_gate_i.reshape(HEADS, 1), pad], axis=0),
        bf8=jnp.concatenate([b_gate_f.reshape(HEADS, 1), pad], axis=0),
        b_i=b_gate_i.reshape(1, HEADS),
        b_f=b_gate_f.reshape(1, HEADS),
        norm_mix=norm_mix.reshape(1, D_MODEL),
        mh_norm=mh_norm.reshape(1, MLSTM_W),
        conv_w=conv_w,
        w_out=w_out_b,
        norm_mlp=norm_mlp.reshape(1, D_MODEL),
        w_up=w_up_b,
        w_down=w_down_b,
        norm_ple=norm_ple.reshape(1, D_MODEL),
        w_pg=w_pg_b,
        w_pp=w_pp_b,
        norm_final=norm_final.reshape(1, D_MODEL),
    )


def kernel(x_prompt, x_sample, state_mlstm_C, state_mlstm_n, state_mlstm_m, state_conv, p_prompt, p_sample,
           norm_mix, w_in, b_gate_i, b_gate_f, mh_norm, conv_w, w_out, norm_mlp, w_up, w_down, norm_ple,
           w_ple_gate, w_ple_proj, norm_final):
    assert norm_mix.shape[0] == 1, "single-layer trunk"
    wts = _prepare_weights(norm_mix[0], w_in[0], b_gate_i[0], b_gate_f[0], mh_norm[0], conv_w[0], w_out[0],
                           norm_mlp[0], w_up[0], w_down[0], norm_ple[0], w_ple_gate[0], w_ple_proj[0], norm_final)

    nb = x_sample.shape[0]
    xs = x_sample.reshape(nb, D_MODEL)
    mix_s, c_new, n_new, m_new, new_buf = _sample_calls(
        xs, state_mlstm_C[0], state_mlstm_n[0].reshape(nb, MLSTM_W), state_mlstm_m[0],
        state_conv[0].reshape(nb, (CONV_WIDTH - 1) * CONV_CH), wts)

    y_prompt, ys, prompt_c, prompt_n, m_rows, conv_tail = _prompt_call(
        x_prompt, p_prompt[0], xs, p_sample[0].reshape(nb, PLE_DIM), mix_s, wts)
    prompt_m = m_rows[:, 0:HEADS, 0]

    return (y_prompt, ys.reshape(nb, 1, D_MODEL),
            prompt_c[None], prompt_n[None], prompt_m[None], conv_tail[None],
            c_new[None], n_new.reshape(1, nb, HEADS, HEAD_DIM), m_new[None],
            new_buf.reshape(1, nb, CONV_WIDTH - 1, CONV_CH))
```

```python
import functools
import math

import jax
import jax.numpy as jnp
from jax import lax
from jax.experimental import pallas as pl
from jax.experimental.pallas import tpu as pltpu

F32 = jnp.float32
BF16 = jnp.bfloat16

D_MODEL = 1024
HEADS = 4
HEAD_DIM = 128
MLSTM_W = HEADS * HEAD_DIM
CONV_CH = D_MODEL - MLSTM_W
CONV_WIDTH = 3
D_FF = 4 * D_MODEL
PLE_DIM = 256
EPS = 1e-6
M_INIT = -1e30
Q_SCALE = HEAD_DIM ** -0.5
LOG2E = math.log2(math.e)

GATE_ROWS = 16
PREP_STEPS = 8
LANES = 128
SUBLANES = 8
VMEM_LIMIT_BYTES = 60000 * 1024

SEQ_TILE = 512
CHUNK = 256
FF_CHUNK = 512
SAMPLE_TILE = 16


def _dot(a, b):
    return jnp.dot(a, b, preferred_element_type=F32)


def _dot_nt(a, b):
    return lax.dot_general(a, b, (((1,), (1,)), ((), ())), preferred_element_type=F32)


def _dot_tn(a, b):
    return lax.dot_general(a, b, (((0,), (0,)), ((), ())), preferred_element_type=F32)


def _rms(x, g):
    y = x * lax.rsqrt(jnp.mean(x * x, axis=-1, keepdims=True) + EPS)
    return y * g


def _log_sigmoid(x):
    return jnp.minimum(x, 0.0) - jnp.log1p(jnp.exp(-jnp.abs(x)))


def _mlp_chunk(xn_bf16, c, wup_ref, wdown_ref):
    cols = slice(c * FF_CHUNK, (c + 1) * FF_CHUNK)
    hf = jnp.maximum(_dot(xn_bf16, wup_ref[:, cols]), 0.0)
    return _dot((hf * hf).astype(BF16), wdown_ref[cols, :])


def _cummax_lanes(a):
    n = a.shape[1]
    lane = lax.broadcasted_iota(jnp.int32, a.shape, 1)
    d = 1
    while d < n:
        shifted = pltpu.roll(a, d, axis=1)
        a = jnp.maximum(a, jnp.where(lane >= d, shifted, -jnp.inf))
        d *= 2
    return a


def _prompt_kernel(tiles_per_seq, num_tiles,
                   x_ref, plag_ref, xs_ref, ps_ref, mixs_ref,
                   wheads_ref, wconv_ref, wgt_ref, bi_ref, bf_ref, nmix_ref, mh_ref, cw_ref,
                   wout_ref, nmlp_ref, wup_ref, wdown_ref, nple_ref, wpg_ref, wpp_ref, nfin_ref, tri_ref,
                   y_ref, ys_ref, cout_ref, nout_ref, mout_ref, convout_ref,
                   caug, m_scr, cu_buf, mix_scr, xres_scr):
    t = SEQ_TILE
    lc = CHUNK
    ns = xs_ref.shape[0]
    s_id = pl.program_id(0)
    is_real = s_id < num_tiles
    j = lax.rem(jnp.minimum(s_id, num_tiles - 1), tiles_per_seq)

    @pl.when(s_id == 0)
    def _():
        mix_scr[0:ns, :] = mixs_ref[...]
        mix_scr[ns:t, :] = jnp.zeros((t - ns, D_MODEL), BF16)
        xres_scr[0:ns, :] = xs_ref[...]
        xres_scr[ns:t, :] = jnp.zeros((t - ns, D_MODEL), F32)

    @pl.when(j == 0)
    def _():
        caug[...] = jnp.zeros_like(caug)
        m_scr[...] = jnp.full(m_scr.shape, M_INIT, F32)
        cu_buf[0:SUBLANES, :] = jnp.zeros((SUBLANES, CONV_CH), F32)

    y_ref[0] = xres_scr[...] + _dot(mix_scr[...], wout_ref[...])
    xn = _rms(y_ref[0], nmlp_ref[...]).astype(BF16)

    x = x_ref[0]
    hb = _rms(x, nmix_ref[...]).astype(BF16)
    xres_scr[...] = x
    gt = _dot_nt(wgt_ref[...], hb)

    row_id = lax.broadcasted_iota(jnp.int32, (lc, lc), 0)
    col_id = lax.broadcasted_iota(jnp.int32, (lc, lc), 1)
    causal = col_id <= row_id
    lane_id = lax.broadcasted_iota(jnp.int32, (lc, HEAD_DIM), 1)

    acc = jnp.zeros((t, D_MODEL), F32)
    n_phases = (t // lc) * HEADS
    mlp_per_phase = D_FF // FF_CHUNK // n_phases
    mlp_first = (mlp_per_phase + 1) // 2
    next_mlp = 0
    m_carry = m_scr[...]
    for c in range(t // lc):
        tok = slice(c * lc, (c + 1) * lc)
        hb_c = hb[tok, :]
        g8 = gt[0:SUBLANES, tok]
        ig = g8 + bi_ref[...]
        lf = _log_sigmoid(pltpu.roll(g8, HEADS, axis=0) + bf_ref[...])
        hi = lf.astype(BF16)
        r1 = lf - hi.astype(F32)
        mid = r1.astype(BF16)
        lo = (r1 - mid.astype(F32)).astype(BF16)
        parts = _dot(jnp.concatenate([hi, mid, lo, jnp.zeros_like(lo)], axis=0), tri_ref[...])
        b = parts[0:8] + parts[8:16] + parts[16:24]
        a = ig - b
        m_prev = jnp.concatenate([m_carry] * (lc // LANES), axis=1)
        g = jnp.maximum(_cummax_lanes(a), m_prev)
        m_t = b + g
        b_last = b[:, lc - 1:lc]
        m_new = m_t[:, lc - 1:lc]
        decay = jnp.exp(m_prev - g)
        e_neg_m = jnp.exp(-m_t)
        w_state = jnp.exp(a + (b_last - m_new))
        c_dec = jnp.exp(b_last + m_prev[:, 0:1] - m_new)
        m_carry = jnp.broadcast_to(m_new, m_carry.shape)
        a2 = a * LOG2E
        rows = jnp.concatenate(
            [g * (-LOG2E), decay, e_neg_m, w_state, jnp.zeros((LANES - 4 * SUBLANES, lc), F32)], axis=0)
        cols = rows.T

        for h in range(HEADS):
            if h % 2 == 0:
                pair = slice(h * HEAD_DIM, (h + 2) * HEAD_DIM)
                zq2, zk2, zv2, zo2 = (
                    _dot(hb_c, wheads_ref[:, grp * MLSTM_W + pair.start:grp * MLSTM_W + pair.stop])
                    for grp in range(4))
            half = slice((h % 2) * HEAD_DIM, (h % 2 + 1) * HEAD_DIM)
            qs = (zq2[:, half] * Q_SCALE).astype(BF16)
            kb = zk2[:, half].astype(BF16)
            v = zv2[:, half]
            c_col = cols[:, h:h + 1]
            dec = cols[:, SUBLANES + h:SUBLANES + h + 1]
            enm = cols[:, 2 * SUBLANES + h:2 * SUBLANES + h + 1]
            wst = cols[:, 3 * SUBLANES + h:3 * SUBLANES + h + 1]
            s = _dot_nt(qs, kb)

            for _ in range(mlp_first):
                acc = acc + _mlp_chunk(xn, next_mlp, wup_ref, wdown_ref)
                next_mlp += 1

            dmat = jnp.exp2(jnp.where(causal, c_col + a2[h:h + 1, :], -jnp.inf))
            pm = s * dmat
            row_sum = jnp.sum(pm, axis=1, keepdims=True)
            intra = _dot(pm.astype(BF16), v.astype(BF16))
            c_state = caug[h]
            inter = _dot(qs, c_state.astype(BF16))
            vw = jnp.concatenate([v * wst, jnp.where(lane_id == 0, wst, 0.0)], axis=1).astype(BF16)
            caug[h] = c_dec[h:h + 1, :] * c_state + _dot_tn(kb, vw)

            for _ in range(mlp_per_phase - mlp_first):
                acc = acc + _mlp_chunk(xn, next_mlp, wup_ref, wdown_ref)
                next_mlp += 1

            num = intra + dec * inter[:, 0:HEAD_DIM]
            den = row_sum + dec * inter[:, HEAD_DIM:HEAD_DIM + 1]
            hh = num / jnp.maximum(jnp.abs(den), enm)
            hn = hh * lax.rsqrt(jnp.mean(hh * hh, axis=-1, keepdims=True) + EPS)
            hn = hn * mh_ref[:, h * HEAD_DIM:(h + 1) * HEAD_DIM]
            mix_scr[tok, h * HEAD_DIM:(h + 1) * HEAD_DIM] = (
                hn * jax.nn.sigmoid(zo2[:, half])).astype(BF16)
    m_scr[...] = m_carry
    assert next_mlp == D_FF // FF_CHUNK

    zc = _dot(hb, wconv_ref[...])
    p_sample = jnp.concatenate([ps_ref[...], jnp.zeros((t - ns, PLE_DIM), F32)], axis=0)
    p_tail = jnp.where(s_id == 0, p_sample, plag_ref[0])
    ple = _dot(p_tail.astype(BF16), wpp_ref[...])

    def tail_rows(rows):
        x2 = y_ref[0, rows, :] + acc[rows, :]
        gate = jax.nn.sigmoid(_dot(_rms(x2, nple_ref[...]).astype(BF16), wpg_ref[...]))
        y_ref[0, rows, :] = _rms(x2 + gate * ple[rows, :], nfin_ref[...])

    tail_rows(slice(0, t // 2))

    cu = zc[:, CONV_CH:2 * CONV_CH] * zc[:, 2 * CONV_CH:3 * CONV_CH]
    cu_buf[SUBLANES:SUBLANES + t, :] = cu
    conv = (cw_ref[0:1, :] * cu_buf[SUBLANES - 2:SUBLANES - 2 + t, :]
            + cw_ref[1:2, :] * cu_buf[SUBLANES - 1:SUBLANES - 1 + t, :]
            + cw_ref[2:3, :] * cu)
    mix_scr[:, MLSTM_W:D_MODEL] = (zc[:, 0:CONV_CH] * conv).astype(BF16)
    cu_buf[0:SUBLANES, :] = cu_buf[t:t + SUBLANES, :]

    tail_rows(slice(t // 2, t))

    @pl.when(s_id == 0)
    def _():
        ys_ref[...] = y_ref[0, 0:ns, :]

    @pl.when(jnp.logical_and(is_real, j == tiles_per_seq - 1))
    def _():
        for h in range(HEADS):
            cout_ref[0, h] = caug[h, :, 0:HEAD_DIM]
            nout_ref[0, h:h + 1, :] = caug[h, :, HEAD_DIM:2 * HEAD_DIM].T[0:1, :]
        mout_ref[0] = m_scr[...]
        convout_ref[0] = cu[t - (CONV_WIDTH - 1):t, :]


def _resident(shape):
    return pl.BlockSpec(shape, lambda *_: (0,) * len(shape), pipeline_mode=pl.Buffered(1))


def _prompt_call(x, p, xs, ps, mix_s, wts):
    bsz, seq, _ = x.shape
    t = SEQ_TILE
    assert seq % t == 0 and t % CHUNK == 0 and CHUNK % LANES == 0 and xs.shape[0] <= t
    tri = (jnp.arange(CHUNK)[:, None] <= jnp.arange(CHUNK)[None, :]).astype(BF16)
    consts = [wts["w_heads"], wts["w_conv"], wts["w_gt"], wts["bi8"], wts["bf8"], wts["norm_mix"], wts["mh_norm"], wts["conv_w"],
              wts["w_out"], wts["norm_mlp"], wts["w_up"], wts["w_down"], wts["norm_ple"], wts["w_pg"],
              wts["w_pp"], wts["norm_final"], tri]
    nj = seq // t
    nt = bsz * nj

    def cur(s):
        c = jnp.minimum(s, nt - 1)
        return c // nj, c % nj

    def lag(s):
        c = jnp.maximum(s - 1, 0)
        return c // nj, c % nj

    in_specs = [pl.BlockSpec((1, t, D_MODEL), lambda s: (*cur(s), 0)),
                pl.BlockSpec((1, t, PLE_DIM), lambda s: (*lag(s), 0))]
    in_specs += [_resident(c.shape) for c in (xs, ps, mix_s, *consts)]
    out_shape = (jax.ShapeDtypeStruct((bsz, seq, D_MODEL), F32),
                 jax.ShapeDtypeStruct(xs.shape, F32),
                 jax.ShapeDtypeStruct((bsz, HEADS, HEAD_DIM, HEAD_DIM), F32),
                 jax.ShapeDtypeStruct((bsz, HEADS, HEAD_DIM), F32),
                 jax.ShapeDtypeStruct((bsz, SUBLANES, LANES), F32),
                 jax.ShapeDtypeStruct((bsz, CONV_WIDTH - 1, CONV_CH), F32))
    out_specs = (pl.BlockSpec((1, t, D_MODEL), lambda s: (*lag(s), 0)),
                 pl.BlockSpec(xs.shape, lambda s: (0, 0)),
                 pl.BlockSpec((1, HEADS, HEAD_DIM, HEAD_DIM), lambda s: (cur(s)[0], 0, 0, 0)),
                 pl.BlockSpec((1, HEADS, HEAD_DIM), lambda s: (cur(s)[0], 0, 0)),
                 pl.BlockSpec((1, SUBLANES, LANES), lambda s: (cur(s)[0], 0, 0)),
                 pl.BlockSpec((1, CONV_WIDTH - 1, CONV_CH), lambda s: (cur(s)[0], 0, 0)))
    scratch = [pltpu.VMEM((HEADS, HEAD_DIM, 2 * HEAD_DIM), F32),
               pltpu.VMEM((SUBLANES, LANES), F32),
               pltpu.VMEM((t + SUBLANES, CONV_CH), F32),
               pltpu.VMEM((t, D_MODEL), BF16),
               pltpu.VMEM((t, D_MODEL), F32)]
    return pl.pallas_call(
        functools.partial(_prompt_kernel, nj, nt),
        grid=(nt + 1,),
        in_specs=in_specs,
        out_specs=out_specs,
        out_shape=out_shape,
        scratch_shapes=scratch,
        compiler_params=pltpu.CompilerParams(
            dimension_semantics=("arbitrary",),
            vmem_limit_bytes=VMEM_LIMIT_BYTES),
        name="prompt_layer",
    )(x, p, xs, ps, mix_s, *consts)


def _sample_inproj_kernel(x_ref, wheads_ref, wconv_ref, wgc_ref, nmix_ref, zh_ref, zc_ref, g_ref):
    hb = _rms(x_ref[...], nmix_ref[...]).astype(BF16)
    zh_ref[...] = _dot(hb, wheads_ref[...])
    zc_ref[...] = _dot(hb, wconv_ref[...])
    g_ref[...] = _dot(hb, wgc_ref[...])


def _sample_state_kernel(zq_ref, zk_ref, zv_ref, zo_ref, zb_ref, zc_ref, zu_ref, g_ref, c0_ref, n0_ref, m0_ref,
                         sc_ref, bi_ref, bf_ref, mh_ref, cw_ref,
                         mix_ref, cnew_ref, nnew_ref, mnew_ref, buf_ref,
                         inter_scr, wv_scr, cd_scr):
    tb = SAMPLE_TILE
    cu = zc_ref[...] * zu_ref[...]
    old0 = sc_ref[:, 0:CONV_CH]
    old1 = sc_ref[:, CONV_CH:2 * CONV_CH]
    conv = cw_ref[0:1, :] * old0 + cw_ref[1:2, :] * old1 + cw_ref[2:3, :] * cu
    mix_ref[:, MLSTM_W:D_MODEL] = (zb_ref[...] * conv).astype(BF16)
    buf_ref[:, 0:CONV_CH] = old1
    buf_ref[:, CONV_CH:2 * CONV_CH] = cu

    g = g_ref[...]
    ig = g[:, 0:HEADS] + bi_ref[...]
    lf = _log_sigmoid(g[:, HEADS:2 * HEADS] + bf_ref[...])
    m_inter = lf + m0_ref[...]
    m_new = jnp.maximum(m_inter, ig)
    w_in = jnp.exp(ig - m_new)
    c_dec = jnp.exp(m_inter - m_new)
    e_neg_m = jnp.exp(-m_new)
    mnew_ref[...] = m_new

    vs, scores, qns = [], [], []
    for h in range(HEADS):
        hs = slice(h * HEAD_DIM, (h + 1) * HEAD_DIM)
        q = zq_ref[:, hs] * Q_SCALE
        k = zk_ref[:, hs]
        v = zv_ref[:, hs]
        n0 = n0_ref[:, hs]
        wi = w_in[:, h:h + 1]
        cd = c_dec[:, h:h + 1]
        scores.append(jnp.sum(q * k, axis=1, keepdims=True) * wi)
        qns.append(jnp.sum(q * n0, axis=1, keepdims=True))
        wv_scr[:, hs] = wi * v
        cd_scr[:, hs] = jnp.broadcast_to(cd, (tb, HEAD_DIM))
        nnew_ref[:, hs] = cd * n0 + wi * k
        vs.append(v)

    eye = (lax.broadcasted_iota(jnp.int32, (HEAD_DIM, HEAD_DIM), 0)
           == lax.broadcasted_iota(jnp.int32, (HEAD_DIM, HEAD_DIM), 1))

    for i in range(tb):
        row = slice(i, i + 1)
        for h in range(HEADS):
            hs = slice(h * HEAD_DIM, (h + 1) * HEAD_DIM)
            c0 = c0_ref[i, h]
            q_rows = jnp.broadcast_to(zq_ref[row, hs] * Q_SCALE, (SUBLANES, HEAD_DIM)).astype(BF16)
            inter_scr[row, hs] = _dot(q_rows, c0.astype(BF16))[0:1, :]
            k_diag = jnp.where(eye, jnp.broadcast_to(zk_ref[row, hs], (HEAD_DIM, HEAD_DIM)), 0.0).astype(BF16)
            v_rows = jnp.broadcast_to(wv_scr[row, hs], (HEAD_DIM, HEAD_DIM)).astype(BF16)
            cd = jnp.broadcast_to(cd_scr[row, hs], (HEAD_DIM, HEAD_DIM))
            cnew_ref[i, h] = cd * c0 + _dot(k_diag, v_rows)

    for h in range(HEADS):
        hs = slice(h * HEAD_DIM, (h + 1) * HEAD_DIM)
        cd = c_dec[:, h:h + 1]
        num = scores[h] * vs[h] + cd * inter_scr[:, hs]
        den = scores[h] + cd * qns[h]
        hh = num / jnp.maximum(jnp.abs(den), e_neg_m[:, h:h + 1])
        hn = hh * lax.rsqrt(jnp.mean(hh * hh, axis=-1, keepdims=True) + EPS)
        mix_ref[:, hs] = (hn * mh_ref[:, hs] * jax.nn.sigmoid(zo_ref[:, hs])).astype(BF16)


def _sample_calls(xs, c0, n0, m0, sconv, wts):
    nb = xs.shape[0]
    tb = SAMPLE_TILE
    assert nb % tb == 0
    zh, zc, g = pl.pallas_call(
        _sample_inproj_kernel,
        out_shape=(jax.ShapeDtypeStruct((nb, 4 * MLSTM_W), F32),
                   jax.ShapeDtypeStruct((nb, 3 * CONV_CH), F32),
                   jax.ShapeDtypeStruct((nb, LANES), F32)),
        compiler_params=pltpu.CompilerParams(vmem_limit_bytes=VMEM_LIMIT_BYTES),
        name="sample_inproj",
    )(xs, wts["w_heads"], wts["w_conv"], wts["w_gc"], wts["norm_mix"])

    def zgroup(grp):
        return pl.BlockSpec((tb, MLSTM_W), lambda i, grp=grp: (i, grp))

    def zconv(grp):
        return pl.BlockSpec((tb, CONV_CH), lambda i, grp=grp: (i, grp))

    row_w = pl.BlockSpec((tb, MLSTM_W), lambda i: (i, 0))
    row_d = pl.BlockSpec((tb, D_MODEL), lambda i: (i, 0))
    row_h = pl.BlockSpec((tb, HEADS), lambda i: (i, 0))
    state = pl.BlockSpec((tb, HEADS, HEAD_DIM, HEAD_DIM), lambda i: (i, 0, 0, 0))

    def whole(shape):
        return pl.BlockSpec(shape, lambda i: (0,) * len(shape))

    mix, c_new, n_new, m_new, new_buf = pl.pallas_call(
        _sample_state_kernel,
        grid=(nb // tb,),
        in_specs=[zgroup(0), zgroup(1), zgroup(2), zgroup(3), zconv(0), zconv(1), zconv(2),
                  pl.BlockSpec((tb, LANES), lambda i: (i, 0)),
                  state, row_w, row_h, row_d,
                  whole((1, HEADS)), whole((1, HEADS)), whole((1, MLSTM_W)), whole((CONV_WIDTH, CONV_CH))],
        out_specs=(row_d, state, row_w, row_h, row_d),
        out_shape=(jax.ShapeDtypeStruct((nb, D_MODEL), BF16),
                   jax.ShapeDtypeStruct((nb, HEADS, HEAD_DIM, HEAD_DIM), F32),
                   jax.ShapeDtypeStruct((nb, MLSTM_W), F32),
                   jax.ShapeDtypeStruct((nb, HEADS), F32),
                   jax.ShapeDtypeStruct((nb, (CONV_WIDTH - 1) * CONV_CH), F32)),
        scratch_shapes=[pltpu.VMEM((tb, MLSTM_W), F32)] * 3,
        compiler_params=pltpu.CompilerParams(dimension_semantics=("arbitrary",)),
        name="sample_state",
    )(zh, zh, zh, zh, zc, zc, zc, g, c0, n0, m0, sconv, wts["b_i"], wts["b_f"], wts["mh_norm"], wts["conv_w"])
    return mix, c_new, n_new, m_new, new_buf


def _weight_prep_kernel(win_ref, wout_ref, wup_ref, wdown_ref, wpg_ref, wpp_ref,
                        heads_ref, conv_ref, gt_ref, gc_ref, out_ref, up_ref, down_ref, pg_ref, pp_ref):
    g0 = 4 * MLSTM_W
    heads_ref[...] = win_ref[:, 0:g0]
    rest = win_ref[:, g0:]
    conv_ref[...] = rest[:, 2 * HEADS:]
    lane = lax.broadcasted_iota(jnp.int32, (rest.shape[0], LANES), 1)
    gates = jnp.where(lane < 2 * HEADS, rest[:, 0:LANES].astype(F32), 0.0)
    gc_ref[...] = gates.astype(BF16)
    gt_ref[...] = gates.T[0:GATE_ROWS, :].astype(BF16)
    out_ref[...] = wout_ref[...].astype(BF16)
    up_ref[...] = wup_ref[...].astype(BF16)
    down_ref[...] = wdown_ref[...].astype(BF16)
    pg_ref[...] = wpg_ref[...].astype(BF16)
    pp_ref[...] = wpp_ref[...].astype(BF16)


def _prepare_weights(norm_mix, w_in, b_gate_i, b_gate_f, mh_norm, conv_w, w_out, norm_mlp, w_up, w_down,
                     norm_ple, w_ple_gate, w_ple_proj, norm_final):
    srcs = (w_in.astype(BF16), w_out, w_up, w_down, w_ple_gate, w_ple_proj)
    n_in = w_in.shape[1]
    outs = ((D_MODEL, 4 * MLSTM_W), (D_MODEL, 3 * CONV_CH), (GATE_ROWS, D_MODEL), (D_MODEL, LANES),
            w_out.shape, w_up.shape, w_down.shape, w_ple_gate.shape, w_ple_proj.shape)
    steps = PREP_STEPS

    def rows(shape):
        return pl.BlockSpec((shape[0] // steps, shape[1]), lambda i: (i, 0))

    out_specs = [rows(o) for o in outs]
    out_specs[2] = pl.BlockSpec((GATE_ROWS, D_MODEL // steps), lambda i: (0, i))
    assert n_in == 4 * MLSTM_W + 2 * HEADS + 3 * CONV_CH
    w_heads, w_conv, w_gt, w_gc, w_out_b, w_up_b, w_down_b, w_pg_b, w_pp_b = pl.pallas_call(
        _weight_prep_kernel,
        grid=(steps,),
        in_specs=[rows(a.shape) for a in srcs],
        out_specs=out_specs,
        out_shape=[jax.ShapeDtypeStruct(o, BF16) for o in outs],
        compiler_params=pltpu.CompilerParams(dimension_semantics=("arbitrary",),
                                             vmem_limit_bytes=VMEM_LIMIT_BYTES),
        name="weight_prep",
    )(*srcs)
    pad = jnp.zeros((SUBLANES - HEADS, 1), F32)
    return dict(
        w_heads=w_heads,
        w_conv=w_conv,
        w_gt=w_gt,
        w_gc=w_gc,
        bi8=jnp.concatenate([b_gate_i.reshape(HEADS, 1), pad], axis=0),
        bf8=jnp.concatenate([b_gate_f.reshape(HEADS, 1), pad], axis=0),
        b_i=b_gate_i.reshape(1, HEADS),
        b_f=b_gate_f.reshape(1, HEADS),
        norm_mix=norm_mix.reshape(1, D_MODEL),
        mh_norm=mh_norm.reshape(1, MLSTM_W),
        conv_w=conv_w,
        w_out=w_out_b,
        norm_mlp=norm_mlp.reshape(1, D_MODEL),
        w_up=w_up_b,
        w_down=w_down_b,
        norm_ple=norm_ple.reshape(1, D_MODEL),
        w_pg=w_pg_b,
        w_pp=w_pp_b,
        norm_final=norm_final.reshape(1, D_MODEL),
    )


def kernel(x_prompt, x_sample, state_mlstm_C, state_mlstm_n, state_mlstm_m, state_conv, p_prompt, p_sample,
           norm_mix, w_in, b_gate_i, b_gate_f, mh_norm, conv_w, w_out, norm_mlp, w_up, w_down, norm_ple,
           w_ple_gate, w_ple_proj, norm_final):
    assert norm_mix.shape[0] == 1, "single-layer trunk"
    wts = _prepare_weights(norm_mix[0], w_in[0], b_gate_i[0], b_gate_f[0], mh_norm[0], conv_w[0], w_out[0],
                           norm_mlp[0], w_up[0], w_down[0], norm_ple[0], w_ple_gate[0], w_ple_proj[0], norm_final)

    nb = x_sample.shape[0]
    xs = x_sample.reshape(nb, D_MODEL)
    mix_s, c_new, n_new, m_new, new_buf = _sample_calls(
        xs, state_mlstm_C[0], state_mlstm_n[0].reshape(nb, MLSTM_W), state_mlstm_m[0],
        state_conv[0].reshape(nb, (CONV_WIDTH - 1) * CONV_CH), wts)

    y_prompt, ys, prompt_c, prompt_n, m_rows, conv_tail = _prompt_call(
        x_prompt, p_prompt[0], xs, p_sample[0].reshape(nb, PLE_DIM), mix_s, wts)
    prompt_m = m_rows[:, 0:HEADS, 0]

    return (y_prompt, ys.reshape(nb, 1, D_MODEL),
            prompt_c[None], prompt_n[None], prompt_m[None], conv_tail[None],
            c_new[None], n_new.reshape(1, nb, HEADS, HEAD_DIM), m_new[None],
            new_buf.reshape(1, nb, CONV_WIDTH - 1, CONV_CH))
```

```python
import functools
import math

import jax
import jax.numpy as jnp
from jax import lax
from jax.experimental import pallas as pl
from jax.experimental.pallas import tpu as pltpu

F32 = jnp.float32
BF16 = jnp.bfloat16

D_MODEL = 1024
HEADS = 4
HEAD_DIM = 128
MLSTM_W = HEADS * HEAD_DIM
CONV_CH = D_MODEL - MLSTM_W
CONV_WIDTH = 3
D_FF = 4 * D_MODEL
PLE_DIM = 256
EPS = 1e-6
M_INIT = -1e30
Q_SCALE = HEAD_DIM ** -0.5
LOG2E = math.log2(math.e)

GATE_ROWS = 16
PREP_STEPS = 8
LANES = 128
SUBLANES = 8
VMEM_LIMIT_BYTES = 60000 * 1024

SEQ_TILE = 512
CHUNK = 256
FF_CHUNK = 512
DOWN_COLS = 256
SAMPLE_TILE = 16


def _dot(a, b):
    return jnp.dot(a, b, preferred_element_type=F32)


def _dot_nt(a, b):
    return lax.dot_general(a, b, (((1,), (1,)), ((), ())), preferred_element_type=F32)


def _dot_tn(a, b):
    return lax.dot_general(a, b, (((0,), (0,)), ((), ())), preferred_element_type=F32)


def _rms(x, g):
    y = x * lax.rsqrt(jnp.mean(x * x, axis=-1, keepdims=True) + EPS)
    return y * g


def _log_sigmoid(x):
    return jnp.minimum(x, 0.0) - jnp.log1p(jnp.exp(-jnp.abs(x)))


def _mlp_up(xn_bf16, c, wup_ref, hf_scr):
    cols = slice(c * FF_CHUNK, (c + 1) * FF_CHUNK)
    hf = jnp.maximum(_dot(xn_bf16, wup_ref[:, cols]), 0.0)
    hf_scr[:, cols] = (hf * hf).astype(BF16)


def _mlp_down(n, hf_scr, wdown_ref, y_ref):
    cols = slice(n * DOWN_COLS, (n + 1) * DOWN_COLS)
    y_ref[0, :, cols] = y_ref[0, :, cols] + _dot(hf_scr[...], wdown_ref[:, cols])


def _cummax_lanes(a):
    n = a.shape[1]
    lane = lax.broadcasted_iota(jnp.int32, a.shape, 1)
    d = 1
    while d < n:
        shifted = pltpu.roll(a, d, axis=1)
        a = jnp.maximum(a, jnp.where(lane >= d, shifted, -jnp.inf))
        d *= 2
    return a


def _prompt_kernel(tiles_per_seq, num_tiles,
                   x_ref, plag_ref, xs_ref, ps_ref, mixs_ref,
                   wheads_ref, wconv_ref, wgt_ref, bi_ref, bf_ref, nmix_ref, mh_ref, cw_ref,
                   wout_ref, nmlp_ref, wup_ref, wdown_ref, nple_ref, wpg_ref, wpp_ref, nfin_ref, tri_ref,
                   y_ref, ys_ref, cout_ref, nout_ref, mout_ref, convout_ref,
                   caug, m_scr, cu_buf, mix_scr, xres_scr, hf_scr):
    t = SEQ_TILE
    lc = CHUNK
    ns = xs_ref.shape[0]
    s_id = pl.program_id(0)
    is_real = s_id < num_tiles
    j = lax.rem(jnp.minimum(s_id, num_tiles - 1), tiles_per_seq)

    @pl.when(s_id == 0)
    def _():
        mix_scr[0:ns, :] = mixs_ref[...]
        mix_scr[ns:t, :] = jnp.zeros((t - ns, D_MODEL), BF16)
        xres_scr[0:ns, :] = xs_ref[...]
        xres_scr[ns:t, :] = jnp.zeros((t - ns, D_MODEL), F32)

    @pl.when(j == 0)
    def _():
        caug[...] = jnp.zeros_like(caug)
        m_scr[...] = jnp.full(m_scr.shape, M_INIT, F32)
        cu_buf[0:SUBLANES, :] = jnp.zeros((SUBLANES, CONV_CH), F32)

    y_ref[0] = xres_scr[...] + _dot(mix_scr[...], wout_ref[...])
    xn = _rms(y_ref[0], nmlp_ref[...]).astype(BF16)

    x = x_ref[0]
    hb = _rms(x, nmix_ref[...]).astype(BF16)
    xres_scr[...] = x
    gt = _dot_nt(wgt_ref[...], hb)

    row_id = lax.broadcasted_iota(jnp.int32, (lc, lc), 0)
    col_id = lax.broadcasted_iota(jnp.int32, (lc, lc), 1)
    causal = col_id <= row_id
    lane_id = lax.broadcasted_iota(jnp.int32, (lc, HEAD_DIM), 1)

    mlp_pieces = ([functools.partial(_mlp_up, xn, c, wup_ref, hf_scr) for c in range(D_FF // FF_CHUNK)]
                  + [functools.partial(_mlp_down, n, hf_scr, wdown_ref, y_ref) for n in range(D_MODEL // DOWN_COLS)])
    n_phases = (t // lc) * HEADS
    assert 2 * (D_FF // FF_CHUNK) == D_MODEL // DOWN_COLS * 4 == 2 * n_phases
    phase_pieces = ([mlp_pieces[2 * i:2 * i + 2] for i in range(n_phases // 2)]
                    + [[p] for p in mlp_pieces[n_phases:]])
    m_carry = m_scr[...]
    for c in range(t // lc):
        tok = slice(c * lc, (c + 1) * lc)
        hb_c = hb[tok, :]
        g8 = gt[0:SUBLANES, tok]
        ig = g8 + bi_ref[...]
        lf = _log_sigmoid(pltpu.roll(g8, HEADS, axis=0) + bf_ref[...])
        hi = lf.astype(BF16)
        r1 = lf - hi.astype(F32)
        mid = r1.astype(BF16)
        lo = (r1 - mid.astype(F32)).astype(BF16)
        parts = _dot(jnp.concatenate([hi, mid, lo, jnp.zeros_like(lo)], axis=0), tri_ref[...])
        b = parts[0:8] + parts[8:16] + parts[16:24]
        a = ig - b
        m_prev = jnp.concatenate([m_carry] * (lc // LANES), axis=1)
        g = jnp.maximum(_cummax_lanes(a), m_prev)
        m_t = b + g
        b_last = b[:, lc - 1:lc]
        m_new = m_t[:, lc - 1:lc]
        decay = jnp.exp(m_prev - g)
        e_neg_m = jnp.exp(-m_t)
        w_state = jnp.exp(a + (b_last - m_new))
        c_dec = jnp.exp(b_last + m_prev[:, 0:1] - m_new)
        m_carry = jnp.broadcast_to(m_new, m_carry.shape)
        a2 = a * LOG2E
        rows = jnp.concatenate(
            [g * (-LOG2E), decay, e_neg_m, w_state, jnp.zeros((LANES - 4 * SUBLANES, lc), F32)], axis=0)
        cols = rows.T

        for h in range(HEADS):
            if h % 2 == 0:
                pair = slice(h * HEAD_DIM, (h + 2) * HEAD_DIM)
                zq2, zk2, zv2, zo2 = (
                    _dot(hb_c, wheads_ref[:, grp * MLSTM_W + pair.start:grp * MLSTM_W + pair.stop])
                    for grp in range(4))
            half = slice((h % 2) * HEAD_DIM, (h % 2 + 1) * HEAD_DIM)
            qs = (zq2[:, half] * Q_SCALE).astype(BF16)
            kb = zk2[:, half].astype(BF16)
            v = zv2[:, half]
            c_col = cols[:, h:h + 1]
            dec = cols[:, SUBLANES + h:SUBLANES + h + 1]
            enm = cols[:, 2 * SUBLANES + h:2 * SUBLANES + h + 1]
            wst = cols[:, 3 * SUBLANES + h:3 * SUBLANES + h + 1]
            s = _dot_nt(qs, kb)

            pieces = phase_pieces[c * HEADS + h]
            pieces[0]()

            dmat = jnp.exp2(jnp.where(causal, c_col + a2[h:h + 1, :], -jnp.inf))
            pm = s * dmat
            row_sum = jnp.sum(pm, axis=1, keepdims=True)
            intra = _dot(pm.astype(BF16), v.astype(BF16))
            c_state = caug[h]
            inter = _dot(qs, c_state.astype(BF16))
            vw = jnp.concatenate([v * wst, jnp.where(lane_id == 0, wst, 0.0)], axis=1).astype(BF16)
            caug[h] = c_dec[h:h + 1, :] * c_state + _dot_tn(kb, vw)

            for piece in pieces[1:]:
                piece()

            num = intra + dec * inter[:, 0:HEAD_DIM]
            den = row_sum + dec * inter[:, HEAD_DIM:HEAD_DIM + 1]
            hh = num / jnp.maximum(jnp.abs(den), enm)
            hn = hh * lax.rsqrt(jnp.mean(hh * hh, axis=-1, keepdims=True) + EPS)
            hn = hn * mh_ref[:, h * HEAD_DIM:(h + 1) * HEAD_DIM]
            mix_scr[tok, h * HEAD_DIM:(h + 1) * HEAD_DIM] = (
                hn * jax.nn.sigmoid(zo2[:, half])).astype(BF16)
    m_scr[...] = m_carry

    zc = _dot(hb, wconv_ref[...])
    p_sample = jnp.concatenate([ps_ref[...], jnp.zeros((t - ns, PLE_DIM), F32)], axis=0)
    p_tail = jnp.where(s_id == 0, p_sample, plag_ref[0])
    ple = _dot(p_tail.astype(BF16), wpp_ref[...])

    def tail_rows(rows):
        x2 = y_ref[0, rows, :]
        gate = jax.nn.sigmoid(_dot(_rms(x2, nple_ref[...]).astype(BF16), wpg_ref[...]))
        y_ref[0, rows, :] = _rms(x2 + gate * ple[rows, :], nfin_ref[...])

    tail_rows(slice(0, t // 2))

    cu = zc[:, CONV_CH:2 * CONV_CH] * zc[:, 2 * CONV_CH:3 * CONV_CH]
    cu_buf[SUBLANES:SUBLANES + t, :] = cu
    conv = (cw_ref[0:1, :] * cu_buf[SUBLANES - 2:SUBLANES - 2 + t, :]
            + cw_ref[1:2, :] * cu_buf[SUBLANES - 1:SUBLANES - 1 + t, :]
            + cw_ref[2:3, :] * cu)
    mix_scr[:, MLSTM_W:D_MODEL] = (zc[:, 0:CONV_CH] * conv).astype(BF16)
    cu_buf[0:SUBLANES, :] = cu_buf[t:t + SUBLANES, :]

    tail_rows(slice(t // 2, t))

    @pl.when(s_id == 0)
    def _():
        ys_ref[...] = y_ref[0, 0:ns, :]

    @pl.when(jnp.logical_and(is_real, j == tiles_per_seq - 1))
    def _():
        for h in range(HEADS):
            cout_ref[0, h] = caug[h, :, 0:HEAD_DIM]
            nout_ref[0, h:h + 1, :] = caug[h, :, HEAD_DIM:2 * HEAD_DIM].T[0:1, :]
        mout_ref[0] = m_scr[...]
        convout_ref[0] = cu[t - (CONV_WIDTH - 1):t, :]


def _resident(shape):
    return pl.BlockSpec(shape, lambda *_: (0,) * len(shape), pipeline_mode=pl.Buffered(1))


def _prompt_call(x, p, xs, ps, mix_s, wts):
    bsz, seq, _ = x.shape
    t = SEQ_TILE
    assert seq % t == 0 and t % CHUNK == 0 and CHUNK % LANES == 0 and xs.shape[0] <= t
    tri = (jnp.arange(CHUNK)[:, None] <= jnp.arange(CHUNK)[None, :]).astype(BF16)
    consts = [wts["w_heads"], wts["w_conv"], wts["w_gt"], wts["bi8"], wts["bf8"], wts["norm_mix"], wts["mh_norm"], wts["conv_w"],
              wts["w_out"], wts["norm_mlp"], wts["w_up"], wts["w_down"], wts["norm_ple"], wts["w_pg"],
              wts["w_pp"], wts["norm_final"], tri]
    nj = seq // t
    nt = bsz * nj

    def cur(s):
        c = jnp.minimum(s, nt - 1)
        return c // nj, c % nj

    def lag(s):
        c = jnp.maximum(s - 1, 0)
        return c // nj, c % nj

    in_specs = [pl.BlockSpec((1, t, D_MODEL), lambda s: (*cur(s), 0)),
                pl.BlockSpec((1, t, PLE_DIM), lambda s: (*lag(s), 0))]
    in_specs += [_resident(c.shape) for c in (xs, ps, mix_s, *consts)]
    out_shape = (jax.ShapeDtypeStruct((bsz, seq, D_MODEL), F32),
                 jax.ShapeDtypeStruct(xs.shape, F32),
                 jax.ShapeDtypeStruct((bsz, HEADS, HEAD_DIM, HEAD_DIM), F32),
                 jax.ShapeDtypeStruct((bsz, HEADS, HEAD_DIM), F32),
                 jax.ShapeDtypeStruct((bsz, SUBLANES, LANES), F32),
                 jax.ShapeDtypeStruct((bsz, CONV_WIDTH - 1, CONV_CH), F32))
    out_specs = (pl.BlockSpec((1, t, D_MODEL), lambda s: (*lag(s), 0)),
                 pl.BlockSpec(xs.shape, lambda s: (0, 0)),
                 pl.BlockSpec((1, HEADS, HEAD_DIM, HEAD_DIM), lambda s: (cur(s)[0], 0, 0, 0)),
                 pl.BlockSpec((1, HEADS, HEAD_DIM), lambda s: (cur(s)[0], 0, 0)),
                 pl.BlockSpec((1, SUBLANES, LANES), lambda s: (cur(s)[0], 0, 0)),
                 pl.BlockSpec((1, CONV_WIDTH - 1, CONV_CH), lambda s: (cur(s)[0], 0, 0)))
    scratch = [pltpu.VMEM((HEADS, HEAD_DIM, 2 * HEAD_DIM), F32),
               pltpu.VMEM((SUBLANES, LANES), F32),
               pltpu.VMEM((t + SUBLANES, CONV_CH), F32),
               pltpu.VMEM((t, D_MODEL), BF16),
               pltpu.VMEM((t, D_MODEL), F32),
               pltpu.VMEM((t, D_FF), BF16)]
    return pl.pallas_call(
        functools.partial(_prompt_kernel, nj, nt),
        grid=(nt + 1,),
        in_specs=in_specs,
        out_specs=out_specs,
        out_shape=out_shape,
        scratch_shapes=scratch,
        compiler_params=pltpu.CompilerParams(
            dimension_semantics=("arbitrary",),
            vmem_limit_bytes=VMEM_LIMIT_BYTES),
        name="prompt_layer",
    )(x, p, xs, ps, mix_s, *consts)


def _sample_inproj_kernel(x_ref, wheads_ref, wconv_ref, wgc_ref, nmix_ref, zh_ref, zc_ref, g_ref):
    hb = _rms(x_ref[...], nmix_ref[...]).astype(BF16)
    zh_ref[...] = _dot(hb, wheads_ref[...])
    zc_ref[...] = _dot(hb, wconv_ref[...])
    g_ref[...] = _dot(hb, wgc_ref[...])


def _sample_state_kernel(zq_ref, zk_ref, zv_ref, zo_ref, zb_ref, zc_ref, zu_ref, g_ref, c0_ref, n0_ref, m0_ref,
                         sc_ref, bi_ref, bf_ref, mh_ref, cw_ref,
                         mix_ref, cnew_ref, nnew_ref, mnew_ref, buf_ref,
                         inter_scr, wv_scr, cd_scr):
    tb = SAMPLE_TILE
    cu = zc_ref[...] * zu_ref[...]
    old0 = sc_ref[:, 0:CONV_CH]
    old1 = sc_ref[:, CONV_CH:2 * CONV_CH]
    conv = cw_ref[0:1, :] * old0 + cw_ref[1:2, :] * old1 + cw_ref[2:3, :] * cu
    mix_ref[:, MLSTM_W:D_MODEL] = (zb_ref[...] * conv).astype(BF16)
    buf_ref[:, 0:CONV_CH] = old1
    buf_ref[:, CONV_CH:2 * CONV_CH] = cu

    g = g_ref[...]
    ig = g[:, 0:HEADS] + bi_ref[...]
    lf = _log_sigmoid(g[:, HEADS:2 * HEADS] + bf_ref[...])
    m_inter = lf + m0_ref[...]
    m_new = jnp.maximum(m_inter, ig)
    w_in = jnp.exp(ig - m_new)
    c_dec = jnp.exp(m_inter - m_new)
    e_neg_m = jnp.exp(-m_new)
    mnew_ref[...] = m_new

    vs, scores, qns = [], [], []
    for h in range(HEADS):
        hs = slice(h * HEAD_DIM, (h + 1) * HEAD_DIM)
        q = zq_ref[:, hs] * Q_SCALE
        k = zk_ref[:, hs]
        v = zv_ref[:, hs]
        n0 = n0_ref[:, hs]
        wi = w_in[:, h:h + 1]
        cd = c_dec[:, h:h + 1]
        scores.append(jnp.sum(q * k, axis=1, keepdims=True) * wi)
        qns.append(jnp.sum(q * n0, axis=1, keepdims=True))
        wv_scr[:, hs] = wi * v
        cd_scr[:, hs] = jnp.broadcast_to(cd, (tb, HEAD_DIM))
        nnew_ref[:, hs] = cd * n0 + wi * k
        vs.append(v)

    eye = (lax.broadcasted_iota(jnp.int32, (HEAD_DIM, HEAD_DIM), 0)
           == lax.broadcasted_iota(jnp.int32, (HEAD_DIM, HEAD_DIM), 1))

    for i in range(tb):
        row = slice(i, i + 1)
        for h in range(HEADS):
            hs = slice(h * HEAD_DIM, (h + 1) * HEAD_DIM)
            c0 = c0_ref[i, h]
            q_rows = jnp.broadcast_to(zq_ref[row, hs] * Q_SCALE, (SUBLANES, HEAD_DIM)).astype(BF16)
            inter_scr[row, hs] = _dot(q_rows, c0.astype(BF16))[0:1, :]
            k_diag = jnp.where(eye, jnp.broadcast_to(zk_ref[row, hs], (HEAD_DIM, HEAD_DIM)), 0.0).astype(BF16)
            v_rows = jnp.broadcast_to(wv_scr[row, hs], (HEAD_DIM, HEAD_DIM)).astype(BF16)
            cd = jnp.broadcast_to(cd_scr[row, hs], (HEAD_DIM, HEAD_DIM))
            cnew_ref[i, h] = cd * c0 + _dot(k_diag, v_rows)

    for h in range(HEADS):
        hs = slice(h * HEAD_DIM, (h + 1) * HEAD_DIM)
        cd = c_dec[:, h:h + 1]
        num = scores[h] * vs[h] + cd * inter_scr[:, hs]
        den = scores[h] + cd * qns[h]
        hh = num / jnp.maximum(jnp.abs(den), e_neg_m[:, h:h + 1])
        hn = hh * lax.rsqrt(jnp.mean(hh * hh, axis=-1, keepdims=True) + EPS)
        mix_ref[:, hs] = (hn * mh_ref[:, hs] * jax.nn.sigmoid(zo_ref[:, hs])).astype(BF16)


def _sample_calls(xs, c0, n0, m0, sconv, wts):
    nb = xs.shape[0]
    tb = SAMPLE_TILE
    assert nb % tb == 0
    zh, zc, g = pl.pallas_call(
        _sample_inproj_kernel,
        out_shape=(jax.ShapeDtypeStruct((nb, 4 * MLSTM_W), F32),
                   jax.ShapeDtypeStruct((nb, 3 * CONV_CH), F32),
                   jax.ShapeDtypeStruct((nb, LANES), F32)),
        compiler_params=pltpu.CompilerParams(vmem_limit_bytes=VMEM_LIMIT_BYTES),
        name="sample_inproj",
    )(xs, wts["w_heads"], wts["w_conv"], wts["w_gc"], wts["norm_mix"])

    def zgroup(grp):
        return pl.BlockSpec((tb, MLSTM_W), lambda i, grp=grp: (i, grp))

    def zconv(grp):
        return pl.BlockSpec((tb, CONV_CH), lambda i, grp=grp: (i, grp))

    row_w = pl.BlockSpec((tb, MLSTM_W), lambda i: (i, 0))
    row_d = pl.BlockSpec((tb, D_MODEL), lambda i: (i, 0))
    row_h = pl.BlockSpec((tb, HEADS), lambda i: (i, 0))
    state = pl.BlockSpec((tb, HEADS, HEAD_DIM, HEAD_DIM), lambda i: (i, 0, 0, 0))

    def whole(shape):
        return pl.BlockSpec(shape, lambda i: (0,) * len(shape))

    mix, c_new, n_new, m_new, new_buf = pl.pallas_call(
        _sample_state_kernel,
        grid=(nb // tb,),
        in_specs=[zgroup(0), zgroup(1), zgroup(2), zgroup(3), zconv(0), zconv(1), zconv(2),
                  pl.BlockSpec((tb, LANES), lambda i: (i, 0)),
                  state, row_w, row_h, row_d,
                  whole((1, HEADS)), whole((1, HEADS)), whole((1, MLSTM_W)), whole((CONV_WIDTH, CONV_CH))],
        out_specs=(row_d, state, row_w, row_h, row_d),
        out_shape=(jax.ShapeDtypeStruct((nb, D_MODEL), BF16),
                   jax.ShapeDtypeStruct((nb, HEADS, HEAD_DIM, HEAD_DIM), F32),
                   jax.ShapeDtypeStruct((nb, MLSTM_W), F32),
                   jax.ShapeDtypeStruct((nb, HEADS), F32),
                   jax.ShapeDtypeStruct((nb, (CONV_WIDTH - 1) * CONV_CH), F32)),
        scratch_shapes=[pltpu.VMEM((tb, MLSTM_W), F32)] * 3,
        compiler_params=pltpu.CompilerParams(dimension_semantics=("arbitrary",)),
        name="sample_state",
    )(zh, zh, zh, zh, zc, zc, zc, g, c0, n0, m0, sconv, wts["b_i"], wts["b_f"], wts["mh_norm"], wts["conv_w"])
    return mix, c_new, n_new, m_new, new_buf


def _weight_prep_kernel(win_ref, wout_ref, wup_ref, wdown_ref, wpg_ref, wpp_ref,
                        heads_ref, conv_ref, gt_ref, gc_ref, out_ref, up_ref, down_ref, pg_ref, pp_ref):
    g0 = 4 * MLSTM_W
    heads_ref[...] = win_ref[:, 0:g0]
    rest = win_ref[:, g0:]
    conv_ref[...] = rest[:, 2 * HEADS:]
    lane = lax.broadcasted_iota(jnp.int32, (rest.shape[0], LANES), 1)
    gates = jnp.where(lane < 2 * HEADS, rest[:, 0:LANES].astype(F32), 0.0)
    gc_ref[...] = gates.astype(BF16)
    gt_ref[...] = gates.T[0:GATE_ROWS, :].astype(BF16)
    out_ref[...] = wout_ref[...].astype(BF16)
    up_ref[...] = wup_ref[...].astype(BF16)
    down_ref[...] = wdown_ref[...].astype(BF16)
    pg_ref[...] = wpg_ref[...].astype(BF16)
    pp_ref[...] = wpp_ref[...].astype(BF16)


def _prepare_weights(norm_mix, w_in, b_gate_i, b_gate_f, mh_norm, conv_w, w_out, norm_mlp, w_up, w_down,
                     norm_ple, w_ple_gate, w_ple_proj, norm_final):
    srcs = (w_in.astype(BF16), w_out, w_up, w_down, w_ple_gate, w_ple_proj)
    n_in = w_in.shape[1]
    outs = ((D_MODEL, 4 * MLSTM_W), (D_MODEL, 3 * CONV_CH), (GATE_ROWS, D_MODEL), (D_MODEL, LANES),
            w_out.shape, w_up.shape, w_down.shape, w_ple_gate.shape, w_ple_proj.shape)
    steps = PREP_STEPS

    def rows(shape):
        return pl.BlockSpec((shape[0] // steps, shape[1]), lambda i: (i, 0))

    out_specs = [rows(o) for o in outs]
    out_specs[2] = pl.BlockSpec((GATE_ROWS, D_MODEL // steps), lambda i: (0, i))
    assert n_in == 4 * MLSTM_W + 2 * HEADS + 3 * CONV_CH
    w_heads, w_conv, w_gt, w_gc, w_out_b, w_up_b, w_down_b, w_pg_b, w_pp_b = pl.pallas_call(
        _weight_prep_kernel,
        grid=(steps,),
        in_specs=[rows(a.shape) for a in srcs],
        out_specs=out_specs,
        out_shape=[jax.ShapeDtypeStruct(o, BF16) for o in outs],
        compiler_params=pltpu.CompilerParams(dimension_semantics=("arbitrary",),
                                             vmem_limit_bytes=VMEM_LIMIT_BYTES),
        name="weight_prep",
    )(*srcs)
    pad = jnp.zeros((SUBLANES - HEADS, 1), F32)
    return dict(
        w_heads=w_heads,
        w_conv=w_conv,
        w_gt=w_gt,
        w_gc=w_gc,
        bi8=jnp.concatenate([b_gate_i.reshape(HEADS, 1), pad], axis=0),
        bf8=jnp.concatenate([b_gate_f.reshape(HEADS, 1), pad], axis=0),
        b_i=b_gate_i.reshape(1, HEADS),
        b_f=b_gate_f.reshape(1, HEADS),
        norm_mix=norm_mix.reshape(1, D_MODEL),
        mh_norm=mh_norm.reshape(1, MLSTM_W),
        conv_w=conv_w,
        w_out=w_out_b,
        norm_mlp=norm_mlp.reshape(1, D_MODEL),
        w_up=w_up_b,
        w_down=w_down_b,
        norm_ple=norm_ple.reshape(1, D_MODEL),
        w_pg=w_pg_b,
        w_pp=w_pp_b,
        norm_final=norm_final.reshape(1, D_MODEL),
    )


def kernel(x_prompt, x_sample, state_mlstm_C, state_mlstm_n, state_mlstm_m, state_conv, p_prompt, p_sample,
           norm_mix, w_in, b_gate_i, b_gate_f, mh_norm, conv_w, w_out, norm_mlp, w_up, w_down, norm_ple,
           w_ple_gate, w_ple_proj, norm_final):
    assert norm_mix.shape[0] == 1, "single-layer trunk"
    wts = _prepare_weights(norm_mix[0], w_in[0], b_gate_i[0], b_gate_f[0], mh_norm[0], conv_w[0], w_out[0],
                           norm_mlp[0], w_up[0], w_down[0], norm_ple[0], w_ple_gate[0], w_ple_proj[0], norm_final)

    nb = x_sample.shape[0]
    xs = x_sample.reshape(nb, D_MODEL)
    mix_s, c_new, n_new, m_new, new_buf = _sample_calls(
        xs, state_mlstm_C[0], state_mlstm_n[0].reshape(nb, MLSTM_W), state_mlstm_m[0],
        state_conv[0].reshape(nb, (CONV_WIDTH - 1) * CONV_CH), wts)

    y_prompt, ys, prompt_c, prompt_n, m_rows, conv_tail = _prompt_call(
        x_prompt, p_prompt[0], xs, p_sample[0].reshape(nb, PLE_DIM), mix_s, wts)
    prompt_m = m_rows[:, 0:HEADS, 0]

    return (y_prompt, ys.reshape(nb, 1, D_MODEL),
            prompt_c[None], prompt_n[None], prompt_m[None], conv_tail[None],
            c_new[None], n_new.reshape(1, nb, HEADS, HEAD_DIM), m_new[None],
            new_buf.reshape(1, nb, CONV_WIDTH - 1, CONV_CH))
```

```python
import functools
import math

import jax
import jax.numpy as jnp
from jax import lax
from jax.experimental import pallas as pl
from jax.experimental.pallas import tpu as pltpu

F32 = jnp.float32
BF16 = jnp.bfloat16

D_MODEL = 1024
HEADS = 4
HEAD_DIM = 128
MLSTM_W = HEADS * HEAD_DIM
CONV_CH = D_MODEL - MLSTM_W
CONV_WIDTH = 3
D_FF = 4 * D_MODEL
PLE_DIM = 256
EPS = 1e-6
M_INIT = -1e30
Q_SCALE = HEAD_DIM ** -0.5
LOG2E = math.log2(math.e)

GATE_ROWS = 16
PREP_STEPS = 8
LANES = 128
SUBLANES = 8
VMEM_LIMIT_BYTES = 60000 * 1024

SEQ_TILE = 512
CHUNK = 256
FF_CHUNK = 512
SAMPLE_TILE = 16


def _dot(a, b):
    return jnp.dot(a, b, preferred_element_type=F32)


def _dot_nt(a, b):
    return lax.dot_general(a, b, (((1,), (1,)), ((), ())), preferred_element_type=F32)


def _dot_tn(a, b):
    return lax.dot_general(a, b, (((0,), (0,)), ((), ())), preferred_element_type=F32)


def _rms(x, g):
    y = x * lax.rsqrt(jnp.mean(x * x, axis=-1, keepdims=True) + EPS)
    return y * g


def _log_sigmoid(x):
    return jnp.minimum(x, 0.0) - jnp.log1p(jnp.exp(-jnp.abs(x)))


def _mlp_chunk(xn_bf16, c, wup_ref, wdown_ref):
    cols = slice(c * FF_CHUNK, (c + 1) * FF_CHUNK)
    hf = jnp.maximum(_dot(xn_bf16, wup_ref[:, cols]), 0.0)
    return _dot((hf * hf).astype(BF16), wdown_ref[cols, :])


def _cummax_lanes(a):
    n = a.shape[1]
    lane = lax.broadcasted_iota(jnp.int32, a.shape, 1)
    d = 1
    while d < n:
        shifted = pltpu.roll(a, d, axis=1)
        a = jnp.maximum(a, jnp.where(lane >= d, shifted, -jnp.inf))
        d *= 2
    return a


def _prompt_kernel(tiles_per_seq, num_tiles,
                   x_ref, plag_ref, xs_ref, ps_ref, mixs_ref,
                   wheads_ref, wconv_ref, wgt_ref, bi_ref, bf_ref, nmix_ref, mh_ref, cw_ref,
                   wout_ref, nmlp_ref, wup_ref, wdown_ref, nple_ref, wpg_ref, wpp_ref, nfin_ref, tri_ref,
                   y_ref, ys_ref, cout_ref, nout_ref, mout_ref, convout_ref,
                   caug, m_scr, cu_buf, mix_scr, xres_scr):
    t = SEQ_TILE
    lc = CHUNK
    ns = xs_ref.shape[0]
    s_id = pl.program_id(0)
    is_real = s_id < num_tiles
    j = lax.rem(jnp.minimum(s_id, num_tiles - 1), tiles_per_seq)

    @pl.when(s_id == 0)
    def _():
        mix_scr[0:ns, :] = mixs_ref[...]
        mix_scr[ns:t, :] = jnp.zeros((t - ns, D_MODEL), BF16)
        xres_scr[0:ns, :] = xs_ref[...]
        xres_scr[ns:t, :] = jnp.zeros((t - ns, D_MODEL), F32)

    @pl.when(j == 0)
    def _():
        caug[...] = jnp.zeros_like(caug)
        m_scr[...] = jnp.full(m_scr.shape, M_INIT, F32)
        cu_buf[0:SUBLANES, :] = jnp.zeros((SUBLANES, CONV_CH), F32)

    y_ref[0] = xres_scr[...] + _dot(mix_scr[...], wout_ref[...])
    xn = _rms(y_ref[0], nmlp_ref[...]).astype(BF16)

    x = x_ref[0]
    hb = _rms(x, nmix_ref[...]).astype(BF16)
    xres_scr[...] = x
    gt = _dot_nt(wgt_ref[...], hb)

    row_id = lax.broadcasted_iota(jnp.int32, (lc, lc), 0)
    col_id = lax.broadcasted_iota(jnp.int32, (lc, lc), 1)
    causal = col_id <= row_id
    lane_id = lax.broadcasted_iota(jnp.int32, (lc, HEAD_DIM), 1)

    acc = jnp.zeros((t, D_MODEL), F32)
    n_phases = (t // lc) * HEADS
    mlp_per_phase = D_FF // FF_CHUNK // n_phases
    mlp_first = (mlp_per_phase + 1) // 2
    next_mlp = 0
    m_carry = m_scr[...]
    for c in range(t // lc):
        tok = slice(c * lc, (c + 1) * lc)
        hb_c = hb[tok, :]
        g8 = gt[0:SUBLANES, tok]
        ig = g8 + bi_ref[...]
        lf = _log_sigmoid(pltpu.roll(g8, HEADS, axis=0) + bf_ref[...])
        hi = lf.astype(BF16)
        r1 = lf - hi.astype(F32)
        mid = r1.astype(BF16)
        lo = (r1 - mid.astype(F32)).astype(BF16)
        parts = _dot(jnp.concatenate([hi, mid, lo, jnp.zeros_like(lo)], axis=0), tri_ref[...])
        b = parts[0:8] + parts[8:16] + parts[16:24]
        a = ig - b
        m_prev = jnp.concatenate([m_carry] * (lc // LANES), axis=1)
        g = jnp.maximum(_cummax_lanes(a), m_prev)
        m_t = b + g
        b_last = b[:, lc - 1:lc]
        m_new = m_t[:, lc - 1:lc]
        decay = jnp.exp(m_prev - g)
        e_neg_m = jnp.exp(-m_t)
        w_state = jnp.exp(a + (b_last - m_new))
        c_dec = jnp.exp(b_last + m_prev[:, 0:1] - m_new)
        m_carry = jnp.broadcast_to(m_new, m_carry.shape)
        a2 = a * LOG2E
        rows = jnp.concatenate(
            [g * (-LOG2E), decay, e_neg_m, w_state, jnp.zeros((LANES - 4 * SUBLANES, lc), F32)], axis=0)
        cols = rows.T

        for h in range(HEADS):
            if h % 2 == 0:
                pair = slice(h * HEAD_DIM, (h + 2) * HEAD_DIM)
                zq2, zk2, zv2, zo2 = (
                    _dot(hb_c, wheads_ref[:, grp * MLSTM_W + pair.start:grp * MLSTM_W + pair.stop])
                    for grp in range(4))
            half = slice((h % 2) * HEAD_DIM, (h % 2 + 1) * HEAD_DIM)
            qs = (zq2[:, half] * Q_SCALE).astype(BF16)
            kb = zk2[:, half].astype(BF16)
            v = zv2[:, half]
            c_col = cols[:, h:h + 1]
            dec = cols[:, SUBLANES + h:SUBLANES + h + 1]
            enm = cols[:, 2 * SUBLANES + h:2 * SUBLANES + h + 1]
            wst = cols[:, 3 * SUBLANES + h:3 * SUBLANES + h + 1]
            s = _dot_nt(qs, kb)

            for _ in range(mlp_first):
                acc = acc + _mlp_chunk(xn, next_mlp, wup_ref, wdown_ref)
                next_mlp += 1

            dmat = jnp.exp2(jnp.where(causal, c_col + a2[h:h + 1, :], -jnp.inf))
            pm = s * dmat
            row_sum = jnp.sum(pm, axis=1, keepdims=True)
            intra = _dot(pm.astype(BF16), v.astype(BF16))
            c_state = caug[h]
            inter = _dot(qs, c_state.astype(BF16))
            vw = jnp.concatenate([v * wst, jnp.where(lane_id == 0, wst, 0.0)], axis=1).astype(BF16)
            caug[h] = c_dec[h:h + 1, :] * c_state + _dot_tn(kb, vw)

            for _ in range(mlp_per_phase - mlp_first):
                acc = acc + _mlp_chunk(xn, next_mlp, wup_ref, wdown_ref)
                next_mlp += 1

            num = intra + dec * inter[:, 0:HEAD_DIM]
            den = row_sum + dec * inter[:, HEAD_DIM:HEAD_DIM + 1]
            hh = num / jnp.maximum(jnp.abs(den), enm)
            hn = hh * lax.rsqrt(jnp.mean(hh * hh, axis=-1, keepdims=True) + EPS)
            hn = hn * mh_ref[:, h * HEAD_DIM:(h + 1) * HEAD_DIM]
            mix_scr[tok, h * HEAD_DIM:(h + 1) * HEAD_DIM] = (
                hn * jax.nn.sigmoid(zo2[:, half])).astype(BF16)
    m_scr[...] = m_carry
    assert next_mlp == D_FF // FF_CHUNK

    zc = _dot(hb, wconv_ref[...])
    p_sample = jnp.concatenate([ps_ref[...], jnp.zeros((t - ns, PLE_DIM), F32)], axis=0)
    p_tail = jnp.where(s_id == 0, p_sample, plag_ref[0])
    ple = _dot(p_tail.astype(BF16), wpp_ref[...])

    def tail_rows(rows):
        x2 = y_ref[0, rows, :] + acc[rows, :]
        gate = jax.nn.sigmoid(_dot(_rms(x2, nple_ref[...]).astype(BF16), wpg_ref[...]))
        y_ref[0, rows, :] = _rms(x2 + gate * ple[rows, :], nfin_ref[...])

    tail_rows(slice(0, t // 2))

    cu = zc[:, CONV_CH:2 * CONV_CH] * zc[:, 2 * CONV_CH:3 * CONV_CH]
    cu_buf[SUBLANES:SUBLANES + t, :] = cu
    conv = (cw_ref[0:1, :] * cu_buf[SUBLANES - 2:SUBLANES - 2 + t, :]
            + cw_ref[1:2, :] * cu_buf[SUBLANES - 1:SUBLANES - 1 + t, :]
            + cw_ref[2:3, :] * cu)
    mix_scr[:, MLSTM_W:D_MODEL] = (zc[:, 0:CONV_CH] * conv).astype(BF16)
    cu_buf[0:SUBLANES, :] = cu_buf[t:t + SUBLANES, :]

    tail_rows(slice(t // 2, t))

    @pl.when(s_id == 0)
    def _():
        ys_ref[...] = y_ref[0, 0:ns, :]

    @pl.when(jnp.logical_and(is_real, j == tiles_per_seq - 1))
    def _():
        for h in range(HEADS):
            cout_ref[0, h] = caug[h, :, 0:HEAD_DIM]
            nout_ref[0, h:h + 1, :] = caug[h, :, HEAD_DIM:2 * HEAD_DIM].T[0:1, :]
        mout_ref[0] = m_scr[...]
        convout_ref[0] = cu[t - (CONV_WIDTH - 1):t, :]


def _resident(shape):
    return pl.BlockSpec(shape, lambda *_: (0,) * len(shape), pipeline_mode=pl.Buffered(1))


def _prompt_call(x, p, xs, ps, mix_s, wts):
    bsz, seq, _ = x.shape
    t = SEQ_TILE
    assert seq % t == 0 and t % CHUNK == 0 and CHUNK % LANES == 0 and xs.shape[0] <= t
    tri = (jnp.arange(CHUNK)[:, None] <= jnp.arange(CHUNK)[None, :]).astype(BF16)
    consts = [wts["w_heads"], wts["w_conv"], wts["w_gt"], wts["bi8"], wts["bf8"], wts["norm_mix"], wts["mh_norm"], wts["conv_w"],
              wts["w_out"], wts["norm_mlp"], wts["w_up"], wts["w_down"], wts["norm_ple"], wts["w_pg"],
              wts["w_pp"], wts["norm_final"], tri]
    nj = seq // t
    nt = bsz * nj

    def cur(s):
        c = jnp.minimum(s, nt - 1)
        return c // nj, c % nj

    def lag(s):
        c = jnp.maximum(s - 1, 0)
        return c // nj, c % nj

    in_specs = [pl.BlockSpec((1, t, D_MODEL), lambda s: (*cur(s), 0)),
                pl.BlockSpec((1, t, PLE_DIM), lambda s: (*lag(s), 0))]
    in_specs += [_resident(c.shape) for c in (xs, ps, mix_s, *consts)]
    out_shape = (jax.ShapeDtypeStruct((bsz, seq, D_MODEL), F32),
                 jax.ShapeDtypeStruct(xs.shape, F32),
                 jax.ShapeDtypeStruct((bsz, HEADS, HEAD_DIM, HEAD_DIM), F32),
                 jax.ShapeDtypeStruct((bsz, HEADS, HEAD_DIM), F32),
                 jax.ShapeDtypeStruct((bsz, SUBLANES, LANES), F32),
                 jax.ShapeDtypeStruct((bsz, CONV_WIDTH - 1, CONV_CH), F32))
    out_specs = (pl.BlockSpec((1, t, D_MODEL), lambda s: (*lag(s), 0)),
                 pl.BlockSpec(xs.shape, lambda s: (0, 0)),
                 pl.BlockSpec((1, HEADS, HEAD_DIM, HEAD_DIM), lambda s: (cur(s)[0], 0, 0, 0)),
                 pl.BlockSpec((1, HEADS, HEAD_DIM), lambda s: (cur(s)[0], 0, 0)),
                 pl.BlockSpec((1, SUBLANES, LANES), lambda s: (cur(s)[0], 0, 0)),
                 pl.BlockSpec((1, CONV_WIDTH - 1, CONV_CH), lambda s: (cur(s)[0], 0, 0)))
    scratch = [pltpu.VMEM((HEADS, HEAD_DIM, 2 * HEAD_DIM), F32),
               pltpu.VMEM((SUBLANES, LANES), F32),
               pltpu.VMEM((t + SUBLANES, CONV_CH), F32),
               pltpu.VMEM((t, D_MODEL), BF16),
               pltpu.VMEM((t, D_MODEL), F32)]
    return pl.pallas_call(
        functools.partial(_prompt_kernel, nj, nt),
        grid=(nt + 1,),
        in_specs=in_specs,
        out_specs=out_specs,
        out_shape=out_shape,
        scratch_shapes=scratch,
        compiler_params=pltpu.CompilerParams(
            dimension_semantics=("arbitrary",),
            vmem_limit_bytes=VMEM_LIMIT_BYTES),
        name="prompt_layer",
    )(x, p, xs, ps, mix_s, *consts)


def _sample_inproj_kernel(x_ref, wheads_ref, wconv_ref, wgc_ref, nmix_ref, zh_ref, zc_ref, g_ref):
    hb = _rms(x_ref[...], nmix_ref[...]).astype(BF16)
    zh_ref[...] = _dot(hb, wheads_ref[...])
    zc_ref[...] = _dot(hb, wconv_ref[...])
    g_ref[...] = _dot(hb, wgc_ref[...])


def _sample_state_kernel(zq_ref, zk_ref, zv_ref, zo_ref, zb_ref, zc_ref, zu_ref, g_ref, c0_ref, n0_ref, m0_ref,
                         sc_ref, bi_ref, bf_ref, mh_ref, cw_ref,
                         mix_ref, cnew_ref, nnew_ref, mnew_ref, buf_ref,
                         inter_scr, wv_scr, cd_scr):
    tb = SAMPLE_TILE
    cu = zc_ref[...] * zu_ref[...]
    old0 = sc_ref[:, 0:CONV_CH]
    old1 = sc_ref[:, CONV_CH:2 * CONV_CH]
    conv = cw_ref[0:1, :] * old0 + cw_ref[1:2, :] * old1 + cw_ref[2:3, :] * cu
    mix_ref[:, MLSTM_W:D_MODEL] = (zb_ref[...] * conv).astype(BF16)
    buf_ref[:, 0:CONV_CH] = old1
    buf_ref[:, CONV_CH:2 * CONV_CH] = cu

    g = g_ref[...]
    ig = g[:, 0:HEADS] + bi_ref[...]
    lf = _log_sigmoid(g[:, HEADS:2 * HEADS] + bf_ref[...])
    m_inter = lf + m0_ref[...]
    m_new = jnp.maximum(m_inter, ig)
    w_in = jnp.exp(ig - m_new)
    c_dec = jnp.exp(m_inter - m_new)
    e_neg_m = jnp.exp(-m_new)
    mnew_ref[...] = m_new

    vs, scores, qns = [], [], []
    for h in range(HEADS):
        hs = slice(h * HEAD_DIM, (h + 1) * HEAD_DIM)
        q = zq_ref[:, hs] * Q_SCALE
        k = zk_ref[:, hs]
        v = zv_ref[:, hs]
        n0 = n0_ref[:, hs]
        wi = w_in[:, h:h + 1]
        cd = c_dec[:, h:h + 1]
        scores.append(jnp.sum(q * k, axis=1, keepdims=True) * wi)
        qns.append(jnp.sum(q * n0, axis=1, keepdims=True))
        wv_scr[:, hs] = wi * v
        cd_scr[:, hs] = jnp.broadcast_to(cd, (tb, HEAD_DIM))
        nnew_ref[:, hs] = cd * n0 + wi * k
        vs.append(v)

    eye = (lax.broadcasted_iota(jnp.int32, (HEAD_DIM, HEAD_DIM), 0)
           == lax.broadcasted_iota(jnp.int32, (HEAD_DIM, HEAD_DIM), 1))

    for i in range(tb):
        row = slice(i, i + 1)
        for h in range(HEADS):
            hs = slice(h * HEAD_DIM, (h + 1) * HEAD_DIM)
            c0 = c0_ref[i, h]
            q_rows = jnp.broadcast_to(zq_ref[row, hs] * Q_SCALE, (SUBLANES, HEAD_DIM)).astype(BF16)
            inter_scr[row, hs] = _dot(q_rows, c0.astype(BF16))[0:1, :]
            k_diag = jnp.where(eye, jnp.broadcast_to(zk_ref[row, hs], (HEAD_DIM, HEAD_DIM)), 0.0).astype(BF16)
            v_rows = jnp.broadcast_to(wv_scr[row, hs], (HEAD_DIM, HEAD_DIM)).astype(BF16)
            cd = jnp.broadcast_to(cd_scr[row, hs], (HEAD_DIM, HEAD_DIM))
            cnew_ref[i, h] = cd * c0 + _dot(k_diag, v_rows)

    for h in range(HEADS):
        hs = slice(h * HEAD_DIM, (h + 1) * HEAD_DIM)
        cd = c_dec[:, h:h + 1]
        num = scores[h] * vs[h] + cd * inter_scr[:, hs]
        den = scores[h] + cd * qns[h]
        hh = num / jnp.maximum(jnp.abs(den), e_neg_m[:, h:h + 1])
        hn = hh * lax.rsqrt(jnp.mean(hh * hh, axis=-1, keepdims=True) + EPS)
        mix_ref[:, hs] = (hn * mh_ref[:, hs] * jax.nn.sigmoid(zo_ref[:, hs])).astype(BF16)


def _sample_calls(xs, c0, n0, m0, sconv, wts):
    nb = xs.shape[0]
    tb = SAMPLE_TILE
    assert nb % tb == 0
    zh, zc, g = pl.pallas_call(
        _sample_inproj_kernel,
        out_shape=(jax.ShapeDtypeStruct((nb, 4 * MLSTM_W), F32),
                   jax.ShapeDtypeStruct((nb, 3 * CONV_CH), F32),
                   jax.ShapeDtypeStruct((nb, LANES), F32)),
        compiler_params=pltpu.CompilerParams(vmem_limit_bytes=VMEM_LIMIT_BYTES),
        name="sample_inproj",
    )(xs, wts["w_heads"], wts["w_conv"], wts["w_gc"], wts["norm_mix"])

    def zgroup(grp):
        return pl.BlockSpec((tb, MLSTM_W), lambda i, grp=grp: (i, grp))

    def zconv(grp):
        return pl.BlockSpec((tb, CONV_CH), lambda i, grp=grp: (i, grp))

    row_w = pl.BlockSpec((tb, MLSTM_W), lambda i: (i, 0))
    row_d = pl.BlockSpec((tb, D_MODEL), lambda i: (i, 0))
    row_h = pl.BlockSpec((tb, HEADS), lambda i: (i, 0))
    state = pl.BlockSpec((tb, HEADS, HEAD_DIM, HEAD_DIM), lambda i: (i, 0, 0, 0))

    def whole(shape):
        return pl.BlockSpec(shape, lambda i: (0,) * len(shape))

    mix, c_new, n_new, m_new, new_buf = pl.pallas_call(
        _sample_state_kernel,
        grid=(nb // tb,),
        in_specs=[zgroup(0), zgroup(1), zgroup(2), zgroup(3), zconv(0), zconv(1), zconv(2),
                  pl.BlockSpec((tb, LANES), lambda i: (i, 0)),
                  state, row_w, row_h, row_d,
                  whole((1, HEADS)), whole((1, HEADS)), whole((1, MLSTM_W)), whole((CONV_WIDTH, CONV_CH))],
        out_specs=(row_d, state, row_w, row_h, row_d),
        out_shape=(jax.ShapeDtypeStruct((nb, D_MODEL), BF16),
                   jax.ShapeDtypeStruct((nb, HEADS, HEAD_DIM, HEAD_DIM), F32),
                   jax.ShapeDtypeStruct((nb, MLSTM_W), F32),
                   jax.ShapeDtypeStruct((nb, HEADS), F32),
                   jax.ShapeDtypeStruct((nb, (CONV_WIDTH - 1) * CONV_CH), F32)),
        scratch_shapes=[pltpu.VMEM((tb, MLSTM_W), F32)] * 3,
        compiler_params=pltpu.CompilerParams(dimension_semantics=("arbitrary",)),
        name="sample_state",
    )(zh, zh, zh, zh, zc, zc, zc, g, c0, n0, m0, sconv, wts["b_i"], wts["b_f"], wts["mh_norm"], wts["conv_w"])
    return mix, c_new, n_new, m_new, new_buf


def _weight_prep_kernel(win_ref, wout_ref, wup_ref, wdown_ref, wpg_ref, wpp_ref,
                        heads_ref, conv_ref, gt_ref, gc_ref, out_ref, up_ref, down_ref, pg_ref, pp_ref):
    g0 = 4 * MLSTM_W
    g1 = g0 + 2 * HEADS
    heads_ref[...] = win_ref[0:g0, :].T.astype(BF16)
    conv_ref[...] = win_ref[g1:, :].T.astype(BF16)
    gates = win_ref[g0:g1, :]
    rows = gates.shape[1]
    gt_ref[...] = jnp.concatenate([gates, jnp.zeros((GATE_ROWS - 2 * HEADS, rows), F32)], axis=0).astype(BF16)
    gc_ref[...] = jnp.concatenate([gates, jnp.zeros((LANES - 2 * HEADS, rows), F32)], axis=0).T.astype(BF16)
    out_ref[...] = wout_ref[...].astype(BF16)
    up_ref[...] = wup_ref[...].astype(BF16)
    down_ref[...] = wdown_ref[...].astype(BF16)
    pg_ref[...] = wpg_ref[...].astype(BF16)
    pp_ref[...] = wpp_ref[...].astype(BF16)


def _prepare_weights(norm_mix, w_in, b_gate_i, b_gate_f, mh_norm, conv_w, w_out, norm_mlp, w_up, w_down,
                     norm_ple, w_ple_gate, w_ple_proj, norm_final):
    w_in_t = jnp.swapaxes(w_in, 0, 1)
    srcs = (w_in_t, w_out, w_up, w_down, w_ple_gate, w_ple_proj)
    n_in = w_in.shape[1]
    outs = ((D_MODEL, 4 * MLSTM_W), (D_MODEL, 3 * CONV_CH), (GATE_ROWS, D_MODEL), (D_MODEL, LANES),
            w_out.shape, w_up.shape, w_down.shape, w_ple_gate.shape, w_ple_proj.shape)
    steps = PREP_STEPS

    def rows(shape):
        return pl.BlockSpec((shape[0] // steps, shape[1]), lambda i: (i, 0))

    out_specs = [rows(o) for o in outs]
    out_specs[2] = pl.BlockSpec((GATE_ROWS, D_MODEL // steps), lambda i: (0, i))
    assert n_in == 4 * MLSTM_W + 2 * HEADS + 3 * CONV_CH
    w_heads, w_conv, w_gt, w_gc, w_out_b, w_up_b, w_down_b, w_pg_b, w_pp_b = pl.pallas_call(
        _weight_prep_kernel,
        grid=(steps,),
        in_specs=[pl.BlockSpec((n_in, D_MODEL // steps), lambda i: (0, i))] + [rows(a.shape) for a in srcs[1:]],
        out_specs=out_specs,
        out_shape=[jax.ShapeDtypeStruct(o, BF16) for o in outs],
        compiler_params=pltpu.CompilerParams(dimension_semantics=("arbitrary",),
                                             vmem_limit_bytes=VMEM_LIMIT_BYTES),
        name="weight_prep",
    )(*srcs)
    pad = jnp.zeros((SUBLANES - HEADS, 1), F32)
    return dict(
        w_heads=w_heads,
        w_conv=w_conv,
        w_gt=w_gt,
        w_gc=w_gc,
        bi8=jnp.concatenate([b_gate_i.reshape(HEADS, 1), pad], axis=0),
        bf8=jnp.concatenate([b_gate_f.reshape(HEADS, 1), pad], axis=0),
        b_i=b_gate_i.reshape(1, HEADS),
        b_f=b_gate_f.reshape(1, HEADS),
        norm_mix=norm_mix.reshape(1, D_MODEL),
        mh_norm=mh_norm.reshape(1, MLSTM_W),
        conv_w=conv_w,
        w_out=w_out_b,
        norm_mlp=norm_mlp.reshape(1, D_MODEL),
        w_up=w_up_b,
        w_down=w_down_b,
        norm_ple=norm_ple.reshape(1, D_MODEL),
        w_pg=w_pg_b,
        w_pp=w_pp_b,
        norm_final=norm_final.reshape(1, D_MODEL),
    )


def kernel(x_prompt, x_sample, state_mlstm_C, state_mlstm_n, state_mlstm_m, state_conv, p_prompt, p_sample,
           norm_mix, w_in, b_gate_i, b_gate_f, mh_norm, conv_w, w_out, norm_mlp, w_up, w_down, norm_ple,
           w_ple_gate, w_ple_proj, norm_final):
    assert norm_mix.shape[0] == 1, "single-layer trunk"
    wts = _prepare_weights(norm_mix[0], w_in[0], b_gate_i[0], b_gate_f[0], mh_norm[0], conv_w[0], w_out[0],
                           norm_mlp[0], w_up[0], w_down[0], norm_ple[0], w_ple_gate[0], w_ple_proj[0], norm_final)

    nb = x_sample.shape[0]
    xs = x_sample.reshape(nb, D_MODEL)
    mix_s, c_new, n_new, m_new, new_buf = _sample_calls(
        xs, state_mlstm_C[0], state_mlstm_n[0].reshape(nb, MLSTM_W), state_mlstm_m[0],
        state_conv[0].reshape(nb, (CONV_WIDTH - 1) * CONV_CH), wts)

    y_prompt, ys, prompt_c, prompt_n, m_rows, conv_tail = _prompt_call(
        x_prompt, p_prompt[0], xs, p_sample[0].reshape(nb, PLE_DIM), mix_s, wts)
    prompt_m = m_rows[:, 0:HEADS, 0]

    return (y_prompt, ys.reshape(nb, 1, D_MODEL),
            prompt_c[None], prompt_n[None], prompt_m[None], conv_tail[None],
            c_new[None], n_new.reshape(1, nb, HEADS, HEAD_DIM), m_new[None],
            new_buf.reshape(1, nb, CONV_WIDTH - 1, CONV_CH))
```

```python
import functools
import math

import jax
import jax.numpy as jnp
from jax import lax
from jax.experimental import pallas as pl
from jax.experimental.pallas import tpu as pltpu

F32 = jnp.float32
BF16 = jnp.bfloat16

D_MODEL = 1024
HEADS = 4
HEAD_DIM = 128
MLSTM_W = HEADS * HEAD_DIM
CONV_CH = D_MODEL - MLSTM_W
CONV_WIDTH = 3
D_FF = 4 * D_MODEL
PLE_DIM = 256
EPS = 1e-6
M_INIT = -1e30
Q_SCALE = HEAD_DIM ** -0.5
LOG2E = math.log2(math.e)

GATE_ROWS = 16
PREP_STEPS = 8
LANES = 128
SUBLANES = 8
VMEM_LIMIT_BYTES = 60000 * 1024

SEQ_TILE = 512
CHUNK = 256
FF_CHUNK = 512
SAMPLE_TILE = 16


def _dot(a, b):
    return jnp.dot(a, b, preferred_element_type=F32)


def _dot_nt(a, b):
    return lax.dot_general(a, b, (((1,), (1,)), ((), ())), preferred_element_type=F32)


def _dot_tn(a, b):
    return lax.dot_general(a, b, (((0,), (0,)), ((), ())), preferred_element_type=F32)


def _rms(x, g):
    y = x * lax.rsqrt(jnp.mean(x * x, axis=-1, keepdims=True) + EPS)
    return y * g


def _log_sigmoid(x):
    return jnp.minimum(x, 0.0) - jnp.log1p(jnp.exp(-jnp.abs(x)))


def _mlp_chunk(xn_bf16, c, wup_ref, wdown_ref):
    cols = slice(c * FF_CHUNK, (c + 1) * FF_CHUNK)
    hf = jnp.maximum(_dot(xn_bf16, wup_ref[:, cols]), 0.0)
    return _dot((hf * hf).astype(BF16), wdown_ref[cols, :])


def _cummax_lanes(a):
    n = a.shape[1]
    lane = lax.broadcasted_iota(jnp.int32, a.shape, 1)
    d = 1
    while d < n:
        shifted = pltpu.roll(a, d, axis=1)
        a = jnp.maximum(a, jnp.where(lane >= d, shifted, -jnp.inf))
        d *= 2
    return a


def _prompt_kernel(tiles_per_seq, num_tiles,
                   x_ref, plag_ref, xs_ref, ps_ref, mixs_ref,
                   wheads_ref, wconv_ref, wgt_ref, bi_ref, bf_ref, nmix_ref, mh_ref, cw_ref,
                   wout_ref, nmlp_ref, wup_ref, wdown_ref, nple_ref, wpg_ref, wpp_ref, nfin_ref,
                   y_ref, ys_ref, cout_ref, nout_ref, mout_ref, convout_ref,
                   caug, m_scr, cu_buf, mix_scr, xres_scr, tri_scr):
    t = SEQ_TILE
    lc = CHUNK
    ns = xs_ref.shape[0]
    s_id = pl.program_id(0)
    is_real = s_id < num_tiles
    j = lax.rem(jnp.minimum(s_id, num_tiles - 1), tiles_per_seq)

    @pl.when(s_id == 0)
    def _():
        mix_scr[0:ns, :] = mixs_ref[...]
        mix_scr[ns:t, :] = jnp.zeros((t - ns, D_MODEL), BF16)
        xres_scr[0:ns, :] = xs_ref[:, 0, :]
        xres_scr[ns:t, :] = jnp.zeros((t - ns, D_MODEL), F32)
        tri_scr[...] = (lax.broadcasted_iota(jnp.int32, (lc, lc), 0)
                        <= lax.broadcasted_iota(jnp.int32, (lc, lc), 1)).astype(F32).astype(BF16)

    @pl.when(j == 0)
    def _():
        caug[...] = jnp.zeros_like(caug)
        m_scr[...] = jnp.full(m_scr.shape, M_INIT, F32)
        cu_buf[0:SUBLANES, :] = jnp.zeros((SUBLANES, CONV_CH), F32)

    y_ref[0] = xres_scr[...] + _dot(mix_scr[...], wout_ref[...])
    xn = _rms(y_ref[0], nmlp_ref[...]).astype(BF16)

    x = x_ref[0]
    hb = _rms(x, nmix_ref[...]).astype(BF16)
    xres_scr[...] = x
    gt = _dot_nt(wgt_ref[...], hb)

    row_id = lax.broadcasted_iota(jnp.int32, (lc, lc), 0)
    col_id = lax.broadcasted_iota(jnp.int32, (lc, lc), 1)
    causal = col_id <= row_id
    lane_id = lax.broadcasted_iota(jnp.int32, (lc, HEAD_DIM), 1)

    acc = jnp.zeros((t, D_MODEL), F32)
    n_phases = (t // lc) * HEADS
    mlp_per_phase = D_FF // FF_CHUNK // n_phases
    mlp_first = (mlp_per_phase + 1) // 2
    next_mlp = 0
    m_carry = m_scr[...]
    sub_id = lax.broadcasted_iota(jnp.int32, (SUBLANES, lc), 0)
    bias_i = jnp.zeros((SUBLANES, lc), F32)
    bias_f = jnp.zeros((SUBLANES, lc), F32)
    for h in range(HEADS):
        bias_i = jnp.where(sub_id == h, bi_ref[0, h], bias_i)
        bias_f = jnp.where(sub_id == h, bf_ref[0, h], bias_f)
    for c in range(t // lc):
        tok = slice(c * lc, (c + 1) * lc)
        hb_c = hb[tok, :]
        g8 = gt[0:SUBLANES, tok]
        ig = g8 + bias_i
        lf = _log_sigmoid(pltpu.roll(g8, HEADS, axis=0) + bias_f)
        hi = lf.astype(BF16)
        r1 = lf - hi.astype(F32)
        mid = r1.astype(BF16)
        lo = (r1 - mid.astype(F32)).astype(BF16)
        parts = _dot(jnp.concatenate([hi, mid, lo, jnp.zeros_like(lo)], axis=0), tri_scr[...])
        b = parts[0:8] + parts[8:16] + parts[16:24]
        a = ig - b
        m_prev = jnp.concatenate([m_carry] * (lc // LANES), axis=1)
        g = jnp.maximum(_cummax_lanes(a), m_prev)
        m_t = b + g
        b_last = b[:, lc - 1:lc]
        m_new = m_t[:, lc - 1:lc]
        decay = jnp.exp(m_prev - g)
        e_neg_m = jnp.exp(-m_t)
        w_state = jnp.exp(a + (b_last - m_new))
        c_dec = jnp.exp(b_last + m_prev[:, 0:1] - m_new)
        m_carry = jnp.broadcast_to(m_new, m_carry.shape)
        a2 = a * LOG2E
        rows = jnp.concatenate(
            [g * (-LOG2E), decay, e_neg_m, w_state, jnp.zeros((LANES - 4 * SUBLANES, lc), F32)], axis=0)
        cols = rows.T

        for h in range(HEADS):
            if h % 2 == 0:
                pair = slice(h * HEAD_DIM, (h + 2) * HEAD_DIM)
                zq2, zk2, zv2, zo2 = (
                    _dot(hb_c, wheads_ref[:, grp * MLSTM_W + pair.start:grp * MLSTM_W + pair.stop])
                    for grp in range(4))
            half = slice((h % 2) * HEAD_DIM, (h % 2 + 1) * HEAD_DIM)
            qs = (zq2[:, half] * Q_SCALE).astype(BF16)
            kb = zk2[:, half].astype(BF16)
            v = zv2[:, half]
            c_col = cols[:, h:h + 1]
            dec = cols[:, SUBLANES + h:SUBLANES + h + 1]
            enm = cols[:, 2 * SUBLANES + h:2 * SUBLANES + h + 1]
            wst = cols[:, 3 * SUBLANES + h:3 * SUBLANES + h + 1]
            s = _dot_nt(qs, kb)

            for _ in range(mlp_first):
                acc = acc + _mlp_chunk(xn, next_mlp, wup_ref, wdown_ref)
                next_mlp += 1

            dmat = jnp.exp2(jnp.where(causal, c_col + a2[h:h + 1, :], -jnp.inf))
            pm = s * dmat
            row_sum = jnp.sum(pm, axis=1, keepdims=True)
            intra = _dot(pm.astype(BF16), v.astype(BF16))
            c_state = caug[h]
            inter = _dot(qs, c_state.astype(BF16))
            vw = jnp.concatenate([v * wst, jnp.where(lane_id == 0, wst, 0.0)], axis=1).astype(BF16)
            caug[h] = c_dec[h:h + 1, :] * c_state + _dot_tn(kb, vw)

            for _ in range(mlp_per_phase - mlp_first):
                acc = acc + _mlp_chunk(xn, next_mlp, wup_ref, wdown_ref)
                next_mlp += 1

            num = intra + dec * inter[:, 0:HEAD_DIM]
            den = row_sum + dec * inter[:, HEAD_DIM:HEAD_DIM + 1]
            hh = num / jnp.maximum(jnp.abs(den), enm)
            hn = hh * lax.rsqrt(jnp.mean(hh * hh, axis=-1, keepdims=True) + EPS)
            hn = hn * mh_ref[:, h * HEAD_DIM:(h + 1) * HEAD_DIM]
            mix_scr[tok, h * HEAD_DIM:(h + 1) * HEAD_DIM] = (
                hn * jax.nn.sigmoid(zo2[:, half])).astype(BF16)
    m_scr[...] = m_carry
    assert next_mlp == D_FF // FF_CHUNK

    zc = _dot(hb, wconv_ref[...])
    p_sample = jnp.concatenate([ps_ref[0, :, 0, :], jnp.zeros((t - ns, PLE_DIM), F32)], axis=0)
    p_tail = jnp.where(s_id == 0, p_sample, plag_ref[0])
    ple = _dot(p_tail.astype(BF16), wpp_ref[...])

    def tail_rows(rows):
        x2 = y_ref[0, rows, :] + acc[rows, :]
        gate = jax.nn.sigmoid(_dot(_rms(x2, nple_ref[...]).astype(BF16), wpg_ref[...]))
        y_ref[0, rows, :] = _rms(x2 + gate * ple[rows, :], nfin_ref[...])

    tail_rows(slice(0, t // 2))

    cu = zc[:, CONV_CH:2 * CONV_CH] * zc[:, 2 * CONV_CH:3 * CONV_CH]
    cu_buf[SUBLANES:SUBLANES + t, :] = cu
    conv = (cw_ref[0, 0:1, :] * cu_buf[SUBLANES - 2:SUBLANES - 2 + t, :]
            + cw_ref[0, 1:2, :] * cu_buf[SUBLANES - 1:SUBLANES - 1 + t, :]
            + cw_ref[0, 2:3, :] * cu)
    mix_scr[:, MLSTM_W:D_MODEL] = (zc[:, 0:CONV_CH] * conv).astype(BF16)
    cu_buf[0:SUBLANES, :] = cu_buf[t:t + SUBLANES, :]

    tail_rows(slice(t // 2, t))

    @pl.when(s_id == 0)
    def _():
        ys_ref[:, 0, :] = y_ref[0, 0:ns, :]

    @pl.when(jnp.logical_and(is_real, j == tiles_per_seq - 1))
    def _():
        for h in range(HEADS):
            cout_ref[0, h] = caug[h, :, 0:HEAD_DIM]
            nout_ref[0, h:h + 1, :] = caug[h, :, HEAD_DIM:2 * HEAD_DIM].T[0:1, :]
        mout_ref[0] = m_scr[...]
        convout_ref[0] = cu[t - (CONV_WIDTH - 1):t, :]


def _resident(shape):
    return pl.BlockSpec(shape, lambda *_: (0,) * len(shape), pipeline_mode=pl.Buffered(1))


def _prompt_call(x, p, xs, ps, mix_s, wts):
    bsz, seq, _ = x.shape
    t = SEQ_TILE
    assert seq % t == 0 and t % CHUNK == 0 and CHUNK % LANES == 0 and xs.shape[0] <= t
    consts = [wts["w_heads"], wts["w_conv"], wts["w_gt"], wts["b_i"], wts["b_f"], wts["norm_mix"], wts["mh_norm"],
              wts["conv_w"], wts["w_out"], wts["norm_mlp"], wts["w_up"], wts["w_down"], wts["norm_ple"], wts["w_pg"],
              wts["w_pp"], wts["norm_final"]]
    in_smem = {3, 4}
    nj = seq // t
    nt = bsz * nj

    def cur(s):
        c = jnp.minimum(s, nt - 1)
        return c // nj, c % nj

    def lag(s):
        c = jnp.maximum(s - 1, 0)
        return c // nj, c % nj

    in_specs = [pl.BlockSpec((1, t, D_MODEL), lambda s: (*cur(s), 0)),
                pl.BlockSpec((1, t, PLE_DIM), lambda s: (*lag(s), 0))]
    in_specs += [_resident(c.shape) for c in (xs, ps, mix_s)]
    in_specs += [pl.BlockSpec(memory_space=pltpu.SMEM) if i in in_smem else _resident(c.shape)
                 for i, c in enumerate(consts)]
    out_shape = (jax.ShapeDtypeStruct((bsz, seq, D_MODEL), F32),
                 jax.ShapeDtypeStruct(xs.shape, F32),
                 jax.ShapeDtypeStruct((bsz, HEADS, HEAD_DIM, HEAD_DIM), F32),
                 jax.ShapeDtypeStruct((bsz, HEADS, HEAD_DIM), F32),
                 jax.ShapeDtypeStruct((bsz, SUBLANES, LANES), F32),
                 jax.ShapeDtypeStruct((bsz, CONV_WIDTH - 1, CONV_CH), F32))
    out_specs = (pl.BlockSpec((1, t, D_MODEL), lambda s: (*lag(s), 0)),
                 pl.BlockSpec(xs.shape, lambda s: (0, 0, 0)),
                 pl.BlockSpec((1, HEADS, HEAD_DIM, HEAD_DIM), lambda s: (cur(s)[0], 0, 0, 0)),
                 pl.BlockSpec((1, HEADS, HEAD_DIM), lambda s: (cur(s)[0], 0, 0)),
                 pl.BlockSpec((1, SUBLANES, LANES), lambda s: (cur(s)[0], 0, 0)),
                 pl.BlockSpec((1, CONV_WIDTH - 1, CONV_CH), lambda s: (cur(s)[0], 0, 0)))
    scratch = [pltpu.VMEM((HEADS, HEAD_DIM, 2 * HEAD_DIM), F32),
               pltpu.VMEM((SUBLANES, LANES), F32),
               pltpu.VMEM((t + SUBLANES, CONV_CH), F32),
               pltpu.VMEM((t, D_MODEL), BF16),
               pltpu.VMEM((t, D_MODEL), F32),
               pltpu.VMEM((CHUNK, CHUNK), BF16)]
    return pl.pallas_call(
        functools.partial(_prompt_kernel, nj, nt),
        grid=(nt + 1,),
        in_specs=in_specs,
        out_specs=out_specs,
        out_shape=out_shape,
        scratch_shapes=scratch,
        compiler_params=pltpu.CompilerParams(
            dimension_semantics=("arbitrary",),
            vmem_limit_bytes=VMEM_LIMIT_BYTES),
        name="prompt_layer",
    )(x, p, xs, ps, mix_s, *consts)


def _sample_inproj_kernel(x_ref, wheads_ref, wconv_ref, wgc_ref, nmix_ref, zh_ref, zc_ref, g_ref):
    hb = _rms(x_ref[:, 0, :], nmix_ref[...]).astype(BF16)
    zh_ref[...] = _dot(hb, wheads_ref[...])
    zc_ref[...] = _dot(hb, wconv_ref[...])
    g_ref[...] = _dot(hb, wgc_ref[...])


def _sample_state_kernel(zq_ref, zk_ref, zv_ref, zo_ref, zb_ref, zc_ref, zu_ref, g_ref, c0_ref, n0_ref, m0_ref,
                         sc_ref, bi_ref, bf_ref, mh_ref, cw_ref,
                         mix_ref, cnew_ref, nnew_ref, mnew_ref, buf_ref,
                         inter_scr, wv_scr, cd_scr):
    tb = SAMPLE_TILE
    cu = zc_ref[...] * zu_ref[...]
    old0 = sc_ref[0, :, 0, :]
    old1 = sc_ref[0, :, 1, :]
    conv = cw_ref[0, 0:1, :] * old0 + cw_ref[0, 1:2, :] * old1 + cw_ref[0, 2:3, :] * cu
    mix_ref[:, MLSTM_W:D_MODEL] = (zb_ref[...] * conv).astype(BF16)
    buf_ref[0, :, 0, :] = old1
    buf_ref[0, :, 1, :] = cu

    g = g_ref[...]
    ig = g[:, 0:HEADS] + bi_ref[...]
    lf = _log_sigmoid(g[:, HEADS:2 * HEADS] + bf_ref[...])
    m_inter = lf + m0_ref[...]
    m_new = jnp.maximum(m_inter, ig)
    w_in = jnp.exp(ig - m_new)
    c_dec = jnp.exp(m_inter - m_new)
    e_neg_m = jnp.exp(-m_new)
    mnew_ref[...] = m_new

    vs, scores, qns = [], [], []
    for h in range(HEADS):
        hs = slice(h * HEAD_DIM, (h + 1) * HEAD_DIM)
        q = zq_ref[:, hs] * Q_SCALE
        k = zk_ref[:, hs]
        v = zv_ref[:, hs]
        n0 = n0_ref[0, :, h, :]
        wi = w_in[:, h:h + 1]
        cd = c_dec[:, h:h + 1]
        scores.append(jnp.sum(q * k, axis=1, keepdims=True) * wi)
        qns.append(jnp.sum(q * n0, axis=1, keepdims=True))
        wv_scr[:, hs] = wi * v
        cd_scr[:, hs] = jnp.broadcast_to(cd, (tb, HEAD_DIM))
        nnew_ref[0, :, h, :] = cd * n0 + wi * k
        vs.append(v)

    eye = (lax.broadcasted_iota(jnp.int32, (HEAD_DIM, HEAD_DIM), 0)
           == lax.broadcasted_iota(jnp.int32, (HEAD_DIM, HEAD_DIM), 1))

    for i in range(tb):
        row = slice(i, i + 1)
        for h in range(HEADS):
            hs = slice(h * HEAD_DIM, (h + 1) * HEAD_DIM)
            c0 = c0_ref[0, i, h]
            q_rows = jnp.broadcast_to(zq_ref[row, hs] * Q_SCALE, (SUBLANES, HEAD_DIM)).astype(BF16)
            inter_scr[row, hs] = _dot(q_rows, c0.astype(BF16))[0:1, :]
            k_diag = jnp.where(eye, jnp.broadcast_to(zk_ref[row, hs], (HEAD_DIM, HEAD_DIM)), 0.0).astype(BF16)
            v_rows = jnp.broadcast_to(wv_scr[row, hs], (HEAD_DIM, HEAD_DIM)).astype(BF16)
            cd = jnp.broadcast_to(cd_scr[row, hs], (HEAD_DIM, HEAD_DIM))
            cnew_ref[0, i, h] = cd * c0 + _dot(k_diag, v_rows)

    for h in range(HEADS):
        hs = slice(h * HEAD_DIM, (h + 1) * HEAD_DIM)
        cd = c_dec[:, h:h + 1]
        num = scores[h] * vs[h] + cd * inter_scr[:, hs]
        den = scores[h] + cd * qns[h]
        hh = num / jnp.maximum(jnp.abs(den), e_neg_m[:, h:h + 1])
        hn = hh * lax.rsqrt(jnp.mean(hh * hh, axis=-1, keepdims=True) + EPS)
        mix_ref[:, hs] = (hn * mh_ref[:, hs] * jax.nn.sigmoid(zo_ref[:, hs])).astype(BF16)


def _sample_calls(xs, c0, n0, m0, sconv, wts):
    nb = xs.shape[0]
    tb = SAMPLE_TILE
    assert nb % tb == 0 and c0.shape[0] == 1
    zh, zc, g = pl.pallas_call(
        _sample_inproj_kernel,
        out_shape=(jax.ShapeDtypeStruct((nb, 4 * MLSTM_W), F32),
                   jax.ShapeDtypeStruct((nb, 3 * CONV_CH), F32),
                   jax.ShapeDtypeStruct((nb, LANES), F32)),
        compiler_params=pltpu.CompilerParams(vmem_limit_bytes=VMEM_LIMIT_BYTES),
        name="sample_inproj",
    )(xs, wts["w_heads"], wts["w_conv"], wts["w_gc"], wts["norm_mix"])

    def zgroup(grp):
        return pl.BlockSpec((tb, MLSTM_W), lambda i, grp=grp: (i, grp))

    def zconv(grp):
        return pl.BlockSpec((tb, CONV_CH), lambda i, grp=grp: (i, grp))

    row_d = pl.BlockSpec((tb, D_MODEL), lambda i: (i, 0))
    row_h = pl.BlockSpec((tb, HEADS), lambda i: (i, 0))
    state = pl.BlockSpec((1, tb, HEADS, HEAD_DIM, HEAD_DIM), lambda i: (0, i, 0, 0, 0))
    nstate = pl.BlockSpec((1, tb, HEADS, HEAD_DIM), lambda i: (0, i, 0, 0))
    cstate = pl.BlockSpec((1, tb, CONV_WIDTH - 1, CONV_CH), lambda i: (0, i, 0, 0))

    def whole(shape):
        return pl.BlockSpec(shape, lambda i: (0,) * len(shape))

    mix, c_new, n_new, m_new, new_buf = pl.pallas_call(
        _sample_state_kernel,
        grid=(nb // tb,),
        in_specs=[zgroup(0), zgroup(1), zgroup(2), zgroup(3), zconv(0), zconv(1), zconv(2),
                  pl.BlockSpec((tb, LANES), lambda i: (i, 0)),
                  state, nstate, row_h, cstate,
                  whole((1, HEADS)), whole((1, HEADS)), whole((1, MLSTM_W)), whole((1, CONV_WIDTH, CONV_CH))],
        out_specs=(row_d, state, nstate, row_h, cstate),
        out_shape=(jax.ShapeDtypeStruct((nb, D_MODEL), BF16),
                   jax.ShapeDtypeStruct(c0.shape, F32),
                   jax.ShapeDtypeStruct(n0.shape, F32),
                   jax.ShapeDtypeStruct((nb, HEADS), F32),
                   jax.ShapeDtypeStruct(sconv.shape, F32)),
        scratch_shapes=[pltpu.VMEM((tb, MLSTM_W), F32)] * 3,
        compiler_params=pltpu.CompilerParams(dimension_semantics=("arbitrary",)),
        name="sample_state",
    )(zh, zh, zh, zh, zc, zc, zc, g, c0, n0, m0, sconv, wts["b_i"], wts["b_f"], wts["mh_norm"], wts["conv_w"])
    return mix, c_new, n_new, m_new, new_buf


def _weight_prep_kernel(win_ref, wout_ref, wup_ref, wdown_ref, wpg_ref, wpp_ref,
                        heads_ref, conv_ref, gt_ref, gc_ref, out_ref, up_ref, down_ref, pg_ref, pp_ref):
    g0 = 4 * MLSTM_W
    g1 = g0 + 2 * HEADS
    heads_ref[...] = win_ref[0:g0, :].T.astype(BF16)
    conv_ref[...] = win_ref[g1:, :].T.astype(BF16)
    gates = win_ref[g0:g1, :]
    rows = gates.shape[1]
    gt_ref[...] = jnp.concatenate([gates, jnp.zeros((GATE_ROWS - 2 * HEADS, rows), F32)], axis=0).astype(BF16)
    gc_ref[...] = jnp.concatenate([gates, jnp.zeros((LANES - 2 * HEADS, rows), F32)], axis=0).T.astype(BF16)
    out_ref[...] = wout_ref[...].astype(BF16)
    up_ref[...] = wup_ref[...].astype(BF16)
    down_ref[...] = wdown_ref[...].astype(BF16)
    pg_ref[...] = wpg_ref[...].astype(BF16)
    pp_ref[...] = wpp_ref[...].astype(BF16)


def _prepare_weights(norm_mix, w_in, b_gate_i, b_gate_f, mh_norm, conv_w, w_out, norm_mlp, w_up, w_down,
                     norm_ple, w_ple_gate, w_ple_proj, norm_final):
    w_in_t = jnp.swapaxes(w_in, 0, 1)
    srcs = (w_in_t, w_out, w_up, w_down, w_ple_gate, w_ple_proj)
    n_in = w_in.shape[1]
    outs = ((D_MODEL, 4 * MLSTM_W), (D_MODEL, 3 * CONV_CH), (GATE_ROWS, D_MODEL), (D_MODEL, LANES),
            w_out.shape, w_up.shape, w_down.shape, w_ple_gate.shape, w_ple_proj.shape)
    steps = PREP_STEPS

    def rows(shape):
        return pl.BlockSpec((shape[0] // steps, shape[1]), lambda i: (i, 0))

    out_specs = [rows(o) for o in outs]
    out_specs[2] = pl.BlockSpec((GATE_ROWS, D_MODEL // steps), lambda i: (0, i))
    assert n_in == 4 * MLSTM_W + 2 * HEADS + 3 * CONV_CH
    w_heads, w_conv, w_gt, w_gc, w_out_b, w_up_b, w_down_b, w_pg_b, w_pp_b = pl.pallas_call(
        _weight_prep_kernel,
        grid=(steps,),
        in_specs=[pl.BlockSpec((n_in, D_MODEL // steps), lambda i: (0, i))] + [rows(a.shape) for a in srcs[1:]],
        out_specs=out_specs,
        out_shape=[jax.ShapeDtypeStruct(o, BF16) for o in outs],
        compiler_params=pltpu.CompilerParams(dimension_semantics=("arbitrary",),
                                             vmem_limit_bytes=VMEM_LIMIT_BYTES),
        name="weight_prep",
    )(*srcs)
    return dict(
        w_heads=w_heads,
        w_conv=w_conv,
        w_gt=w_gt,
        w_gc=w_gc,
        b_i=b_gate_i.reshape(1, HEADS),
        b_f=b_gate_f.reshape(1, HEADS),
        norm_mix=norm_mix.reshape(1, D_MODEL),
        mh_norm=mh_norm.reshape(1, MLSTM_W),
        conv_w=conv_w,
        w_out=w_out_b,
        norm_mlp=norm_mlp.reshape(1, D_MODEL),
        w_up=w_up_b,
        w_down=w_down_b,
        norm_ple=norm_ple.reshape(1, D_MODEL),
        w_pg=w_pg_b,
        w_pp=w_pp_b,
        norm_final=norm_final.reshape(1, D_MODEL),
    )


def kernel(x_prompt, x_sample, state_mlstm_C, state_mlstm_n, state_mlstm_m, state_conv, p_prompt, p_sample,
           norm_mix, w_in, b_gate_i, b_gate_f, mh_norm, conv_w, w_out, norm_mlp, w_up, w_down, norm_ple,
           w_ple_gate, w_ple_proj, norm_final):
    assert norm_mix.shape[0] == 1, "single-layer trunk"
    wts = _prepare_weights(norm_mix[0], w_in[0], b_gate_i[0], b_gate_f[0], mh_norm[0], conv_w, w_out[0],
                           norm_mlp[0], w_up[0], w_down[0], norm_ple[0], w_ple_gate[0], w_ple_proj[0], norm_final)

    nb = x_sample.shape[0]
    xs = x_sample
    mix_s, c_new, n_new, m_new, new_buf = _sample_calls(
        xs, state_mlstm_C, state_mlstm_n, state_mlstm_m[0], state_conv, wts)

    y_prompt, ys, prompt_c, prompt_n, m_rows, conv_tail = _prompt_call(
        x_prompt, p_prompt[0], xs, p_sample, mix_s, wts)
    prompt_m = m_rows[:, 0:HEADS, 0]

    return (y_prompt, ys, prompt_c[None], prompt_n[None], prompt_m[None], conv_tail[None],
            c_new, n_new, m_new[None], new_buf)
```

```python
import functools
import math

import jax
import jax.numpy as jnp
from jax import lax
from jax.experimental import pallas as pl
from jax.experimental.pallas import tpu as pltpu

F32 = jnp.float32
BF16 = jnp.bfloat16

D_MODEL = 1024
HEADS = 4
HEAD_DIM = 128
MLSTM_W = HEADS * HEAD_DIM
CONV_CH = D_MODEL - MLSTM_W
CONV_WIDTH = 3
D_FF = 4 * D_MODEL
PLE_DIM = 256
EPS = 1e-6
M_INIT = -1e30
Q_SCALE = HEAD_DIM ** -0.5
LOG2E = math.log2(math.e)

GATE_ROWS = 16
PREP_STEPS = 8
LANES = 128
SUBLANES = 8
VMEM_LIMIT_BYTES = 60000 * 1024

SEQ_TILE = 512
CHUNK = 256
FF_CHUNK = 512
SAMPLE_TILE = 16


def _dot(a, b):
    return jnp.dot(a, b, preferred_element_type=F32)


def _dot_nt(a, b):
    return lax.dot_general(a, b, (((1,), (1,)), ((), ())), preferred_element_type=F32)


def _dot_tn(a, b):
    return lax.dot_general(a, b, (((0,), (0,)), ((), ())), preferred_element_type=F32)


def _rms(x, g):
    y = x * lax.rsqrt(jnp.mean(x * x, axis=-1, keepdims=True) + EPS)
    return y * g


def _log_sigmoid(x):
    return jnp.minimum(x, 0.0) - jnp.log1p(jnp.exp(-jnp.abs(x)))


def _mlp_chunk(xn_bf16, c, wup_ref, wdown_ref):
    cols = slice(c * FF_CHUNK, (c + 1) * FF_CHUNK)
    hf = jnp.maximum(_dot(xn_bf16, wup_ref[:, cols]), 0.0)
    return _dot((hf * hf).astype(BF16), wdown_ref[cols, :])


def _cummax_lanes(a):
    n = a.shape[1]
    lane = lax.broadcasted_iota(jnp.int32, a.shape, 1)
    d = 1
    while d < n:
        shifted = pltpu.roll(a, d, axis=1)
        a = jnp.maximum(a, jnp.where(lane >= d, shifted, -jnp.inf))
        d *= 2
    return a


def _prompt_kernel(tiles_per_seq, num_tiles,
                   x_ref, plag_ref, xs_ref, ps_ref, mixs_ref,
                   wheads_ref, wconv_ref, wgt_ref, bi_ref, bf_ref, nmix_ref, mh_ref, cw_ref,
                   wout_ref, nmlp_ref, wup_ref, wdown_ref, nple_ref, wpg_ref, wpp_ref, nfin_ref,
                   y_ref, ys_ref, cout_ref, nout_ref, mout_ref, convout_ref,
                   caug, m_scr, cu_buf, mix_scr, xres_scr, tri_scr):
    t = SEQ_TILE
    lc = CHUNK
    ns = xs_ref.shape[0]
    s_id = pl.program_id(0)
    is_real = s_id < num_tiles
    j = lax.rem(jnp.minimum(s_id, num_tiles - 1), tiles_per_seq)

    @pl.when(s_id == 0)
    def _():
        mix_scr[0:ns, :] = mixs_ref[...]
        mix_scr[ns:t, :] = jnp.zeros((t - ns, D_MODEL), BF16)
        xres_scr[0:ns, :] = xs_ref[:, 0, :]
        xres_scr[ns:t, :] = jnp.zeros((t - ns, D_MODEL), F32)
        tri_scr[...] = (lax.broadcasted_iota(jnp.int32, (lc, lc), 0)
                        <= lax.broadcasted_iota(jnp.int32, (lc, lc), 1)).astype(F32).astype(BF16)

    @pl.when(j == 0)
    def _():
        caug[...] = jnp.zeros_like(caug)
        m_scr[...] = jnp.full(m_scr.shape, M_INIT, F32)
        cu_buf[0:SUBLANES, :] = jnp.zeros((SUBLANES, CONV_CH), F32)

    x1 = xres_scr[...] + _dot(mix_scr[...], wout_ref[...])
    xn = _rms(x1, nmlp_ref[...]).astype(BF16)

    x = x_ref[0]
    hb = _rms(x, nmix_ref[...]).astype(BF16)
    xres_scr[...] = x
    gt = _dot_nt(wgt_ref[...], hb)

    row_id = lax.broadcasted_iota(jnp.int32, (lc, lc), 0)
    col_id = lax.broadcasted_iota(jnp.int32, (lc, lc), 1)
    causal = col_id <= row_id
    lane_id = lax.broadcasted_iota(jnp.int32, (lc, HEAD_DIM), 1)

    acc = jnp.zeros((t, D_MODEL), F32)
    n_phases = (t // lc) * HEADS
    mlp_per_phase = D_FF // FF_CHUNK // n_phases
    mlp_first = (mlp_per_phase + 1) // 2
    next_mlp = 0
    m_carry = m_scr[...]
    sub_id = lax.broadcasted_iota(jnp.int32, (SUBLANES, lc), 0)
    bias_i = jnp.zeros((SUBLANES, lc), F32)
    bias_f = jnp.zeros((SUBLANES, lc), F32)
    for h in range(HEADS):
        bias_i = jnp.where(sub_id == h, bi_ref[0, h], bias_i)
        bias_f = jnp.where(sub_id == h, bf_ref[0, h], bias_f)
    for c in range(t // lc):
        tok = slice(c * lc, (c + 1) * lc)
        hb_c = hb[tok, :]
        g8 = gt[0:SUBLANES, tok]
        ig = g8 + bias_i
        lf = _log_sigmoid(pltpu.roll(g8, HEADS, axis=0) + bias_f)
        hi = lf.astype(BF16)
        r1 = lf - hi.astype(F32)
        mid = r1.astype(BF16)
        lo = (r1 - mid.astype(F32)).astype(BF16)
        parts = _dot(jnp.concatenate([hi, mid, lo, jnp.zeros_like(lo)], axis=0), tri_scr[...])
        b = parts[0:8] + parts[8:16] + parts[16:24]
        a = ig - b
        m_prev = jnp.concatenate([m_carry] * (lc // LANES), axis=1)
        g = jnp.maximum(_cummax_lanes(a), m_prev)
        m_t = b + g
        b_last = b[:, lc - 1:lc]
        m_new = m_t[:, lc - 1:lc]
        decay = jnp.exp(m_prev - g)
        e_neg_m = jnp.exp(-m_t)
        w_state = jnp.exp(a + (b_last - m_new))
        c_dec = jnp.exp(b_last + m_prev[:, 0:1] - m_new)
        m_carry = jnp.broadcast_to(m_new, m_carry.shape)
        a2 = a * LOG2E
        rows = jnp.concatenate(
            [g * (-LOG2E), decay, e_neg_m, w_state, jnp.zeros((LANES - 4 * SUBLANES, lc), F32)], axis=0)
        cols = rows.T

        for h in range(HEADS):
            if h % 2 == 0:
                pair = slice(h * HEAD_DIM, (h + 2) * HEAD_DIM)
                zq2, zk2, zv2, zo2 = (
                    _dot(hb_c, wheads_ref[:, grp * MLSTM_W + pair.start:grp * MLSTM_W + pair.stop])
                    for grp in range(4))
            half = slice((h % 2) * HEAD_DIM, (h % 2 + 1) * HEAD_DIM)
            qs = (zq2[:, half] * Q_SCALE).astype(BF16)
            kb = zk2[:, half].astype(BF16)
            v = zv2[:, half]
            c_col = cols[:, h:h + 1]
            dec = cols[:, SUBLANES + h:SUBLANES + h + 1]
            enm = cols[:, 2 * SUBLANES + h:2 * SUBLANES + h + 1]
            wst = cols[:, 3 * SUBLANES + h:3 * SUBLANES + h + 1]
            s = _dot_nt(qs, kb)

            for _ in range(mlp_first):
                acc = acc + _mlp_chunk(xn, next_mlp, wup_ref, wdown_ref)
                next_mlp += 1

            dmat = jnp.exp2(jnp.where(causal, c_col + a2[h:h + 1, :], -jnp.inf))
            pm = s * dmat
            row_sum = jnp.sum(pm, axis=1, keepdims=True)
            intra = _dot(pm.astype(BF16), v.astype(BF16))
            c_state = caug[h]
            inter = _dot(qs, c_state.astype(BF16))
            vw = jnp.concatenate([v * wst, jnp.where(lane_id == 0, wst, 0.0)], axis=1).astype(BF16)
            caug[h] = c_dec[h:h + 1, :] * c_state + _dot_tn(kb, vw)

            for _ in range(mlp_per_phase - mlp_first):
                acc = acc + _mlp_chunk(xn, next_mlp, wup_ref, wdown_ref)
                next_mlp += 1

            num = intra + dec * inter[:, 0:HEAD_DIM]
            den = row_sum + dec * inter[:, HEAD_DIM:HEAD_DIM + 1]
            hh = num / jnp.maximum(jnp.abs(den), enm)
            hn = hh * lax.rsqrt(jnp.mean(hh * hh, axis=-1, keepdims=True) + EPS)
            hn = hn * mh_ref[:, h * HEAD_DIM:(h + 1) * HEAD_DIM]
            mix_scr[tok, h * HEAD_DIM:(h + 1) * HEAD_DIM] = (
                hn * jax.nn.sigmoid(zo2[:, half])).astype(BF16)
    m_scr[...] = m_carry
    assert next_mlp == D_FF // FF_CHUNK

    zc = _dot(hb, wconv_ref[...])
    p_sample = jnp.concatenate([ps_ref[0, :, 0, :], jnp.zeros((t - ns, PLE_DIM), F32)], axis=0)
    p_tail = jnp.where(s_id == 0, p_sample, plag_ref[0])
    ple = _dot(p_tail.astype(BF16), wpp_ref[...])

    def tail_rows(rows):
        x2 = x1[rows, :] + acc[rows, :]
        gate = jax.nn.sigmoid(_dot(_rms(x2, nple_ref[...]).astype(BF16), wpg_ref[...]))
        y_ref[0, rows, :] = _rms(x2 + gate * ple[rows, :], nfin_ref[...])

    tail_rows(slice(0, t // 2))

    cu = zc[:, CONV_CH:2 * CONV_CH] * zc[:, 2 * CONV_CH:3 * CONV_CH]
    cu_buf[SUBLANES:SUBLANES + t, :] = cu
    conv = (cw_ref[0, 0:1, :] * cu_buf[SUBLANES - 2:SUBLANES - 2 + t, :]
            + cw_ref[0, 1:2, :] * cu_buf[SUBLANES - 1:SUBLANES - 1 + t, :]
            + cw_ref[0, 2:3, :] * cu)
    mix_scr[:, MLSTM_W:D_MODEL] = (zc[:, 0:CONV_CH] * conv).astype(BF16)
    cu_buf[0:SUBLANES, :] = cu_buf[t:t + SUBLANES, :]

    tail_rows(slice(t // 2, t))

    @pl.when(s_id == 0)
    def _():
        ys_ref[:, 0, :] = y_ref[0, 0:ns, :]

    @pl.when(jnp.logical_and(is_real, j == tiles_per_seq - 1))
    def _():
        for h in range(HEADS):
            cout_ref[0, h] = caug[h, :, 0:HEAD_DIM]
            nout_ref[0, h:h + 1, :] = caug[h, :, HEAD_DIM:2 * HEAD_DIM].T[0:1, :]
        mout_ref[0] = m_scr[...]
        convout_ref[0] = cu[t - (CONV_WIDTH - 1):t, :]


def _resident(shape):
    return pl.BlockSpec(shape, lambda *_: (0,) * len(shape), pipeline_mode=pl.Buffered(1))


def _prompt_call(x, p, xs, ps, mix_s, wts):
    bsz, seq, _ = x.shape
    t = SEQ_TILE
    assert seq % t == 0 and t % CHUNK == 0 and CHUNK % LANES == 0 and xs.shape[0] <= t
    consts = [wts["w_heads"], wts["w_conv"], wts["w_gt"], wts["b_i"], wts["b_f"], wts["norm_mix"], wts["mh_norm"],
              wts["conv_w"], wts["w_out"], wts["norm_mlp"], wts["w_up"], wts["w_down"], wts["norm_ple"], wts["w_pg"],
              wts["w_pp"], wts["norm_final"]]
    in_smem = {3, 4}
    nj = seq // t
    nt = bsz * nj

    def cur(s):
        c = jnp.minimum(s, nt - 1)
        return c // nj, c % nj

    def lag(s):
        c = jnp.maximum(s - 1, 0)
        return c // nj, c % nj

    in_specs = [pl.BlockSpec((1, t, D_MODEL), lambda s: (*cur(s), 0)),
                pl.BlockSpec((1, t, PLE_DIM), lambda s: (*lag(s), 0))]
    in_specs += [_resident(c.shape) for c in (xs, ps, mix_s)]
    in_specs += [pl.BlockSpec(memory_space=pltpu.SMEM) if i in in_smem else _resident(c.shape)
                 for i, c in enumerate(consts)]
    out_shape = (jax.ShapeDtypeStruct((bsz, seq, D_MODEL), F32),
                 jax.ShapeDtypeStruct(xs.shape, F32),
                 jax.ShapeDtypeStruct((bsz, HEADS, HEAD_DIM, HEAD_DIM), F32),
                 jax.ShapeDtypeStruct((bsz, HEADS, HEAD_DIM), F32),
                 jax.ShapeDtypeStruct((bsz, SUBLANES, LANES), F32),
                 jax.ShapeDtypeStruct((bsz, CONV_WIDTH - 1, CONV_CH), F32))
    out_specs = (pl.BlockSpec((1, t, D_MODEL), lambda s: (*lag(s), 0)),
                 pl.BlockSpec(xs.shape, lambda s: (0, 0, 0)),
                 pl.BlockSpec((1, HEADS, HEAD_DIM, HEAD_DIM), lambda s: (cur(s)[0], 0, 0, 0)),
                 pl.BlockSpec((1, HEADS, HEAD_DIM), lambda s: (cur(s)[0], 0, 0)),
                 pl.BlockSpec((1, SUBLANES, LANES), lambda s: (cur(s)[0], 0, 0)),
                 pl.BlockSpec((1, CONV_WIDTH - 1, CONV_CH), lambda s: (cur(s)[0], 0, 0)))
    scratch = [pltpu.VMEM((HEADS, HEAD_DIM, 2 * HEAD_DIM), F32),
               pltpu.VMEM((SUBLANES, LANES), F32),
               pltpu.VMEM((t + SUBLANES, CONV_CH), F32),
               pltpu.VMEM((t, D_MODEL), BF16),
               pltpu.VMEM((t, D_MODEL), F32),
               pltpu.VMEM((CHUNK, CHUNK), BF16)]
    return pl.pallas_call(
        functools.partial(_prompt_kernel, nj, nt),
        grid=(nt + 1,),
        in_specs=in_specs,
        out_specs=out_specs,
        out_shape=out_shape,
        scratch_shapes=scratch,
        compiler_params=pltpu.CompilerParams(
            dimension_semantics=("arbitrary",),
            vmem_limit_bytes=VMEM_LIMIT_BYTES),
        name="prompt_layer",
    )(x, p, xs, ps, mix_s, *consts)


def _sample_inproj_kernel(x_ref, wheads_ref, wconv_ref, wgc_ref, nmix_ref, zh_ref, zc_ref, g_ref):
    hb = _rms(x_ref[:, 0, :], nmix_ref[...]).astype(BF16)
    zh_ref[...] = _dot(hb, wheads_ref[...])
    zc_ref[...] = _dot(hb, wconv_ref[...])
    g_ref[...] = _dot(hb, wgc_ref[...])


def _sample_state_kernel(zq_ref, zk_ref, zv_ref, zo_ref, zb_ref, zc_ref, zu_ref, g_ref, c0_ref, n0_ref, m0_ref,
                         sc_ref, bi_ref, bf_ref, mh_ref, cw_ref,
                         mix_ref, cnew_ref, nnew_ref, mnew_ref, buf_ref,
                         inter_scr, wv_scr, cd_scr):
    tb = SAMPLE_TILE
    cu = zc_ref[...] * zu_ref[...]
    old0 = sc_ref[0, :, 0, :]
    old1 = sc_ref[0, :, 1, :]
    conv = cw_ref[0, 0:1, :] * old0 + cw_ref[0, 1:2, :] * old1 + cw_ref[0, 2:3, :] * cu
    mix_ref[:, MLSTM_W:D_MODEL] = (zb_ref[...] * conv).astype(BF16)
    buf_ref[0, :, 0, :] = old1
    buf_ref[0, :, 1, :] = cu

    g = g_ref[...]
    ig = g[:, 0:HEADS] + bi_ref[...]
    lf = _log_sigmoid(g[:, HEADS:2 * HEADS] + bf_ref[...])
    m_inter = lf + m0_ref[...]
    m_new = jnp.maximum(m_inter, ig)
    w_in = jnp.exp(ig - m_new)
    c_dec = jnp.exp(m_inter - m_new)
    e_neg_m = jnp.exp(-m_new)
    mnew_ref[...] = m_new

    vs, scores, qns = [], [], []
    for h in range(HEADS):
        hs = slice(h * HEAD_DIM, (h + 1) * HEAD_DIM)
        q = zq_ref[:, hs] * Q_SCALE
        k = zk_ref[:, hs]
        v = zv_ref[:, hs]
        n0 = n0_ref[0, :, h, :]
        wi = w_in[:, h:h + 1]
        cd = c_dec[:, h:h + 1]
        scores.append(jnp.sum(q * k, axis=1, keepdims=True) * wi)
        qns.append(jnp.sum(q * n0, axis=1, keepdims=True))
        wv_scr[:, hs] = wi * v
        cd_scr[:, hs] = jnp.broadcast_to(cd, (tb, HEAD_DIM))
        nnew_ref[0, :, h, :] = cd * n0 + wi * k
        vs.append(v)

    eye = (lax.broadcasted_iota(jnp.int32, (HEAD_DIM, HEAD_DIM), 0)
           == lax.broadcasted_iota(jnp.int32, (HEAD_DIM, HEAD_DIM), 1))

    for i in range(tb):
        row = slice(i, i + 1)
        for h in range(HEADS):
            hs = slice(h * HEAD_DIM, (h + 1) * HEAD_DIM)
            c0 = c0_ref[0, i, h]
            q_rows = jnp.broadcast_to(zq_ref[row, hs] * Q_SCALE, (SUBLANES, HEAD_DIM)).astype(BF16)
            inter_scr[row, hs] = _dot(q_rows, c0.astype(BF16))[0:1, :]
            k_diag = jnp.where(eye, jnp.broadcast_to(zk_ref[row, hs], (HEAD_DIM, HEAD_DIM)), 0.0).astype(BF16)
            v_rows = jnp.broadcast_to(wv_scr[row, hs], (HEAD_DIM, HEAD_DIM)).astype(BF16)
            cd = jnp.broadcast_to(cd_scr[row, hs], (HEAD_DIM, HEAD_DIM))
            cnew_ref[0, i, h] = cd * c0 + _dot(k_diag, v_rows)

    for h in range(HEADS):
        hs = slice(h * HEAD_DIM, (h + 1) * HEAD_DIM)
        cd = c_dec[:, h:h + 1]
        num = scores[h] * vs[h] + cd * inter_scr[:, hs]
        den = scores[h] + cd * qns[h]
        hh = num / jnp.maximum(jnp.abs(den), e_neg_m[:, h:h + 1])
        hn = hh * lax.rsqrt(jnp.mean(hh * hh, axis=-1, keepdims=True) + EPS)
        mix_ref[:, hs] = (hn * mh_ref[:, hs] * jax.nn.sigmoid(zo_ref[:, hs])).astype(BF16)


def _sample_calls(xs, c0, n0, m0, sconv, wts):
    nb = xs.shape[0]
    tb = SAMPLE_TILE
    assert nb % tb == 0 and c0.shape[0] == 1
    zh, zc, g = pl.pallas_call(
        _sample_inproj_kernel,
        out_shape=(jax.ShapeDtypeStruct((nb, 4 * MLSTM_W), F32),
                   jax.ShapeDtypeStruct((nb, 3 * CONV_CH), F32),
                   jax.ShapeDtypeStruct((nb, LANES), F32)),
        compiler_params=pltpu.CompilerParams(vmem_limit_bytes=VMEM_LIMIT_BYTES),
        name="sample_inproj",
    )(xs, wts["w_heads"], wts["w_conv"], wts["w_gc"], wts["norm_mix"])

    def zgroup(grp):
        return pl.BlockSpec((tb, MLSTM_W), lambda i, grp=grp: (i, grp))

    def zconv(grp):
        return pl.BlockSpec((tb, CONV_CH), lambda i, grp=grp: (i, grp))

    row_d = pl.BlockSpec((tb, D_MODEL), lambda i: (i, 0))
    row_h = pl.BlockSpec((tb, HEADS), lambda i: (i, 0))
    state = pl.BlockSpec((1, tb, HEADS, HEAD_DIM, HEAD_DIM), lambda i: (0, i, 0, 0, 0))
    nstate = pl.BlockSpec((1, tb, HEADS, HEAD_DIM), lambda i: (0, i, 0, 0))
    cstate = pl.BlockSpec((1, tb, CONV_WIDTH - 1, CONV_CH), lambda i: (0, i, 0, 0))

    def whole(shape):
        return pl.BlockSpec(shape, lambda i: (0,) * len(shape))

    mix, c_new, n_new, m_new, new_buf = pl.pallas_call(
        _sample_state_kernel,
        grid=(nb // tb,),
        in_specs=[zgroup(0), zgroup(1), zgroup(2), zgroup(3), zconv(0), zconv(1), zconv(2),
                  pl.BlockSpec((tb, LANES), lambda i: (i, 0)),
                  state, nstate, row_h, cstate,
                  whole((1, HEADS)), whole((1, HEADS)), whole((1, MLSTM_W)), whole((1, CONV_WIDTH, CONV_CH))],
        out_specs=(row_d, state, nstate, row_h, cstate),
        out_shape=(jax.ShapeDtypeStruct((nb, D_MODEL), BF16),
                   jax.ShapeDtypeStruct(c0.shape, F32),
                   jax.ShapeDtypeStruct(n0.shape, F32),
                   jax.ShapeDtypeStruct((nb, HEADS), F32),
                   jax.ShapeDtypeStruct(sconv.shape, F32)),
        scratch_shapes=[pltpu.VMEM((tb, MLSTM_W), F32)] * 3,
        compiler_params=pltpu.CompilerParams(dimension_semantics=("arbitrary",)),
        name="sample_state",
    )(zh, zh, zh, zh, zc, zc, zc, g, c0, n0, m0, sconv, wts["b_i"], wts["b_f"], wts["mh_norm"], wts["conv_w"])
    return mix, c_new, n_new, m_new, new_buf


def _weight_prep_kernel(win_ref, wout_ref, wup_ref, wdown_ref, wpg_ref, wpp_ref,
                        heads_ref, conv_ref, gt_ref, gc_ref, out_ref, up_ref, down_ref, pg_ref, pp_ref):
    g0 = 4 * MLSTM_W
    g1 = g0 + 2 * HEADS
    heads_ref[...] = win_ref[0:g0, :].T.astype(BF16)
    conv_ref[...] = win_ref[g1:, :].T.astype(BF16)
    gates = win_ref[g0:g1, :]
    rows = gates.shape[1]
    gt_ref[...] = jnp.concatenate([gates, jnp.zeros((GATE_ROWS - 2 * HEADS, rows), F32)], axis=0).astype(BF16)
    gc_ref[...] = jnp.concatenate([gates, jnp.zeros((LANES - 2 * HEADS, rows), F32)], axis=0).T.astype(BF16)
    out_ref[...] = wout_ref[...].astype(BF16)
    up_ref[...] = wup_ref[...].astype(BF16)
    down_ref[...] = wdown_ref[...].astype(BF16)
    pg_ref[...] = wpg_ref[...].astype(BF16)
    pp_ref[...] = wpp_ref[...].astype(BF16)


def _prepare_weights(norm_mix, w_in, b_gate_i, b_gate_f, mh_norm, conv_w, w_out, norm_mlp, w_up, w_down,
                     norm_ple, w_ple_gate, w_ple_proj, norm_final):
    w_in_t = jnp.swapaxes(w_in, 0, 1)
    srcs = (w_in_t, w_out, w_up, w_down, w_ple_gate, w_ple_proj)
    n_in = w_in.shape[1]
    outs = ((D_MODEL, 4 * MLSTM_W), (D_MODEL, 3 * CONV_CH), (GATE_ROWS, D_MODEL), (D_MODEL, LANES),
            w_out.shape, w_up.shape, w_down.shape, w_ple_gate.shape, w_ple_proj.shape)
    steps = PREP_STEPS

    def rows(shape):
        return pl.BlockSpec((shape[0] // steps, shape[1]), lambda i: (i, 0))

    out_specs = [rows(o) for o in outs]
    out_specs[2] = pl.BlockSpec((GATE_ROWS, D_MODEL // steps), lambda i: (0, i))
    assert n_in == 4 * MLSTM_W + 2 * HEADS + 3 * CONV_CH
    w_heads, w_conv, w_gt, w_gc, w_out_b, w_up_b, w_down_b, w_pg_b, w_pp_b = pl.pallas_call(
        _weight_prep_kernel,
        grid=(steps,),
        in_specs=[pl.BlockSpec((n_in, D_MODEL // steps), lambda i: (0, i))] + [rows(a.shape) for a in srcs[1:]],
        out_specs=out_specs,
        out_shape=[jax.ShapeDtypeStruct(o, BF16) for o in outs],
        compiler_params=pltpu.CompilerParams(dimension_semantics=("arbitrary",),
                                             vmem_limit_bytes=VMEM_LIMIT_BYTES),
        name="weight_prep",
    )(*srcs)
    return dict(
        w_heads=w_heads,
        w_conv=w_conv,
        w_gt=w_gt,
        w_gc=w_gc,
        b_i=b_gate_i.reshape(1, HEADS),
        b_f=b_gate_f.reshape(1, HEADS),
        norm_mix=norm_mix.reshape(1, D_MODEL),
        mh_norm=mh_norm.reshape(1, MLSTM_W),
        conv_w=conv_w,
        w_out=w_out_b,
        norm_mlp=norm_mlp.reshape(1, D_MODEL),
        w_up=w_up_b,
        w_down=w_down_b,
        norm_ple=norm_ple.reshape(1, D_MODEL),
        w_pg=w_pg_b,
        w_pp=w_pp_b,
        norm_final=norm_final.reshape(1, D_MODEL),
    )


def kernel(x_prompt, x_sample, state_mlstm_C, state_mlstm_n, state_mlstm_m, state_conv, p_prompt, p_sample,
           norm_mix, w_in, b_gate_i, b_gate_f, mh_norm, conv_w, w_out, norm_mlp, w_up, w_down, norm_ple,
           w_ple_gate, w_ple_proj, norm_final):
    assert norm_mix.shape[0] == 1, "single-layer trunk"
    wts = _prepare_weights(norm_mix[0], w_in[0], b_gate_i[0], b_gate_f[0], mh_norm[0], conv_w, w_out[0],
                           norm_mlp[0], w_up[0], w_down[0], norm_ple[0], w_ple_gate[0], w_ple_proj[0], norm_final)

    nb = x_sample.shape[0]
    xs = x_sample
    mix_s, c_new, n_new, m_new, new_buf = _sample_calls(
        xs, state_mlstm_C, state_mlstm_n, state_mlstm_m[0], state_conv, wts)

    y_prompt, ys, prompt_c, prompt_n, m_rows, conv_tail = _prompt_call(
        x_prompt, p_prompt[0], xs, p_sample, mix_s, wts)
    prompt_m = m_rows[:, 0:HEADS, 0]

    return (y_prompt, ys, prompt_c[None], prompt_n[None], prompt_m[None], conv_tail[None],
            c_new, n_new, m_new[None], new_buf)
```

```python
import functools
import math

import jax
import jax.numpy as jnp
from jax import lax
from jax.experimental import pallas as pl
from jax.experimental.pallas import tpu as pltpu

F32 = jnp.float32
BF16 = jnp.bfloat16

D_MODEL = 1024
HEADS = 4
HEAD_DIM = 128
MLSTM_W = HEADS * HEAD_DIM
CONV_CH = D_MODEL - MLSTM_W
CONV_WIDTH = 3
D_FF = 4 * D_MODEL
PLE_DIM = 256
EPS = 1e-6
M_INIT = -1e30
Q_SCALE = HEAD_DIM ** -0.5
LOG2E = math.log2(math.e)

GATE_ROWS = 16
PREP_STEPS = 8
LANES = 128
SUBLANES = 8
VMEM_LIMIT_BYTES = 60000 * 1024

SEQ_TILE = 512
CHUNK = 256
FF_CHUNK = 512
SAMPLE_TILE = 16


def _dot(a, b):
    return jnp.dot(a, b, preferred_element_type=F32)


def _dot_nt(a, b):
    return lax.dot_general(a, b, (((1,), (1,)), ((), ())), preferred_element_type=F32)


def _dot_tn(a, b):
    return lax.dot_general(a, b, (((0,), (0,)), ((), ())), preferred_element_type=F32)


def _rms(x, g):
    y = x * lax.rsqrt(jnp.mean(x * x, axis=-1, keepdims=True) + EPS)
    return y * g


def _log_sigmoid(x):
    return jnp.minimum(x, 0.0) - jnp.log1p(jnp.exp(-jnp.abs(x)))


def _mlp_chunk(xn_bf16, c, wup_ref, wdown_ref):
    cols = slice(c * FF_CHUNK, (c + 1) * FF_CHUNK)
    hf = jnp.maximum(_dot(xn_bf16, wup_ref[:, cols]), 0.0)
    return _dot((hf * hf).astype(BF16), wdown_ref[cols, :])


def _cummax_lanes(a):
    n = a.shape[1]
    lane = lax.broadcasted_iota(jnp.int32, a.shape, 1)
    d = 1
    while d < n:
        shifted = pltpu.roll(a, d, axis=1)
        a = jnp.maximum(a, jnp.where(lane >= d, shifted, -jnp.inf))
        d *= 2
    return a


def _prompt_kernel(tiles_per_seq, num_tiles,
                   x_ref, plag_ref, xs_ref, ps_ref, mixs_ref,
                   wheads_ref, wconv_ref, wgt_ref, bi_ref, bf_ref, nmix_ref, mh_ref, cw_ref,
                   wout_ref, nmlp_ref, wup_ref, wdown_ref, nple_ref, wpg_ref, wpp_ref, nfin_ref,
                   y_ref, ys_ref, cout_ref, nout_ref, mout_ref, convout_ref,
                   caug, m_scr, cu_buf, mix_scr, xres_scr, tri_scr, ps_scr):
    t = SEQ_TILE
    lc = CHUNK
    ns = xs_ref.shape[0]
    s_id = pl.program_id(0)
    is_real = s_id < num_tiles
    j = lax.rem(jnp.minimum(s_id, num_tiles - 1), tiles_per_seq)

    @pl.when(s_id == 0)
    def _():
        mix_scr[0:ns, :] = mixs_ref[...]
        mix_scr[ns:t, :] = jnp.zeros((t - ns, D_MODEL), BF16)
        xres_scr[0:ns, :] = xs_ref[:, 0, :]
        xres_scr[ns:t, :] = jnp.zeros((t - ns, D_MODEL), F32)
        ps_scr[0:ns, :] = ps_ref[0, :, 0, :]
        ps_scr[ns:t, :] = jnp.zeros((t - ns, PLE_DIM), F32)
        tri_scr[...] = (lax.broadcasted_iota(jnp.int32, (lc, lc), 0)
                        <= lax.broadcasted_iota(jnp.int32, (lc, lc), 1)).astype(F32).astype(BF16)

    @pl.when(j == 0)
    def _():
        caug[...] = jnp.zeros_like(caug)
        m_scr[...] = jnp.full(m_scr.shape, M_INIT, F32)
        cu_buf[0:SUBLANES, :] = jnp.zeros((SUBLANES, CONV_CH), F32)

    x1 = xres_scr[...] + _dot(mix_scr[...], wout_ref[...])
    xn = _rms(x1, nmlp_ref[...]).astype(BF16)

    x = x_ref[0]
    hb = _rms(x, nmix_ref[...]).astype(BF16)
    xres_scr[...] = x
    gt = _dot_nt(wgt_ref[...], hb)

    row_id = lax.broadcasted_iota(jnp.int32, (lc, lc), 0)
    col_id = lax.broadcasted_iota(jnp.int32, (lc, lc), 1)
    causal = col_id <= row_id
    lane_id = lax.broadcasted_iota(jnp.int32, (lc, HEAD_DIM), 1)

    acc = jnp.zeros((t, D_MODEL), F32)
    n_phases = (t // lc) * HEADS
    mlp_per_phase = D_FF // FF_CHUNK // n_phases
    mlp_first = (mlp_per_phase + 1) // 2
    next_mlp = 0
    m_carry = m_scr[...]
    sub_id = lax.broadcasted_iota(jnp.int32, (SUBLANES, lc), 0)
    bias_i = jnp.zeros((SUBLANES, lc), F32)
    bias_f = jnp.zeros((SUBLANES, lc), F32)
    for h in range(HEADS):
        bias_i = jnp.where(sub_id == h, bi_ref[0, h], bias_i)
        bias_f = jnp.where(sub_id == h, bf_ref[0, h], bias_f)
    for c in range(t // lc):
        tok = slice(c * lc, (c + 1) * lc)
        hb_c = hb[tok, :]
        g8 = gt[0:SUBLANES, tok]
        ig = g8 + bias_i
        lf = _log_sigmoid(pltpu.roll(g8, HEADS, axis=0) + bias_f)
        hi = lf.astype(BF16)
        r1 = lf - hi.astype(F32)
        mid = r1.astype(BF16)
        lo = (r1 - mid.astype(F32)).astype(BF16)
        parts = _dot(jnp.concatenate([hi, mid, lo, jnp.zeros_like(lo)], axis=0), tri_scr[...])
        b = parts[0:8] + parts[8:16] + parts[16:24]
        a = ig - b
        m_prev = jnp.concatenate([m_carry] * (lc // LANES), axis=1)
        g = jnp.maximum(_cummax_lanes(a), m_prev)
        m_t = b + g
        b_last = b[:, lc - 1:lc]
        m_new = m_t[:, lc - 1:lc]
        decay = jnp.exp(m_prev - g)
        e_neg_m = jnp.exp(-m_t)
        w_state = jnp.exp(a + (b_last - m_new))
        c_dec = jnp.exp(b_last + m_prev[:, 0:1] - m_new)
        m_carry = jnp.broadcast_to(m_new, m_carry.shape)
        a2 = a * LOG2E
        rows = jnp.concatenate(
            [g * (-LOG2E), decay, e_neg_m, w_state, jnp.zeros((LANES - 4 * SUBLANES, lc), F32)], axis=0)
        cols = rows.T

        for h in range(HEADS):
            if h % 2 == 0:
                pair = slice(h * HEAD_DIM, (h + 2) * HEAD_DIM)
                zq2, zk2, zv2, zo2 = (
                    _dot(hb_c, wheads_ref[:, grp * MLSTM_W + pair.start:grp * MLSTM_W + pair.stop])
                    for grp in range(4))
            half = slice((h % 2) * HEAD_DIM, (h % 2 + 1) * HEAD_DIM)
            qs = (zq2[:, half] * Q_SCALE).astype(BF16)
            kb = zk2[:, half].astype(BF16)
            v = zv2[:, half]
            c_col = cols[:, h:h + 1]
            dec = cols[:, SUBLANES + h:SUBLANES + h + 1]
            enm = cols[:, 2 * SUBLANES + h:2 * SUBLANES + h + 1]
            wst = cols[:, 3 * SUBLANES + h:3 * SUBLANES + h + 1]
            s = _dot_nt(qs, kb)

            for _ in range(mlp_first):
                acc = acc + _mlp_chunk(xn, next_mlp, wup_ref, wdown_ref)
                next_mlp += 1

            dmat = jnp.exp2(jnp.where(causal, c_col + a2[h:h + 1, :], -jnp.inf))
            pm = s * dmat
            row_sum = jnp.sum(pm, axis=1, keepdims=True)
            intra = _dot(pm.astype(BF16), v.astype(BF16))
            c_state = caug[h]
            inter = _dot(qs, c_state.astype(BF16))
            vw = jnp.concatenate([v * wst, jnp.where(lane_id == 0, wst, 0.0)], axis=1).astype(BF16)
            caug[h] = c_dec[h:h + 1, :] * c_state + _dot_tn(kb, vw)

            for _ in range(mlp_per_phase - mlp_first):
                acc = acc + _mlp_chunk(xn, next_mlp, wup_ref, wdown_ref)
                next_mlp += 1

            num = intra + dec * inter[:, 0:HEAD_DIM]
            den = row_sum + dec * inter[:, HEAD_DIM:HEAD_DIM + 1]
            hh = num / jnp.maximum(jnp.abs(den), enm)
            hn = hh * lax.rsqrt(jnp.mean(hh * hh, axis=-1, keepdims=True) + EPS)
            hn = hn * mh_ref[:, h * HEAD_DIM:(h + 1) * HEAD_DIM]
            mix_scr[tok, h * HEAD_DIM:(h + 1) * HEAD_DIM] = (
                hn * jax.nn.sigmoid(zo2[:, half])).astype(BF16)
    m_scr[...] = m_carry
    assert next_mlp == D_FF // FF_CHUNK

    zc = _dot(hb, wconv_ref[...])
    p_tail = jnp.where(s_id == 0, ps_scr[...], plag_ref[0])
    ple = _dot(p_tail.astype(BF16), wpp_ref[...])

    def tail_rows(rows):
        x2 = x1[rows, :] + acc[rows, :]
        gate = jax.nn.sigmoid(_dot(_rms(x2, nple_ref[...]).astype(BF16), wpg_ref[...]))
        y_ref[0, rows, :] = _rms(x2 + gate * ple[rows, :], nfin_ref[...])

    tail_rows(slice(0, t // 2))

    cu = zc[:, CONV_CH:2 * CONV_CH] * zc[:, 2 * CONV_CH:3 * CONV_CH]
    cu_buf[SUBLANES:SUBLANES + t, :] = cu
    conv = (cw_ref[0, 0:1, :] * cu_buf[SUBLANES - 2:SUBLANES - 2 + t, :]
            + cw_ref[0, 1:2, :] * cu_buf[SUBLANES - 1:SUBLANES - 1 + t, :]
            + cw_ref[0, 2:3, :] * cu)
    mix_scr[:, MLSTM_W:D_MODEL] = (zc[:, 0:CONV_CH] * conv).astype(BF16)
    cu_buf[0:SUBLANES, :] = cu_buf[t:t + SUBLANES, :]

    tail_rows(slice(t // 2, t))

    @pl.when(s_id == 0)
    def _():
        ys_ref[:, 0, :] = y_ref[0, 0:ns, :]

    @pl.when(jnp.logical_and(is_real, j == tiles_per_seq - 1))
    def _():
        for h in range(HEADS):
            cout_ref[0, h] = caug[h, :, 0:HEAD_DIM]
            nout_ref[0, h:h + 1, :] = caug[h, :, HEAD_DIM:2 * HEAD_DIM].T[0:1, :]
        mout_ref[0] = m_scr[...]
        convout_ref[0] = cu[t - (CONV_WIDTH - 1):t, :]


def _resident(shape):
    return pl.BlockSpec(shape, lambda *_: (0,) * len(shape), pipeline_mode=pl.Buffered(1))


def _prompt_call(x, p, xs, ps, mix_s, wts):
    bsz, seq, _ = x.shape
    t = SEQ_TILE
    assert seq % t == 0 and t % CHUNK == 0 and CHUNK % LANES == 0 and xs.shape[0] <= t
    consts = [wts["w_heads"], wts["w_conv"], wts["w_gt"], wts["b_i"], wts["b_f"], wts["norm_mix"], wts["mh_norm"],
              wts["conv_w"], wts["w_out"], wts["norm_mlp"], wts["w_up"], wts["w_down"], wts["norm_ple"], wts["w_pg"],
              wts["w_pp"], wts["norm_final"]]
    in_smem = {3, 4}
    nj = seq // t
    nt = bsz * nj

    def cur(s):
        c = jnp.minimum(s, nt - 1)
        return c // nj, c % nj

    def lag(s):
        c = jnp.maximum(s - 1, 0)
        return c // nj, c % nj

    in_specs = [pl.BlockSpec((1, t, D_MODEL), lambda s: (*cur(s), 0)),
                pl.BlockSpec((1, t, PLE_DIM), lambda s: (*lag(s), 0))]
    in_specs += [_resident(c.shape) for c in (xs, ps, mix_s)]
    in_specs += [pl.BlockSpec(memory_space=pltpu.SMEM) if i in in_smem else _resident(c.shape)
                 for i, c in enumerate(consts)]
    out_shape = (jax.ShapeDtypeStruct((bsz, seq, D_MODEL), F32),
                 jax.ShapeDtypeStruct(xs.shape, F32),
                 jax.ShapeDtypeStruct((bsz, HEADS, HEAD_DIM, HEAD_DIM), F32),
                 jax.ShapeDtypeStruct((bsz, HEADS, HEAD_DIM), F32),
                 jax.ShapeDtypeStruct((bsz, SUBLANES, LANES), F32),
                 jax.ShapeDtypeStruct((bsz, CONV_WIDTH - 1, CONV_CH), F32))
    out_specs = (pl.BlockSpec((1, t, D_MODEL), lambda s: (*lag(s), 0)),
                 pl.BlockSpec(xs.shape, lambda s: (0, 0, 0)),
                 pl.BlockSpec((1, HEADS, HEAD_DIM, HEAD_DIM), lambda s: (cur(s)[0], 0, 0, 0)),
                 pl.BlockSpec((1, HEADS, HEAD_DIM), lambda s: (cur(s)[0], 0, 0)),
                 pl.BlockSpec((1, SUBLANES, LANES), lambda s: (cur(s)[0], 0, 0)),
                 pl.BlockSpec((1, CONV_WIDTH - 1, CONV_CH), lambda s: (cur(s)[0], 0, 0)))
    scratch = [pltpu.VMEM((HEADS, HEAD_DIM, 2 * HEAD_DIM), F32),
               pltpu.VMEM((SUBLANES, LANES), F32),
               pltpu.VMEM((t + SUBLANES, CONV_CH), F32),
               pltpu.VMEM((t, D_MODEL), BF16),
               pltpu.VMEM((t, D_MODEL), F32),
               pltpu.VMEM((CHUNK, CHUNK), BF16),
               pltpu.VMEM((t, PLE_DIM), F32)]
    return pl.pallas_call(
        functools.partial(_prompt_kernel, nj, nt),
        grid=(nt + 1,),
        in_specs=in_specs,
        out_specs=out_specs,
        out_shape=out_shape,
        scratch_shapes=scratch,
        compiler_params=pltpu.CompilerParams(
            dimension_semantics=("arbitrary",),
            vmem_limit_bytes=VMEM_LIMIT_BYTES),
        name="prompt_layer",
    )(x, p, xs, ps, mix_s, *consts)


def _sample_inproj_kernel(x_ref, wheads_ref, wconv_ref, wgc_ref, nmix_ref, zh_ref, zc_ref, g_ref):
    hb = _rms(x_ref[:, 0, :], nmix_ref[...]).astype(BF16)
    zh_ref[...] = _dot(hb, wheads_ref[...])
    zc_ref[...] = _dot(hb, wconv_ref[...])
    g_ref[...] = _dot(hb, wgc_ref[...])


def _sample_state_kernel(zq_ref, zk_ref, zv_ref, zo_ref, zb_ref, zc_ref, zu_ref, g_ref, c0_ref, n0_ref, m0_ref,
                         sc_ref, bi_ref, bf_ref, mh_ref, cw_ref,
                         mix_ref, cnew_ref, nnew_ref, mnew_ref, buf_ref,
                         inter_scr, wv_scr, cd_scr):
    tb = SAMPLE_TILE
    cu = zc_ref[...] * zu_ref[...]
    old0 = sc_ref[0, :, 0, :]
    old1 = sc_ref[0, :, 1, :]
    conv = cw_ref[0, 0:1, :] * old0 + cw_ref[0, 1:2, :] * old1 + cw_ref[0, 2:3, :] * cu
    mix_ref[:, MLSTM_W:D_MODEL] = (zb_ref[...] * conv).astype(BF16)
    buf_ref[0, :, 0, :] = old1
    buf_ref[0, :, 1, :] = cu

    g = g_ref[...]
    ig = g[:, 0:HEADS] + bi_ref[...]
    lf = _log_sigmoid(g[:, HEADS:2 * HEADS] + bf_ref[...])
    m_inter = lf + m0_ref[...]
    m_new = jnp.maximum(m_inter, ig)
    w_in = jnp.exp(ig - m_new)
    c_dec = jnp.exp(m_inter - m_new)
    e_neg_m = jnp.exp(-m_new)
    mnew_ref[...] = m_new

    vs, scores, qns = [], [], []
    for h in range(HEADS):
        hs = slice(h * HEAD_DIM, (h + 1) * HEAD_DIM)
        q = zq_ref[:, hs] * Q_SCALE
        k = zk_ref[:, hs]
        v = zv_ref[:, hs]
        n0 = n0_ref[0, :, h, :]
        wi = w_in[:, h:h + 1]
        cd = c_dec[:, h:h + 1]
        scores.append(jnp.sum(q * k, axis=1, keepdims=True) * wi)
        qns.append(jnp.sum(q * n0, axis=1, keepdims=True))
        wv_scr[:, hs] = wi * v
        cd_scr[:, hs] = jnp.broadcast_to(cd, (tb, HEAD_DIM))
        nnew_ref[0, :, h, :] = cd * n0 + wi * k
        vs.append(v)

    eye = (lax.broadcasted_iota(jnp.int32, (HEAD_DIM, HEAD_DIM), 0)
           == lax.broadcasted_iota(jnp.int32, (HEAD_DIM, HEAD_DIM), 1))

    for i in range(tb):
        row = slice(i, i + 1)
        for h in range(HEADS):
            hs = slice(h * HEAD_DIM, (h + 1) * HEAD_DIM)
            c0 = c0_ref[0, i, h]
            q_rows = jnp.broadcast_to(zq_ref[row, hs] * Q_SCALE, (SUBLANES, HEAD_DIM)).astype(BF16)
            inter_scr[row, hs] = _dot(q_rows, c0.astype(BF16))[0:1, :]
            k_diag = jnp.where(eye, jnp.broadcast_to(zk_ref[row, hs], (HEAD_DIM, HEAD_DIM)), 0.0).astype(BF16)
            v_rows = jnp.broadcast_to(wv_scr[row, hs], (HEAD_DIM, HEAD_DIM)).astype(BF16)
            cd = jnp.broadcast_to(cd_scr[row, hs], (HEAD_DIM, HEAD_DIM))
            cnew_ref[0, i, h] = cd * c0 + _dot(k_diag, v_rows)

    for h in range(HEADS):
        hs = slice(h * HEAD_DIM, (h + 1) * HEAD_DIM)
        cd = c_dec[:, h:h + 1]
        num = scores[h] * vs[h] + cd * inter_scr[:, hs]
        den = scores[h] + cd * qns[h]
        hh = num / jnp.maximum(jnp.abs(den), e_neg_m[:, h:h + 1])
        hn = hh * lax.rsqrt(jnp.mean(hh * hh, axis=-1, keepdims=True) + EPS)
        mix_ref[:, hs] = (hn * mh_ref[:, hs] * jax.nn.sigmoid(zo_ref[:, hs])).astype(BF16)


def _sample_calls(xs, c0, n0, m0, sconv, wts):
    nb = xs.shape[0]
    tb = SAMPLE_TILE
    assert nb % tb == 0 and c0.shape[0] == 1
    zh, zc, g = pl.pallas_call(
        _sample_inproj_kernel,
        out_shape=(jax.ShapeDtypeStruct((nb, 4 * MLSTM_W), F32),
                   jax.ShapeDtypeStruct((nb, 3 * CONV_CH), F32),
                   jax.ShapeDtypeStruct((nb, LANES), F32)),
        compiler_params=pltpu.CompilerParams(vmem_limit_bytes=VMEM_LIMIT_BYTES),
        name="sample_inproj",
    )(xs, wts["w_heads"], wts["w_conv"], wts["w_gc"], wts["norm_mix"])

    def zgroup(grp):
        return pl.BlockSpec((tb, MLSTM_W), lambda i, grp=grp: (i, grp))

    def zconv(grp):
        return pl.BlockSpec((tb, CONV_CH), lambda i, grp=grp: (i, grp))

    row_d = pl.BlockSpec((tb, D_MODEL), lambda i: (i, 0))
    row_h = pl.BlockSpec((tb, HEADS), lambda i: (i, 0))
    state = pl.BlockSpec((1, tb, HEADS, HEAD_DIM, HEAD_DIM), lambda i: (0, i, 0, 0, 0))
    nstate = pl.BlockSpec((1, tb, HEADS, HEAD_DIM), lambda i: (0, i, 0, 0))
    cstate = pl.BlockSpec((1, tb, CONV_WIDTH - 1, CONV_CH), lambda i: (0, i, 0, 0))

    def whole(shape):
        return pl.BlockSpec(shape, lambda i: (0,) * len(shape))

    mix, c_new, n_new, m_new, new_buf = pl.pallas_call(
        _sample_state_kernel,
        grid=(nb // tb,),
        in_specs=[zgroup(0), zgroup(1), zgroup(2), zgroup(3), zconv(0), zconv(1), zconv(2),
                  pl.BlockSpec((tb, LANES), lambda i: (i, 0)),
                  state, nstate, row_h, cstate,
                  whole((1, HEADS)), whole((1, HEADS)), whole((1, MLSTM_W)), whole((1, CONV_WIDTH, CONV_CH))],
        out_specs=(row_d, state, nstate, row_h, cstate),
        out_shape=(jax.ShapeDtypeStruct((nb, D_MODEL), BF16),
                   jax.ShapeDtypeStruct(c0.shape, F32),
                   jax.ShapeDtypeStruct(n0.shape, F32),
                   jax.ShapeDtypeStruct((nb, HEADS), F32),
                   jax.ShapeDtypeStruct(sconv.shape, F32)),
        scratch_shapes=[pltpu.VMEM((tb, MLSTM_W), F32)] * 3,
        compiler_params=pltpu.CompilerParams(dimension_semantics=("arbitrary",)),
        name="sample_state",
    )(zh, zh, zh, zh, zc, zc, zc, g, c0, n0, m0, sconv, wts["b_i"], wts["b_f"], wts["mh_norm"], wts["conv_w"])
    return mix, c_new, n_new, m_new, new_buf


def _weight_prep_kernel(win_ref, wout_ref, wup_ref, wdown_ref, wpg_ref, wpp_ref,
                        heads_ref, conv_ref, gt_ref, gc_ref, out_ref, up_ref, down_ref, pg_ref, pp_ref):
    g0 = 4 * MLSTM_W
    g1 = g0 + 2 * HEADS
    heads_ref[...] = win_ref[0:g0, :].T.astype(BF16)
    conv_ref[...] = win_ref[g1:, :].T.astype(BF16)
    gates = win_ref[g0:g1, :]
    rows = gates.shape[1]
    gt_ref[...] = jnp.concatenate([gates, jnp.zeros((GATE_ROWS - 2 * HEADS, rows), F32)], axis=0).astype(BF16)
    gc_ref[...] = jnp.concatenate([gates, jnp.zeros((LANES - 2 * HEADS, rows), F32)], axis=0).T.astype(BF16)
    out_ref[...] = wout_ref[...].astype(BF16)
    up_ref[...] = wup_ref[...].astype(BF16)
    down_ref[...] = wdown_ref[...].astype(BF16)
    pg_ref[...] = wpg_ref[...].astype(BF16)
    pp_ref[...] = wpp_ref[...].astype(BF16)


def _prepare_weights(norm_mix, w_in, b_gate_i, b_gate_f, mh_norm, conv_w, w_out, norm_mlp, w_up, w_down,
                     norm_ple, w_ple_gate, w_ple_proj, norm_final):
    w_in_t = jnp.swapaxes(w_in, 0, 1)
    srcs = (w_in_t, w_out, w_up, w_down, w_ple_gate, w_ple_proj)
    n_in = w_in.shape[1]
    outs = ((D_MODEL, 4 * MLSTM_W), (D_MODEL, 3 * CONV_CH), (GATE_ROWS, D_MODEL), (D_MODEL, LANES),
            w_out.shape, w_up.shape, w_down.shape, w_ple_gate.shape, w_ple_proj.shape)
    steps = PREP_STEPS

    def rows(shape):
        return pl.BlockSpec((shape[0] // steps, shape[1]), lambda i: (i, 0))

    out_specs = [rows(o) for o in outs]
    out_specs[2] = pl.BlockSpec((GATE_ROWS, D_MODEL // steps), lambda i: (0, i))
    assert n_in == 4 * MLSTM_W + 2 * HEADS + 3 * CONV_CH
    w_heads, w_conv, w_gt, w_gc, w_out_b, w_up_b, w_down_b, w_pg_b, w_pp_b = pl.pallas_call(
        _weight_prep_kernel,
        grid=(steps,),
        in_specs=[pl.BlockSpec((n_in, D_MODEL // steps), lambda i: (0, i))] + [rows(a.shape) for a in srcs[1:]],
        out_specs=out_specs,
        out_shape=[jax.ShapeDtypeStruct(o, BF16) for o in outs],
        compiler_params=pltpu.CompilerParams(dimension_semantics=("arbitrary",),
                                             vmem_limit_bytes=VMEM_LIMIT_BYTES),
        name="weight_prep",
    )(*srcs)
    return dict(
        w_heads=w_heads,
        w_conv=w_conv,
        w_gt=w_gt,
        w_gc=w_gc,
        b_i=b_gate_i.reshape(1, HEADS),
        b_f=b_gate_f.reshape(1, HEADS),
        norm_mix=norm_mix.reshape(1, D_MODEL),
        mh_norm=mh_norm.reshape(1, MLSTM_W),
        conv_w=conv_w,
        w_out=w_out_b,
        norm_mlp=norm_mlp.reshape(1, D_MODEL),
        w_up=w_up_b,
        w_down=w_down_b,
        norm_ple=norm_ple.reshape(1, D_MODEL),
        w_pg=w_pg_b,
        w_pp=w_pp_b,
        norm_final=norm_final.reshape(1, D_MODEL),
    )


def kernel(x_prompt, x_sample, state_mlstm_C, state_mlstm_n, state_mlstm_m, state_conv, p_prompt, p_sample,
           norm_mix, w_in, b_gate_i, b_gate_f, mh_norm, conv_w, w_out, norm_mlp, w_up, w_down, norm_ple,
           w_ple_gate, w_ple_proj, norm_final):
    assert norm_mix.shape[0] == 1, "single-layer trunk"
    wts = _prepare_weights(norm_mix[0], w_in[0], b_gate_i[0], b_gate_f[0], mh_norm[0], conv_w, w_out[0],
                           norm_mlp[0], w_up[0], w_down[0], norm_ple[0], w_ple_gate[0], w_ple_proj[0], norm_final)

    nb = x_sample.shape[0]
    xs = x_sample
    mix_s, c_new, n_new, m_new, new_buf = _sample_calls(
        xs, state_mlstm_C, state_mlstm_n, state_mlstm_m[0], state_conv, wts)

    y_prompt, ys, prompt_c, prompt_n, m_rows, conv_tail = _prompt_call(
        x_prompt, p_prompt[0], xs, p_sample, mix_s, wts)
    prompt_m = m_rows[:, 0:HEADS, 0]

    return (y_prompt, ys, prompt_c[None], prompt_n[None], prompt_m[None], conv_tail[None],
            c_new, n_new, m_new[None], new_buf)
```

```python
import functools
import math

import jax
import jax.numpy as jnp
from jax import lax
from jax.experimental import pallas as pl
from jax.experimental.pallas import tpu as pltpu

F32 = jnp.float32
BF16 = jnp.bfloat16

D_MODEL = 1024
HEADS = 4
HEAD_DIM = 128
MLSTM_W = HEADS * HEAD_DIM
CONV_CH = D_MODEL - MLSTM_W
CONV_WIDTH = 3
D_FF = 4 * D_MODEL
PLE_DIM = 256
EPS = 1e-6
M_INIT = -1e30
Q_SCALE = HEAD_DIM ** -0.5
LOG2E = math.log2(math.e)

GATE_ROWS = 16
PREP_STEPS = 8
LANES = 128
SUBLANES = 8
VMEM_LIMIT_BYTES = 60000 * 1024

SEQ_TILE = 512
CHUNK = 256
FF_CHUNK = 512
SAMPLE_TILE = 16


def _dot(a, b):
    return jnp.dot(a, b, preferred_element_type=F32)


def _dot_nt(a, b):
    return lax.dot_general(a, b, (((1,), (1,)), ((), ())), preferred_element_type=F32)


def _dot_tn(a, b):
    return lax.dot_general(a, b, (((0,), (0,)), ((), ())), preferred_element_type=F32)


def _rms(x, g):
    y = x * lax.rsqrt(jnp.mean(x * x, axis=-1, keepdims=True) + EPS)
    return y * g


def _log_sigmoid(x):
    return jnp.minimum(x, 0.0) - jnp.log1p(jnp.exp(-jnp.abs(x)))


def _mlp_chunk(xn_bf16, c, wup_ref, wdown_ref):
    cols = slice(c * FF_CHUNK, (c + 1) * FF_CHUNK)
    hf = jnp.maximum(_dot(xn_bf16, wup_ref[:, cols]), 0.0)
    return _dot((hf * hf).astype(BF16), wdown_ref[cols, :])


def _cummax_lanes(a):
    n = a.shape[1]
    lane = lax.broadcasted_iota(jnp.int32, a.shape, 1)
    d = 1
    while d < n:
        shifted = pltpu.roll(a, d, axis=1)
        a = jnp.maximum(a, jnp.where(lane >= d, shifted, -jnp.inf))
        d *= 2
    return a


def _prompt_kernel(tiles_per_seq, num_tiles,
                   x_ref, plag_ref, xs_ref, ps_ref, mixs_ref,
                   wheads_ref, wconv_ref, wgt_ref, bi_ref, bf_ref, nmix_ref, mh_ref, cw_ref,
                   wout_ref, nmlp_ref, wup_ref, wdown_ref, nple_ref, wpg_ref, wpp_ref, nfin_ref,
                   y_ref, ys_ref, cout_ref, nout_ref, mout_ref, convout_ref,
                   caug, m_scr, cu_buf, mix_scr, xres_scr, tri_scr, ps_scr, bias_scr):
    t = SEQ_TILE
    lc = CHUNK
    ns = xs_ref.shape[0]
    s_id = pl.program_id(0)
    is_real = s_id < num_tiles
    j = lax.rem(jnp.minimum(s_id, num_tiles - 1), tiles_per_seq)

    @pl.when(s_id == 0)
    def _():
        mix_scr[0:ns, :] = mixs_ref[...]
        mix_scr[ns:t, :] = jnp.zeros((t - ns, D_MODEL), BF16)
        xres_scr[0:ns, :] = xs_ref[:, 0, :]
        xres_scr[ns:t, :] = jnp.zeros((t - ns, D_MODEL), F32)
        ps_scr[0:ns, :] = ps_ref[0, :, 0, :]
        ps_scr[ns:t, :] = jnp.zeros((t - ns, PLE_DIM), F32)
        sub_id = lax.broadcasted_iota(jnp.int32, (SUBLANES, lc), 0)
        for which, b_ref in enumerate((bi_ref, bf_ref)):
            col = jnp.zeros((SUBLANES, lc), F32)
            for h in range(HEADS):
                col = jnp.where(sub_id == h, b_ref[0, h], col)
            bias_scr[which] = col
        tri_scr[...] = (lax.broadcasted_iota(jnp.int32, (lc, lc), 0)
                        <= lax.broadcasted_iota(jnp.int32, (lc, lc), 1)).astype(F32).astype(BF16)

    @pl.when(j == 0)
    def _():
        caug[...] = jnp.zeros_like(caug)
        m_scr[...] = jnp.full(m_scr.shape, M_INIT, F32)
        cu_buf[0:SUBLANES, :] = jnp.zeros((SUBLANES, CONV_CH), F32)

    x1 = xres_scr[...] + _dot(mix_scr[...], wout_ref[...])
    xn = _rms(x1, nmlp_ref[...]).astype(BF16)

    x = x_ref[0]
    hb = _rms(x, nmix_ref[...]).astype(BF16)
    xres_scr[...] = x
    gt = _dot_nt(wgt_ref[...], hb)

    row_id = lax.broadcasted_iota(jnp.int32, (lc, lc), 0)
    col_id = lax.broadcasted_iota(jnp.int32, (lc, lc), 1)
    causal = col_id <= row_id
    lane_id = lax.broadcasted_iota(jnp.int32, (lc, HEAD_DIM), 1)

    acc = jnp.zeros((t, D_MODEL), F32)
    n_phases = (t // lc) * HEADS
    mlp_per_phase = D_FF // FF_CHUNK // n_phases
    mlp_first = (mlp_per_phase + 1) // 2
    next_mlp = 0
    m_carry = m_scr[...]
    bias_i = bias_scr[0]
    bias_f = bias_scr[1]
    for c in range(t // lc):
        tok = slice(c * lc, (c + 1) * lc)
        hb_c = hb[tok, :]
        g8 = gt[0:SUBLANES, tok]
        ig = g8 + bias_i
        lf = _log_sigmoid(pltpu.roll(g8, HEADS, axis=0) + bias_f)
        hi = lf.astype(BF16)
        r1 = lf - hi.astype(F32)
        mid = r1.astype(BF16)
        lo = (r1 - mid.astype(F32)).astype(BF16)
        parts = _dot(jnp.concatenate([hi, mid, lo, jnp.zeros_like(lo)], axis=0), tri_scr[...])
        b = parts[0:8] + parts[8:16] + parts[16:24]
        a = ig - b
        m_prev = jnp.concatenate([m_carry] * (lc // LANES), axis=1)
        g = jnp.maximum(_cummax_lanes(a), m_prev)
        m_t = b + g
        b_last = b[:, lc - 1:lc]
        m_new = m_t[:, lc - 1:lc]
        decay = jnp.exp(m_prev - g)
        e_neg_m = jnp.exp(-m_t)
        w_state = jnp.exp(a + (b_last - m_new))
        c_dec = jnp.exp(b_last + m_prev[:, 0:1] - m_new)
        m_carry = jnp.broadcast_to(m_new, m_carry.shape)
        a2 = a * LOG2E
        rows = jnp.concatenate(
            [g * (-LOG2E), decay, e_neg_m, w_state, jnp.zeros((LANES - 4 * SUBLANES, lc), F32)], axis=0)
        cols = rows.T

        for h in range(HEADS):
            if h % 2 == 0:
                pair = slice(h * HEAD_DIM, (h + 2) * HEAD_DIM)
                zq2, zk2, zv2, zo2 = (
                    _dot(hb_c, wheads_ref[:, grp * MLSTM_W + pair.start:grp * MLSTM_W + pair.stop])
                    for grp in range(4))
            half = slice((h % 2) * HEAD_DIM, (h % 2 + 1) * HEAD_DIM)
            qs = (zq2[:, half] * Q_SCALE).astype(BF16)
            kb = zk2[:, half].astype(BF16)
            v = zv2[:, half]
            c_col = cols[:, h:h + 1]
            dec = cols[:, SUBLANES + h:SUBLANES + h + 1]
            enm = cols[:, 2 * SUBLANES + h:2 * SUBLANES + h + 1]
            wst = cols[:, 3 * SUBLANES + h:3 * SUBLANES + h + 1]
            s = _dot_nt(qs, kb)

            for _ in range(mlp_first):
                acc = acc + _mlp_chunk(xn, next_mlp, wup_ref, wdown_ref)
                next_mlp += 1

            dmat = jnp.exp2(jnp.where(causal, c_col + a2[h:h + 1, :], -jnp.inf))
            pm = s * dmat
            row_sum = jnp.sum(pm, axis=1, keepdims=True)
            intra = _dot(pm.astype(BF16), v.astype(BF16))
            c_state = caug[h]
            inter = _dot(qs, c_state.astype(BF16))
            vw = jnp.concatenate([v * wst, jnp.where(lane_id == 0, wst, 0.0)], axis=1).astype(BF16)
            caug[h] = c_dec[h:h + 1, :] * c_state + _dot_tn(kb, vw)

            for _ in range(mlp_per_phase - mlp_first):
                acc = acc + _mlp_chunk(xn, next_mlp, wup_ref, wdown_ref)
                next_mlp += 1

            num = intra + dec * inter[:, 0:HEAD_DIM]
            den = row_sum + dec * inter[:, HEAD_DIM:HEAD_DIM + 1]
            hh = num / jnp.maximum(jnp.abs(den), enm)
            hn = hh * lax.rsqrt(jnp.mean(hh * hh, axis=-1, keepdims=True) + EPS)
            hn = hn * mh_ref[:, h * HEAD_DIM:(h + 1) * HEAD_DIM]
            mix_scr[tok, h * HEAD_DIM:(h + 1) * HEAD_DIM] = (
                hn * jax.nn.sigmoid(zo2[:, half])).astype(BF16)
    m_scr[...] = m_carry
    assert next_mlp == D_FF // FF_CHUNK

    zc = _dot(hb, wconv_ref[...])
    p_tail = jnp.where(s_id == 0, ps_scr[...], plag_ref[0])
    ple = _dot(p_tail.astype(BF16), wpp_ref[...])

    def tail_rows(rows):
        x2 = x1[rows, :] + acc[rows, :]
        gate = jax.nn.sigmoid(_dot(_rms(x2, nple_ref[...]).astype(BF16), wpg_ref[...]))
        y_ref[0, rows, :] = _rms(x2 + gate * ple[rows, :], nfin_ref[...])

    tail_rows(slice(0, t // 2))

    cu = zc[:, CONV_CH:2 * CONV_CH] * zc[:, 2 * CONV_CH:3 * CONV_CH]
    cu_buf[SUBLANES:SUBLANES + t, :] = cu
    conv = (cw_ref[0, 0:1, :] * cu_buf[SUBLANES - 2:SUBLANES - 2 + t, :]
            + cw_ref[0, 1:2, :] * cu_buf[SUBLANES - 1:SUBLANES - 1 + t, :]
            + cw_ref[0, 2:3, :] * cu)
    mix_scr[:, MLSTM_W:D_MODEL] = (zc[:, 0:CONV_CH] * conv).astype(BF16)
    cu_buf[0:SUBLANES, :] = cu_buf[t:t + SUBLANES, :]

    tail_rows(slice(t // 2, t))

    @pl.when(s_id == 0)
    def _():
        ys_ref[:, 0, :] = y_ref[0, 0:ns, :]

    @pl.when(jnp.logical_and(is_real, j == tiles_per_seq - 1))
    def _():
        for h in range(HEADS):
            cout_ref[0, h] = caug[h, :, 0:HEAD_DIM]
            nout_ref[0, h:h + 1, :] = caug[h, :, HEAD_DIM:2 * HEAD_DIM].T[0:1, :]
        mout_ref[0] = m_scr[...]
        convout_ref[0] = cu[t - (CONV_WIDTH - 1):t, :]


def _resident(shape):
    return pl.BlockSpec(shape, lambda *_: (0,) * len(shape), pipeline_mode=pl.Buffered(1))


def _prompt_call(x, p, xs, ps, mix_s, wts):
    bsz, seq, _ = x.shape
    t = SEQ_TILE
    assert seq % t == 0 and t % CHUNK == 0 and CHUNK % LANES == 0 and xs.shape[0] <= t
    consts = [wts["w_heads"], wts["w_conv"], wts["w_gt"], wts["b_i"], wts["b_f"], wts["norm_mix"], wts["mh_norm"],
              wts["conv_w"], wts["w_out"], wts["norm_mlp"], wts["w_up"], wts["w_down"], wts["norm_ple"], wts["w_pg"],
              wts["w_pp"], wts["norm_final"]]
    in_smem = {3, 4}
    nj = seq // t
    nt = bsz * nj

    def cur(s):
        c = jnp.minimum(s, nt - 1)
        return c // nj, c % nj

    def lag(s):
        c = jnp.maximum(s - 1, 0)
        return c // nj, c % nj

    in_specs = [pl.BlockSpec((1, t, D_MODEL), lambda s: (*cur(s), 0)),
                pl.BlockSpec((1, t, PLE_DIM), lambda s: (*lag(s), 0))]
    in_specs += [_resident(c.shape) for c in (xs, ps, mix_s)]
    in_specs += [pl.BlockSpec(memory_space=pltpu.SMEM) if i in in_smem else _resident(c.shape)
                 for i, c in enumerate(consts)]
    out_shape = (jax.ShapeDtypeStruct((bsz, seq, D_MODEL), F32),
                 jax.ShapeDtypeStruct(xs.shape, F32),
                 jax.ShapeDtypeStruct((bsz, HEADS, HEAD_DIM, HEAD_DIM), F32),
                 jax.ShapeDtypeStruct((bsz, HEADS, HEAD_DIM), F32),
                 jax.ShapeDtypeStruct((bsz, SUBLANES, LANES), F32),
                 jax.ShapeDtypeStruct((bsz, CONV_WIDTH - 1, CONV_CH), F32))
    out_specs = (pl.BlockSpec((1, t, D_MODEL), lambda s: (*lag(s), 0)),
                 pl.BlockSpec(xs.shape, lambda s: (0, 0, 0)),
                 pl.BlockSpec((1, HEADS, HEAD_DIM, HEAD_DIM), lambda s: (cur(s)[0], 0, 0, 0)),
                 pl.BlockSpec((1, HEADS, HEAD_DIM), lambda s: (cur(s)[0], 0, 0)),
                 pl.BlockSpec((1, SUBLANES, LANES), lambda s: (cur(s)[0], 0, 0)),
                 pl.BlockSpec((1, CONV_WIDTH - 1, CONV_CH), lambda s: (cur(s)[0], 0, 0)))
    scratch = [pltpu.VMEM((HEADS, HEAD_DIM, 2 * HEAD_DIM), F32),
               pltpu.VMEM((SUBLANES, LANES), F32),
               pltpu.VMEM((t + SUBLANES, CONV_CH), F32),
               pltpu.VMEM((t, D_MODEL), BF16),
               pltpu.VMEM((t, D_MODEL), F32),
               pltpu.VMEM((CHUNK, CHUNK), BF16),
               pltpu.VMEM((t, PLE_DIM), F32),
               pltpu.VMEM((2, SUBLANES, CHUNK), F32)]
    return pl.pallas_call(
        functools.partial(_prompt_kernel, nj, nt),
        grid=(nt + 1,),
        in_specs=in_specs,
        out_specs=out_specs,
        out_shape=out_shape,
        scratch_shapes=scratch,
        compiler_params=pltpu.CompilerParams(
            dimension_semantics=("arbitrary",),
            vmem_limit_bytes=VMEM_LIMIT_BYTES),
        name="prompt_layer",
    )(x, p, xs, ps, mix_s, *consts)


def _sample_inproj_kernel(x_ref, wheads_ref, wconv_ref, wgc_ref, nmix_ref, zh_ref, zc_ref, g_ref):
    hb = _rms(x_ref[:, 0, :], nmix_ref[...]).astype(BF16)
    zh_ref[...] = _dot(hb, wheads_ref[...])
    zc_ref[...] = _dot(hb, wconv_ref[...])
    g_ref[...] = _dot(hb, wgc_ref[...])


def _sample_state_kernel(zq_ref, zk_ref, zv_ref, zo_ref, zb_ref, zc_ref, zu_ref, g_ref, c0_ref, n0_ref, m0_ref,
                         sc_ref, bi_ref, bf_ref, mh_ref, cw_ref,
                         mix_ref, cnew_ref, nnew_ref, mnew_ref, buf_ref,
                         inter_scr, wv_scr, cd_scr):
    tb = SAMPLE_TILE
    cu = zc_ref[...] * zu_ref[...]
    old0 = sc_ref[0, :, 0, :]
    old1 = sc_ref[0, :, 1, :]
    conv = cw_ref[0, 0:1, :] * old0 + cw_ref[0, 1:2, :] * old1 + cw_ref[0, 2:3, :] * cu
    mix_ref[:, MLSTM_W:D_MODEL] = (zb_ref[...] * conv).astype(BF16)
    buf_ref[0, :, 0, :] = old1
    buf_ref[0, :, 1, :] = cu

    g = g_ref[...]
    ig = g[:, 0:HEADS] + bi_ref[...]
    lf = _log_sigmoid(g[:, HEADS:2 * HEADS] + bf_ref[...])
    m_inter = lf + m0_ref[...]
    m_new = jnp.maximum(m_inter, ig)
    w_in = jnp.exp(ig - m_new)
    c_dec = jnp.exp(m_inter - m_new)
    e_neg_m = jnp.exp(-m_new)
    mnew_ref[...] = m_new

    vs, scores, qns = [], [], []
    for h in range(HEADS):
        hs = slice(h * HEAD_DIM, (h + 1) * HEAD_DIM)
        q = zq_ref[:, hs] * Q_SCALE
        k = zk_ref[:, hs]
        v = zv_ref[:, hs]
        n0 = n0_ref[0, :, h, :]
        wi = w_in[:, h:h + 1]
        cd = c_dec[:, h:h + 1]
        scores.append(jnp.sum(q * k, axis=1, keepdims=True) * wi)
        qns.append(jnp.sum(q * n0, axis=1, keepdims=True))
        wv_scr[:, hs] = wi * v
        cd_scr[:, hs] = jnp.broadcast_to(cd, (tb, HEAD_DIM))
        nnew_ref[0, :, h, :] = cd * n0 + wi * k
        vs.append(v)

    eye = (lax.broadcasted_iota(jnp.int32, (HEAD_DIM, HEAD_DIM), 0)
           == lax.broadcasted_iota(jnp.int32, (HEAD_DIM, HEAD_DIM), 1))

    for i in range(tb):
        row = slice(i, i + 1)
        for h in range(HEADS):
            hs = slice(h * HEAD_DIM, (h + 1) * HEAD_DIM)
            c0 = c0_ref[0, i, h]
            q_rows = jnp.broadcast_to(zq_ref[row, hs] * Q_SCALE, (SUBLANES, HEAD_DIM)).astype(BF16)
            inter_scr[row, hs] = _dot(q_rows, c0.astype(BF16))[0:1, :]
            k_diag = jnp.where(eye, jnp.broadcast_to(zk_ref[row, hs], (HEAD_DIM, HEAD_DIM)), 0.0).astype(BF16)
            v_rows = jnp.broadcast_to(wv_scr[row, hs], (HEAD_DIM, HEAD_DIM)).astype(BF16)
            cd = jnp.broadcast_to(cd_scr[row, hs], (HEAD_DIM, HEAD_DIM))
            cnew_ref[0, i, h] = cd * c0 + _dot(k_diag, v_rows)

    for h in range(HEADS):
        hs = slice(h * HEAD_DIM, (h + 1) * HEAD_DIM)
        cd = c_dec[:, h:h + 1]
        num = scores[h] * vs[h] + cd * inter_scr[:, hs]
        den = scores[h] + cd * qns[h]
        hh = num / jnp.maximum(jnp.abs(den), e_neg_m[:, h:h + 1])
        hn = hh * lax.rsqrt(jnp.mean(hh * hh, axis=-1, keepdims=True) + EPS)
        mix_ref[:, hs] = (hn * mh_ref[:, hs] * jax.nn.sigmoid(zo_ref[:, hs])).astype(BF16)


def _sample_calls(xs, c0, n0, m0, sconv, wts):
    nb = xs.shape[0]
    tb = SAMPLE_TILE
    assert nb % tb == 0 and c0.shape[0] == 1
    zh, zc, g = pl.pallas_call(
        _sample_inproj_kernel,
        out_shape=(jax.ShapeDtypeStruct((nb, 4 * MLSTM_W), F32),
                   jax.ShapeDtypeStruct((nb, 3 * CONV_CH), F32),
                   jax.ShapeDtypeStruct((nb, LANES), F32)),
        compiler_params=pltpu.CompilerParams(vmem_limit_bytes=VMEM_LIMIT_BYTES),
        name="sample_inproj",
    )(xs, wts["w_heads"], wts["w_conv"], wts["w_gc"], wts["norm_mix"])

    def zgroup(grp):
        return pl.BlockSpec((tb, MLSTM_W), lambda i, grp=grp: (i, grp))

    def zconv(grp):
        return pl.BlockSpec((tb, CONV_CH), lambda i, grp=grp: (i, grp))

    row_d = pl.BlockSpec((tb, D_MODEL), lambda i: (i, 0))
    row_h = pl.BlockSpec((tb, HEADS), lambda i: (i, 0))
    state = pl.BlockSpec((1, tb, HEADS, HEAD_DIM, HEAD_DIM), lambda i: (0, i, 0, 0, 0))
    nstate = pl.BlockSpec((1, tb, HEADS, HEAD_DIM), lambda i: (0, i, 0, 0))
    cstate = pl.BlockSpec((1, tb, CONV_WIDTH - 1, CONV_CH), lambda i: (0, i, 0, 0))

    def whole(shape):
        return pl.BlockSpec(shape, lambda i: (0,) * len(shape))

    mix, c_new, n_new, m_new, new_buf = pl.pallas_call(
        _sample_state_kernel,
        grid=(nb // tb,),
        in_specs=[zgroup(0), zgroup(1), zgroup(2), zgroup(3), zconv(0), zconv(1), zconv(2),
                  pl.BlockSpec((tb, LANES), lambda i: (i, 0)),
                  state, nstate, row_h, cstate,
                  whole((1, HEADS)), whole((1, HEADS)), whole((1, MLSTM_W)), whole((1, CONV_WIDTH, CONV_CH))],
        out_specs=(row_d, state, nstate, row_h, cstate),
        out_shape=(jax.ShapeDtypeStruct((nb, D_MODEL), BF16),
                   jax.ShapeDtypeStruct(c0.shape, F32),
                   jax.ShapeDtypeStruct(n0.shape, F32),
                   jax.ShapeDtypeStruct((nb, HEADS), F32),
                   jax.ShapeDtypeStruct(sconv.shape, F32)),
        scratch_shapes=[pltpu.VMEM((tb, MLSTM_W), F32)] * 3,
        compiler_params=pltpu.CompilerParams(dimension_semantics=("arbitrary",)),
        name="sample_state",
    )(zh, zh, zh, zh, zc, zc, zc, g, c0, n0, m0, sconv, wts["b_i"], wts["b_f"], wts["mh_norm"], wts["conv_w"])
    return mix, c_new, n_new, m_new, new_buf


def _weight_prep_kernel(win_ref, wout_ref, wup_ref, wdown_ref, wpg_ref, wpp_ref,
                        heads_ref, conv_ref, gt_ref, gc_ref, out_ref, up_ref, down_ref, pg_ref, pp_ref):
    g0 = 4 * MLSTM_W
    g1 = g0 + 2 * HEADS
    heads_ref[...] = win_ref[0:g0, :].T.astype(BF16)
    conv_ref[...] = win_ref[g1:, :].T.astype(BF16)
    gates = win_ref[g0:g1, :]
    rows = gates.shape[1]
    gt_ref[...] = jnp.concatenate([gates, jnp.zeros((GATE_ROWS - 2 * HEADS, rows), F32)], axis=0).astype(BF16)
    gc_ref[...] = jnp.concatenate([gates, jnp.zeros((LANES - 2 * HEADS, rows), F32)], axis=0).T.astype(BF16)
    out_ref[...] = wout_ref[...].astype(BF16)
    up_ref[...] = wup_ref[...].astype(BF16)
    down_ref[...] = wdown_ref[...].astype(BF16)
    pg_ref[...] = wpg_ref[...].astype(BF16)
    pp_ref[...] = wpp_ref[...].astype(BF16)


def _prepare_weights(norm_mix, w_in, b_gate_i, b_gate_f, mh_norm, conv_w, w_out, norm_mlp, w_up, w_down,
                     norm_ple, w_ple_gate, w_ple_proj, norm_final):
    w_in_t = jnp.swapaxes(w_in, 0, 1)
    srcs = (w_in_t, w_out, w_up, w_down, w_ple_gate, w_ple_proj)
    n_in = w_in.shape[1]
    outs = ((D_MODEL, 4 * MLSTM_W), (D_MODEL, 3 * CONV_CH), (GATE_ROWS, D_MODEL), (D_MODEL, LANES),
            w_out.shape, w_up.shape, w_down.shape, w_ple_gate.shape, w_ple_proj.shape)
    steps = PREP_STEPS

    def rows(shape):
        return pl.BlockSpec((shape[0] // steps, shape[1]), lambda i: (i, 0))

    out_specs = [rows(o) for o in outs]
    out_specs[2] = pl.BlockSpec((GATE_ROWS, D_MODEL // steps), lambda i: (0, i))
    assert n_in == 4 * MLSTM_W + 2 * HEADS + 3 * CONV_CH
    w_heads, w_conv, w_gt, w_gc, w_out_b, w_up_b, w_down_b, w_pg_b, w_pp_b = pl.pallas_call(
        _weight_prep_kernel,
        grid=(steps,),
        in_specs=[pl.BlockSpec((n_in, D_MODEL // steps), lambda i: (0, i))] + [rows(a.shape) for a in srcs[1:]],
        out_specs=out_specs,
        out_shape=[jax.ShapeDtypeStruct(o, BF16) for o in outs],
        compiler_params=pltpu.CompilerParams(dimension_semantics=("arbitrary",),
                                             vmem_limit_bytes=VMEM_LIMIT_BYTES),
        name="weight_prep",
    )(*srcs)
    return dict(
        w_heads=w_heads,
        w_conv=w_conv,
        w_gt=w_gt,
        w_gc=w_gc,
        b_i=b_gate_i.reshape(1, HEADS),
        b_f=b_gate_f.reshape(1, HEADS),
        norm_mix=norm_mix.reshape(1, D_MODEL),
        mh_norm=mh_norm.reshape(1, MLSTM_W),
        conv_w=conv_w,
        w_out=w_out_b,
        norm_mlp=norm_mlp.reshape(1, D_MODEL),
        w_up=w_up_b,
        w_down=w_down_b,
        norm_ple=norm_ple.reshape(1, D_MODEL),
        w_pg=w_pg_b,
        w_pp=w_pp_b,
        norm_final=norm_final.reshape(1, D_MODEL),
    )


def kernel(x_prompt, x_sample, state_mlstm_C, state_mlstm_n, state_mlstm_m, state_conv, p_prompt, p_sample,
           norm_mix, w_in, b_gate_i, b_gate_f, mh_norm, conv_w, w_out, norm_mlp, w_up, w_down, norm_ple,
           w_ple_gate, w_ple_proj, norm_final):
    assert norm_mix.shape[0] == 1, "single-layer trunk"
    wts = _prepare_weights(norm_mix[0], w_in[0], b_gate_i[0], b_gate_f[0], mh_norm[0], conv_w, w_out[0],
                           norm_mlp[0], w_up[0], w_down[0], norm_ple[0], w_ple_gate[0], w_ple_proj[0], norm_final)

    nb = x_sample.shape[0]
    xs = x_sample
    mix_s, c_new, n_new, m_new, new_buf = _sample_calls(
        xs, state_mlstm_C, state_mlstm_n, state_mlstm_m[0], state_conv, wts)

    y_prompt, ys, prompt_c, prompt_n, m_rows, conv_tail = _prompt_call(
        x_prompt, p_prompt[0], xs, p_sample, mix_s, wts)
    prompt_m = m_rows[:, 0:HEADS, 0]

    return (y_prompt, ys, prompt_c[None], prompt_n[None], prompt_m[None], conv_tail[None],
            c_new, n_new, m_new[None], new_buf)
```

```python
import functools
import math

import jax
import jax.numpy as jnp
from jax import lax
from jax.experimental import pallas as pl
from jax.experimental.pallas import tpu as pltpu

F32 = jnp.float32
BF16 = jnp.bfloat16

D_MODEL = 1024
HEADS = 4
HEAD_DIM = 128
MLSTM_W = HEADS * HEAD_DIM
CONV_CH = D_MODEL - MLSTM_W
CONV_WIDTH = 3
D_FF = 4 * D_MODEL
PLE_DIM = 256
EPS = 1e-6
M_INIT = -1e30
Q_SCALE = HEAD_DIM ** -0.5
LOG2E = math.log2(math.e)

GATE_ROWS = 16
PREP_STEPS = 4
LANES = 128
SUBLANES = 8
VMEM_LIMIT_BYTES = 60000 * 1024

SEQ_TILE = 512
CHUNK = 256
FF_CHUNK = 512
SAMPLE_TILE = 32


def _dot(a, b):
    return jnp.dot(a, b, preferred_element_type=F32)


def _dot_nt(a, b):
    return lax.dot_general(a, b, (((1,), (1,)), ((), ())), preferred_element_type=F32)


def _dot_tn(a, b):
    return lax.dot_general(a, b, (((0,), (0,)), ((), ())), preferred_element_type=F32)


def _rms(x, g):
    y = x * lax.rsqrt(jnp.mean(x * x, axis=-1, keepdims=True) + EPS)
    return y * g


def _log_sigmoid(x):
    return jnp.minimum(x, 0.0) - jnp.log1p(jnp.exp(-jnp.abs(x)))


def _mlp_chunk(xn_bf16, c, wup_ref, wdown_ref):
    cols = slice(c * FF_CHUNK, (c + 1) * FF_CHUNK)
    hf = jnp.maximum(_dot(xn_bf16, wup_ref[:, cols]), 0.0)
    return _dot((hf * hf).astype(BF16), wdown_ref[cols, :])


def _cummax_lanes(a):
    n = a.shape[1]
    lane = lax.broadcasted_iota(jnp.int32, a.shape, 1)
    d = 1
    while d < n:
        shifted = pltpu.roll(a, d, axis=1)
        a = jnp.maximum(a, jnp.where(lane >= d, shifted, -jnp.inf))
        d *= 2
    return a


def _prompt_kernel(tiles_per_seq, num_tiles,
                   x_ref, plag_ref, xs_ref, ps_ref, mixs_ref,
                   wheads_ref, wconv_ref, wgt_ref, bi_ref, bf_ref, nmix_ref, mh_ref, cw_ref,
                   wout_ref, nmlp_ref, wup_ref, wdown_ref, nple_ref, wpg_ref, wpp_ref, nfin_ref,
                   y_ref, ys_ref, cout_ref, nout_ref, mout_ref, convout_ref,
                   caug, m_scr, cu_buf, mix_scr, xres_scr, tri_scr, ps_scr):
    t = SEQ_TILE
    lc = CHUNK
    ns = xs_ref.shape[0]
    s_id = pl.program_id(0)
    is_real = s_id < num_tiles
    j = lax.rem(jnp.minimum(s_id, num_tiles - 1), tiles_per_seq)

    @pl.when(s_id == 0)
    def _():
        mix_scr[0:ns, :] = mixs_ref[...]
        mix_scr[ns:t, :] = jnp.zeros((t - ns, D_MODEL), BF16)
        xres_scr[0:ns, :] = xs_ref[:, 0, :]
        xres_scr[ns:t, :] = jnp.zeros((t - ns, D_MODEL), F32)
        ps_scr[0:ns, :] = ps_ref[0, :, 0, :]
        ps_scr[ns:t, :] = jnp.zeros((t - ns, PLE_DIM), F32)
        tri_scr[...] = (lax.broadcasted_iota(jnp.int32, (lc, lc), 0)
                        <= lax.broadcasted_iota(jnp.int32, (lc, lc), 1)).astype(F32).astype(BF16)

    @pl.when(j == 0)
    def _():
        caug[...] = jnp.zeros_like(caug)
        m_scr[...] = jnp.full(m_scr.shape, M_INIT, F32)
        cu_buf[0:SUBLANES, :] = jnp.zeros((SUBLANES, CONV_CH), F32)

    x1 = xres_scr[...] + _dot(mix_scr[...], wout_ref[...])
    xn = _rms(x1, nmlp_ref[...]).astype(BF16)

    x = x_ref[0]
    hb = _rms(x, nmix_ref[...]).astype(BF16)
    xres_scr[...] = x
    gt = _dot_nt(wgt_ref[...], hb)

    row_id = lax.broadcasted_iota(jnp.int32, (lc, lc), 0)
    col_id = lax.broadcasted_iota(jnp.int32, (lc, lc), 1)
    causal = col_id <= row_id
    lane_id = lax.broadcasted_iota(jnp.int32, (lc, HEAD_DIM), 1)

    acc = jnp.zeros((t, D_MODEL), F32)
    n_phases = (t // lc) * HEADS
    mlp_per_phase = D_FF // FF_CHUNK // n_phases
    mlp_first = (mlp_per_phase + 1) // 2
    next_mlp = 0
    m_carry = m_scr[...]
    sub_id = lax.broadcasted_iota(jnp.int32, (SUBLANES, lc), 0)
    bias_i = jnp.zeros((SUBLANES, lc), F32)
    bias_f = jnp.zeros((SUBLANES, lc), F32)
    for h in range(HEADS):
        bias_i = jnp.where(sub_id == h, bi_ref[0, h], bias_i)
        bias_f = jnp.where(sub_id == h, bf_ref[0, h], bias_f)
    for c in range(t // lc):
        tok = slice(c * lc, (c + 1) * lc)
        hb_c = hb[tok, :]
        g8 = gt[0:SUBLANES, tok]
        ig = g8 + bias_i
        lf = _log_sigmoid(pltpu.roll(g8, HEADS, axis=0) + bias_f)
        hi = lf.astype(BF16)
        r1 = lf - hi.astype(F32)
        mid = r1.astype(BF16)
        lo = (r1 - mid.astype(F32)).astype(BF16)
        parts = _dot(jnp.concatenate([hi, mid, lo, jnp.zeros_like(lo)], axis=0), tri_scr[...])
        b = parts[0:8] + parts[8:16] + parts[16:24]
        a = ig - b
        m_prev = jnp.concatenate([m_carry] * (lc // LANES), axis=1)
        g = jnp.maximum(_cummax_lanes(a), m_prev)
        m_t = b + g
        b_last = b[:, lc - 1:lc]
        m_new = m_t[:, lc - 1:lc]
        decay = jnp.exp(m_prev - g)
        e_neg_m = jnp.exp(-m_t)
        w_state = jnp.exp(a + (b_last - m_new))
        c_dec = jnp.exp(b_last + m_prev[:, 0:1] - m_new)
        m_carry = jnp.broadcast_to(m_new, m_carry.shape)
        a2 = a * LOG2E
        rows = jnp.concatenate(
            [g * (-LOG2E), decay, e_neg_m, w_state, jnp.zeros((LANES - 4 * SUBLANES, lc), F32)], axis=0)
        cols = rows.T

        for h in range(HEADS):
            if h % 2 == 0:
                pair = slice(h * HEAD_DIM, (h + 2) * HEAD_DIM)
                zq2, zk2, zv2, zo2 = (
                    _dot(hb_c, wheads_ref[:, grp * MLSTM_W + pair.start:grp * MLSTM_W + pair.stop])
                    for grp in range(4))
            half = slice((h % 2) * HEAD_DIM, (h % 2 + 1) * HEAD_DIM)
            qs = (zq2[:, half] * Q_SCALE).astype(BF16)
            kb = zk2[:, half].astype(BF16)
            v = zv2[:, half]
            c_col = cols[:, h:h + 1]
            dec = cols[:, SUBLANES + h:SUBLANES + h + 1]
            enm = cols[:, 2 * SUBLANES + h:2 * SUBLANES + h + 1]
            wst = cols[:, 3 * SUBLANES + h:3 * SUBLANES + h + 1]
            s = _dot_nt(qs, kb)

            for _ in range(mlp_first):
                acc = acc + _mlp_chunk(xn, next_mlp, wup_ref, wdown_ref)
                next_mlp += 1

            dmat = jnp.exp2(jnp.where(causal, c_col + a2[h:h + 1, :], -jnp.inf))
            pm = s * dmat
            row_sum = jnp.sum(pm, axis=1, keepdims=True)
            intra = _dot(pm.astype(BF16), v.astype(BF16))
            c_state = caug[h]
            inter = _dot(qs, c_state.astype(BF16))
            vw = jnp.concatenate([v * wst, jnp.where(lane_id == 0, wst, 0.0)], axis=1).astype(BF16)
            caug[h] = c_dec[h:h + 1, :] * c_state + _dot_tn(kb, vw)

            for _ in range(mlp_per_phase - mlp_first):
                acc = acc + _mlp_chunk(xn, next_mlp, wup_ref, wdown_ref)
                next_mlp += 1

            num = intra + dec * inter[:, 0:HEAD_DIM]
            den = row_sum + dec * inter[:, HEAD_DIM:HEAD_DIM + 1]
            hh = num / jnp.maximum(jnp.abs(den), enm)
            hn = hh * lax.rsqrt(jnp.mean(hh * hh, axis=-1, keepdims=True) + EPS)
            hn = hn * mh_ref[:, h * HEAD_DIM:(h + 1) * HEAD_DIM]
            mix_scr[tok, h * HEAD_DIM:(h + 1) * HEAD_DIM] = (
                hn * jax.nn.sigmoid(zo2[:, half])).astype(BF16)
    m_scr[...] = m_carry
    assert next_mlp == D_FF // FF_CHUNK

    zc = _dot(hb, wconv_ref[...])
    p_tail = jnp.where(s_id == 0, ps_scr[...], plag_ref[0])
    ple = _dot(p_tail.astype(BF16), wpp_ref[...])

    def tail_rows(rows):
        x2 = x1[rows, :] + acc[rows, :]
        gate = jax.nn.sigmoid(_dot(_rms(x2, nple_ref[...]).astype(BF16), wpg_ref[...]))
        y_ref[0, rows, :] = _rms(x2 + gate * ple[rows, :], nfin_ref[...])

    tail_rows(slice(0, t // 2))

    cu = zc[:, CONV_CH:2 * CONV_CH] * zc[:, 2 * CONV_CH:3 * CONV_CH]
    cu_buf[SUBLANES:SUBLANES + t, :] = cu
    conv = (cw_ref[0, 0:1, :] * cu_buf[SUBLANES - 2:SUBLANES - 2 + t, :]
            + cw_ref[0, 1:2, :] * cu_buf[SUBLANES - 1:SUBLANES - 1 + t, :]
            + cw_ref[0, 2:3, :] * cu)
    mix_scr[:, MLSTM_W:D_MODEL] = (zc[:, 0:CONV_CH] * conv).astype(BF16)
    cu_buf[0:SUBLANES, :] = cu_buf[t:t + SUBLANES, :]

    tail_rows(slice(t // 2, t))

    @pl.when(s_id == 0)
    def _():
        ys_ref[:, 0, :] = y_ref[0, 0:ns, :]

    @pl.when(jnp.logical_and(is_real, j == tiles_per_seq - 1))
    def _():
        for h in range(HEADS):
            cout_ref[0, h] = caug[h, :, 0:HEAD_DIM]
            nout_ref[0, h:h + 1, :] = caug[h, :, HEAD_DIM:2 * HEAD_DIM].T[0:1, :]
        mout_ref[0] = m_scr[...]
        convout_ref[0] = cu[t - (CONV_WIDTH - 1):t, :]


def _resident(shape):
    return pl.BlockSpec(shape, lambda *_: (0,) * len(shape), pipeline_mode=pl.Buffered(1))


def _prompt_call(x, p, xs, ps, mix_s, wts):
    bsz, seq, _ = x.shape
    t = SEQ_TILE
    assert seq % t == 0 and t % CHUNK == 0 and CHUNK % LANES == 0 and xs.shape[0] <= t
    consts = [wts["w_heads"], wts["w_conv"], wts["w_gt"], wts["b_i"], wts["b_f"], wts["norm_mix"], wts["mh_norm"],
              wts["conv_w"], wts["w_out"], wts["norm_mlp"], wts["w_up"], wts["w_down"], wts["norm_ple"], wts["w_pg"],
              wts["w_pp"], wts["norm_final"]]
    in_smem = {3, 4}
    nj = seq // t
    nt = bsz * nj

    def cur(s):
        c = jnp.minimum(s, nt - 1)
        return c // nj, c % nj

    def lag(s):
        c = jnp.maximum(s - 1, 0)
        return c // nj, c % nj

    in_specs = [pl.BlockSpec((1, t, D_MODEL), lambda s: (*cur(s), 0)),
                pl.BlockSpec((1, t, PLE_DIM), lambda s: (*lag(s), 0))]
    in_specs += [_resident(c.shape) for c in (xs, ps, mix_s)]
    in_specs += [pl.BlockSpec(memory_space=pltpu.SMEM) if i in in_smem else _resident(c.shape)
                 for i, c in enumerate(consts)]
    out_shape = (jax.ShapeDtypeStruct((bsz, seq, D_MODEL), F32),
                 jax.ShapeDtypeStruct(xs.shape, F32),
                 jax.ShapeDtypeStruct((bsz, HEADS, HEAD_DIM, HEAD_DIM), F32),
                 jax.ShapeDtypeStruct((bsz, HEADS, HEAD_DIM), F32),
                 jax.ShapeDtypeStruct((bsz, SUBLANES, LANES), F32),
                 jax.ShapeDtypeStruct((bsz, CONV_WIDTH - 1, CONV_CH), F32))
    out_specs = (pl.BlockSpec((1, t, D_MODEL), lambda s: (*lag(s), 0)),
                 pl.BlockSpec(xs.shape, lambda s: (0, 0, 0)),
                 pl.BlockSpec((1, HEADS, HEAD_DIM, HEAD_DIM), lambda s: (cur(s)[0], 0, 0, 0)),
                 pl.BlockSpec((1, HEADS, HEAD_DIM), lambda s: (cur(s)[0], 0, 0)),
                 pl.BlockSpec((1, SUBLANES, LANES), lambda s: (cur(s)[0], 0, 0)),
                 pl.BlockSpec((1, CONV_WIDTH - 1, CONV_CH), lambda s: (cur(s)[0], 0, 0)))
    scratch = [pltpu.VMEM((HEADS, HEAD_DIM, 2 * HEAD_DIM), F32),
               pltpu.VMEM((SUBLANES, LANES), F32),
               pltpu.VMEM((t + SUBLANES, CONV_CH), F32),
               pltpu.VMEM((t, D_MODEL), BF16),
               pltpu.VMEM((t, D_MODEL), F32),
               pltpu.VMEM((CHUNK, CHUNK), BF16),
               pltpu.VMEM((t, PLE_DIM), F32)]
    return pl.pallas_call(
        functools.partial(_prompt_kernel, nj, nt),
        grid=(nt + 1,),
        in_specs=in_specs,
        out_specs=out_specs,
        out_shape=out_shape,
        scratch_shapes=scratch,
        compiler_params=pltpu.CompilerParams(
            dimension_semantics=("arbitrary",),
            vmem_limit_bytes=VMEM_LIMIT_BYTES),
        name="prompt_layer",
    )(x, p, xs, ps, mix_s, *consts)


def _sample_inproj_kernel(x_ref, wheads_ref, wconv_ref, wgc_ref, nmix_ref, zh_ref, zc_ref, g_ref):
    hb = _rms(x_ref[:, 0, :], nmix_ref[...]).astype(BF16)
    zh_ref[...] = _dot(hb, wheads_ref[...])
    zc_ref[...] = _dot(hb, wconv_ref[...])
    g_ref[...] = _dot(hb, wgc_ref[...])


def _sample_state_kernel(zq_ref, zk_ref, zv_ref, zo_ref, zb_ref, zc_ref, zu_ref, g_ref, c0_ref, n0_ref, m0_ref,
                         sc_ref, bi_ref, bf_ref, mh_ref, cw_ref,
                         mix_ref, cnew_ref, nnew_ref, mnew_ref, buf_ref,
                         inter_scr, wv_scr, cd_scr):
    tb = SAMPLE_TILE
    cu = zc_ref[...] * zu_ref[...]
    old0 = sc_ref[0, :, 0, :]
    old1 = sc_ref[0, :, 1, :]
    conv = cw_ref[0, 0:1, :] * old0 + cw_ref[0, 1:2, :] * old1 + cw_ref[0, 2:3, :] * cu
    mix_ref[:, MLSTM_W:D_MODEL] = (zb_ref[...] * conv).astype(BF16)
    buf_ref[0, :, 0, :] = old1
    buf_ref[0, :, 1, :] = cu

    g = g_ref[...]
    ig = g[:, 0:HEADS] + bi_ref[...]
    lf = _log_sigmoid(g[:, HEADS:2 * HEADS] + bf_ref[...])
    m_inter = lf + m0_ref[...]
    m_new = jnp.maximum(m_inter, ig)
    w_in = jnp.exp(ig - m_new)
    c_dec = jnp.exp(m_inter - m_new)
    e_neg_m = jnp.exp(-m_new)
    mnew_ref[...] = m_new

    vs, scores, qns = [], [], []
    for h in range(HEADS):
        hs = slice(h * HEAD_DIM, (h + 1) * HEAD_DIM)
        q = zq_ref[:, hs] * Q_SCALE
        k = zk_ref[:, hs]
        v = zv_ref[:, hs]
        n0 = n0_ref[0, :, h, :]
        wi = w_in[:, h:h + 1]
        cd = c_dec[:, h:h + 1]
        scores.append(jnp.sum(q * k, axis=1, keepdims=True) * wi)
        qns.append(jnp.sum(q * n0, axis=1, keepdims=True))
        wv_scr[:, hs] = wi * v
        cd_scr[:, hs] = jnp.broadcast_to(cd, (tb, HEAD_DIM))
        nnew_ref[0, :, h, :] = cd * n0 + wi * k
        vs.append(v)

    eye = (lax.broadcasted_iota(jnp.int32, (HEAD_DIM, HEAD_DIM), 0)
           == lax.broadcasted_iota(jnp.int32, (HEAD_DIM, HEAD_DIM), 1))

    for i in range(tb):
        row = slice(i, i + 1)
        for h in range(HEADS):
            hs = slice(h * HEAD_DIM, (h + 1) * HEAD_DIM)
            c0 = c0_ref[0, i, h]
            q_rows = jnp.broadcast_to(zq_ref[row, hs] * Q_SCALE, (SUBLANES, HEAD_DIM)).astype(BF16)
            inter_scr[row, hs] = _dot(q_rows, c0.astype(BF16))[0:1, :]
            k_diag = jnp.where(eye, jnp.broadcast_to(zk_ref[row, hs], (HEAD_DIM, HEAD_DIM)), 0.0).astype(BF16)
            v_rows = jnp.broadcast_to(wv_scr[row, hs], (HEAD_DIM, HEAD_DIM)).astype(BF16)
            cd = jnp.broadcast_to(cd_scr[row, hs], (HEAD_DIM, HEAD_DIM))
            cnew_ref[0, i, h] = cd * c0 + _dot(k_diag, v_rows)

    for h in range(HEADS):
        hs = slice(h * HEAD_DIM, (h + 1) * HEAD_DIM)
        cd = c_dec[:, h:h + 1]
        num = scores[h] * vs[h] + cd * inter_scr[:, hs]
        den = scores[h] + cd * qns[h]
        hh = num / jnp.maximum(jnp.abs(den), e_neg_m[:, h:h + 1])
        hn = hh * lax.rsqrt(jnp.mean(hh * hh, axis=-1, keepdims=True) + EPS)
        mix_ref[:, hs] = (hn * mh_ref[:, hs] * jax.nn.sigmoid(zo_ref[:, hs])).astype(BF16)


def _sample_calls(xs, c0, n0, m0, sconv, wts):
    nb = xs.shape[0]
    tb = SAMPLE_TILE
    assert nb % tb == 0 and c0.shape[0] == 1
    zh, zc, g = pl.pallas_call(
        _sample_inproj_kernel,
        out_shape=(jax.ShapeDtypeStruct((nb, 4 * MLSTM_W), F32),
                   jax.ShapeDtypeStruct((nb, 3 * CONV_CH), F32),
                   jax.ShapeDtypeStruct((nb, LANES), F32)),
        compiler_params=pltpu.CompilerParams(vmem_limit_bytes=VMEM_LIMIT_BYTES),
        name="sample_inproj",
    )(xs, wts["w_heads"], wts["w_conv"], wts["w_gc"], wts["norm_mix"])

    def zgroup(grp):
        return pl.BlockSpec((tb, MLSTM_W), lambda i, grp=grp: (i, grp))

    def zconv(grp):
        return pl.BlockSpec((tb, CONV_CH), lambda i, grp=grp: (i, grp))

    row_d = pl.BlockSpec((tb, D_MODEL), lambda i: (i, 0))
    row_h = pl.BlockSpec((tb, HEADS), lambda i: (i, 0))
    state = pl.BlockSpec((1, tb, HEADS, HEAD_DIM, HEAD_DIM), lambda i: (0, i, 0, 0, 0))
    nstate = pl.BlockSpec((1, tb, HEADS, HEAD_DIM), lambda i: (0, i, 0, 0))
    cstate = pl.BlockSpec((1, tb, CONV_WIDTH - 1, CONV_CH), lambda i: (0, i, 0, 0))

    def whole(shape):
        return pl.BlockSpec(shape, lambda i: (0,) * len(shape))

    mix, c_new, n_new, m_new, new_buf = pl.pallas_call(
        _sample_state_kernel,
        grid=(nb // tb,),
        in_specs=[zgroup(0), zgroup(1), zgroup(2), zgroup(3), zconv(0), zconv(1), zconv(2),
                  pl.BlockSpec((tb, LANES), lambda i: (i, 0)),
                  state, nstate, row_h, cstate,
                  whole((1, HEADS)), whole((1, HEADS)), whole((1, MLSTM_W)), whole((1, CONV_WIDTH, CONV_CH))],
        out_specs=(row_d, state, nstate, row_h, cstate),
        out_shape=(jax.ShapeDtypeStruct((nb, D_MODEL), BF16),
                   jax.ShapeDtypeStruct(c0.shape, F32),
                   jax.ShapeDtypeStruct(n0.shape, F32),
                   jax.ShapeDtypeStruct((nb, HEADS), F32),
                   jax.ShapeDtypeStruct(sconv.shape, F32)),
        scratch_shapes=[pltpu.VMEM((tb, MLSTM_W), F32)] * 3,
        compiler_params=pltpu.CompilerParams(dimension_semantics=("arbitrary",),
                                             vmem_limit_bytes=VMEM_LIMIT_BYTES),
        name="sample_state",
    )(zh, zh, zh, zh, zc, zc, zc, g, c0, n0, m0, sconv, wts["b_i"], wts["b_f"], wts["mh_norm"], wts["conv_w"])
    return mix, c_new, n_new, m_new, new_buf


def _weight_prep_kernel(win_ref, wout_ref, wup_ref, wdown_ref, wpg_ref, wpp_ref,
                        heads_ref, conv_ref, gt_ref, gc_ref, out_ref, up_ref, down_ref, pg_ref, pp_ref):
    g0 = 4 * MLSTM_W
    g1 = g0 + 2 * HEADS
    heads_ref[...] = win_ref[0:g0, :].T.astype(BF16)
    conv_ref[...] = win_ref[g1:, :].T.astype(BF16)
    gates = win_ref[g0:g1, :]
    rows = gates.shape[1]
    gt_ref[...] = jnp.concatenate([gates, jnp.zeros((GATE_ROWS - 2 * HEADS, rows), F32)], axis=0).astype(BF16)
    gc_ref[...] = jnp.concatenate([gates, jnp.zeros((LANES - 2 * HEADS, rows), F32)], axis=0).T.astype(BF16)
    out_ref[...] = wout_ref[...].astype(BF16)
    up_ref[...] = wup_ref[...].astype(BF16)
    down_ref[...] = wdown_ref[...].astype(BF16)
    pg_ref[...] = wpg_ref[...].astype(BF16)
    pp_ref[...] = wpp_ref[...].astype(BF16)


def _prepare_weights(norm_mix, w_in, b_gate_i, b_gate_f, mh_norm, conv_w, w_out, norm_mlp, w_up, w_down,
                     norm_ple, w_ple_gate, w_ple_proj, norm_final):
    w_in_t = jnp.swapaxes(w_in, 0, 1)
    srcs = (w_in_t, w_out, w_up, w_down, w_ple_gate, w_ple_proj)
    n_in = w_in.shape[1]
    outs = ((D_MODEL, 4 * MLSTM_W), (D_MODEL, 3 * CONV_CH), (GATE_ROWS, D_MODEL), (D_MODEL, LANES),
            w_out.shape, w_up.shape, w_down.shape, w_ple_gate.shape, w_ple_proj.shape)
    steps = PREP_STEPS

    def rows(shape):
        return pl.BlockSpec((shape[0] // steps, shape[1]), lambda i: (i, 0))

    out_specs = [rows(o) for o in outs]
    out_specs[2] = pl.BlockSpec((GATE_ROWS, D_MODEL // steps), lambda i: (0, i))
    assert n_in == 4 * MLSTM_W + 2 * HEADS + 3 * CONV_CH
    w_heads, w_conv, w_gt, w_gc, w_out_b, w_up_b, w_down_b, w_pg_b, w_pp_b = pl.pallas_call(
        _weight_prep_kernel,
        grid=(steps,),
        in_specs=[pl.BlockSpec((n_in, D_MODEL // steps), lambda i: (0, i))] + [rows(a.shape) for a in srcs[1:]],
        out_specs=out_specs,
        out_shape=[jax.ShapeDtypeStruct(o, BF16) for o in outs],
        compiler_params=pltpu.CompilerParams(dimension_semantics=("arbitrary",),
                                             vmem_limit_bytes=VMEM_LIMIT_BYTES),
        name="weight_prep",
    )(*srcs)
    return dict(
        w_heads=w_heads,
        w_conv=w_conv,
        w_gt=w_gt,
        w_gc=w_gc,
        b_i=b_gate_i.reshape(1, HEADS),
        b_f=b_gate_f.reshape(1, HEADS),
        norm_mix=norm_mix.reshape(1, D_MODEL),
        mh_norm=mh_norm.reshape(1, MLSTM_W),
        conv_w=conv_w,
        w_out=w_out_b,
        norm_mlp=norm_mlp.reshape(1, D_MODEL),
        w_up=w_up_b,
        w_down=w_down_b,
        norm_ple=norm_ple.reshape(1, D_MODEL),
        w_pg=w_pg_b,
        w_pp=w_pp_b,
        norm_final=norm_final.reshape(1, D_MODEL),
    )


def kernel(x_prompt, x_sample, state_mlstm_C, state_mlstm_n, state_mlstm_m, state_conv, p_prompt, p_sample,
           norm_mix, w_in, b_gate_i, b_gate_f, mh_norm, conv_w, w_out, norm_mlp, w_up, w_down, norm_ple,
           w_ple_gate, w_ple_proj, norm_final):
    assert norm_mix.shape[0] == 1, "single-layer trunk"
    wts = _prepare_weights(norm_mix[0], w_in[0], b_gate_i[0], b_gate_f[0], mh_norm[0], conv_w, w_out[0],
                           norm_mlp[0], w_up[0], w_down[0], norm_ple[0], w_ple_gate[0], w_ple_proj[0], norm_final)

    nb = x_sample.shape[0]
    xs = x_sample
    mix_s, c_new, n_new, m_new, new_buf = _sample_calls(
        xs, state_mlstm_C, state_mlstm_n, state_mlstm_m[0], state_conv, wts)

    y_prompt, ys, prompt_c, prompt_n, m_rows, conv_tail = _prompt_call(
        x_prompt, p_prompt[0], xs, p_sample, mix_s, wts)
    prompt_m = m_rows[:, 0:HEADS, 0]

    return (y_prompt, ys, prompt_c[None], prompt_n[None], prompt_m[None], conv_tail[None],
            c_new, n_new, m_new[None], new_buf)
```

```python
import functools
import math

import jax
import jax.numpy as jnp
from jax import lax
from jax.experimental import pallas as pl
from jax.experimental.pallas import tpu as pltpu

F32 = jnp.float32
BF16 = jnp.bfloat16

D_MODEL = 1024
HEADS = 4
HEAD_DIM = 128
MLSTM_W = HEADS * HEAD_DIM
CONV_CH = D_MODEL - MLSTM_W
CONV_WIDTH = 3
D_FF = 4 * D_MODEL
PLE_DIM = 256
EPS = 1e-6
M_INIT = -1e30
Q_SCALE = HEAD_DIM ** -0.5
LOG2E = math.log2(math.e)

GATE_ROWS = 16
PREP_STEPS = 4
LANES = 128
SUBLANES = 8
VMEM_LIMIT_BYTES = 60000 * 1024

SEQ_TILE = 512
CHUNK = 256
FF_CHUNK = 512
LAST_ROWS = 256
SAMPLE_TILE = 32


def _dot(a, b):
    return jnp.dot(a, b, preferred_element_type=F32)


def _dot_nt(a, b):
    return lax.dot_general(a, b, (((1,), (1,)), ((), ())), preferred_element_type=F32)


def _dot_tn(a, b):
    return lax.dot_general(a, b, (((0,), (0,)), ((), ())), preferred_element_type=F32)


def _rms(x, g):
    y = x * lax.rsqrt(jnp.mean(x * x, axis=-1, keepdims=True) + EPS)
    return y * g


def _log_sigmoid(x):
    return jnp.minimum(x, 0.0) - jnp.log1p(jnp.exp(-jnp.abs(x)))


def _mlp_chunk(xn_bf16, c, wup_ref, wdown_ref):
    cols = slice(c * FF_CHUNK, (c + 1) * FF_CHUNK)
    hf = jnp.maximum(_dot(xn_bf16, wup_ref[:, cols]), 0.0)
    return _dot((hf * hf).astype(BF16), wdown_ref[cols, :])


def _cummax_lanes(a):
    n = a.shape[1]
    lane = lax.broadcasted_iota(jnp.int32, a.shape, 1)
    d = 1
    while d < n:
        shifted = pltpu.roll(a, d, axis=1)
        a = jnp.maximum(a, jnp.where(lane >= d, shifted, -jnp.inf))
        d *= 2
    return a


def _prompt_kernel(tiles_per_seq, num_tiles,
                   x_ref, plag_ref, pcur_ref, xs_ref, ps_ref, mixs_ref,
                   wheads_ref, wconv_ref, wgt_ref, bi_ref, bf_ref, nmix_ref, mh_ref, cw_ref,
                   wout_ref, nmlp_ref, wup_ref, wdown_ref, nple_ref, wpg_ref, wpp_ref, nfin_ref,
                   y_ref, ys_ref, cout_ref, nout_ref, mout_ref, convout_ref, ylast_ref,
                   caug, m_scr, cu_buf, mix_scr, xres_scr, tri_scr, ps_scr, ylast_sem):
    t = SEQ_TILE
    lc = CHUNK
    ns = xs_ref.shape[0]
    s_id = pl.program_id(0)
    j = lax.rem(s_id, tiles_per_seq)

    @pl.when(s_id == 0)
    def _():
        mix_scr[0:ns, :] = mixs_ref[...]
        mix_scr[ns:t, :] = jnp.zeros((t - ns, D_MODEL), BF16)
        xres_scr[0:ns, :] = xs_ref[:, 0, :]
        xres_scr[ns:t, :] = jnp.zeros((t - ns, D_MODEL), F32)
        ps_scr[0:ns, :] = ps_ref[0, :, 0, :]
        ps_scr[ns:t, :] = jnp.zeros((t - ns, PLE_DIM), F32)
        tri_scr[...] = (lax.broadcasted_iota(jnp.int32, (lc, lc), 0)
                        <= lax.broadcasted_iota(jnp.int32, (lc, lc), 1)).astype(F32).astype(BF16)

    @pl.when(j == 0)
    def _():
        caug[...] = jnp.zeros_like(caug)
        m_scr[...] = jnp.full(m_scr.shape, M_INIT, F32)
        cu_buf[0:SUBLANES, :] = jnp.zeros((SUBLANES, CONV_CH), F32)

    x1 = xres_scr[...] + _dot(mix_scr[...], wout_ref[...])
    xn = _rms(x1, nmlp_ref[...]).astype(BF16)

    x = x_ref[0]
    hb = _rms(x, nmix_ref[...]).astype(BF16)
    xres_scr[...] = x
    gt = _dot_nt(wgt_ref[...], hb)

    row_id = lax.broadcasted_iota(jnp.int32, (lc, lc), 0)
    col_id = lax.broadcasted_iota(jnp.int32, (lc, lc), 1)
    causal = col_id <= row_id
    lane_id = lax.broadcasted_iota(jnp.int32, (lc, HEAD_DIM), 1)

    acc = jnp.zeros((t, D_MODEL), F32)
    n_phases = (t // lc) * HEADS
    mlp_per_phase = D_FF // FF_CHUNK // n_phases
    mlp_first = (mlp_per_phase + 1) // 2
    next_mlp = 0
    m_carry = m_scr[...]
    sub_id = lax.broadcasted_iota(jnp.int32, (SUBLANES, lc), 0)
    bias_i = jnp.zeros((SUBLANES, lc), F32)
    bias_f = jnp.zeros((SUBLANES, lc), F32)
    for h in range(HEADS):
        bias_i = jnp.where(sub_id == h, bi_ref[0, h], bias_i)
        bias_f = jnp.where(sub_id == h, bf_ref[0, h], bias_f)
    for c in range(t // lc):
        tok = slice(c * lc, (c + 1) * lc)
        hb_c = hb[tok, :]
        g8 = gt[0:SUBLANES, tok]
        ig = g8 + bias_i
        lf = _log_sigmoid(pltpu.roll(g8, HEADS, axis=0) + bias_f)
        hi = lf.astype(BF16)
        r1 = lf - hi.astype(F32)
        mid = r1.astype(BF16)
        lo = (r1 - mid.astype(F32)).astype(BF16)
        parts = _dot(jnp.concatenate([hi, mid, lo, jnp.zeros_like(lo)], axis=0), tri_scr[...])
        b = parts[0:8] + parts[8:16] + parts[16:24]
        a = ig - b
        m_prev = jnp.concatenate([m_carry] * (lc // LANES), axis=1)
        g = jnp.maximum(_cummax_lanes(a), m_prev)
        m_t = b + g
        b_last = b[:, lc - 1:lc]
        m_new = m_t[:, lc - 1:lc]
        decay = jnp.exp(m_prev - g)
        e_neg_m = jnp.exp(-m_t)
        w_state = jnp.exp(a + (b_last - m_new))
        c_dec = jnp.exp(b_last + m_prev[:, 0:1] - m_new)
        m_carry = jnp.broadcast_to(m_new, m_carry.shape)
        a2 = a * LOG2E
        rows = jnp.concatenate(
            [g * (-LOG2E), decay, e_neg_m, w_state, jnp.zeros((LANES - 4 * SUBLANES, lc), F32)], axis=0)
        cols = rows.T

        for h in range(HEADS):
            if h % 2 == 0:
                pair = slice(h * HEAD_DIM, (h + 2) * HEAD_DIM)
                zq2, zk2, zv2, zo2 = (
                    _dot(hb_c, wheads_ref[:, grp * MLSTM_W + pair.start:grp * MLSTM_W + pair.stop])
                    for grp in range(4))
            half = slice((h % 2) * HEAD_DIM, (h % 2 + 1) * HEAD_DIM)
            qs = (zq2[:, half] * Q_SCALE).astype(BF16)
            kb = zk2[:, half].astype(BF16)
            v = zv2[:, half]
            c_col = cols[:, h:h + 1]
            dec = cols[:, SUBLANES + h:SUBLANES + h + 1]
            enm = cols[:, 2 * SUBLANES + h:2 * SUBLANES + h + 1]
            wst = cols[:, 3 * SUBLANES + h:3 * SUBLANES + h + 1]
            s = _dot_nt(qs, kb)

            for _ in range(mlp_first):
                acc = acc + _mlp_chunk(xn, next_mlp, wup_ref, wdown_ref)
                next_mlp += 1

            dmat = jnp.exp2(jnp.where(causal, c_col + a2[h:h + 1, :], -jnp.inf))
            pm = s * dmat
            row_sum = jnp.sum(pm, axis=1, keepdims=True)
            intra = _dot(pm.astype(BF16), v.astype(BF16))
            c_state = caug[h]
            inter = _dot(qs, c_state.astype(BF16))
            vw = jnp.concatenate([v * wst, jnp.where(lane_id == 0, wst, 0.0)], axis=1).astype(BF16)
            caug[h] = c_dec[h:h + 1, :] * c_state + _dot_tn(kb, vw)

            for _ in range(mlp_per_phase - mlp_first):
                acc = acc + _mlp_chunk(xn, next_mlp, wup_ref, wdown_ref)
                next_mlp += 1

            num = intra + dec * inter[:, 0:HEAD_DIM]
            den = row_sum + dec * inter[:, HEAD_DIM:HEAD_DIM + 1]
            hh = num / jnp.maximum(jnp.abs(den), enm)
            hn = hh * lax.rsqrt(jnp.mean(hh * hh, axis=-1, keepdims=True) + EPS)
            hn = hn * mh_ref[:, h * HEAD_DIM:(h + 1) * HEAD_DIM]
            mix_scr[tok, h * HEAD_DIM:(h + 1) * HEAD_DIM] = (
                hn * jax.nn.sigmoid(zo2[:, half])).astype(BF16)
    m_scr[...] = m_carry
    assert next_mlp == D_FF // FF_CHUNK

    zc = _dot(hb, wconv_ref[...])
    p_tail = jnp.where(s_id == 0, ps_scr[...], plag_ref[0])
    ple = _dot(p_tail.astype(BF16), wpp_ref[...])

    def tail_rows(rows):
        x2 = x1[rows, :] + acc[rows, :]
        gate = jax.nn.sigmoid(_dot(_rms(x2, nple_ref[...]).astype(BF16), wpg_ref[...]))
        y_ref[0, rows, :] = _rms(x2 + gate * ple[rows, :], nfin_ref[...])

    tail_rows(slice(0, t // 2))

    cu = zc[:, CONV_CH:2 * CONV_CH] * zc[:, 2 * CONV_CH:3 * CONV_CH]
    cu_buf[SUBLANES:SUBLANES + t, :] = cu
    conv = (cw_ref[0, 0:1, :] * cu_buf[SUBLANES - 2:SUBLANES - 2 + t, :]
            + cw_ref[0, 1:2, :] * cu_buf[SUBLANES - 1:SUBLANES - 1 + t, :]
            + cw_ref[0, 2:3, :] * cu)
    mix_scr[:, MLSTM_W:D_MODEL] = (zc[:, 0:CONV_CH] * conv).astype(BF16)
    cu_buf[0:SUBLANES, :] = cu_buf[t:t + SUBLANES, :]

    tail_rows(slice(t // 2, t))

    @pl.when(s_id == 0)
    def _():
        ys_ref[:, 0, :] = y_ref[0, 0:ns, :]

    @pl.when(s_id == num_tiles - 1)
    def _():
        rb = LAST_ROWS

        def row_block(r, carry):
            rows = pl.ds(pl.multiple_of(r * rb, rb), rb)
            x1_l = x_ref[0, rows, :] + _dot(mix_scr[rows, :], wout_ref[...])
            xn_l = _rms(x1_l, nmlp_ref[...]).astype(BF16)
            acc_l = _mlp_chunk(xn_l, 0, wup_ref, wdown_ref)
            for ck in range(1, D_FF // FF_CHUNK):
                acc_l = acc_l + _mlp_chunk(xn_l, ck, wup_ref, wdown_ref)
            x2 = x1_l + acc_l
            gate = jax.nn.sigmoid(_dot(_rms(x2, nple_ref[...]).astype(BF16), wpg_ref[...]))
            ple_l = _dot(pcur_ref[0, rows, :].astype(BF16), wpp_ref[...])
            xres_scr[rows, :] = _rms(x2 + gate * ple_l, nfin_ref[...])
            return carry

        lax.fori_loop(0, t // rb, row_block, 0)
        cp = pltpu.make_async_copy(xres_scr, ylast_ref, ylast_sem)
        cp.start()
        cp.wait()

    @pl.when(j == tiles_per_seq - 1)
    def _():
        for h in range(HEADS):
            cout_ref[0, h] = caug[h, :, 0:HEAD_DIM]
            nout_ref[0, h:h + 1, :] = caug[h, :, HEAD_DIM:2 * HEAD_DIM].T[0:1, :]
        mout_ref[0] = m_scr[...]
        convout_ref[0] = cu[t - (CONV_WIDTH - 1):t, :]


def _resident(shape):
    return pl.BlockSpec(shape, lambda *_: (0,) * len(shape), pipeline_mode=pl.Buffered(1))


def _prompt_call(x, p, xs, ps, mix_s, wts):
    bsz, seq, _ = x.shape
    t = SEQ_TILE
    assert seq % t == 0 and t % CHUNK == 0 and CHUNK % LANES == 0 and xs.shape[0] <= t
    consts = [wts["w_heads"], wts["w_conv"], wts["w_gt"], wts["b_i"], wts["b_f"], wts["norm_mix"], wts["mh_norm"],
              wts["conv_w"], wts["w_out"], wts["norm_mlp"], wts["w_up"], wts["w_down"], wts["norm_ple"], wts["w_pg"],
              wts["w_pp"], wts["norm_final"]]
    in_smem = {3, 4}
    nj = seq // t
    nt = bsz * nj

    def cur(s):
        return s // nj, s % nj

    def lag(s):
        c = jnp.maximum(s - 1, 0)
        return c // nj, c % nj

    in_specs = [pl.BlockSpec((1, t, D_MODEL), lambda s: (*cur(s), 0)),
                pl.BlockSpec((1, t, PLE_DIM), lambda s: (*lag(s), 0)),
                pl.BlockSpec((1, t, PLE_DIM), lambda s: (*cur(s), 0))]
    in_specs += [_resident(c.shape) for c in (xs, ps, mix_s)]
    in_specs += [pl.BlockSpec(memory_space=pltpu.SMEM) if i in in_smem else _resident(c.shape)
                 for i, c in enumerate(consts)]
    out_shape = (jax.ShapeDtypeStruct((bsz, seq, D_MODEL), F32),
                 jax.ShapeDtypeStruct(xs.shape, F32),
                 jax.ShapeDtypeStruct((bsz, HEADS, HEAD_DIM, HEAD_DIM), F32),
                 jax.ShapeDtypeStruct((bsz, HEADS, HEAD_DIM), F32),
                 jax.ShapeDtypeStruct((bsz, SUBLANES, LANES), F32),
                 jax.ShapeDtypeStruct((bsz, CONV_WIDTH - 1, CONV_CH), F32),
                 jax.ShapeDtypeStruct((t, D_MODEL), F32))
    out_specs = (pl.BlockSpec((1, t, D_MODEL), lambda s: (*lag(s), 0)),
                 pl.BlockSpec(xs.shape, lambda s: (0, 0, 0)),
                 pl.BlockSpec((1, HEADS, HEAD_DIM, HEAD_DIM), lambda s: (cur(s)[0], 0, 0, 0)),
                 pl.BlockSpec((1, HEADS, HEAD_DIM), lambda s: (cur(s)[0], 0, 0)),
                 pl.BlockSpec((1, SUBLANES, LANES), lambda s: (cur(s)[0], 0, 0)),
                 pl.BlockSpec((1, CONV_WIDTH - 1, CONV_CH), lambda s: (cur(s)[0], 0, 0)),
                 pl.BlockSpec(memory_space=pl.ANY))
    scratch = [pltpu.VMEM((HEADS, HEAD_DIM, 2 * HEAD_DIM), F32),
               pltpu.VMEM((SUBLANES, LANES), F32),
               pltpu.VMEM((t + SUBLANES, CONV_CH), F32),
               pltpu.VMEM((t, D_MODEL), BF16),
               pltpu.VMEM((t, D_MODEL), F32),
               pltpu.VMEM((CHUNK, CHUNK), BF16),
               pltpu.VMEM((t, PLE_DIM), F32),
               pltpu.SemaphoreType.DMA(())]
    *outs, y_last = pl.pallas_call(
        functools.partial(_prompt_kernel, nj, nt),
        grid=(nt,),
        in_specs=in_specs,
        out_specs=out_specs,
        out_shape=out_shape,
        scratch_shapes=scratch,
        compiler_params=pltpu.CompilerParams(
            dimension_semantics=("arbitrary",),
            vmem_limit_bytes=VMEM_LIMIT_BYTES),
        name="prompt_layer",
    )(x, p, p, xs, ps, mix_s, *consts)
    outs[0] = lax.dynamic_update_slice(outs[0], y_last[None], (bsz - 1, seq - t, 0))
    return tuple(outs)


def _sample_inproj_kernel(x_ref, wheads_ref, wconv_ref, wgc_ref, nmix_ref, zh_ref, zc_ref, g_ref):
    hb = _rms(x_ref[:, 0, :], nmix_ref[...]).astype(BF16)
    zh_ref[...] = _dot(hb, wheads_ref[...])
    zc_ref[...] = _dot(hb, wconv_ref[...])
    g_ref[...] = _dot(hb, wgc_ref[...])


def _sample_state_kernel(zq_ref, zk_ref, zv_ref, zo_ref, zb_ref, zc_ref, zu_ref, g_ref, c0_ref, n0_ref, m0_ref,
                         sc_ref, bi_ref, bf_ref, mh_ref, cw_ref,
                         mix_ref, cnew_ref, nnew_ref, mnew_ref, buf_ref,
                         inter_scr, wv_scr, cd_scr):
    tb = SAMPLE_TILE
    cu = zc_ref[...] * zu_ref[...]
    old0 = sc_ref[0, :, 0, :]
    old1 = sc_ref[0, :, 1, :]
    conv = cw_ref[0, 0:1, :] * old0 + cw_ref[0, 1:2, :] * old1 + cw_ref[0, 2:3, :] * cu
    mix_ref[:, MLSTM_W:D_MODEL] = (zb_ref[...] * conv).astype(BF16)
    buf_ref[0, :, 0, :] = old1
    buf_ref[0, :, 1, :] = cu

    g = g_ref[...]
    ig = g[:, 0:HEADS] + bi_ref[...]
    lf = _log_sigmoid(g[:, HEADS:2 * HEADS] + bf_ref[...])
    m_inter = lf + m0_ref[...]
    m_new = jnp.maximum(m_inter, ig)
    w_in = jnp.exp(ig - m_new)
    c_dec = jnp.exp(m_inter - m_new)
    e_neg_m = jnp.exp(-m_new)
    mnew_ref[...] = m_new

    vs, scores, qns = [], [], []
    for h in range(HEADS):
        hs = slice(h * HEAD_DIM, (h + 1) * HEAD_DIM)
        q = zq_ref[:, hs] * Q_SCALE
        k = zk_ref[:, hs]
        v = zv_ref[:, hs]
        n0 = n0_ref[0, :, h, :]
        wi = w_in[:, h:h + 1]
        cd = c_dec[:, h:h + 1]
        scores.append(jnp.sum(q * k, axis=1, keepdims=True) * wi)
        qns.append(jnp.sum(q * n0, axis=1, keepdims=True))
        wv_scr[:, hs] = wi * v
        cd_scr[:, hs] = jnp.broadcast_to(cd, (tb, HEAD_DIM))
        nnew_ref[0, :, h, :] = cd * n0 + wi * k
        vs.append(v)

    eye = (lax.broadcasted_iota(jnp.int32, (HEAD_DIM, HEAD_DIM), 0)
           == lax.broadcasted_iota(jnp.int32, (HEAD_DIM, HEAD_DIM), 1))

    for i in range(tb):
        row = slice(i, i + 1)
        for h in range(HEADS):
            hs = slice(h * HEAD_DIM, (h + 1) * HEAD_DIM)
            c0 = c0_ref[0, i, h]
            q_rows = jnp.broadcast_to(zq_ref[row, hs] * Q_SCALE, (SUBLANES, HEAD_DIM)).astype(BF16)
            inter_scr[row, hs] = _dot(q_rows, c0.astype(BF16))[0:1, :]
            k_diag = jnp.where(eye, jnp.broadcast_to(zk_ref[row, hs], (HEAD_DIM, HEAD_DIM)), 0.0).astype(BF16)
            v_rows = jnp.broadcast_to(wv_scr[row, hs], (HEAD_DIM, HEAD_DIM)).astype(BF16)
            cd = jnp.broadcast_to(cd_scr[row, hs], (HEAD_DIM, HEAD_DIM))
            cnew_ref[0, i, h] = cd * c0 + _dot(k_diag, v_rows)

    for h in range(HEADS):
        hs = slice(h * HEAD_DIM, (h + 1) * HEAD_DIM)
        cd = c_dec[:, h:h + 1]
        num = scores[h] * vs[h] + cd * inter_scr[:, hs]
        den = scores[h] + cd * qns[h]
        hh = num / jnp.maximum(jnp.abs(den), e_neg_m[:, h:h + 1])
        hn = hh * lax.rsqrt(jnp.mean(hh * hh, axis=-1, keepdims=True) + EPS)
        mix_ref[:, hs] = (hn * mh_ref[:, hs] * jax.nn.sigmoid(zo_ref[:, hs])).astype(BF16)


def _sample_calls(xs, c0, n0, m0, sconv, wts):
    nb = xs.shape[0]
    tb = SAMPLE_TILE
    assert nb % tb == 0 and c0.shape[0] == 1
    zh, zc, g = pl.pallas_call(
        _sample_inproj_kernel,
        out_shape=(jax.ShapeDtypeStruct((nb, 4 * MLSTM_W), F32),
                   jax.ShapeDtypeStruct((nb, 3 * CONV_CH), F32),
                   jax.ShapeDtypeStruct((nb, LANES), F32)),
        compiler_params=pltpu.CompilerParams(vmem_limit_bytes=VMEM_LIMIT_BYTES),
        name="sample_inproj",
    )(xs, wts["w_heads"], wts["w_conv"], wts["w_gc"], wts["norm_mix"])

    def zgroup(grp):
        return pl.BlockSpec((tb, MLSTM_W), lambda i, grp=grp: (i, grp))

    def zconv(grp):
        return pl.BlockSpec((tb, CONV_CH), lambda i, grp=grp: (i, grp))

    row_d = pl.BlockSpec((tb, D_MODEL), lambda i: (i, 0))
    row_h = pl.BlockSpec((tb, HEADS), lambda i: (i, 0))
    state = pl.BlockSpec((1, tb, HEADS, HEAD_DIM, HEAD_DIM), lambda i: (0, i, 0, 0, 0))
    nstate = pl.BlockSpec((1, tb, HEADS, HEAD_DIM), lambda i: (0, i, 0, 0))
    cstate = pl.BlockSpec((1, tb, CONV_WIDTH - 1, CONV_CH), lambda i: (0, i, 0, 0))

    def whole(shape):
        return pl.BlockSpec(shape, lambda i: (0,) * len(shape))

    mix, c_new, n_new, m_new, new_buf = pl.pallas_call(
        _sample_state_kernel,
        grid=(nb // tb,),
        in_specs=[zgroup(0), zgroup(1), zgroup(2), zgroup(3), zconv(0), zconv(1), zconv(2),
                  pl.BlockSpec((tb, LANES), lambda i: (i, 0)),
                  state, nstate, row_h, cstate,
                  whole((1, HEADS)), whole((1, HEADS)), whole((1, MLSTM_W)), whole((1, CONV_WIDTH, CONV_CH))],
        out_specs=(row_d, state, nstate, row_h, cstate),
        out_shape=(jax.ShapeDtypeStruct((nb, D_MODEL), BF16),
                   jax.ShapeDtypeStruct(c0.shape, F32),
                   jax.ShapeDtypeStruct(n0.shape, F32),
                   jax.ShapeDtypeStruct((nb, HEADS), F32),
                   jax.ShapeDtypeStruct(sconv.shape, F32)),
        scratch_shapes=[pltpu.VMEM((tb, MLSTM_W), F32)] * 3,
        compiler_params=pltpu.CompilerParams(dimension_semantics=("arbitrary",),
                                             vmem_limit_bytes=VMEM_LIMIT_BYTES),
        name="sample_state",
    )(zh, zh, zh, zh, zc, zc, zc, g, c0, n0, m0, sconv, wts["b_i"], wts["b_f"], wts["mh_norm"], wts["conv_w"])
    return mix, c_new, n_new, m_new, new_buf


def _weight_prep_kernel(win_ref, wout_ref, wup_ref, wdown_ref, wpg_ref, wpp_ref,
                        heads_ref, conv_ref, gt_ref, gc_ref, out_ref, up_ref, down_ref, pg_ref, pp_ref):
    g0 = 4 * MLSTM_W
    g1 = g0 + 2 * HEADS
    heads_ref[...] = win_ref[0:g0, :].T.astype(BF16)
    conv_ref[...] = win_ref[g1:, :].T.astype(BF16)
    gates = win_ref[g0:g1, :]
    rows = gates.shape[1]
    gt_ref[...] = jnp.concatenate([gates, jnp.zeros((GATE_ROWS - 2 * HEADS, rows), F32)], axis=0).astype(BF16)
    gc_ref[...] = jnp.concatenate([gates, jnp.zeros((LANES - 2 * HEADS, rows), F32)], axis=0).T.astype(BF16)
    out_ref[...] = wout_ref[...].astype(BF16)
    up_ref[...] = wup_ref[...].astype(BF16)
    down_ref[...] = wdown_ref[...].astype(BF16)
    pg_ref[...] = wpg_ref[...].astype(BF16)
    pp_ref[...] = wpp_ref[...].astype(BF16)


def _prepare_weights(norm_mix, w_in, b_gate_i, b_gate_f, mh_norm, conv_w, w_out, norm_mlp, w_up, w_down,
                     norm_ple, w_ple_gate, w_ple_proj, norm_final):
    w_in_t = jnp.swapaxes(w_in, 0, 1)
    srcs = (w_in_t, w_out, w_up, w_down, w_ple_gate, w_ple_proj)
    n_in = w_in.shape[1]
    outs = ((D_MODEL, 4 * MLSTM_W), (D_MODEL, 3 * CONV_CH), (GATE_ROWS, D_MODEL), (D_MODEL, LANES),
            w_out.shape, w_up.shape, w_down.shape, w_ple_gate.shape, w_ple_proj.shape)
    steps = PREP_STEPS

    def rows(shape):
        return pl.BlockSpec((shape[0] // steps, shape[1]), lambda i: (i, 0))

    out_specs = [rows(o) for o in outs]
    out_specs[2] = pl.BlockSpec((GATE_ROWS, D_MODEL // steps), lambda i: (0, i))
    assert n_in == 4 * MLSTM_W + 2 * HEADS + 3 * CONV_CH
    w_heads, w_conv, w_gt, w_gc, w_out_b, w_up_b, w_down_b, w_pg_b, w_pp_b = pl.pallas_call(
        _weight_prep_kernel,
        grid=(steps,),
        in_specs=[pl.BlockSpec((n_in, D_MODEL // steps), lambda i: (0, i))] + [rows(a.shape) for a in srcs[1:]],
        out_specs=out_specs,
        out_shape=[jax.ShapeDtypeStruct(o, BF16) for o in outs],
        compiler_params=pltpu.CompilerParams(dimension_semantics=("arbitrary",),
                                             vmem_limit_bytes=VMEM_LIMIT_BYTES),
        name="weight_prep",
    )(*srcs)
    return dict(
        w_heads=w_heads,
        w_conv=w_conv,
        w_gt=w_gt,
        w_gc=w_gc,
        b_i=b_gate_i.reshape(1, HEADS),
        b_f=b_gate_f.reshape(1, HEADS),
        norm_mix=norm_mix.reshape(1, D_MODEL),
        mh_norm=mh_norm.reshape(1, MLSTM_W),
        conv_w=conv_w,
        w_out=w_out_b,
        norm_mlp=norm_mlp.reshape(1, D_MODEL),
        w_up=w_up_b,
        w_down=w_down_b,
        norm_ple=norm_ple.reshape(1, D_MODEL),
        w_pg=w_pg_b,
        w_pp=w_pp_b,
        norm_final=norm_final.reshape(1, D_MODEL),
    )


def kernel(x_prompt, x_sample, state_mlstm_C, state_mlstm_n, state_mlstm_m, state_conv, p_prompt, p_sample,
           norm_mix, w_in, b_gate_i, b_gate_f, mh_norm, conv_w, w_out, norm_mlp, w_up, w_down, norm_ple,
           w_ple_gate, w_ple_proj, norm_final):
    assert norm_mix.shape[0] == 1, "single-layer trunk"
    wts = _prepare_weights(norm_mix[0], w_in[0], b_gate_i[0], b_gate_f[0], mh_norm[0], conv_w, w_out[0],
                           norm_mlp[0], w_up[0], w_down[0], norm_ple[0], w_ple_gate[0], w_ple_proj[0], norm_final)

    nb = x_sample.shape[0]
    xs = x_sample
    mix_s, c_new, n_new, m_new, new_buf = _sample_calls(
        xs, state_mlstm_C, state_mlstm_n, state_mlstm_m[0], state_conv, wts)

    y_prompt, ys, prompt_c, prompt_n, m_rows, conv_tail = _prompt_call(
        x_prompt, p_prompt[0], xs, p_sample, mix_s, wts)
    prompt_m = m_rows[:, 0:HEADS, 0]

    return (y_prompt, ys, prompt_c[None], prompt_n[None], prompt_m[None], conv_tail[None],
            c_new, n_new, m_new[None], new_buf)
```

```python
import functools
import math

import jax
import jax.numpy as jnp
from jax import lax
from jax.experimental import pallas as pl
from jax.experimental.pallas import tpu as pltpu

F32 = jnp.float32
BF16 = jnp.bfloat16

D_MODEL = 1024
HEADS = 4
HEAD_DIM = 128
MLSTM_W = HEADS * HEAD_DIM
CONV_CH = D_MODEL - MLSTM_W
CONV_WIDTH = 3
D_FF = 4 * D_MODEL
PLE_DIM = 256
EPS = 1e-6
M_INIT = -1e30
Q_SCALE = HEAD_DIM ** -0.5
LOG2E = math.log2(math.e)

GATE_ROWS = 16
PREP_STEPS = 4
LANES = 128
SUBLANES = 8
VMEM_LIMIT_BYTES = 60000 * 1024

SEQ_TILE = 512
CHUNK = 256
FF_CHUNK = 512
SAMPLE_TILE = 32


def _dot(a, b):
    return jnp.dot(a, b, preferred_element_type=F32)


def _dot_nt(a, b):
    return lax.dot_general(a, b, (((1,), (1,)), ((), ())), preferred_element_type=F32)


def _dot_tn(a, b):
    return lax.dot_general(a, b, (((0,), (0,)), ((), ())), preferred_element_type=F32)


def _rms(x, g):
    y = x * lax.rsqrt(jnp.mean(x * x, axis=-1, keepdims=True) + EPS)
    return y * g


def _log_sigmoid(x):
    return jnp.minimum(x, 0.0) - jnp.log1p(jnp.exp(-jnp.abs(x)))


def _mlp_chunk(xn_bf16, c, wup_ref, wdown_ref):
    cols = slice(c * FF_CHUNK, (c + 1) * FF_CHUNK)
    hf = jnp.maximum(_dot(xn_bf16, wup_ref[:, cols]), 0.0)
    return _dot((hf * hf).astype(BF16), wdown_ref[cols, :])


def _cummax_lanes(a):
    n = a.shape[1]
    lane = lax.broadcasted_iota(jnp.int32, a.shape, 1)
    d = 1
    while d < n:
        shifted = pltpu.roll(a, d, axis=1)
        a = jnp.maximum(a, jnp.where(lane >= d, shifted, -jnp.inf))
        d *= 2
    return a


def _prompt_kernel(tiles_per_seq, num_tiles,
                   x_ref, plag_ref, xs_ref, ps_ref, mixs_ref,
                   wheads_ref, wconv_ref, wgt_ref, bi_ref, bf_ref, nmix_ref, mh_ref, cw_ref,
                   wout_ref, nmlp_ref, wup_ref, wdown_ref, nple_ref, wpg_ref, wpp_ref, nfin_ref,
                   y_ref, ys_ref, cout_ref, nout_ref, mout_ref, convout_ref,
                   caug, m_scr, cu_buf, mix_scr, xres_scr, tri_scr, ps_scr, mall_scr):
    t = SEQ_TILE
    lc = CHUNK
    ns = xs_ref.shape[0]
    s_id = pl.program_id(0)
    is_real = s_id < num_tiles
    j = lax.rem(jnp.minimum(s_id, num_tiles - 1), tiles_per_seq)

    @pl.when(s_id == 0)
    def _():
        mix_scr[0:ns, :] = mixs_ref[...]
        mix_scr[ns:t, :] = jnp.zeros((t - ns, D_MODEL), BF16)
        xres_scr[0:ns, :] = xs_ref[:, 0, :]
        xres_scr[ns:t, :] = jnp.zeros((t - ns, D_MODEL), F32)
        ps_scr[0:ns, :] = ps_ref[0, :, 0, :]
        ps_scr[ns:t, :] = jnp.zeros((t - ns, PLE_DIM), F32)
        mall_scr[...] = jnp.zeros(mall_scr.shape, F32)
        tri_scr[...] = (lax.broadcasted_iota(jnp.int32, (lc, lc), 0)
                        <= lax.broadcasted_iota(jnp.int32, (lc, lc), 1)).astype(F32).astype(BF16)

    @pl.when(j == 0)
    def _():
        caug[...] = jnp.zeros_like(caug)
        m_scr[...] = jnp.full(m_scr.shape, M_INIT, F32)
        cu_buf[0:SUBLANES, :] = jnp.zeros((SUBLANES, CONV_CH), F32)

    x1 = xres_scr[...] + _dot(mix_scr[...], wout_ref[...])
    xn = _rms(x1, nmlp_ref[...]).astype(BF16)

    x = x_ref[0]
    hb = _rms(x, nmix_ref[...]).astype(BF16)
    xres_scr[...] = x
    gt = _dot_nt(wgt_ref[...], hb)

    row_id = lax.broadcasted_iota(jnp.int32, (lc, lc), 0)
    col_id = lax.broadcasted_iota(jnp.int32, (lc, lc), 1)
    causal = col_id <= row_id
    lane_id = lax.broadcasted_iota(jnp.int32, (lc, HEAD_DIM), 1)

    acc = jnp.zeros((t, D_MODEL), F32)
    n_phases = (t // lc) * HEADS
    mlp_per_phase = D_FF // FF_CHUNK // n_phases
    mlp_first = (mlp_per_phase + 1) // 2
    next_mlp = 0
    m_carry = m_scr[...]
    sub_id = lax.broadcasted_iota(jnp.int32, (SUBLANES, lc), 0)
    bias_i = jnp.zeros((SUBLANES, lc), F32)
    bias_f = jnp.zeros((SUBLANES, lc), F32)
    for h in range(HEADS):
        bias_i = jnp.where(sub_id == h, bi_ref[0, h], bias_i)
        bias_f = jnp.where(sub_id == h, bf_ref[0, h], bias_f)
    for c in range(t // lc):
        tok = slice(c * lc, (c + 1) * lc)
        hb_c = hb[tok, :]
        g8 = gt[0:SUBLANES, tok]
        ig = g8 + bias_i
        lf = _log_sigmoid(pltpu.roll(g8, HEADS, axis=0) + bias_f)
        hi = lf.astype(BF16)
        r1 = lf - hi.astype(F32)
        mid = r1.astype(BF16)
        lo = (r1 - mid.astype(F32)).astype(BF16)
        parts = _dot(jnp.concatenate([hi, mid, lo, jnp.zeros_like(lo)], axis=0), tri_scr[...])
        b = parts[0:8] + parts[8:16] + parts[16:24]
        a = ig - b
        m_prev = jnp.concatenate([m_carry] * (lc // LANES), axis=1)
        g = jnp.maximum(_cummax_lanes(a), m_prev)
        m_t = b + g
        b_last = b[:, lc - 1:lc]
        m_new = m_t[:, lc - 1:lc]
        decay = jnp.exp(m_prev - g)
        e_neg_m = jnp.exp(-m_t)
        w_state = jnp.exp(a + (b_last - m_new))
        c_dec = jnp.exp(b_last + m_prev[:, 0:1] - m_new)
        m_carry = jnp.broadcast_to(m_new, m_carry.shape)
        a2 = a * LOG2E
        rows = jnp.concatenate(
            [g * (-LOG2E), decay, e_neg_m, w_state, jnp.zeros((LANES - 4 * SUBLANES, lc), F32)], axis=0)
        cols = rows.T

        for h in range(HEADS):
            if h % 2 == 0:
                pair = slice(h * HEAD_DIM, (h + 2) * HEAD_DIM)
                zq2, zk2, zv2, zo2 = (
                    _dot(hb_c, wheads_ref[:, grp * MLSTM_W + pair.start:grp * MLSTM_W + pair.stop])
                    for grp in range(4))
            half = slice((h % 2) * HEAD_DIM, (h % 2 + 1) * HEAD_DIM)
            qs = (zq2[:, half] * Q_SCALE).astype(BF16)
            kb = zk2[:, half].astype(BF16)
            v = zv2[:, half]
            c_col = cols[:, h:h + 1]
            dec = cols[:, SUBLANES + h:SUBLANES + h + 1]
            enm = cols[:, 2 * SUBLANES + h:2 * SUBLANES + h + 1]
            wst = cols[:, 3 * SUBLANES + h:3 * SUBLANES + h + 1]
            s = _dot_nt(qs, kb)

            for _ in range(mlp_first):
                acc = acc + _mlp_chunk(xn, next_mlp, wup_ref, wdown_ref)
                next_mlp += 1

            dmat = jnp.exp2(jnp.where(causal, c_col + a2[h:h + 1, :], -jnp.inf))
            pm = s * dmat
            row_sum = jnp.sum(pm, axis=1, keepdims=True)
            intra = _dot(pm.astype(BF16), v.astype(BF16))
            c_state = caug[h]
            inter = _dot(qs, c_state.astype(BF16))
            vw = jnp.concatenate([v * wst, jnp.where(lane_id == 0, wst, 0.0)], axis=1).astype(BF16)
            caug[h] = c_dec[h:h + 1, :] * c_state + _dot_tn(kb, vw)

            for _ in range(mlp_per_phase - mlp_first):
                acc = acc + _mlp_chunk(xn, next_mlp, wup_ref, wdown_ref)
                next_mlp += 1

            num = intra + dec * inter[:, 0:HEAD_DIM]
            den = row_sum + dec * inter[:, HEAD_DIM:HEAD_DIM + 1]
            hh = num / jnp.maximum(jnp.abs(den), enm)
            hn = hh * lax.rsqrt(jnp.mean(hh * hh, axis=-1, keepdims=True) + EPS)
            hn = hn * mh_ref[:, h * HEAD_DIM:(h + 1) * HEAD_DIM]
            mix_scr[tok, h * HEAD_DIM:(h + 1) * HEAD_DIM] = (
                hn * jax.nn.sigmoid(zo2[:, half])).astype(BF16)
    m_scr[...] = m_carry
    assert next_mlp == D_FF // FF_CHUNK

    zc = _dot(hb, wconv_ref[...])
    p_tail = jnp.where(s_id == 0, ps_scr[...], plag_ref[0])
    ple = _dot(p_tail.astype(BF16), wpp_ref[...])

    def tail_rows(rows):
        x2 = x1[rows, :] + acc[rows, :]
        gate = jax.nn.sigmoid(_dot(_rms(x2, nple_ref[...]).astype(BF16), wpg_ref[...]))
        y_ref[0, rows, :] = _rms(x2 + gate * ple[rows, :], nfin_ref[...])

    tail_rows(slice(0, t // 2))

    cu = zc[:, CONV_CH:2 * CONV_CH] * zc[:, 2 * CONV_CH:3 * CONV_CH]
    cu_buf[SUBLANES:SUBLANES + t, :] = cu
    conv = (cw_ref[0] * cu_buf[SUBLANES - 2:SUBLANES - 2 + t, :]
            + cw_ref[1] * cu_buf[SUBLANES - 1:SUBLANES - 1 + t, :]
            + cw_ref[2] * cu)
    mix_scr[:, MLSTM_W:D_MODEL] = (zc[:, 0:CONV_CH] * conv).astype(BF16)
    cu_buf[0:SUBLANES, :] = cu_buf[t:t + SUBLANES, :]

    tail_rows(slice(t // 2, t))

    @pl.when(s_id == 0)
    def _():
        ys_ref[:, 0, :] = y_ref[0, 0:ns, :]

    @pl.when(jnp.logical_and(is_real, j == tiles_per_seq - 1))
    def _():
        for h in range(HEADS):
            cout_ref[0, h] = caug[h, :, 0:HEAD_DIM]
            nout_ref[0, h:h + 1, :] = caug[h, :, HEAD_DIM:2 * HEAD_DIM].T[0:1, :]
        seq_lane = lax.broadcasted_iota(jnp.int32, m_scr.shape, 1) == s_id // tiles_per_seq
        m_all = jnp.where(seq_lane, m_scr[...], mall_scr[...])
        mall_scr[...] = m_all
        mout_ref[...] = m_all[0:HEADS, 0:mout_ref.shape[1]]
        convout_ref[0] = cu[t - (CONV_WIDTH - 1):t, :]


def _resident(shape):
    return pl.BlockSpec(shape, lambda *_: (0,) * len(shape), pipeline_mode=pl.Buffered(1))


def _prompt_call(x, p, xs, ps, mix_s, wts):
    bsz, seq, _ = x.shape
    t = SEQ_TILE
    assert seq % t == 0 and t % CHUNK == 0 and CHUNK % LANES == 0 and xs.shape[0] <= t and bsz <= LANES
    consts = [wts["w_heads"], wts["w_conv"], wts["w_gt"], wts["b_i"], wts["b_f"], wts["norm_mix"], wts["mh_norm"],
              wts["conv_w"], wts["w_out"], wts["norm_mlp"], wts["w_up"], wts["w_down"], wts["norm_ple"], wts["w_pg"],
              wts["w_pp"], wts["norm_final"]]
    in_smem = {3, 4}
    nj = seq // t
    nt = bsz * nj

    def cur(s):
        c = jnp.minimum(s, nt - 1)
        return c // nj, c % nj

    def lag(s):
        c = jnp.maximum(s - 1, 0)
        return c // nj, c % nj

    in_specs = [pl.BlockSpec((1, t, D_MODEL), lambda s: (*cur(s), 0)),
                pl.BlockSpec((1, t, PLE_DIM), lambda s: (*lag(s), 0))]
    in_specs += [_resident(c.shape) for c in (xs, ps, mix_s)]
    in_specs += [pl.BlockSpec(memory_space=pltpu.SMEM) if i in in_smem else _resident(c.shape)
                 for i, c in enumerate(consts)]
    out_shape = (jax.ShapeDtypeStruct((bsz, seq, D_MODEL), F32),
                 jax.ShapeDtypeStruct(xs.shape, F32),
                 jax.ShapeDtypeStruct((bsz, HEADS, HEAD_DIM, HEAD_DIM), F32),
                 jax.ShapeDtypeStruct((bsz, HEADS, HEAD_DIM), F32),
                 jax.ShapeDtypeStruct((HEADS, bsz), F32),
                 jax.ShapeDtypeStruct((bsz, CONV_WIDTH - 1, CONV_CH), F32))
    out_specs = (pl.BlockSpec((1, t, D_MODEL), lambda s: (*lag(s), 0)),
                 pl.BlockSpec(xs.shape, lambda s: (0, 0, 0)),
                 pl.BlockSpec((1, HEADS, HEAD_DIM, HEAD_DIM), lambda s: (cur(s)[0], 0, 0, 0)),
                 pl.BlockSpec((1, HEADS, HEAD_DIM), lambda s: (cur(s)[0], 0, 0)),
                 pl.BlockSpec((HEADS, bsz), lambda s: (0, 0)),
                 pl.BlockSpec((1, CONV_WIDTH - 1, CONV_CH), lambda s: (cur(s)[0], 0, 0)))
    scratch = [pltpu.VMEM((HEADS, HEAD_DIM, 2 * HEAD_DIM), F32),
               pltpu.VMEM((SUBLANES, LANES), F32),
               pltpu.VMEM((t + SUBLANES, CONV_CH), F32),
               pltpu.VMEM((t, D_MODEL), BF16),
               pltpu.VMEM((t, D_MODEL), F32),
               pltpu.VMEM((CHUNK, CHUNK), BF16),
               pltpu.VMEM((t, PLE_DIM), F32),
               pltpu.VMEM((SUBLANES, LANES), F32)]
    return pl.pallas_call(
        functools.partial(_prompt_kernel, nj, nt),
        grid=(nt + 1,),
        in_specs=in_specs,
        out_specs=out_specs,
        out_shape=out_shape,
        scratch_shapes=scratch,
        compiler_params=pltpu.CompilerParams(
            dimension_semantics=("arbitrary",),
            vmem_limit_bytes=VMEM_LIMIT_BYTES),
        name="prompt_layer",
    )(x, p, xs, ps, mix_s, *consts)


def _sample_inproj_kernel(x_ref, wheads_ref, wconv_ref, wgc_ref, nmix_ref, zh_ref, zc_ref, g_ref):
    hb = _rms(x_ref[:, 0, :], nmix_ref[...]).astype(BF16)
    zh_ref[...] = _dot(hb, wheads_ref[...])
    zc_ref[...] = _dot(hb, wconv_ref[...])
    g_ref[...] = _dot(hb, wgc_ref[...])


def _sample_state_kernel(zq_ref, zk_ref, zv_ref, zo_ref, zb_ref, zc_ref, zu_ref, g_ref, c0_ref, n0_ref, m0_ref,
                         sc_ref, bi_ref, bf_ref, mh_ref, cw_ref,
                         mix_ref, cnew_ref, nnew_ref, mnew_ref, buf_ref,
                         inter_scr, wv_scr, cd_scr, m0_scr, mnew_scr):
    tb = SAMPLE_TILE
    step = pl.program_id(0)
    seqs = pl.ds(pl.multiple_of(step * tb, tb), tb)

    @pl.when(step == 0)
    def _():
        m0_scr[...] = jnp.zeros(m0_scr.shape, F32)
        m0_scr[0:HEADS, :] = m0_ref[...]
        m0_scr[...] = m0_scr[...].T

    cu = zc_ref[...] * zu_ref[...]
    old0 = sc_ref[0, :, 0, :]
    old1 = sc_ref[0, :, 1, :]
    conv = cw_ref[0] * old0 + cw_ref[1] * old1 + cw_ref[2] * cu
    mix_ref[:, MLSTM_W:D_MODEL] = (zb_ref[...] * conv).astype(BF16)
    buf_ref[0, :, 0, :] = old1
    buf_ref[0, :, 1, :] = cu

    g = g_ref[...]
    ig = g[:, 0:HEADS] + bi_ref[...]
    lf = _log_sigmoid(g[:, HEADS:2 * HEADS] + bf_ref[...])
    m_inter = lf + m0_scr[seqs, 0:HEADS]
    m_new = jnp.maximum(m_inter, ig)
    w_in = jnp.exp(ig - m_new)
    c_dec = jnp.exp(m_inter - m_new)
    e_neg_m = jnp.exp(-m_new)
    lane = lax.broadcasted_iota(jnp.int32, (tb, LANES), 1)
    m_wide = jnp.zeros((tb, LANES), F32)
    for h in range(HEADS):
        m_wide = jnp.where(lane == h, m_new[:, h:h + 1], m_wide)
    mnew_scr[seqs, :] = m_wide

    @pl.when(step == pl.num_programs(0) - 1)
    def _():
        mnew_ref[...] = mnew_scr[...].T[0:HEADS, :]

    vs, scores, qns = [], [], []
    for h in range(HEADS):
        hs = slice(h * HEAD_DIM, (h + 1) * HEAD_DIM)
        q = zq_ref[:, hs] * Q_SCALE
        k = zk_ref[:, hs]
        v = zv_ref[:, hs]
        n0 = n0_ref[0, :, h, :]
        wi = w_in[:, h:h + 1]
        cd = c_dec[:, h:h + 1]
        scores.append(jnp.sum(q * k, axis=1, keepdims=True) * wi)
        qns.append(jnp.sum(q * n0, axis=1, keepdims=True))
        wv_scr[:, hs] = wi * v
        cd_scr[:, hs] = jnp.broadcast_to(cd, (tb, HEAD_DIM))
        nnew_ref[0, :, h, :] = cd * n0 + wi * k
        vs.append(v)

    eye = (lax.broadcasted_iota(jnp.int32, (HEAD_DIM, HEAD_DIM), 0)
           == lax.broadcasted_iota(jnp.int32, (HEAD_DIM, HEAD_DIM), 1))

    for i in range(tb):
        row = slice(i, i + 1)
        for h in range(HEADS):
            hs = slice(h * HEAD_DIM, (h + 1) * HEAD_DIM)
            c0 = c0_ref[0, i, h]
            q_rows = jnp.broadcast_to(zq_ref[row, hs] * Q_SCALE, (SUBLANES, HEAD_DIM)).astype(BF16)
            inter_scr[row, hs] = _dot(q_rows, c0.astype(BF16))[0:1, :]
            k_diag = jnp.where(eye, jnp.broadcast_to(zk_ref[row, hs], (HEAD_DIM, HEAD_DIM)), 0.0).astype(BF16)
            v_rows = jnp.broadcast_to(wv_scr[row, hs], (HEAD_DIM, HEAD_DIM)).astype(BF16)
            cd = jnp.broadcast_to(cd_scr[row, hs], (HEAD_DIM, HEAD_DIM))
            cnew_ref[0, i, h] = cd * c0 + _dot(k_diag, v_rows)

    for h in range(HEADS):
        hs = slice(h * HEAD_DIM, (h + 1) * HEAD_DIM)
        cd = c_dec[:, h:h + 1]
        num = scores[h] * vs[h] + cd * inter_scr[:, hs]
        den = scores[h] + cd * qns[h]
        hh = num / jnp.maximum(jnp.abs(den), e_neg_m[:, h:h + 1])
        hn = hh * lax.rsqrt(jnp.mean(hh * hh, axis=-1, keepdims=True) + EPS)
        mix_ref[:, hs] = (hn * mh_ref[:, hs] * jax.nn.sigmoid(zo_ref[:, hs])).astype(BF16)


def _sample_calls(xs, c0, n0, m0, sconv, wts):
    nb = xs.shape[0]
    tb = SAMPLE_TILE
    assert nb % tb == 0 and c0.shape[0] == 1 and nb == LANES and m0.shape == (HEADS, nb)
    zh, zc, g = pl.pallas_call(
        _sample_inproj_kernel,
        out_shape=(jax.ShapeDtypeStruct((nb, 4 * MLSTM_W), F32),
                   jax.ShapeDtypeStruct((nb, 3 * CONV_CH), F32),
                   jax.ShapeDtypeStruct((nb, LANES), F32)),
        compiler_params=pltpu.CompilerParams(vmem_limit_bytes=VMEM_LIMIT_BYTES),
        name="sample_inproj",
    )(xs, wts["w_heads"], wts["w_conv"], wts["w_gc"], wts["norm_mix"])

    def zgroup(grp):
        return pl.BlockSpec((tb, MLSTM_W), lambda i, grp=grp: (i, grp))

    def zconv(grp):
        return pl.BlockSpec((tb, CONV_CH), lambda i, grp=grp: (i, grp))

    row_d = pl.BlockSpec((tb, D_MODEL), lambda i: (i, 0))
    state = pl.BlockSpec((1, tb, HEADS, HEAD_DIM, HEAD_DIM), lambda i: (0, i, 0, 0, 0))
    nstate = pl.BlockSpec((1, tb, HEADS, HEAD_DIM), lambda i: (0, i, 0, 0))
    cstate = pl.BlockSpec((1, tb, CONV_WIDTH - 1, CONV_CH), lambda i: (0, i, 0, 0))

    def whole(shape):
        return pl.BlockSpec(shape, lambda i: (0,) * len(shape))

    mix, c_new, n_new, m_new, new_buf = pl.pallas_call(
        _sample_state_kernel,
        grid=(nb // tb,),
        in_specs=[zgroup(0), zgroup(1), zgroup(2), zgroup(3), zconv(0), zconv(1), zconv(2),
                  pl.BlockSpec((tb, LANES), lambda i: (i, 0)),
                  state, nstate, whole((HEADS, nb)), cstate,
                  whole((1, HEADS)), whole((1, HEADS)), whole((1, MLSTM_W)), whole((CONV_WIDTH, 1, CONV_CH))],
        out_specs=(row_d, state, nstate, whole((HEADS, nb)), cstate),
        out_shape=(jax.ShapeDtypeStruct((nb, D_MODEL), BF16),
                   jax.ShapeDtypeStruct(c0.shape, F32),
                   jax.ShapeDtypeStruct(n0.shape, F32),
                   jax.ShapeDtypeStruct((HEADS, nb), F32),
                   jax.ShapeDtypeStruct(sconv.shape, F32)),
        scratch_shapes=[pltpu.VMEM((tb, MLSTM_W), F32)] * 3 + [pltpu.VMEM((nb, LANES), F32)] * 2,
        compiler_params=pltpu.CompilerParams(dimension_semantics=("arbitrary",),
                                             vmem_limit_bytes=VMEM_LIMIT_BYTES),
        name="sample_state",
    )(zh, zh, zh, zh, zc, zc, zc, g, c0, n0, m0, sconv, wts["b_i"], wts["b_f"], wts["mh_norm"], wts["conv_w"])
    return mix, c_new, n_new, m_new, new_buf


def _weight_prep_kernel(win_ref, wout_ref, wup_ref, wdown_ref, wpg_ref, wpp_ref,
                        heads_ref, conv_ref, gt_ref, gc_ref, out_ref, up_ref, down_ref, pg_ref, pp_ref):
    g0 = 4 * MLSTM_W
    g1 = g0 + 2 * HEADS
    heads_ref[...] = win_ref[0:g0, :].T.astype(BF16)
    conv_ref[...] = win_ref[g1:, :].T.astype(BF16)
    gates = win_ref[g0:g1, :]
    rows = gates.shape[1]
    gt_ref[...] = jnp.concatenate([gates, jnp.zeros((GATE_ROWS - 2 * HEADS, rows), F32)], axis=0).astype(BF16)
    gc_ref[...] = jnp.concatenate([gates, jnp.zeros((LANES - 2 * HEADS, rows), F32)], axis=0).T.astype(BF16)
    out_ref[...] = wout_ref[...].astype(BF16)
    up_ref[...] = wup_ref[...].astype(BF16)
    down_ref[...] = wdown_ref[...].astype(BF16)
    pg_ref[...] = wpg_ref[...].astype(BF16)
    pp_ref[...] = wpp_ref[...].astype(BF16)


def _prepare_weights(norm_mix, w_in, b_gate_i, b_gate_f, mh_norm, conv_w, w_out, norm_mlp, w_up, w_down,
                     norm_ple, w_ple_gate, w_ple_proj, norm_final):
    w_in_t = jnp.swapaxes(w_in, 0, 1)
    srcs = (w_in_t, w_out, w_up, w_down, w_ple_gate, w_ple_proj)
    n_in = w_in.shape[1]
    outs = ((D_MODEL, 4 * MLSTM_W), (D_MODEL, 3 * CONV_CH), (GATE_ROWS, D_MODEL), (D_MODEL, LANES),
            w_out.shape, w_up.shape, w_down.shape, w_ple_gate.shape, w_ple_proj.shape)
    steps = PREP_STEPS

    def rows(shape):
        return pl.BlockSpec((shape[0] // steps, shape[1]), lambda i: (i, 0))

    out_specs = [rows(o) for o in outs]
    out_specs[2] = pl.BlockSpec((GATE_ROWS, D_MODEL // steps), lambda i: (0, i))
    assert n_in == 4 * MLSTM_W + 2 * HEADS + 3 * CONV_CH
    w_heads, w_conv, w_gt, w_gc, w_out_b, w_up_b, w_down_b, w_pg_b, w_pp_b = pl.pallas_call(
        _weight_prep_kernel,
        grid=(steps,),
        in_specs=[pl.BlockSpec((n_in, D_MODEL // steps), lambda i: (0, i))] + [rows(a.shape) for a in srcs[1:]],
        out_specs=out_specs,
        out_shape=[jax.ShapeDtypeStruct(o, BF16) for o in outs],
        compiler_params=pltpu.CompilerParams(dimension_semantics=("arbitrary",),
                                             vmem_limit_bytes=VMEM_LIMIT_BYTES),
        name="weight_prep",
    )(*srcs)
    return dict(
        w_heads=w_heads,
        w_conv=w_conv,
        w_gt=w_gt,
        w_gc=w_gc,
        b_i=b_gate_i.reshape(1, HEADS),
        b_f=b_gate_f.reshape(1, HEADS),
        norm_mix=norm_mix.reshape(1, D_MODEL),
        mh_norm=mh_norm.reshape(1, MLSTM_W),
        conv_w=jnp.swapaxes(conv_w, 0, 1),
        w_out=w_out_b,
        norm_mlp=norm_mlp.reshape(1, D_MODEL),
        w_up=w_up_b,
        w_down=w_down_b,
        norm_ple=norm_ple.reshape(1, D_MODEL),
        w_pg=w_pg_b,
        w_pp=w_pp_b,
        norm_final=norm_final.reshape(1, D_MODEL),
    )


def kernel(x_prompt, x_sample, state_mlstm_C, state_mlstm_n, state_mlstm_m, state_conv, p_prompt, p_sample,
           norm_mix, w_in, b_gate_i, b_gate_f, mh_norm, conv_w, w_out, norm_mlp, w_up, w_down, norm_ple,
           w_ple_gate, w_ple_proj, norm_final):
    assert norm_mix.shape[0] == 1, "single-layer trunk"
    wts = _prepare_weights(norm_mix[0], w_in[0], b_gate_i[0], b_gate_f[0], mh_norm[0], conv_w, w_out[0],
                           norm_mlp[0], w_up[0], w_down[0], norm_ple[0], w_ple_gate[0], w_ple_proj[0], norm_final)

    xs = x_sample
    mix_s, c_new, n_new, m_new, new_buf = _sample_calls(
        xs, state_mlstm_C, state_mlstm_n, jnp.swapaxes(state_mlstm_m[0], 0, 1), state_conv, wts)

    y_prompt, ys, prompt_c, prompt_n, prompt_m, conv_tail = _prompt_call(
        x_prompt, p_prompt[0], xs, p_sample, mix_s, wts)

    return (y_prompt, ys, prompt_c[None], prompt_n[None], jnp.swapaxes(prompt_m, 0, 1)[None], conv_tail[None],
            c_new, n_new, jnp.swapaxes(m_new, 0, 1)[None], new_buf)
```

```python
import functools
import math

import jax
import jax.numpy as jnp
from jax import lax
from jax.experimental import pallas as pl
from jax.experimental.pallas import tpu as pltpu

F32 = jnp.float32
BF16 = jnp.bfloat16

D_MODEL = 1024
HEADS = 4
HEAD_DIM = 128
MLSTM_W = HEADS * HEAD_DIM
CONV_CH = D_MODEL - MLSTM_W
CONV_WIDTH = 3
D_FF = 4 * D_MODEL
PLE_DIM = 256
EPS = 1e-6
M_INIT = -1e30
Q_SCALE = HEAD_DIM ** -0.5
LOG2E = math.log2(math.e)

GATE_ROWS = 16
PREP_STEPS = 4
LANES = 128
SUBLANES = 8
VMEM_LIMIT_BYTES = 60000 * 1024

SEQ_TILE = 512
CHUNK = 256
FF_CHUNK = 512
SAMPLE_TILE = 32


def _dot(a, b):
    return jnp.dot(a, b, preferred_element_type=F32)


def _dot_nt(a, b):
    return lax.dot_general(a, b, (((1,), (1,)), ((), ())), preferred_element_type=F32)


def _dot_tn(a, b):
    return lax.dot_general(a, b, (((0,), (0,)), ((), ())), preferred_element_type=F32)


def _rms(x, g):
    y = x * lax.rsqrt(jnp.mean(x * x, axis=-1, keepdims=True) + EPS)
    return y * g


def _log_sigmoid(x):
    return jnp.minimum(x, 0.0) - jnp.log1p(jnp.exp(-jnp.abs(x)))


def _mlp_chunk(xn_bf16, c, wup_ref, wdown_ref):
    cols = slice(c * FF_CHUNK, (c + 1) * FF_CHUNK)
    hf = jnp.maximum(_dot(xn_bf16, wup_ref[:, cols]), 0.0)
    return _dot((hf * hf).astype(BF16), wdown_ref[cols, :])


def _cummax_lanes(a):
    n = a.shape[1]
    lane = lax.broadcasted_iota(jnp.int32, a.shape, 1)
    d = 1
    while d < n:
        shifted = pltpu.roll(a, d, axis=1)
        a = jnp.maximum(a, jnp.where(lane >= d, shifted, -jnp.inf))
        d *= 2
    return a


def _prompt_kernel(tiles_per_seq, num_tiles,
                   x_ref, plag_ref, xs_ref, ps_ref, mixs_ref,
                   wheads_ref, wconv_ref, wgt_ref, bi_ref, bf_ref, nmix_ref, mh_ref, cw_ref,
                   wout_ref, nmlp_ref, wup_ref, wdown_ref, nple_ref, wpg_ref, wpp_ref, nfin_ref,
                   y_ref, ys_ref, cout_ref, nout_ref, mout_ref, convout_ref,
                   caug, m_scr, cu_buf, mix_scr, xres_scr, tri_scr, ps_scr, mall_scr):
    t = SEQ_TILE
    lc = CHUNK
    ns = xs_ref.shape[0]
    s_id = pl.program_id(0)
    is_real = s_id < num_tiles
    j = lax.rem(jnp.minimum(s_id, num_tiles - 1), tiles_per_seq)

    @pl.when(s_id == 0)
    def _():
        mix_scr[0:ns, :] = mixs_ref[...]
        mix_scr[ns:t, :] = jnp.zeros((t - ns, D_MODEL), BF16)
        xres_scr[0:ns, :] = xs_ref[:, 0, :]
        xres_scr[ns:t, :] = jnp.zeros((t - ns, D_MODEL), F32)
        ps_scr[0:ns, :] = ps_ref[0, :, 0, :]
        ps_scr[ns:t, :] = jnp.zeros((t - ns, PLE_DIM), F32)
        mall_scr[...] = jnp.zeros(mall_scr.shape, F32)
        tri_scr[...] = (lax.broadcasted_iota(jnp.int32, (lc, lc), 0)
                        <= lax.broadcasted_iota(jnp.int32, (lc, lc), 1)).astype(F32).astype(BF16)

    @pl.when(j == 0)
    def _():
        caug[...] = jnp.zeros_like(caug)
        m_scr[...] = jnp.full(m_scr.shape, M_INIT, F32)
        cu_buf[0:SUBLANES, :] = jnp.zeros((SUBLANES, CONV_CH), F32)

    x1 = xres_scr[...] + _dot(mix_scr[...], wout_ref[...])
    xn = _rms(x1, nmlp_ref[...]).astype(BF16)

    x = x_ref[0]
    hb = _rms(x, nmix_ref[...]).astype(BF16)
    xres_scr[...] = x
    gt = _dot_nt(wgt_ref[...], hb)

    row_id = lax.broadcasted_iota(jnp.int32, (lc, lc), 0)
    col_id = lax.broadcasted_iota(jnp.int32, (lc, lc), 1)
    causal = col_id <= row_id
    lane_id = lax.broadcasted_iota(jnp.int32, (lc, HEAD_DIM), 1)

    acc = jnp.zeros((t, D_MODEL), F32)
    n_phases = (t // lc) * HEADS
    mlp_per_phase = D_FF // FF_CHUNK // n_phases
    mlp_first = (mlp_per_phase + 1) // 2
    next_mlp = 0
    m_carry = m_scr[...]
    sub_id = lax.broadcasted_iota(jnp.int32, (SUBLANES, lc), 0)
    bias_i = jnp.zeros((SUBLANES, lc), F32)
    bias_f = jnp.zeros((SUBLANES, lc), F32)
    for h in range(HEADS):
        bias_i = jnp.where(sub_id == h, bi_ref[0, h], bias_i)
        bias_f = jnp.where(sub_id == h, bf_ref[0, h], bias_f)
    for c in range(t // lc):
        tok = slice(c * lc, (c + 1) * lc)
        hb_c = hb[tok, :]
        g8 = gt[0:SUBLANES, tok]
        ig = g8 + bias_i
        lf = _log_sigmoid(pltpu.roll(g8, HEADS, axis=0) + bias_f)
        hi = lf.astype(BF16)
        r1 = lf - hi.astype(F32)
        mid = r1.astype(BF16)
        lo = (r1 - mid.astype(F32)).astype(BF16)
        parts = _dot(jnp.concatenate([hi, mid, lo, jnp.zeros_like(lo)], axis=0), tri_scr[...])
        b = parts[0:8] + parts[8:16] + parts[16:24]
        a = ig - b
        m_prev = jnp.concatenate([m_carry] * (lc // LANES), axis=1)
        g = jnp.maximum(_cummax_lanes(a), m_prev)
        m_t = b + g
        b_last = b[:, lc - 1:lc]
        m_new = m_t[:, lc - 1:lc]
        decay = jnp.exp(m_prev - g)
        e_neg_m = jnp.exp(-m_t)
        w_state = jnp.exp(a + (b_last - m_new))
        c_dec = jnp.exp(b_last + m_prev[:, 0:1] - m_new)
        m_carry = jnp.broadcast_to(m_new, m_carry.shape)
        a2 = a * LOG2E
        rows = jnp.concatenate(
            [g * (-LOG2E), decay, e_neg_m, w_state, jnp.zeros((LANES - 4 * SUBLANES, lc), F32)], axis=0)
        cols = rows.T

        for h in range(HEADS):
            if h % 2 == 0:
                pair = slice(h * HEAD_DIM, (h + 2) * HEAD_DIM)
                zq2, zk2, zv2, zo2 = (
                    _dot(hb_c, wheads_ref[:, grp * MLSTM_W + pair.start:grp * MLSTM_W + pair.stop])
                    for grp in range(4))
            half = slice((h % 2) * HEAD_DIM, (h % 2 + 1) * HEAD_DIM)
            qs = (zq2[:, half] * Q_SCALE).astype(BF16)
            kb = zk2[:, half].astype(BF16)
            v = zv2[:, half]
            c_col = cols[:, h:h + 1]
            dec = cols[:, SUBLANES + h:SUBLANES + h + 1]
            enm = cols[:, 2 * SUBLANES + h:2 * SUBLANES + h + 1]
            wst = cols[:, 3 * SUBLANES + h:3 * SUBLANES + h + 1]
            s = _dot_nt(qs, kb)

            for _ in range(mlp_first):
                acc = acc + _mlp_chunk(xn, next_mlp, wup_ref, wdown_ref)
                next_mlp += 1

            dmat = jnp.exp2(jnp.where(causal, c_col + a2[h:h + 1, :], -jnp.inf))
            pm = s * dmat
            row_sum = jnp.sum(pm, axis=1, keepdims=True)
            intra = _dot(pm.astype(BF16), v.astype(BF16))
            c_state = caug[h]
            inter = _dot(qs, c_state.astype(BF16))
            vw = jnp.concatenate([v * wst, jnp.where(lane_id == 0, wst, 0.0)], axis=1).astype(BF16)
            caug[h] = c_dec[h:h + 1, :] * c_state + _dot_tn(kb, vw)

            for _ in range(mlp_per_phase - mlp_first):
                acc = acc + _mlp_chunk(xn, next_mlp, wup_ref, wdown_ref)
                next_mlp += 1

            num = intra + dec * inter[:, 0:HEAD_DIM]
            den = row_sum + dec * inter[:, HEAD_DIM:HEAD_DIM + 1]
            hh = num / jnp.maximum(jnp.abs(den), enm)
            hn = hh * lax.rsqrt(jnp.mean(hh * hh, axis=-1, keepdims=True) + EPS)
            hn = hn * mh_ref[:, h * HEAD_DIM:(h + 1) * HEAD_DIM]
            mix_scr[tok, h * HEAD_DIM:(h + 1) * HEAD_DIM] = (
                hn * jax.nn.sigmoid(zo2[:, half])).astype(BF16)
    m_scr[...] = m_carry
    assert next_mlp == D_FF // FF_CHUNK

    zc = _dot(hb, wconv_ref[...])
    p_tail = jnp.where(s_id == 0, ps_scr[...], plag_ref[0])
    ple = _dot(p_tail.astype(BF16), wpp_ref[...])

    def tail_rows(rows):
        x2 = x1[rows, :] + acc[rows, :]
        gate = jax.nn.sigmoid(_dot(_rms(x2, nple_ref[...]).astype(BF16), wpg_ref[...]))
        y_ref[0, rows, :] = _rms(x2 + gate * ple[rows, :], nfin_ref[...])

    tail_rows(slice(0, t // 2))

    cu = zc[:, CONV_CH:2 * CONV_CH] * zc[:, 2 * CONV_CH:3 * CONV_CH]
    cu_buf[SUBLANES:SUBLANES + t, :] = cu
    conv = (cw_ref[0] * cu_buf[SUBLANES - 2:SUBLANES - 2 + t, :]
            + cw_ref[1] * cu_buf[SUBLANES - 1:SUBLANES - 1 + t, :]
            + cw_ref[2] * cu)
    mix_scr[:, MLSTM_W:D_MODEL] = (zc[:, 0:CONV_CH] * conv).astype(BF16)
    cu_buf[0:SUBLANES, :] = cu_buf[t:t + SUBLANES, :]

    tail_rows(slice(t // 2, t))

    @pl.when(s_id == 0)
    def _():
        ys_ref[:, 0, :] = y_ref[0, 0:ns, :]

    @pl.when(jnp.logical_and(is_real, j == tiles_per_seq - 1))
    def _():
        for h in range(HEADS):
            cout_ref[0, h] = caug[h, :, 0:HEAD_DIM]
            nout_ref[0, h:h + 1, :] = caug[h, :, HEAD_DIM:2 * HEAD_DIM].T[0:1, :]
        seq_lane = lax.broadcasted_iota(jnp.int32, m_scr.shape, 1) == s_id // tiles_per_seq
        m_all = jnp.where(seq_lane, m_scr[...], mall_scr[...])
        mall_scr[...] = m_all
        mout_ref[...] = m_all[0:HEADS, 0:mout_ref.shape[1]]
        convout_ref[0] = cu[t - (CONV_WIDTH - 1):t, :]


def _resident(shape):
    return pl.BlockSpec(shape, lambda *_: (0,) * len(shape), pipeline_mode=pl.Buffered(1))


def _prompt_call(x, p, xs, ps, mix_s, wts):
    bsz, seq, _ = x.shape
    t = SEQ_TILE
    assert seq % t == 0 and t % CHUNK == 0 and CHUNK % LANES == 0 and xs.shape[0] <= t and bsz <= LANES
    consts = [wts["w_heads"], wts["w_conv"], wts["w_gt"], wts["b_i"], wts["b_f"], wts["norm_mix"], wts["mh_norm"],
              wts["conv_w"], wts["w_out"], wts["norm_mlp"], wts["w_up"], wts["w_down"], wts["norm_ple"], wts["w_pg"],
              wts["w_pp"], wts["norm_final"]]
    in_smem = {3, 4}
    nj = seq // t
    nt = bsz * nj

    def cur(s):
        c = jnp.minimum(s, nt - 1)
        return c // nj, c % nj

    def lag(s):
        c = jnp.maximum(s - 1, 0)
        return c // nj, c % nj

    in_specs = [pl.BlockSpec((1, t, D_MODEL), lambda s: (*cur(s), 0)),
                pl.BlockSpec((1, t, PLE_DIM), lambda s: (*lag(s), 0))]
    in_specs += [_resident(c.shape) for c in (xs, ps, mix_s)]
    in_specs += [pl.BlockSpec(memory_space=pltpu.SMEM) if i in in_smem else _resident(c.shape)
                 for i, c in enumerate(consts)]
    out_shape = (jax.ShapeDtypeStruct((bsz, seq, D_MODEL), F32),
                 jax.ShapeDtypeStruct(xs.shape, F32),
                 jax.ShapeDtypeStruct((bsz, HEADS, HEAD_DIM, HEAD_DIM), F32),
                 jax.ShapeDtypeStruct((bsz, HEADS, HEAD_DIM), F32),
                 jax.ShapeDtypeStruct((HEADS, bsz), F32),
                 jax.ShapeDtypeStruct((bsz, CONV_WIDTH - 1, CONV_CH), F32))
    out_specs = (pl.BlockSpec((1, t, D_MODEL), lambda s: (*lag(s), 0)),
                 pl.BlockSpec(xs.shape, lambda s: (0, 0, 0)),
                 pl.BlockSpec((1, HEADS, HEAD_DIM, HEAD_DIM), lambda s: (cur(s)[0], 0, 0, 0)),
                 pl.BlockSpec((1, HEADS, HEAD_DIM), lambda s: (cur(s)[0], 0, 0)),
                 pl.BlockSpec((HEADS, bsz), lambda s: (0, 0)),
                 pl.BlockSpec((1, CONV_WIDTH - 1, CONV_CH), lambda s: (cur(s)[0], 0, 0)))
    scratch = [pltpu.VMEM((HEADS, HEAD_DIM, 2 * HEAD_DIM), F32),
               pltpu.VMEM((SUBLANES, LANES), F32),
               pltpu.VMEM((t + SUBLANES, CONV_CH), F32),
               pltpu.VMEM((t, D_MODEL), BF16),
               pltpu.VMEM((t, D_MODEL), F32),
               pltpu.VMEM((CHUNK, CHUNK), BF16),
               pltpu.VMEM((t, PLE_DIM), F32),
               pltpu.VMEM((SUBLANES, LANES), F32)]
    return pl.pallas_call(
        functools.partial(_prompt_kernel, nj, nt),
        grid=(nt + 1,),
        in_specs=in_specs,
        out_specs=out_specs,
        out_shape=out_shape,
        scratch_shapes=scratch,
        compiler_params=pltpu.CompilerParams(
            dimension_semantics=("arbitrary",),
            vmem_limit_bytes=VMEM_LIMIT_BYTES),
        name="prompt_layer",
    )(x, p, xs, ps, mix_s, *consts)


def _sample_state_kernel(zq_ref, zk_ref, zv_ref, zo_ref, zb_ref, zc_ref, zu_ref, g_ref, c0_ref, n0_ref, m0_ref,
                         sc_ref, bi_ref, bf_ref, mh_ref, cw_ref,
                         mix_ref, cnew_ref, nnew_ref, mnew_ref, buf_ref,
                         inter_scr, wv_scr, cd_scr, m0_scr, mnew_scr):
    tb = SAMPLE_TILE
    step = pl.program_id(0)
    seqs = pl.ds(pl.multiple_of(step * tb, tb), tb)

    @pl.when(step == 0)
    def _():
        m0_scr[...] = jnp.zeros(m0_scr.shape, F32)
        m0_scr[0:HEADS, :] = m0_ref[...]
        m0_scr[...] = m0_scr[...].T

    cu = zc_ref[...] * zu_ref[...]
    old0 = sc_ref[0, :, 0, :]
    old1 = sc_ref[0, :, 1, :]
    conv = cw_ref[0] * old0 + cw_ref[1] * old1 + cw_ref[2] * cu
    mix_ref[:, MLSTM_W:D_MODEL] = (zb_ref[...] * conv).astype(BF16)
    buf_ref[0, :, 0, :] = old1
    buf_ref[0, :, 1, :] = cu

    g = g_ref[...]
    ig = g[:, 0:HEADS] + bi_ref[...]
    lf = _log_sigmoid(g[:, HEADS:2 * HEADS] + bf_ref[...])
    m_inter = lf + m0_scr[seqs, 0:HEADS]
    m_new = jnp.maximum(m_inter, ig)
    w_in = jnp.exp(ig - m_new)
    c_dec = jnp.exp(m_inter - m_new)
    e_neg_m = jnp.exp(-m_new)
    lane = lax.broadcasted_iota(jnp.int32, (tb, LANES), 1)
    m_wide = jnp.zeros((tb, LANES), F32)
    for h in range(HEADS):
        m_wide = jnp.where(lane == h, m_new[:, h:h + 1], m_wide)
    mnew_scr[seqs, :] = m_wide

    @pl.when(step == pl.num_programs(0) - 1)
    def _():
        mnew_ref[...] = mnew_scr[...].T[0:HEADS, :]

    vs, scores, qns = [], [], []
    for h in range(HEADS):
        hs = slice(h * HEAD_DIM, (h + 1) * HEAD_DIM)
        q = zq_ref[:, hs] * Q_SCALE
        k = zk_ref[:, hs]
        v = zv_ref[:, hs]
        n0 = n0_ref[0, :, h, :]
        wi = w_in[:, h:h + 1]
        cd = c_dec[:, h:h + 1]
        scores.append(jnp.sum(q * k, axis=1, keepdims=True) * wi)
        qns.append(jnp.sum(q * n0, axis=1, keepdims=True))
        wv_scr[:, hs] = wi * v
        cd_scr[:, hs] = jnp.broadcast_to(cd, (tb, HEAD_DIM))
        nnew_ref[0, :, h, :] = cd * n0 + wi * k
        vs.append(v)

    eye = (lax.broadcasted_iota(jnp.int32, (HEAD_DIM, HEAD_DIM), 0)
           == lax.broadcasted_iota(jnp.int32, (HEAD_DIM, HEAD_DIM), 1))

    for i in range(tb):
        row = slice(i, i + 1)
        for h in range(HEADS):
            hs = slice(h * HEAD_DIM, (h + 1) * HEAD_DIM)
            c0 = c0_ref[0, i, h]
            q_rows = jnp.broadcast_to(zq_ref[row, hs] * Q_SCALE, (SUBLANES, HEAD_DIM)).astype(BF16)
            inter_scr[row, hs] = _dot(q_rows, c0.astype(BF16))[0:1, :]
            k_diag = jnp.where(eye, jnp.broadcast_to(zk_ref[row, hs], (HEAD_DIM, HEAD_DIM)), 0.0).astype(BF16)
            v_rows = jnp.broadcast_to(wv_scr[row, hs], (HEAD_DIM, HEAD_DIM)).astype(BF16)
            cd = jnp.broadcast_to(cd_scr[row, hs], (HEAD_DIM, HEAD_DIM))
            cnew_ref[0, i, h] = cd * c0 + _dot(k_diag, v_rows)

    for h in range(HEADS):
        hs = slice(h * HEAD_DIM, (h + 1) * HEAD_DIM)
        cd = c_dec[:, h:h + 1]
        num = scores[h] * vs[h] + cd * inter_scr[:, hs]
        den = scores[h] + cd * qns[h]
        hh = num / jnp.maximum(jnp.abs(den), e_neg_m[:, h:h + 1])
        hn = hh * lax.rsqrt(jnp.mean(hh * hh, axis=-1, keepdims=True) + EPS)
        mix_ref[:, hs] = (hn * mh_ref[:, hs] * jax.nn.sigmoid(zo_ref[:, hs])).astype(BF16)


def _sample_state_call(c0, n0, m0, sconv, wts):
    zh, zc, g = wts["zh"], wts["zc"], wts["g"]
    nb = zh.shape[0]
    tb = SAMPLE_TILE
    assert nb % tb == 0 and c0.shape[0] == 1 and nb == LANES and m0.shape == (HEADS, nb)

    def zgroup(grp):
        return pl.BlockSpec((tb, MLSTM_W), lambda i, grp=grp: (i, grp))

    def zconv(grp):
        return pl.BlockSpec((tb, CONV_CH), lambda i, grp=grp: (i, grp))

    row_d = pl.BlockSpec((tb, D_MODEL), lambda i: (i, 0))
    state = pl.BlockSpec((1, tb, HEADS, HEAD_DIM, HEAD_DIM), lambda i: (0, i, 0, 0, 0))
    nstate = pl.BlockSpec((1, tb, HEADS, HEAD_DIM), lambda i: (0, i, 0, 0))
    cstate = pl.BlockSpec((1, tb, CONV_WIDTH - 1, CONV_CH), lambda i: (0, i, 0, 0))

    def whole(shape):
        return pl.BlockSpec(shape, lambda i: (0,) * len(shape))

    mix, c_new, n_new, m_new, new_buf = pl.pallas_call(
        _sample_state_kernel,
        grid=(nb // tb,),
        in_specs=[zgroup(0), zgroup(1), zgroup(2), zgroup(3), zconv(0), zconv(1), zconv(2),
                  pl.BlockSpec((tb, LANES), lambda i: (i, 0)),
                  state, nstate, whole((HEADS, nb)), cstate,
                  whole((1, HEADS)), whole((1, HEADS)), whole((1, MLSTM_W)), whole((CONV_WIDTH, 1, CONV_CH))],
        out_specs=(row_d, state, nstate, whole((HEADS, nb)), cstate),
        out_shape=(jax.ShapeDtypeStruct((nb, D_MODEL), BF16),
                   jax.ShapeDtypeStruct(c0.shape, F32),
                   jax.ShapeDtypeStruct(n0.shape, F32),
                   jax.ShapeDtypeStruct((HEADS, nb), F32),
                   jax.ShapeDtypeStruct(sconv.shape, F32)),
        scratch_shapes=[pltpu.VMEM((tb, MLSTM_W), F32)] * 3 + [pltpu.VMEM((nb, LANES), F32)] * 2,
        compiler_params=pltpu.CompilerParams(dimension_semantics=("arbitrary",),
                                             vmem_limit_bytes=VMEM_LIMIT_BYTES),
        name="sample_state",
    )(zh, zh, zh, zh, zc, zc, zc, g, c0, n0, m0, sconv, wts["b_i"], wts["b_f"], wts["mh_norm"], wts["conv_w"])
    return mix, c_new, n_new, m_new, new_buf


def _weight_prep_kernel(win_ref, wout_ref, wup_ref, wdown_ref, wpg_ref, wpp_ref, xs_ref, nmix_ref,
                        heads_ref, conv_ref, gt_ref, out_ref, up_ref, down_ref, pg_ref, pp_ref,
                        zh_ref, zc_ref, g_ref, hb_scr):
    step = pl.program_id(0)
    g0 = 4 * MLSTM_W
    g1 = g0 + 2 * HEADS

    @pl.when(step == 0)
    def _():
        hb = _rms(xs_ref[:, 0, :], nmix_ref[...]).astype(BF16)
        kb = hb_scr.shape[2]
        for i in range(hb_scr.shape[0]):
            hb_scr[i] = hb[:, i * kb:(i + 1) * kb]
        zh_ref[...] = jnp.zeros(zh_ref.shape, F32)
        zc_ref[...] = jnp.zeros(zc_ref.shape, F32)
        g_ref[...] = jnp.zeros(g_ref.shape, F32)

    w_heads = win_ref[0:g0, :].T.astype(BF16)
    w_conv = win_ref[g1:, :].T.astype(BF16)
    heads_ref[...] = w_heads
    conv_ref[...] = w_conv
    gates = win_ref[g0:g1, :]
    rows = gates.shape[1]
    gt_ref[...] = jnp.concatenate([gates, jnp.zeros((GATE_ROWS - 2 * HEADS, rows), F32)], axis=0).astype(BF16)
    w_gc = jnp.concatenate([gates, jnp.zeros((LANES - 2 * HEADS, rows), F32)], axis=0).T.astype(BF16)
    hb_k = hb_scr[step]
    zh_ref[...] += _dot(hb_k, w_heads)
    zc_ref[...] += _dot(hb_k, w_conv)
    g_ref[...] += _dot(hb_k, w_gc)
    out_ref[...] = wout_ref[...].astype(BF16)
    up_ref[...] = wup_ref[...].astype(BF16)
    down_ref[...] = wdown_ref[...].astype(BF16)
    pg_ref[...] = wpg_ref[...].astype(BF16)
    pp_ref[...] = wpp_ref[...].astype(BF16)


def _prepare_weights(xs, norm_mix, w_in, b_gate_i, b_gate_f, mh_norm, conv_w, w_out, norm_mlp, w_up, w_down,
                     norm_ple, w_ple_gate, w_ple_proj, norm_final):
    w_in_t = jnp.swapaxes(w_in, 0, 1)
    srcs = (w_in_t, w_out, w_up, w_down, w_ple_gate, w_ple_proj)
    n_in = w_in.shape[1]
    nb = xs.shape[0]
    outs = ((D_MODEL, 4 * MLSTM_W), (D_MODEL, 3 * CONV_CH), (GATE_ROWS, D_MODEL),
            w_out.shape, w_up.shape, w_down.shape, w_ple_gate.shape, w_ple_proj.shape)
    z_outs = ((nb, 4 * MLSTM_W), (nb, 3 * CONV_CH), (nb, LANES))
    steps = PREP_STEPS

    def rows(shape):
        return pl.BlockSpec((shape[0] // steps, shape[1]), lambda i: (i, 0))

    def whole(shape):
        return pl.BlockSpec(shape, lambda i: (0,) * len(shape))

    out_specs = [rows(o) for o in outs]
    out_specs[2] = pl.BlockSpec((GATE_ROWS, D_MODEL // steps), lambda i: (0, i))
    out_specs += [whole(z) for z in z_outs]
    assert n_in == 4 * MLSTM_W + 2 * HEADS + 3 * CONV_CH
    norm_mix = norm_mix.reshape(1, D_MODEL)
    w_heads, w_conv, w_gt, w_out_b, w_up_b, w_down_b, w_pg_b, w_pp_b, zh, zc, g = pl.pallas_call(
        _weight_prep_kernel,
        grid=(steps,),
        in_specs=([pl.BlockSpec((n_in, D_MODEL // steps), lambda i: (0, i))] + [rows(a.shape) for a in srcs[1:]]
                  + [whole(xs.shape), whole(norm_mix.shape)]),
        out_specs=out_specs,
        out_shape=[jax.ShapeDtypeStruct(o, BF16) for o in outs] + [jax.ShapeDtypeStruct(z, F32) for z in z_outs],
        scratch_shapes=[pltpu.VMEM((steps, nb, D_MODEL // steps), BF16)],
        compiler_params=pltpu.CompilerParams(dimension_semantics=("arbitrary",),
                                             vmem_limit_bytes=VMEM_LIMIT_BYTES),
        name="weight_prep",
    )(*srcs, xs, norm_mix)
    return dict(
        w_heads=w_heads,
        w_conv=w_conv,
        w_gt=w_gt,
        zh=zh,
        zc=zc,
        g=g,
        b_i=b_gate_i.reshape(1, HEADS),
        b_f=b_gate_f.reshape(1, HEADS),
        norm_mix=norm_mix,
        mh_norm=mh_norm.reshape(1, MLSTM_W),
        conv_w=jnp.swapaxes(conv_w, 0, 1),
        w_out=w_out_b,
        norm_mlp=norm_mlp.reshape(1, D_MODEL),
        w_up=w_up_b,
        w_down=w_down_b,
        norm_ple=norm_ple.reshape(1, D_MODEL),
        w_pg=w_pg_b,
        w_pp=w_pp_b,
        norm_final=norm_final.reshape(1, D_MODEL),
    )


def kernel(x_prompt, x_sample, state_mlstm_C, state_mlstm_n, state_mlstm_m, state_conv, p_prompt, p_sample,
           norm_mix, w_in, b_gate_i, b_gate_f, mh_norm, conv_w, w_out, norm_mlp, w_up, w_down, norm_ple,
           w_ple_gate, w_ple_proj, norm_final):
    assert norm_mix.shape[0] == 1, "single-layer trunk"
    xs = x_sample
    wts = _prepare_weights(xs, norm_mix[0], w_in[0], b_gate_i[0], b_gate_f[0], mh_norm[0], conv_w, w_out[0],
                           norm_mlp[0], w_up[0], w_down[0], norm_ple[0], w_ple_gate[0], w_ple_proj[0], norm_final)

    mix_s, c_new, n_new, m_new, new_buf = _sample_state_call(
        state_mlstm_C, state_mlstm_n, jnp.swapaxes(state_mlstm_m[0], 0, 1), state_conv, wts)

    y_prompt, ys, prompt_c, prompt_n, prompt_m, conv_tail = _prompt_call(
        x_prompt, p_prompt[0], xs, p_sample, mix_s, wts)

    return (y_prompt, ys, prompt_c[None], prompt_n[None], jnp.swapaxes(prompt_m, 0, 1)[None], conv_tail[None],
            c_new, n_new, jnp.swapaxes(m_new, 0, 1)[None], new_buf)
```

```python
import functools
import math

import jax
import jax.numpy as jnp
from jax import lax
from jax.experimental import pallas as pl
from jax.experimental.pallas import tpu as pltpu

F32 = jnp.float32
BF16 = jnp.bfloat16

D_MODEL = 1024
HEADS = 4
HEAD_DIM = 128
MLSTM_W = HEADS * HEAD_DIM
CONV_CH = D_MODEL - MLSTM_W
CONV_WIDTH = 3
D_FF = 4 * D_MODEL
PLE_DIM = 256
EPS = 1e-6
M_INIT = -1e30
Q_SCALE = HEAD_DIM ** -0.5
LOG2E = math.log2(math.e)

GATE_ROWS = 16
PREP_STEPS = 4
LANES = 128
SUBLANES = 8
VMEM_LIMIT_BYTES = 60000 * 1024

SEQ_TILE = 512
CHUNK = 256
FF_CHUNK = 512
STAGE_ROWS, STAGE_COLS = 256, 1024
SAMPLE_TILE = 32


def _dot(a, b):
    return jnp.dot(a, b, preferred_element_type=F32)


def _dot_nt(a, b):
    return lax.dot_general(a, b, (((1,), (1,)), ((), ())), preferred_element_type=F32)


def _dot_tn(a, b):
    return lax.dot_general(a, b, (((0,), (0,)), ((), ())), preferred_element_type=F32)


def _rms(x, g):
    y = x * lax.rsqrt(jnp.mean(x * x, axis=-1, keepdims=True) + EPS)
    return y * g


def _log_sigmoid(x):
    return jnp.minimum(x, 0.0) - jnp.log1p(jnp.exp(-jnp.abs(x)))


def _mlp_chunk(xn_bf16, c, wup_ref, wdown_ref):
    cols = slice(c * FF_CHUNK, (c + 1) * FF_CHUNK)
    hf = jnp.maximum(_dot(xn_bf16, wup_ref[:, cols]), 0.0)
    return _dot((hf * hf).astype(BF16), wdown_ref[cols, :])


def _cummax_lanes(a):
    n = a.shape[1]
    lane = lax.broadcasted_iota(jnp.int32, a.shape, 1)
    d = 1
    while d < n:
        shifted = pltpu.roll(a, d, axis=1)
        a = jnp.maximum(a, jnp.where(lane >= d, shifted, -jnp.inf))
        d *= 2
    return a


def _convert_weights(pairs, stage_scr, sem):
    jobs = [(src, dst, r, c)
            for src, dst in pairs
            for r in range(0, src.shape[0], STAGE_ROWS)
            for c in range(0, src.shape[1], STAGE_COLS)]

    def fetch(k):
        src, _, r, c = jobs[k]
        return pltpu.make_async_copy(src.at[pl.ds(r, STAGE_ROWS), pl.ds(c, STAGE_COLS)],
                                     stage_scr.at[k % 2], sem.at[k % 2])

    fetch(0).start()
    for k, (_, dst, r, c) in enumerate(jobs):
        if k + 1 < len(jobs):
            fetch(k + 1).start()
        fetch(k).wait()
        dst[r:r + STAGE_ROWS, c:c + STAGE_COLS] = stage_scr[k % 2].astype(BF16)


def _prompt_kernel(tiles_per_seq, num_tiles,
                   x_ref, plag_ref, xs_ref, ps_ref, mixs_ref,
                   wheads_ref, wconv_ref, wgt_ref, bi_ref, bf_ref, nmix_ref, mh_ref, cw_ref,
                   wout_hbm, nmlp_ref, wup_hbm, wdown_hbm, nple_ref, wpg_hbm, wpp_hbm, nfin_ref,
                   y_ref, ys_ref, cout_ref, nout_ref, mout_ref, convout_ref,
                   caug, m_scr, cu_buf, mix_scr, xres_scr, tri_scr, ps_scr, mall_scr,
                   wout_ref, wup_ref, wdown_ref, wpg_ref, wpp_ref, stage_scr, stage_sem):
    t = SEQ_TILE
    lc = CHUNK
    ns = xs_ref.shape[0]
    s_id = pl.program_id(0)
    is_real = s_id < num_tiles
    j = lax.rem(jnp.minimum(s_id, num_tiles - 1), tiles_per_seq)

    @pl.when(s_id == 0)
    def _():
        mix_scr[0:ns, :] = mixs_ref[...]
        mix_scr[ns:t, :] = jnp.zeros((t - ns, D_MODEL), BF16)
        xres_scr[0:ns, :] = xs_ref[:, 0, :]
        xres_scr[ns:t, :] = jnp.zeros((t - ns, D_MODEL), F32)
        ps_scr[0:ns, :] = ps_ref[0, :, 0, :]
        ps_scr[ns:t, :] = jnp.zeros((t - ns, PLE_DIM), F32)
        mall_scr[...] = jnp.zeros(mall_scr.shape, F32)
        tri_scr[...] = (lax.broadcasted_iota(jnp.int32, (lc, lc), 0)
                        <= lax.broadcasted_iota(jnp.int32, (lc, lc), 1)).astype(F32).astype(BF16)
        _convert_weights(((wout_hbm, wout_ref), (wup_hbm, wup_ref), (wdown_hbm, wdown_ref),
                          (wpg_hbm, wpg_ref), (wpp_hbm, wpp_ref)), stage_scr, stage_sem)

    @pl.when(j == 0)
    def _():
        caug[...] = jnp.zeros_like(caug)
        m_scr[...] = jnp.full(m_scr.shape, M_INIT, F32)
        cu_buf[0:SUBLANES, :] = jnp.zeros((SUBLANES, CONV_CH), F32)

    x1 = xres_scr[...] + _dot(mix_scr[...], wout_ref[...])
    xn = _rms(x1, nmlp_ref[...]).astype(BF16)

    x = x_ref[0]
    hb = _rms(x, nmix_ref[...]).astype(BF16)
    xres_scr[...] = x
    gt = _dot_nt(wgt_ref[...], hb)

    row_id = lax.broadcasted_iota(jnp.int32, (lc, lc), 0)
    col_id = lax.broadcasted_iota(jnp.int32, (lc, lc), 1)
    causal = col_id <= row_id
    lane_id = lax.broadcasted_iota(jnp.int32, (lc, HEAD_DIM), 1)

    acc = jnp.zeros((t, D_MODEL), F32)
    n_phases = (t // lc) * HEADS
    mlp_per_phase = D_FF // FF_CHUNK // n_phases
    mlp_first = (mlp_per_phase + 1) // 2
    next_mlp = 0
    m_carry = m_scr[...]
    sub_id = lax.broadcasted_iota(jnp.int32, (SUBLANES, lc), 0)
    bias_i = jnp.zeros((SUBLANES, lc), F32)
    bias_f = jnp.zeros((SUBLANES, lc), F32)
    for h in range(HEADS):
        bias_i = jnp.where(sub_id == h, bi_ref[0, h], bias_i)
        bias_f = jnp.where(sub_id == h, bf_ref[0, h], bias_f)
    for c in range(t // lc):
        tok = slice(c * lc, (c + 1) * lc)
        hb_c = hb[tok, :]
        g8 = gt[0:SUBLANES, tok]
        ig = g8 + bias_i
        lf = _log_sigmoid(pltpu.roll(g8, HEADS, axis=0) + bias_f)
        hi = lf.astype(BF16)
        r1 = lf - hi.astype(F32)
        mid = r1.astype(BF16)
        lo = (r1 - mid.astype(F32)).astype(BF16)
        parts = _dot(jnp.concatenate([hi, mid, lo, jnp.zeros_like(lo)], axis=0), tri_scr[...])
        b = parts[0:8] + parts[8:16] + parts[16:24]
        a = ig - b
        m_prev = jnp.concatenate([m_carry] * (lc // LANES), axis=1)
        g = jnp.maximum(_cummax_lanes(a), m_prev)
        m_t = b + g
        b_last = b[:, lc - 1:lc]
        m_new = m_t[:, lc - 1:lc]
        decay = jnp.exp(m_prev - g)
        e_neg_m = jnp.exp(-m_t)
        w_state = jnp.exp(a + (b_last - m_new))
        c_dec = jnp.exp(b_last + m_prev[:, 0:1] - m_new)
        m_carry = jnp.broadcast_to(m_new, m_carry.shape)
        a2 = a * LOG2E
        rows = jnp.concatenate(
            [g * (-LOG2E), decay, e_neg_m, w_state, jnp.zeros((LANES - 4 * SUBLANES, lc), F32)], axis=0)
        cols = rows.T

        for h in range(HEADS):
            if h % 2 == 0:
                pair = slice(h * HEAD_DIM, (h + 2) * HEAD_DIM)
                zq2, zk2, zv2, zo2 = (
                    _dot(hb_c, wheads_ref[:, grp * MLSTM_W + pair.start:grp * MLSTM_W + pair.stop])
                    for grp in range(4))
            half = slice((h % 2) * HEAD_DIM, (h % 2 + 1) * HEAD_DIM)
            qs = (zq2[:, half] * Q_SCALE).astype(BF16)
            kb = zk2[:, half].astype(BF16)
            v = zv2[:, half]
            c_col = cols[:, h:h + 1]
            dec = cols[:, SUBLANES + h:SUBLANES + h + 1]
            enm = cols[:, 2 * SUBLANES + h:2 * SUBLANES + h + 1]
            wst = cols[:, 3 * SUBLANES + h:3 * SUBLANES + h + 1]
            s = _dot_nt(qs, kb)

            for _ in range(mlp_first):
                acc = acc + _mlp_chunk(xn, next_mlp, wup_ref, wdown_ref)
                next_mlp += 1

            dmat = jnp.exp2(jnp.where(causal, c_col + a2[h:h + 1, :], -jnp.inf))
            pm = s * dmat
            row_sum = jnp.sum(pm, axis=1, keepdims=True)
            intra = _dot(pm.astype(BF16), v.astype(BF16))
            c_state = caug[h]
            inter = _dot(qs, c_state.astype(BF16))
            vw = jnp.concatenate([v * wst, jnp.where(lane_id == 0, wst, 0.0)], axis=1).astype(BF16)
            caug[h] = c_dec[h:h + 1, :] * c_state + _dot_tn(kb, vw)

            for _ in range(mlp_per_phase - mlp_first):
                acc = acc + _mlp_chunk(xn, next_mlp, wup_ref, wdown_ref)
                next_mlp += 1

            num = intra + dec * inter[:, 0:HEAD_DIM]
            den = row_sum + dec * inter[:, HEAD_DIM:HEAD_DIM + 1]
            hh = num / jnp.maximum(jnp.abs(den), enm)
            hn = hh * lax.rsqrt(jnp.mean(hh * hh, axis=-1, keepdims=True) + EPS)
            hn = hn * mh_ref[:, h * HEAD_DIM:(h + 1) * HEAD_DIM]
            mix_scr[tok, h * HEAD_DIM:(h + 1) * HEAD_DIM] = (
                hn * jax.nn.sigmoid(zo2[:, half])).astype(BF16)
    m_scr[...] = m_carry
    assert next_mlp == D_FF // FF_CHUNK

    zc = _dot(hb, wconv_ref[...])
    p_tail = jnp.where(s_id == 0, ps_scr[...], plag_ref[0])
    ple = _dot(p_tail.astype(BF16), wpp_ref[...])

    def tail_rows(rows):
        x2 = x1[rows, :] + acc[rows, :]
        gate = jax.nn.sigmoid(_dot(_rms(x2, nple_ref[...]).astype(BF16), wpg_ref[...]))
        y_ref[0, rows, :] = _rms(x2 + gate * ple[rows, :], nfin_ref[...])

    tail_rows(slice(0, t // 2))

    cu = zc[:, CONV_CH:2 * CONV_CH] * zc[:, 2 * CONV_CH:3 * CONV_CH]
    cu_buf[SUBLANES:SUBLANES + t, :] = cu
    conv = (cw_ref[0] * cu_buf[SUBLANES - 2:SUBLANES - 2 + t, :]
            + cw_ref[1] * cu_buf[SUBLANES - 1:SUBLANES - 1 + t, :]
            + cw_ref[2] * cu)
    mix_scr[:, MLSTM_W:D_MODEL] = (zc[:, 0:CONV_CH] * conv).astype(BF16)
    cu_buf[0:SUBLANES, :] = cu_buf[t:t + SUBLANES, :]

    tail_rows(slice(t // 2, t))

    @pl.when(s_id == 0)
    def _():
        ys_ref[:, 0, :] = y_ref[0, 0:ns, :]

    @pl.when(jnp.logical_and(is_real, j == tiles_per_seq - 1))
    def _():
        for h in range(HEADS):
            cout_ref[0, h] = caug[h, :, 0:HEAD_DIM]
            nout_ref[0, h:h + 1, :] = caug[h, :, HEAD_DIM:2 * HEAD_DIM].T[0:1, :]
        seq_lane = lax.broadcasted_iota(jnp.int32, m_scr.shape, 1) == s_id // tiles_per_seq
        m_all = jnp.where(seq_lane, m_scr[...], mall_scr[...])
        mall_scr[...] = m_all
        mout_ref[...] = m_all[0:HEADS, 0:mout_ref.shape[1]]
        convout_ref[0] = cu[t - (CONV_WIDTH - 1):t, :]


def _resident(shape):
    return pl.BlockSpec(shape, lambda *_: (0,) * len(shape), pipeline_mode=pl.Buffered(1))


def _prompt_call(x, p, xs, ps, mix_s, wts):
    bsz, seq, _ = x.shape
    t = SEQ_TILE
    assert seq % t == 0 and t % CHUNK == 0 and CHUNK % LANES == 0 and xs.shape[0] <= t and bsz <= LANES
    consts = [wts["w_heads"], wts["w_conv"], wts["w_gt"], wts["b_i"], wts["b_f"], wts["norm_mix"], wts["mh_norm"],
              wts["conv_w"], wts["w_out"], wts["norm_mlp"], wts["w_up"], wts["w_down"], wts["norm_ple"], wts["w_pg"],
              wts["w_pp"], wts["norm_final"]]
    in_smem = {3, 4}
    in_hbm = {8, 10, 11, 13, 14}
    assert all(consts[i].dtype == F32 and consts[i].shape[0] % STAGE_ROWS == 0
               and consts[i].shape[1] % STAGE_COLS == 0 for i in in_hbm)
    nj = seq // t
    nt = bsz * nj

    def cur(s):
        c = jnp.minimum(s, nt - 1)
        return c // nj, c % nj

    def lag(s):
        c = jnp.maximum(s - 1, 0)
        return c // nj, c % nj

    in_specs = [pl.BlockSpec((1, t, D_MODEL), lambda s: (*cur(s), 0)),
                pl.BlockSpec((1, t, PLE_DIM), lambda s: (*lag(s), 0))]
    in_specs += [_resident(c.shape) for c in (xs, ps, mix_s)]
    in_specs += [pl.BlockSpec(memory_space=pltpu.SMEM) if i in in_smem
                 else pl.BlockSpec(memory_space=pl.ANY) if i in in_hbm else _resident(c.shape)
                 for i, c in enumerate(consts)]
    out_shape = (jax.ShapeDtypeStruct((bsz, seq, D_MODEL), F32),
                 jax.ShapeDtypeStruct(xs.shape, F32),
                 jax.ShapeDtypeStruct((bsz, HEADS, HEAD_DIM, HEAD_DIM), F32),
                 jax.ShapeDtypeStruct((bsz, HEADS, HEAD_DIM), F32),
                 jax.ShapeDtypeStruct((HEADS, bsz), F32),
                 jax.ShapeDtypeStruct((bsz, CONV_WIDTH - 1, CONV_CH), F32))
    out_specs = (pl.BlockSpec((1, t, D_MODEL), lambda s: (*lag(s), 0)),
                 pl.BlockSpec(xs.shape, lambda s: (0, 0, 0)),
                 pl.BlockSpec((1, HEADS, HEAD_DIM, HEAD_DIM), lambda s: (cur(s)[0], 0, 0, 0)),
                 pl.BlockSpec((1, HEADS, HEAD_DIM), lambda s: (cur(s)[0], 0, 0)),
                 pl.BlockSpec((HEADS, bsz), lambda s: (0, 0)),
                 pl.BlockSpec((1, CONV_WIDTH - 1, CONV_CH), lambda s: (cur(s)[0], 0, 0)))
    scratch = [pltpu.VMEM((HEADS, HEAD_DIM, 2 * HEAD_DIM), F32),
               pltpu.VMEM((SUBLANES, LANES), F32),
               pltpu.VMEM((t + SUBLANES, CONV_CH), F32),
               pltpu.VMEM((t, D_MODEL), BF16),
               pltpu.VMEM((t, D_MODEL), F32),
               pltpu.VMEM((CHUNK, CHUNK), BF16),
               pltpu.VMEM((t, PLE_DIM), F32),
               pltpu.VMEM((SUBLANES, LANES), F32)]
    scratch += [pltpu.VMEM(consts[i].shape, BF16) for i in sorted(in_hbm)]
    scratch += [pltpu.VMEM((2, STAGE_ROWS, STAGE_COLS), F32), pltpu.SemaphoreType.DMA((2,))]
    return pl.pallas_call(
        functools.partial(_prompt_kernel, nj, nt),
        grid=(nt + 1,),
        in_specs=in_specs,
        out_specs=out_specs,
        out_shape=out_shape,
        scratch_shapes=scratch,
        compiler_params=pltpu.CompilerParams(
            dimension_semantics=("arbitrary",),
            vmem_limit_bytes=VMEM_LIMIT_BYTES),
        name="prompt_layer",
    )(x, p, xs, ps, mix_s, *consts)


def _sample_state_kernel(zq_ref, zk_ref, zv_ref, zo_ref, zb_ref, zc_ref, zu_ref, g_ref, c0_ref, n0_ref, m0_ref,
                         sc_ref, bi_ref, bf_ref, mh_ref, cw_ref,
                         mix_ref, cnew_ref, nnew_ref, mnew_ref, buf_ref,
                         inter_scr, wv_scr, cd_scr, m0_scr, mnew_scr):
    tb = SAMPLE_TILE
    step = pl.program_id(0)
    seqs = pl.ds(pl.multiple_of(step * tb, tb), tb)

    @pl.when(step == 0)
    def _():
        m0_scr[...] = jnp.zeros(m0_scr.shape, F32)
        m0_scr[0:HEADS, :] = m0_ref[...]
        m0_scr[...] = m0_scr[...].T

    cu = zc_ref[...] * zu_ref[...]
    old0 = sc_ref[0, :, 0, :]
    old1 = sc_ref[0, :, 1, :]
    conv = cw_ref[0] * old0 + cw_ref[1] * old1 + cw_ref[2] * cu
    mix_ref[:, MLSTM_W:D_MODEL] = (zb_ref[...] * conv).astype(BF16)
    buf_ref[0, :, 0, :] = old1
    buf_ref[0, :, 1, :] = cu

    g = g_ref[...]
    ig = g[:, 0:HEADS] + bi_ref[...]
    lf = _log_sigmoid(g[:, HEADS:2 * HEADS] + bf_ref[...])
    m_inter = lf + m0_scr[seqs, 0:HEADS]
    m_new = jnp.maximum(m_inter, ig)
    w_in = jnp.exp(ig - m_new)
    c_dec = jnp.exp(m_inter - m_new)
    e_neg_m = jnp.exp(-m_new)
    lane = lax.broadcasted_iota(jnp.int32, (tb, LANES), 1)
    m_wide = jnp.zeros((tb, LANES), F32)
    for h in range(HEADS):
        m_wide = jnp.where(lane == h, m_new[:, h:h + 1], m_wide)
    mnew_scr[seqs, :] = m_wide

    @pl.when(step == pl.num_programs(0) - 1)
    def _():
        mnew_ref[...] = mnew_scr[...].T[0:HEADS, :]

    vs, scores, qns = [], [], []
    for h in range(HEADS):
        hs = slice(h * HEAD_DIM, (h + 1) * HEAD_DIM)
        q = zq_ref[:, hs] * Q_SCALE
        k = zk_ref[:, hs]
        v = zv_ref[:, hs]
        n0 = n0_ref[0, :, h, :]
        wi = w_in[:, h:h + 1]
        cd = c_dec[:, h:h + 1]
        scores.append(jnp.sum(q * k, axis=1, keepdims=True) * wi)
        qns.append(jnp.sum(q * n0, axis=1, keepdims=True))
        wv_scr[:, hs] = wi * v
        cd_scr[:, hs] = jnp.broadcast_to(cd, (tb, HEAD_DIM))
        nnew_ref[0, :, h, :] = cd * n0 + wi * k
        vs.append(v)

    eye = (lax.broadcasted_iota(jnp.int32, (HEAD_DIM, HEAD_DIM), 0)
           == lax.broadcasted_iota(jnp.int32, (HEAD_DIM, HEAD_DIM), 1))

    for i in range(tb):
        row = slice(i, i + 1)
        for h in range(HEADS):
            hs = slice(h * HEAD_DIM, (h + 1) * HEAD_DIM)
            c0 = c0_ref[0, i, h]
            q_rows = jnp.broadcast_to(zq_ref[row, hs] * Q_SCALE, (SUBLANES, HEAD_DIM)).astype(BF16)
            inter_scr[row, hs] = _dot(q_rows, c0.astype(BF16))[0:1, :]
            k_diag = jnp.where(eye, jnp.broadcast_to(zk_ref[row, hs], (HEAD_DIM, HEAD_DIM)), 0.0).astype(BF16)
            v_rows = jnp.broadcast_to(wv_scr[row, hs], (HEAD_DIM, HEAD_DIM)).astype(BF16)
            cd = jnp.broadcast_to(cd_scr[row, hs], (HEAD_DIM, HEAD_DIM))
            cnew_ref[0, i, h] = cd * c0 + _dot(k_diag, v_rows)

    for h in range(HEADS):
        hs = slice(h * HEAD_DIM, (h + 1) * HEAD_DIM)
        cd = c_dec[:, h:h + 1]
        num = scores[h] * vs[h] + cd * inter_scr[:, hs]
        den = scores[h] + cd * qns[h]
        hh = num / jnp.maximum(jnp.abs(den), e_neg_m[:, h:h + 1])
        hn = hh * lax.rsqrt(jnp.mean(hh * hh, axis=-1, keepdims=True) + EPS)
        mix_ref[:, hs] = (hn * mh_ref[:, hs] * jax.nn.sigmoid(zo_ref[:, hs])).astype(BF16)


def _sample_state_call(c0, n0, m0, sconv, wts):
    zh, zc, g = wts["zh"], wts["zc"], wts["g"]
    nb = zh.shape[0]
    tb = SAMPLE_TILE
    assert nb % tb == 0 and c0.shape[0] == 1 and nb == LANES and m0.shape == (HEADS, nb)

    def zgroup(grp):
        return pl.BlockSpec((tb, MLSTM_W), lambda i, grp=grp: (i, grp))

    def zconv(grp):
        return pl.BlockSpec((tb, CONV_CH), lambda i, grp=grp: (i, grp))

    row_d = pl.BlockSpec((tb, D_MODEL), lambda i: (i, 0))
    state = pl.BlockSpec((1, tb, HEADS, HEAD_DIM, HEAD_DIM), lambda i: (0, i, 0, 0, 0))
    nstate = pl.BlockSpec((1, tb, HEADS, HEAD_DIM), lambda i: (0, i, 0, 0))
    cstate = pl.BlockSpec((1, tb, CONV_WIDTH - 1, CONV_CH), lambda i: (0, i, 0, 0))

    def whole(shape):
        return pl.BlockSpec(shape, lambda i: (0,) * len(shape))

    mix, c_new, n_new, m_new, new_buf = pl.pallas_call(
        _sample_state_kernel,
        grid=(nb // tb,),
        in_specs=[zgroup(0), zgroup(1), zgroup(2), zgroup(3), zconv(0), zconv(1), zconv(2),
                  pl.BlockSpec((tb, LANES), lambda i: (i, 0)),
                  state, nstate, whole((HEADS, nb)), cstate,
                  whole((1, HEADS)), whole((1, HEADS)), whole((1, MLSTM_W)), whole((CONV_WIDTH, 1, CONV_CH))],
        out_specs=(row_d, state, nstate, whole((HEADS, nb)), cstate),
        out_shape=(jax.ShapeDtypeStruct((nb, D_MODEL), BF16),
                   jax.ShapeDtypeStruct(c0.shape, F32),
                   jax.ShapeDtypeStruct(n0.shape, F32),
                   jax.ShapeDtypeStruct((HEADS, nb), F32),
                   jax.ShapeDtypeStruct(sconv.shape, F32)),
        scratch_shapes=[pltpu.VMEM((tb, MLSTM_W), F32)] * 3 + [pltpu.VMEM((nb, LANES), F32)] * 2,
        compiler_params=pltpu.CompilerParams(dimension_semantics=("arbitrary",),
                                             vmem_limit_bytes=VMEM_LIMIT_BYTES),
        name="sample_state",
    )(zh, zh, zh, zh, zc, zc, zc, g, c0, n0, m0, sconv, wts["b_i"], wts["b_f"], wts["mh_norm"], wts["conv_w"])
    return mix, c_new, n_new, m_new, new_buf


def _weight_prep_kernel(win_ref, xs_ref, nmix_ref,
                        heads_ref, conv_ref, gt_ref, zh_ref, zc_ref, g_ref, hb_scr):
    step = pl.program_id(0)
    g0 = 4 * MLSTM_W
    g1 = g0 + 2 * HEADS

    @pl.when(step == 0)
    def _():
        hb = _rms(xs_ref[:, 0, :], nmix_ref[...]).astype(BF16)
        kb = hb_scr.shape[2]
        for i in range(hb_scr.shape[0]):
            hb_scr[i] = hb[:, i * kb:(i + 1) * kb]
        zh_ref[...] = jnp.zeros(zh_ref.shape, F32)
        zc_ref[...] = jnp.zeros(zc_ref.shape, F32)
        g_ref[...] = jnp.zeros(g_ref.shape, F32)

    w_heads = win_ref[0:g0, :].T.astype(BF16)
    w_conv = win_ref[g1:, :].T.astype(BF16)
    heads_ref[...] = w_heads
    conv_ref[...] = w_conv
    gates = win_ref[g0:g1, :]
    rows = gates.shape[1]
    gt_ref[...] = jnp.concatenate([gates, jnp.zeros((GATE_ROWS - 2 * HEADS, rows), F32)], axis=0).astype(BF16)
    w_gc = jnp.concatenate([gates, jnp.zeros((LANES - 2 * HEADS, rows), F32)], axis=0).T.astype(BF16)
    hb_k = hb_scr[step]
    zh_ref[...] += _dot(hb_k, w_heads)
    zc_ref[...] += _dot(hb_k, w_conv)
    g_ref[...] += _dot(hb_k, w_gc)


def _prepare_weights(xs, norm_mix, w_in, b_gate_i, b_gate_f, mh_norm, conv_w, w_out, norm_mlp, w_up, w_down,
                     norm_ple, w_ple_gate, w_ple_proj, norm_final):
    w_in_t = jnp.swapaxes(w_in, 0, 1)
    n_in = w_in.shape[1]
    nb = xs.shape[0]
    outs = ((D_MODEL, 4 * MLSTM_W), (D_MODEL, 3 * CONV_CH), (GATE_ROWS, D_MODEL))
    z_outs = ((nb, 4 * MLSTM_W), (nb, 3 * CONV_CH), (nb, LANES))
    steps = PREP_STEPS

    def rows(shape):
        return pl.BlockSpec((shape[0] // steps, shape[1]), lambda i: (i, 0))

    def whole(shape):
        return pl.BlockSpec(shape, lambda i: (0,) * len(shape))

    out_specs = [rows(o) for o in outs]
    out_specs[2] = pl.BlockSpec((GATE_ROWS, D_MODEL // steps), lambda i: (0, i))
    out_specs += [whole(z) for z in z_outs]
    assert n_in == 4 * MLSTM_W + 2 * HEADS + 3 * CONV_CH
    norm_mix = norm_mix.reshape(1, D_MODEL)
    w_heads, w_conv, w_gt, zh, zc, g = pl.pallas_call(
        _weight_prep_kernel,
        grid=(steps,),
        in_specs=[pl.BlockSpec((n_in, D_MODEL // steps), lambda i: (0, i)), whole(xs.shape), whole(norm_mix.shape)],
        out_specs=out_specs,
        out_shape=[jax.ShapeDtypeStruct(o, BF16) for o in outs] + [jax.ShapeDtypeStruct(z, F32) for z in z_outs],
        scratch_shapes=[pltpu.VMEM((steps, nb, D_MODEL // steps), BF16)],
        compiler_params=pltpu.CompilerParams(dimension_semantics=("arbitrary",),
                                             vmem_limit_bytes=VMEM_LIMIT_BYTES),
        name="weight_prep",
    )(w_in_t, xs, norm_mix)
    return dict(
        w_heads=w_heads,
        w_conv=w_conv,
        w_gt=w_gt,
        zh=zh,
        zc=zc,
        g=g,
        b_i=b_gate_i.reshape(1, HEADS),
        b_f=b_gate_f.reshape(1, HEADS),
        norm_mix=norm_mix,
        mh_norm=mh_norm.reshape(1, MLSTM_W),
        conv_w=jnp.swapaxes(conv_w, 0, 1),
        w_out=w_out,
        norm_mlp=norm_mlp.reshape(1, D_MODEL),
        w_up=w_up,
        w_down=w_down,
        norm_ple=norm_ple.reshape(1, D_MODEL),
        w_pg=w_ple_gate,
        w_pp=w_ple_proj,
        norm_final=norm_final.reshape(1, D_MODEL),
    )


def kernel(x_prompt, x_sample, state_mlstm_C, state_mlstm_n, state_mlstm_m, state_conv, p_prompt, p_sample,
           norm_mix, w_in, b_gate_i, b_gate_f, mh_norm, conv_w, w_out, norm_mlp, w_up, w_down, norm_ple,
           w_ple_gate, w_ple_proj, norm_final):
    assert norm_mix.shape[0] == 1, "single-layer trunk"
    xs = x_sample
    wts = _prepare_weights(xs, norm_mix[0], w_in[0], b_gate_i[0], b_gate_f[0], mh_norm[0], conv_w, w_out[0],
                           norm_mlp[0], w_up[0], w_down[0], norm_ple[0], w_ple_gate[0], w_ple_proj[0], norm_final)

    mix_s, c_new, n_new, m_new, new_buf = _sample_state_call(
        state_mlstm_C, state_mlstm_n, jnp.swapaxes(state_mlstm_m[0], 0, 1), state_conv, wts)

    y_prompt, ys, prompt_c, prompt_n, prompt_m, conv_tail = _prompt_call(
        x_prompt, p_prompt[0], xs, p_sample, mix_s, wts)

    return (y_prompt, ys, prompt_c[None], prompt_n[None], jnp.swapaxes(prompt_m, 0, 1)[None], conv_tail[None],
            c_new, n_new, jnp.swapaxes(m_new, 0, 1)[None], new_buf)
```

```python
import functools
import math

import jax
import jax.numpy as jnp
from jax import lax
from jax.experimental import pallas as pl
from jax.experimental.pallas import tpu as pltpu

F32 = jnp.float32
BF16 = jnp.bfloat16

D_MODEL = 1024
HEADS = 4
HEAD_DIM = 128
MLSTM_W = HEADS * HEAD_DIM
CONV_CH = D_MODEL - MLSTM_W
CONV_WIDTH = 3
D_FF = 4 * D_MODEL
PLE_DIM = 256
EPS = 1e-6
M_INIT = -1e30
Q_SCALE = HEAD_DIM ** -0.5
LOG2E = math.log2(math.e)

GATE_ROWS = 16
PREP_STEPS = 4
LANES = 128
SUBLANES = 8
VMEM_LIMIT_BYTES = 60000 * 1024

SEQ_TILE = 512
CHUNK = 256
FF_CHUNK = 512
STAGE_ROWS, STAGE_COLS = 256, 1024
SAMPLE_TILE = 32


def _dot(a, b):
    return jnp.dot(a, b, preferred_element_type=F32)


def _dot_nt(a, b):
    return lax.dot_general(a, b, (((1,), (1,)), ((), ())), preferred_element_type=F32)


def _dot_tn(a, b):
    return lax.dot_general(a, b, (((0,), (0,)), ((), ())), preferred_element_type=F32)


def _rms(x, g):
    y = x * lax.rsqrt(jnp.mean(x * x, axis=-1, keepdims=True) + EPS)
    return y * g


def _log_sigmoid(x):
    return jnp.minimum(x, 0.0) - jnp.log1p(jnp.exp(-jnp.abs(x)))


def _mlp_chunk(xn_bf16, c, wup_ref, wdown_ref):
    cols = slice(c * FF_CHUNK, (c + 1) * FF_CHUNK)
    hf = jnp.maximum(_dot(xn_bf16, wup_ref[:, cols]), 0.0)
    return _dot((hf * hf).astype(BF16), wdown_ref[cols, :])


def _cummax_lanes(a):
    n = a.shape[1]
    lane = lax.broadcasted_iota(jnp.int32, a.shape, 1)
    d = 1
    while d < n:
        shifted = pltpu.roll(a, d, axis=1)
        a = jnp.maximum(a, jnp.where(lane >= d, shifted, -jnp.inf))
        d *= 2
    return a


def _convert_weights(pairs, slots, sem):
    jobs = [(src, dst, r, c)
            for src, dst in pairs
            for r in range(0, src.shape[0], STAGE_ROWS)
            for c in range(0, src.shape[1], STAGE_COLS)]
    ns = len(slots)

    def fetch(k):
        src, _, r, c = jobs[k]
        ref, idx = slots[k % ns]
        return pltpu.make_async_copy(src.at[pl.ds(r, STAGE_ROWS), pl.ds(c, STAGE_COLS)], ref.at[idx], sem.at[k % ns])

    for k in range(min(ns - 1, len(jobs))):
        fetch(k).start()
    for k, (_, dst, r, c) in enumerate(jobs):
        if k + ns - 1 < len(jobs):
            fetch(k + ns - 1).start()
        fetch(k).wait()
        ref, idx = slots[k % ns]
        dst[r:r + STAGE_ROWS, c:c + STAGE_COLS] = ref[idx].astype(BF16)


def _prompt_kernel(tiles_per_seq, num_tiles,
                   x_ref, plag_ref, xs_ref, ps_ref, mixs_ref,
                   wheads_ref, wconv_ref, wgt_ref, bi_ref, bf_ref, nmix_ref, mh_ref, cw_ref,
                   wout_hbm, nmlp_ref, wup_hbm, wdown_hbm, nple_ref, wpg_hbm, wpp_hbm, nfin_ref,
                   y_ref, ys_ref, cout_ref, nout_ref, mout_ref, convout_ref,
                   caug, m_scr, cu_buf, mix_scr, xres_scr, tri_scr, ps_scr, mall_scr,
                   wout_ref, wup_ref, wdown_ref, wpg_ref, wpp_ref, stage_scr, stage_sem):
    t = SEQ_TILE
    lc = CHUNK
    ns = xs_ref.shape[0]
    s_id = pl.program_id(0)
    is_real = s_id < num_tiles
    j = lax.rem(jnp.minimum(s_id, num_tiles - 1), tiles_per_seq)

    @pl.when(s_id == 0)
    def _():
        half = (slice(0, STAGE_ROWS), slice(STAGE_ROWS, 2 * STAGE_ROWS))
        cols = slice(0, STAGE_COLS)
        slots = ([(stage_scr, (i,)) for i in range(2)] + [(xres_scr, (h, cols)) for h in half]
                 + [(y_ref, (0, h, cols)) for h in half])
        _convert_weights(((wout_hbm, wout_ref), (wup_hbm, wup_ref), (wdown_hbm, wdown_ref),
                          (wpg_hbm, wpg_ref), (wpp_hbm, wpp_ref)), slots, stage_sem)
        mix_scr[0:ns, :] = mixs_ref[...]
        mix_scr[ns:t, :] = jnp.zeros((t - ns, D_MODEL), BF16)
        xres_scr[0:ns, :] = xs_ref[:, 0, :]
        xres_scr[ns:t, :] = jnp.zeros((t - ns, D_MODEL), F32)
        ps_scr[0:ns, :] = ps_ref[0, :, 0, :]
        ps_scr[ns:t, :] = jnp.zeros((t - ns, PLE_DIM), F32)
        mall_scr[...] = jnp.zeros(mall_scr.shape, F32)
        tri_scr[...] = (lax.broadcasted_iota(jnp.int32, (lc, lc), 0)
                        <= lax.broadcasted_iota(jnp.int32, (lc, lc), 1)).astype(F32).astype(BF16)

    @pl.when(j == 0)
    def _():
        caug[...] = jnp.zeros_like(caug)
        m_scr[...] = jnp.full(m_scr.shape, M_INIT, F32)
        cu_buf[0:SUBLANES, :] = jnp.zeros((SUBLANES, CONV_CH), F32)

    x1 = xres_scr[...] + _dot(mix_scr[...], wout_ref[...])
    xn = _rms(x1, nmlp_ref[...]).astype(BF16)

    x = x_ref[0]
    hb = _rms(x, nmix_ref[...]).astype(BF16)
    xres_scr[...] = x
    gt = _dot_nt(wgt_ref[...], hb)

    row_id = lax.broadcasted_iota(jnp.int32, (lc, lc), 0)
    col_id = lax.broadcasted_iota(jnp.int32, (lc, lc), 1)
    causal = col_id <= row_id
    lane_id = lax.broadcasted_iota(jnp.int32, (lc, HEAD_DIM), 1)

    acc = jnp.zeros((t, D_MODEL), F32)
    n_phases = (t // lc) * HEADS
    mlp_per_phase = D_FF // FF_CHUNK // n_phases
    mlp_first = (mlp_per_phase + 1) // 2
    next_mlp = 0
    m_carry = m_scr[...]
    sub_id = lax.broadcasted_iota(jnp.int32, (SUBLANES, lc), 0)
    bias_i = jnp.zeros((SUBLANES, lc), F32)
    bias_f = jnp.zeros((SUBLANES, lc), F32)
    for h in range(HEADS):
        bias_i = jnp.where(sub_id == h, bi_ref[0, h], bias_i)
        bias_f = jnp.where(sub_id == h, bf_ref[0, h], bias_f)
    for c in range(t // lc):
        tok = slice(c * lc, (c + 1) * lc)
        hb_c = hb[tok, :]
        g8 = gt[0:SUBLANES, tok]
        ig = g8 + bias_i
        lf = _log_sigmoid(pltpu.roll(g8, HEADS, axis=0) + bias_f)
        hi = lf.astype(BF16)
        r1 = lf - hi.astype(F32)
        mid = r1.astype(BF16)
        lo = (r1 - mid.astype(F32)).astype(BF16)
        parts = _dot(jnp.concatenate([hi, mid, lo, jnp.zeros_like(lo)], axis=0), tri_scr[...])
        b = parts[0:8] + parts[8:16] + parts[16:24]
        a = ig - b
        m_prev = jnp.concatenate([m_carry] * (lc // LANES), axis=1)
        g = jnp.maximum(_cummax_lanes(a), m_prev)
        m_t = b + g
        b_last = b[:, lc - 1:lc]
        m_new = m_t[:, lc - 1:lc]
        decay = jnp.exp(m_prev - g)
        e_neg_m = jnp.exp(-m_t)
        w_state = jnp.exp(a + (b_last - m_new))
        c_dec = jnp.exp(b_last + m_prev[:, 0:1] - m_new)
        m_carry = jnp.broadcast_to(m_new, m_carry.shape)
        a2 = a * LOG2E
        rows = jnp.concatenate(
            [g * (-LOG2E), decay, e_neg_m, w_state, jnp.zeros((LANES - 4 * SUBLANES, lc), F32)], axis=0)
        cols = rows.T

        for h in range(HEADS):
            if h % 2 == 0:
                pair = slice(h * HEAD_DIM, (h + 2) * HEAD_DIM)
                zq2, zk2, zv2, zo2 = (
                    _dot(hb_c, wheads_ref[:, grp * MLSTM_W + pair.start:grp * MLSTM_W + pair.stop])
                    for grp in range(4))
            half = slice((h % 2) * HEAD_DIM, (h % 2 + 1) * HEAD_DIM)
            qs = (zq2[:, half] * Q_SCALE).astype(BF16)
            kb = zk2[:, half].astype(BF16)
            v = zv2[:, half]
            c_col = cols[:, h:h + 1]
            dec = cols[:, SUBLANES + h:SUBLANES + h + 1]
            enm = cols[:, 2 * SUBLANES + h:2 * SUBLANES + h + 1]
            wst = cols[:, 3 * SUBLANES + h:3 * SUBLANES + h + 1]
            s = _dot_nt(qs, kb)

            for _ in range(mlp_first):
                acc = acc + _mlp_chunk(xn, next_mlp, wup_ref, wdown_ref)
                next_mlp += 1

            dmat = jnp.exp2(jnp.where(causal, c_col + a2[h:h + 1, :], -jnp.inf))
            pm = s * dmat
            row_sum = jnp.sum(pm, axis=1, keepdims=True)
            intra = _dot(pm.astype(BF16), v.astype(BF16))
            c_state = caug[h]
            inter = _dot(qs, c_state.astype(BF16))
            vw = jnp.concatenate([v * wst, jnp.where(lane_id == 0, wst, 0.0)], axis=1).astype(BF16)
            caug[h] = c_dec[h:h + 1, :] * c_state + _dot_tn(kb, vw)

            for _ in range(mlp_per_phase - mlp_first):
                acc = acc + _mlp_chunk(xn, next_mlp, wup_ref, wdown_ref)
                next_mlp += 1

            num = intra + dec * inter[:, 0:HEAD_DIM]
            den = row_sum + dec * inter[:, HEAD_DIM:HEAD_DIM + 1]
            hh = num / jnp.maximum(jnp.abs(den), enm)
            hn = hh * lax.rsqrt(jnp.mean(hh * hh, axis=-1, keepdims=True) + EPS)
            hn = hn * mh_ref[:, h * HEAD_DIM:(h + 1) * HEAD_DIM]
            mix_scr[tok, h * HEAD_DIM:(h + 1) * HEAD_DIM] = (
                hn * jax.nn.sigmoid(zo2[:, half])).astype(BF16)
    m_scr[...] = m_carry
    assert next_mlp == D_FF // FF_CHUNK

    zc = _dot(hb, wconv_ref[...])
    p_tail = jnp.where(s_id == 0, ps_scr[...], plag_ref[0])
    ple = _dot(p_tail.astype(BF16), wpp_ref[...])

    def tail_rows(rows):
        x2 = x1[rows, :] + acc[rows, :]
        gate = jax.nn.sigmoid(_dot(_rms(x2, nple_ref[...]).astype(BF16), wpg_ref[...]))
        y_ref[0, rows, :] = _rms(x2 + gate * ple[rows, :], nfin_ref[...])

    tail_rows(slice(0, t // 2))

    cu = zc[:, CONV_CH:2 * CONV_CH] * zc[:, 2 * CONV_CH:3 * CONV_CH]
    cu_buf[SUBLANES:SUBLANES + t, :] = cu
    conv = (cw_ref[0] * cu_buf[SUBLANES - 2:SUBLANES - 2 + t, :]
            + cw_ref[1] * cu_buf[SUBLANES - 1:SUBLANES - 1 + t, :]
            + cw_ref[2] * cu)
    mix_scr[:, MLSTM_W:D_MODEL] = (zc[:, 0:CONV_CH] * conv).astype(BF16)
    cu_buf[0:SUBLANES, :] = cu_buf[t:t + SUBLANES, :]

    tail_rows(slice(t // 2, t))

    @pl.when(s_id == 0)
    def _():
        ys_ref[:, 0, :] = y_ref[0, 0:ns, :]

    @pl.when(jnp.logical_and(is_real, j == tiles_per_seq - 1))
    def _():
        for h in range(HEADS):
            cout_ref[0, h] = caug[h, :, 0:HEAD_DIM]
            nout_ref[0, h:h + 1, :] = caug[h, :, HEAD_DIM:2 * HEAD_DIM].T[0:1, :]
        seq_lane = lax.broadcasted_iota(jnp.int32, m_scr.shape, 1) == s_id // tiles_per_seq
        m_all = jnp.where(seq_lane, m_scr[...], mall_scr[...])
        mall_scr[...] = m_all
        mout_ref[...] = m_all[0:HEADS, 0:mout_ref.shape[1]]
        convout_ref[0] = cu[t - (CONV_WIDTH - 1):t, :]


def _resident(shape):
    return pl.BlockSpec(shape, lambda *_: (0,) * len(shape), pipeline_mode=pl.Buffered(1))


def _prompt_call(x, p, xs, ps, mix_s, wts):
    bsz, seq, _ = x.shape
    t = SEQ_TILE
    assert seq % t == 0 and t % CHUNK == 0 and CHUNK % LANES == 0 and xs.shape[0] <= t and bsz <= LANES
    consts = [wts["w_heads"], wts["w_conv"], wts["w_gt"], wts["b_i"], wts["b_f"], wts["norm_mix"], wts["mh_norm"],
              wts["conv_w"], wts["w_out"], wts["norm_mlp"], wts["w_up"], wts["w_down"], wts["norm_ple"], wts["w_pg"],
              wts["w_pp"], wts["norm_final"]]
    in_smem = {3, 4}
    in_hbm = {8, 10, 11, 13, 14}
    assert all(consts[i].dtype == F32 and consts[i].shape[0] % STAGE_ROWS == 0
               and consts[i].shape[1] % STAGE_COLS == 0 for i in in_hbm)
    nj = seq // t
    nt = bsz * nj

    def cur(s):
        c = jnp.minimum(s, nt - 1)
        return c // nj, c % nj

    def lag(s):
        c = jnp.maximum(s - 1, 0)
        return c // nj, c % nj

    in_specs = [pl.BlockSpec((1, t, D_MODEL), lambda s: (*cur(s), 0)),
                pl.BlockSpec((1, t, PLE_DIM), lambda s: (*lag(s), 0))]
    in_specs += [_resident(c.shape) for c in (xs, ps, mix_s)]
    in_specs += [pl.BlockSpec(memory_space=pltpu.SMEM) if i in in_smem
                 else pl.BlockSpec(memory_space=pl.ANY) if i in in_hbm else _resident(c.shape)
                 for i, c in enumerate(consts)]
    out_shape = (jax.ShapeDtypeStruct((bsz, seq, D_MODEL), F32),
                 jax.ShapeDtypeStruct(xs.shape, F32),
                 jax.ShapeDtypeStruct((bsz, HEADS, HEAD_DIM, HEAD_DIM), F32),
                 jax.ShapeDtypeStruct((bsz, HEADS, HEAD_DIM), F32),
                 jax.ShapeDtypeStruct((HEADS, bsz), F32),
                 jax.ShapeDtypeStruct((bsz, CONV_WIDTH - 1, CONV_CH), F32))
    out_specs = (pl.BlockSpec((1, t, D_MODEL), lambda s: (*lag(s), 0)),
                 pl.BlockSpec(xs.shape, lambda s: (0, 0, 0)),
                 pl.BlockSpec((1, HEADS, HEAD_DIM, HEAD_DIM), lambda s: (cur(s)[0], 0, 0, 0)),
                 pl.BlockSpec((1, HEADS, HEAD_DIM), lambda s: (cur(s)[0], 0, 0)),
                 pl.BlockSpec((HEADS, bsz), lambda s: (0, 0)),
                 pl.BlockSpec((1, CONV_WIDTH - 1, CONV_CH), lambda s: (cur(s)[0], 0, 0)))
    scratch = [pltpu.VMEM((HEADS, HEAD_DIM, 2 * HEAD_DIM), F32),
               pltpu.VMEM((SUBLANES, LANES), F32),
               pltpu.VMEM((t + SUBLANES, CONV_CH), F32),
               pltpu.VMEM((t, D_MODEL), BF16),
               pltpu.VMEM((t, D_MODEL), F32),
               pltpu.VMEM((CHUNK, CHUNK), BF16),
               pltpu.VMEM((t, PLE_DIM), F32),
               pltpu.VMEM((SUBLANES, LANES), F32)]
    scratch += [pltpu.VMEM(consts[i].shape, BF16) for i in sorted(in_hbm)]
    scratch += [pltpu.VMEM((2, STAGE_ROWS, STAGE_COLS), F32), pltpu.SemaphoreType.DMA((6,))]
    assert 2 * STAGE_ROWS <= t and STAGE_COLS <= D_MODEL
    return pl.pallas_call(
        functools.partial(_prompt_kernel, nj, nt),
        grid=(nt + 1,),
        in_specs=in_specs,
        out_specs=out_specs,
        out_shape=out_shape,
        scratch_shapes=scratch,
        compiler_params=pltpu.CompilerParams(
            dimension_semantics=("arbitrary",),
            vmem_limit_bytes=VMEM_LIMIT_BYTES),
        name="prompt_layer",
    )(x, p, xs, ps, mix_s, *consts)


def _sample_state_kernel(zq_ref, zk_ref, zv_ref, zo_ref, zb_ref, zc_ref, zu_ref, g_ref, c0_ref, n0_ref, m0_ref,
                         sc_ref, bi_ref, bf_ref, mh_ref, cw_ref,
                         mix_ref, cnew_ref, nnew_ref, mnew_ref, buf_ref,
                         inter_scr, wv_scr, cd_scr, m0_scr, mnew_scr):
    tb = SAMPLE_TILE
    step = pl.program_id(0)
    seqs = pl.ds(pl.multiple_of(step * tb, tb), tb)

    @pl.when(step == 0)
    def _():
        m0_scr[...] = jnp.zeros(m0_scr.shape, F32)
        m0_scr[0:HEADS, :] = m0_ref[...]
        m0_scr[...] = m0_scr[...].T

    cu = zc_ref[...] * zu_ref[...]
    old0 = sc_ref[0, :, 0, :]
    old1 = sc_ref[0, :, 1, :]
    conv = cw_ref[0] * old0 + cw_ref[1] * old1 + cw_ref[2] * cu
    mix_ref[:, MLSTM_W:D_MODEL] = (zb_ref[...] * conv).astype(BF16)
    buf_ref[0, :, 0, :] = old1
    buf_ref[0, :, 1, :] = cu

    g = g_ref[...]
    ig = g[:, 0:HEADS] + bi_ref[...]
    lf = _log_sigmoid(g[:, HEADS:2 * HEADS] + bf_ref[...])
    m_inter = lf + m0_scr[seqs, 0:HEADS]
    m_new = jnp.maximum(m_inter, ig)
    w_in = jnp.exp(ig - m_new)
    c_dec = jnp.exp(m_inter - m_new)
    e_neg_m = jnp.exp(-m_new)
    lane = lax.broadcasted_iota(jnp.int32, (tb, LANES), 1)
    m_wide = jnp.zeros((tb, LANES), F32)
    for h in range(HEADS):
        m_wide = jnp.where(lane == h, m_new[:, h:h + 1], m_wide)
    mnew_scr[seqs, :] = m_wide

    @pl.when(step == pl.num_programs(0) - 1)
    def _():
        mnew_ref[...] = mnew_scr[...].T[0:HEADS, :]

    vs, scores, qns = [], [], []
    for h in range(HEADS):
        hs = slice(h * HEAD_DIM, (h + 1) * HEAD_DIM)
        q = zq_ref[:, hs] * Q_SCALE
        k = zk_ref[:, hs]
        v = zv_ref[:, hs]
        n0 = n0_ref[0, :, h, :]
        wi = w_in[:, h:h + 1]
        cd = c_dec[:, h:h + 1]
        scores.append(jnp.sum(q * k, axis=1, keepdims=True) * wi)
        qns.append(jnp.sum(q * n0, axis=1, keepdims=True))
        wv_scr[:, hs] = wi * v
        cd_scr[:, hs] = jnp.broadcast_to(cd, (tb, HEAD_DIM))
        nnew_ref[0, :, h, :] = cd * n0 + wi * k
        vs.append(v)

    eye = (lax.broadcasted_iota(jnp.int32, (HEAD_DIM, HEAD_DIM), 0)
           == lax.broadcasted_iota(jnp.int32, (HEAD_DIM, HEAD_DIM), 1))

    for i in range(tb):
        row = slice(i, i + 1)
        for h in range(HEADS):
            hs = slice(h * HEAD_DIM, (h + 1) * HEAD_DIM)
            c0 = c0_ref[0, i, h]
            q_rows = jnp.broadcast_to(zq_ref[row, hs] * Q_SCALE, (SUBLANES, HEAD_DIM)).astype(BF16)
            inter_scr[row, hs] = _dot(q_rows, c0.astype(BF16))[0:1, :]
            k_diag = jnp.where(eye, jnp.broadcast_to(zk_ref[row, hs], (HEAD_DIM, HEAD_DIM)), 0.0).astype(BF16)
            v_rows = jnp.broadcast_to(wv_scr[row, hs], (HEAD_DIM, HEAD_DIM)).astype(BF16)
            cd = jnp.broadcast_to(cd_scr[row, hs], (HEAD_DIM, HEAD_DIM))
            cnew_ref[0, i, h] = cd * c0 + _dot(k_diag, v_rows)

    for h in range(HEADS):
        hs = slice(h * HEAD_DIM, (h + 1) * HEAD_DIM)
        cd = c_dec[:, h:h + 1]
        num = scores[h] * vs[h] + cd * inter_scr[:, hs]
        den = scores[h] + cd * qns[h]
        hh = num / jnp.maximum(jnp.abs(den), e_neg_m[:, h:h + 1])
        hn = hh * lax.rsqrt(jnp.mean(hh * hh, axis=-1, keepdims=True) + EPS)
        mix_ref[:, hs] = (hn * mh_ref[:, hs] * jax.nn.sigmoid(zo_ref[:, hs])).astype(BF16)


def _sample_state_call(c0, n0, m0, sconv, wts):
    zh, zc, g = wts["zh"], wts["zc"], wts["g"]
    nb = zh.shape[0]
    tb = SAMPLE_TILE
    assert nb % tb == 0 and c0.shape[0] == 1 and nb == LANES and m0.shape == (HEADS, nb)

    def zgroup(grp):
        return pl.BlockSpec((tb, MLSTM_W), lambda i, grp=grp: (i, grp))

    def zconv(grp):
        return pl.BlockSpec((tb, CONV_CH), lambda i, grp=grp: (i, grp))

    row_d = pl.BlockSpec((tb, D_MODEL), lambda i: (i, 0))
    state = pl.BlockSpec((1, tb, HEADS, HEAD_DIM, HEAD_DIM), lambda i: (0, i, 0, 0, 0))
    nstate = pl.BlockSpec((1, tb, HEADS, HEAD_DIM), lambda i: (0, i, 0, 0))
    cstate = pl.BlockSpec((1, tb, CONV_WIDTH - 1, CONV_CH), lambda i: (0, i, 0, 0))

    def whole(shape):
        return pl.BlockSpec(shape, lambda i: (0,) * len(shape))

    mix, c_new, n_new, m_new, new_buf = pl.pallas_call(
        _sample_state_kernel,
        grid=(nb // tb,),
        in_specs=[zgroup(0), zgroup(1), zgroup(2), zgroup(3), zconv(0), zconv(1), zconv(2),
                  pl.BlockSpec((tb, LANES), lambda i: (i, 0)),
                  state, nstate, whole((HEADS, nb)), cstate,
                  whole((1, HEADS)), whole((1, HEADS)), whole((1, MLSTM_W)), whole((CONV_WIDTH, 1, CONV_CH))],
        out_specs=(row_d, state, nstate, whole((HEADS, nb)), cstate),
        out_shape=(jax.ShapeDtypeStruct((nb, D_MODEL), BF16),
                   jax.ShapeDtypeStruct(c0.shape, F32),
                   jax.ShapeDtypeStruct(n0.shape, F32),
                   jax.ShapeDtypeStruct((HEADS, nb), F32),
                   jax.ShapeDtypeStruct(sconv.shape, F32)),
        scratch_shapes=[pltpu.VMEM((tb, MLSTM_W), F32)] * 3 + [pltpu.VMEM((nb, LANES), F32)] * 2,
        compiler_params=pltpu.CompilerParams(dimension_semantics=("arbitrary",),
                                             vmem_limit_bytes=VMEM_LIMIT_BYTES),
        name="sample_state",
    )(zh, zh, zh, zh, zc, zc, zc, g, c0, n0, m0, sconv, wts["b_i"], wts["b_f"], wts["mh_norm"], wts["conv_w"])
    return mix, c_new, n_new, m_new, new_buf


def _weight_prep_kernel(win_ref, xs_ref, nmix_ref,
                        heads_ref, conv_ref, gt_ref, zh_ref, zc_ref, g_ref, hb_scr):
    step = pl.program_id(0)
    g0 = 4 * MLSTM_W
    g1 = g0 + 2 * HEADS

    @pl.when(step == 0)
    def _():
        hb = _rms(xs_ref[:, 0, :], nmix_ref[...]).astype(BF16)
        kb = hb_scr.shape[2]
        for i in range(hb_scr.shape[0]):
            hb_scr[i] = hb[:, i * kb:(i + 1) * kb]
        zh_ref[...] = jnp.zeros(zh_ref.shape, F32)
        zc_ref[...] = jnp.zeros(zc_ref.shape, F32)
        g_ref[...] = jnp.zeros(g_ref.shape, F32)

    w_heads = win_ref[0:g0, :].T.astype(BF16)
    w_conv = win_ref[g1:, :].T.astype(BF16)
    heads_ref[...] = w_heads
    conv_ref[...] = w_conv
    gates = win_ref[g0:g1, :]
    rows = gates.shape[1]
    gt_ref[...] = jnp.concatenate([gates, jnp.zeros((GATE_ROWS - 2 * HEADS, rows), F32)], axis=0).astype(BF16)
    w_gc = jnp.concatenate([gates, jnp.zeros((LANES - 2 * HEADS, rows), F32)], axis=0).T.astype(BF16)
    hb_k = hb_scr[step]
    zh_ref[...] += _dot(hb_k, w_heads)
    zc_ref[...] += _dot(hb_k, w_conv)
    g_ref[...] += _dot(hb_k, w_gc)


def _prepare_weights(xs, norm_mix, w_in, b_gate_i, b_gate_f, mh_norm, conv_w, w_out, norm_mlp, w_up, w_down,
                     norm_ple, w_ple_gate, w_ple_proj, norm_final):
    w_in_t = jnp.swapaxes(w_in, 0, 1)
    n_in = w_in.shape[1]
    nb = xs.shape[0]
    outs = ((D_MODEL, 4 * MLSTM_W), (D_MODEL, 3 * CONV_CH), (GATE_ROWS, D_MODEL))
    z_outs = ((nb, 4 * MLSTM_W), (nb, 3 * CONV_CH), (nb, LANES))
    steps = PREP_STEPS

    def rows(shape):
        return pl.BlockSpec((shape[0] // steps, shape[1]), lambda i: (i, 0))

    def whole(shape):
        return pl.BlockSpec(shape, lambda i: (0,) * len(shape))

    out_specs = [rows(o) for o in outs]
    out_specs[2] = pl.BlockSpec((GATE_ROWS, D_MODEL // steps), lambda i: (0, i))
    out_specs += [whole(z) for z in z_outs]
    assert n_in == 4 * MLSTM_W + 2 * HEADS + 3 * CONV_CH
    norm_mix = norm_mix.reshape(1, D_MODEL)
    w_heads, w_conv, w_gt, zh, zc, g = pl.pallas_call(
        _weight_prep_kernel,
        grid=(steps,),
        in_specs=[pl.BlockSpec((n_in, D_MODEL // steps), lambda i: (0, i)), whole(xs.shape), whole(norm_mix.shape)],
        out_specs=out_specs,
        out_shape=[jax.ShapeDtypeStruct(o, BF16) for o in outs] + [jax.ShapeDtypeStruct(z, F32) for z in z_outs],
        scratch_shapes=[pltpu.VMEM((steps, nb, D_MODEL // steps), BF16)],
        compiler_params=pltpu.CompilerParams(dimension_semantics=("arbitrary",),
                                             vmem_limit_bytes=VMEM_LIMIT_BYTES),
        name="weight_prep",
    )(w_in_t, xs, norm_mix)
    return dict(
        w_heads=w_heads,
        w_conv=w_conv,
        w_gt=w_gt,
        zh=zh,
        zc=zc,
        g=g,
        b_i=b_gate_i.reshape(1, HEADS),
        b_f=b_gate_f.reshape(1, HEADS),
        norm_mix=norm_mix,
        mh_norm=mh_norm.reshape(1, MLSTM_W),
        conv_w=jnp.swapaxes(conv_w, 0, 1),
        w_out=w_out,
        norm_mlp=norm_mlp.reshape(1, D_MODEL),
        w_up=w_up,
        w_down=w_down,
        norm_ple=norm_ple.reshape(1, D_MODEL),
        w_pg=w_ple_gate,
        w_pp=w_ple_proj,
        norm_final=norm_final.reshape(1, D_MODEL),
    )


def kernel(x_prompt, x_sample, state_mlstm_C, state_mlstm_n, state_mlstm_m, state_conv, p_prompt, p_sample,
           norm_mix, w_in, b_gate_i, b_gate_f, mh_norm, conv_w, w_out, norm_mlp, w_up, w_down, norm_ple,
           w_ple_gate, w_ple_proj, norm_final):
    assert norm_mix.shape[0] == 1, "single-layer trunk"
    xs = x_sample
    wts = _prepare_weights(xs, norm_mix[0], w_in[0], b_gate_i[0], b_gate_f[0], mh_norm[0], conv_w, w_out[0],
                           norm_mlp[0], w_up[0], w_down[0], norm_ple[0], w_ple_gate[0], w_ple_proj[0], norm_final)

    mix_s, c_new, n_new, m_new, new_buf = _sample_state_call(
        state_mlstm_C, state_mlstm_n, jnp.swapaxes(state_mlstm_m[0], 0, 1), state_conv, wts)

    y_prompt, ys, prompt_c, prompt_n, prompt_m, conv_tail = _prompt_call(
        x_prompt, p_prompt[0], xs, p_sample, mix_s, wts)

    return (y_prompt, ys, prompt_c[None], prompt_n[None], jnp.swapaxes(prompt_m, 0, 1)[None], conv_tail[None],
            c_new, n_new, jnp.swapaxes(m_new, 0, 1)[None], new_buf)
```

```python
import functools
import math

import jax
import jax.numpy as jnp
from jax import lax
from jax.experimental import pallas as pl
from jax.experimental.pallas import tpu as pltpu

F32 = jnp.float32
BF16 = jnp.bfloat16

D_MODEL = 1024
HEADS = 4
HEAD_DIM = 128
MLSTM_W = HEADS * HEAD_DIM
CONV_CH = D_MODEL - MLSTM_W
CONV_WIDTH = 3
D_FF = 4 * D_MODEL
PLE_DIM = 256
EPS = 1e-6
M_INIT = -1e30
Q_SCALE = HEAD_DIM ** -0.5
LOG2E = math.log2(math.e)

GATE_ROWS = 16
PREP_STEPS = 4
LANES = 128
SUBLANES = 8
VMEM_LIMIT_BYTES = 60000 * 1024

SEQ_TILE = 512
CHUNK = 256
FF_CHUNK = 512
STAGE_ROWS, STAGE_COLS = 128, 1024
SAMPLE_TILE = 32


def _dot(a, b):
    return jnp.dot(a, b, preferred_element_type=F32)


def _dot_nt(a, b):
    return lax.dot_general(a, b, (((1,), (1,)), ((), ())), preferred_element_type=F32)


def _dot_tn(a, b):
    return lax.dot_general(a, b, (((0,), (0,)), ((), ())), preferred_element_type=F32)


def _rms(x, g):
    y = x * lax.rsqrt(jnp.mean(x * x, axis=-1, keepdims=True) + EPS)
    return y * g


def _log_sigmoid(x):
    return jnp.minimum(x, 0.0) - jnp.log1p(jnp.exp(-jnp.abs(x)))


def _mlp_chunk(xn_bf16, c, wup_ref, wdown_ref):
    cols = slice(c * FF_CHUNK, (c + 1) * FF_CHUNK)
    hf = jnp.maximum(_dot(xn_bf16, wup_ref[:, cols]), 0.0)
    return _dot((hf * hf).astype(BF16), wdown_ref[cols, :])


def _cummax_lanes(a):
    n = a.shape[1]
    lane = lax.broadcasted_iota(jnp.int32, a.shape, 1)
    d = 1
    while d < n:
        shifted = pltpu.roll(a, d, axis=1)
        a = jnp.maximum(a, jnp.where(lane >= d, shifted, -jnp.inf))
        d *= 2
    return a


def _convert_weights(pairs, slots, sem):
    jobs = [(src, dst, r, c)
            for src, dst in pairs
            for r in range(0, src.shape[0], STAGE_ROWS)
            for c in range(0, src.shape[1], STAGE_COLS)]
    ns = len(slots)

    def fetch(k):
        src, _, r, c = jobs[k]
        ref, idx = slots[k % ns]
        return pltpu.make_async_copy(src.at[pl.ds(r, STAGE_ROWS), pl.ds(c, STAGE_COLS)], ref.at[idx], sem.at[k % ns])

    for k in range(min(ns - 1, len(jobs))):
        fetch(k).start()
    for k, (_, dst, r, c) in enumerate(jobs):
        if k + ns - 1 < len(jobs):
            fetch(k + ns - 1).start()
        fetch(k).wait()
        ref, idx = slots[k % ns]
        dst[r:r + STAGE_ROWS, c:c + STAGE_COLS] = ref[idx].astype(BF16)


N_PROMPT_CONSTS = 16
N_PROMPT_OUTS = 6


def _prompt_kernel(tiles_per_seq, num_tiles, in_specs, out_specs, x_hbm, p_hbm, xs_ref, ps_ref, mixs_ref, *rest):
    consts = rest[:N_PROMPT_CONSTS]
    y_hbm, ys_ref, cout_hbm, nout_hbm, mout_ref, convout_hbm = rest[N_PROMPT_CONSTS:N_PROMPT_CONSTS + N_PROMPT_OUTS]
    scratch = rest[N_PROMPT_CONSTS + N_PROMPT_OUTS:]
    wout_hbm, wup_hbm, wdown_hbm, wpg_hbm, wpp_hbm = (consts[i] for i in (8, 10, 11, 13, 14))
    (_, _, _, mix_scr, xres_scr, tri_scr, ps_scr, mall_scr,
     wout_ref, wup_ref, wdown_ref, wpg_ref, wpp_ref, stage_scr, stage_sem) = scratch
    t = SEQ_TILE
    lc = CHUNK
    ns = xs_ref.shape[0]

    cols = slice(0, STAGE_COLS)
    slots = ([(stage_scr, (i,)) for i in range(stage_scr.shape[0])]
             + [(xres_scr, (slice(r, r + STAGE_ROWS), cols)) for r in range(0, t, STAGE_ROWS)])
    _convert_weights(((wout_hbm, wout_ref), (wup_hbm, wup_ref), (wdown_hbm, wdown_ref),
                      (wpg_hbm, wpg_ref), (wpp_hbm, wpp_ref)), slots, stage_sem)
    mix_scr[0:ns, :] = mixs_ref[...]
    mix_scr[ns:t, :] = jnp.zeros((t - ns, D_MODEL), BF16)
    xres_scr[0:ns, :] = xs_ref[:, 0, :]
    xres_scr[ns:t, :] = jnp.zeros((t - ns, D_MODEL), F32)
    ps_scr[0:ns, :] = ps_ref[0, :, 0, :]
    ps_scr[ns:t, :] = jnp.zeros((t - ns, PLE_DIM), F32)
    mall_scr[...] = jnp.zeros(mall_scr.shape, F32)
    tri_scr[...] = (lax.broadcasted_iota(jnp.int32, (lc, lc), 0)
                    <= lax.broadcasted_iota(jnp.int32, (lc, lc), 1)).astype(F32).astype(BF16)

    def step(x_ref, plag_ref, y_ref, cout_ref, nout_ref, convout_ref):
        _prompt_step(tiles_per_seq, num_tiles, x_ref, plag_ref, xs_ref, ps_ref, mixs_ref, *consts,
                     y_ref, ys_ref, cout_ref, nout_ref, mout_ref, convout_ref, *scratch)

    pltpu.emit_pipeline(step, grid=(num_tiles + 1,), in_specs=in_specs, out_specs=out_specs)(
        x_hbm, p_hbm, y_hbm, cout_hbm, nout_hbm, convout_hbm)


def _prompt_step(tiles_per_seq, num_tiles,
                 x_ref, plag_ref, xs_ref, ps_ref, mixs_ref,
                 wheads_ref, wconv_ref, wgt_ref, bi_ref, bf_ref, nmix_ref, mh_ref, cw_ref,
                 wout_hbm, nmlp_ref, wup_hbm, wdown_hbm, nple_ref, wpg_hbm, wpp_hbm, nfin_ref,
                 y_ref, ys_ref, cout_ref, nout_ref, mout_ref, convout_ref,
                 caug, m_scr, cu_buf, mix_scr, xres_scr, tri_scr, ps_scr, mall_scr,
                 wout_ref, wup_ref, wdown_ref, wpg_ref, wpp_ref, stage_scr, stage_sem):
    t = SEQ_TILE
    lc = CHUNK
    ns = xs_ref.shape[0]
    s_id = pl.program_id(0)
    is_real = s_id < num_tiles
    j = lax.rem(jnp.minimum(s_id, num_tiles - 1), tiles_per_seq)

    @pl.when(j == 0)
    def _():
        caug[...] = jnp.zeros_like(caug)
        m_scr[...] = jnp.full(m_scr.shape, M_INIT, F32)
        cu_buf[0:SUBLANES, :] = jnp.zeros((SUBLANES, CONV_CH), F32)

    x1 = xres_scr[...] + _dot(mix_scr[...], wout_ref[...])
    xn = _rms(x1, nmlp_ref[...]).astype(BF16)

    x = x_ref[0]
    hb = _rms(x, nmix_ref[...]).astype(BF16)
    xres_scr[...] = x
    gt = _dot_nt(wgt_ref[...], hb)

    row_id = lax.broadcasted_iota(jnp.int32, (lc, lc), 0)
    col_id = lax.broadcasted_iota(jnp.int32, (lc, lc), 1)
    causal = col_id <= row_id
    lane_id = lax.broadcasted_iota(jnp.int32, (lc, HEAD_DIM), 1)

    acc = jnp.zeros((t, D_MODEL), F32)
    n_phases = (t // lc) * HEADS
    mlp_per_phase = D_FF // FF_CHUNK // n_phases
    mlp_first = (mlp_per_phase + 1) // 2
    next_mlp = 0
    m_carry = m_scr[...]
    sub_id = lax.broadcasted_iota(jnp.int32, (SUBLANES, lc), 0)
    bias_i = jnp.zeros((SUBLANES, lc), F32)
    bias_f = jnp.zeros((SUBLANES, lc), F32)
    for h in range(HEADS):
        bias_i = jnp.where(sub_id == h, bi_ref[0, h], bias_i)
        bias_f = jnp.where(sub_id == h, bf_ref[0, h], bias_f)
    for c in range(t // lc):
        tok = slice(c * lc, (c + 1) * lc)
        hb_c = hb[tok, :]
        g8 = gt[0:SUBLANES, tok]
        ig = g8 + bias_i
        lf = _log_sigmoid(pltpu.roll(g8, HEADS, axis=0) + bias_f)
        hi = lf.astype(BF16)
        r1 = lf - hi.astype(F32)
        mid = r1.astype(BF16)
        lo = (r1 - mid.astype(F32)).astype(BF16)
        parts = _dot(jnp.concatenate([hi, mid, lo, jnp.zeros_like(lo)], axis=0), tri_scr[...])
        b = parts[0:8] + parts[8:16] + parts[16:24]
        a = ig - b
        m_prev = jnp.concatenate([m_carry] * (lc // LANES), axis=1)
        g = jnp.maximum(_cummax_lanes(a), m_prev)
        m_t = b + g
        b_last = b[:, lc - 1:lc]
        m_new = m_t[:, lc - 1:lc]
        decay = jnp.exp(m_prev - g)
        e_neg_m = jnp.exp(-m_t)
        w_state = jnp.exp(a + (b_last - m_new))
        c_dec = jnp.exp(b_last + m_prev[:, 0:1] - m_new)
        m_carry = jnp.broadcast_to(m_new, m_carry.shape)
        a2 = a * LOG2E
        rows = jnp.concatenate(
            [g * (-LOG2E), decay, e_neg_m, w_state, jnp.zeros((LANES - 4 * SUBLANES, lc), F32)], axis=0)
        cols = rows.T

        for h in range(HEADS):
            if h % 2 == 0:
                pair = slice(h * HEAD_DIM, (h + 2) * HEAD_DIM)
                zq2, zk2, zv2, zo2 = (
                    _dot(hb_c, wheads_ref[:, grp * MLSTM_W + pair.start:grp * MLSTM_W + pair.stop])
                    for grp in range(4))
            half = slice((h % 2) * HEAD_DIM, (h % 2 + 1) * HEAD_DIM)
            qs = (zq2[:, half] * Q_SCALE).astype(BF16)
            kb = zk2[:, half].astype(BF16)
            v = zv2[:, half]
            c_col = cols[:, h:h + 1]
            dec = cols[:, SUBLANES + h:SUBLANES + h + 1]
            enm = cols[:, 2 * SUBLANES + h:2 * SUBLANES + h + 1]
            wst = cols[:, 3 * SUBLANES + h:3 * SUBLANES + h + 1]
            s = _dot_nt(qs, kb)

            for _ in range(mlp_first):
                acc = acc + _mlp_chunk(xn, next_mlp, wup_ref, wdown_ref)
                next_mlp += 1

            dmat = jnp.exp2(jnp.where(causal, c_col + a2[h:h + 1, :], -jnp.inf))
            pm = s * dmat
            row_sum = jnp.sum(pm, axis=1, keepdims=True)
            intra = _dot(pm.astype(BF16), v.astype(BF16))
            c_state = caug[h]
            inter = _dot(qs, c_state.astype(BF16))
            vw = jnp.concatenate([v * wst, jnp.where(lane_id == 0, wst, 0.0)], axis=1).astype(BF16)
            caug[h] = c_dec[h:h + 1, :] * c_state + _dot_tn(kb, vw)

            for _ in range(mlp_per_phase - mlp_first):
                acc = acc + _mlp_chunk(xn, next_mlp, wup_ref, wdown_ref)
                next_mlp += 1

            num = intra + dec * inter[:, 0:HEAD_DIM]
            den = row_sum + dec * inter[:, HEAD_DIM:HEAD_DIM + 1]
            hh = num / jnp.maximum(jnp.abs(den), enm)
            hn = hh * lax.rsqrt(jnp.mean(hh * hh, axis=-1, keepdims=True) + EPS)
            hn = hn * mh_ref[:, h * HEAD_DIM:(h + 1) * HEAD_DIM]
            mix_scr[tok, h * HEAD_DIM:(h + 1) * HEAD_DIM] = (
                hn * jax.nn.sigmoid(zo2[:, half])).astype(BF16)
    m_scr[...] = m_carry
    assert next_mlp == D_FF // FF_CHUNK

    zc = _dot(hb, wconv_ref[...])
    p_tail = jnp.where(s_id == 0, ps_scr[...], plag_ref[0])
    ple = _dot(p_tail.astype(BF16), wpp_ref[...])

    def tail_rows(rows):
        x2 = x1[rows, :] + acc[rows, :]
        gate = jax.nn.sigmoid(_dot(_rms(x2, nple_ref[...]).astype(BF16), wpg_ref[...]))
        y_ref[0, rows, :] = _rms(x2 + gate * ple[rows, :], nfin_ref[...])

    tail_rows(slice(0, t // 2))

    cu = zc[:, CONV_CH:2 * CONV_CH] * zc[:, 2 * CONV_CH:3 * CONV_CH]
    cu_buf[SUBLANES:SUBLANES + t, :] = cu
    conv = (cw_ref[0] * cu_buf[SUBLANES - 2:SUBLANES - 2 + t, :]
            + cw_ref[1] * cu_buf[SUBLANES - 1:SUBLANES - 1 + t, :]
            + cw_ref[2] * cu)
    mix_scr[:, MLSTM_W:D_MODEL] = (zc[:, 0:CONV_CH] * conv).astype(BF16)
    cu_buf[0:SUBLANES, :] = cu_buf[t:t + SUBLANES, :]

    tail_rows(slice(t // 2, t))

    @pl.when(s_id == 0)
    def _():
        ys_ref[:, 0, :] = y_ref[0, 0:ns, :]

    @pl.when(jnp.logical_and(is_real, j == tiles_per_seq - 1))
    def _():
        for h in range(HEADS):
            cout_ref[0, h] = caug[h, :, 0:HEAD_DIM]
            nout_ref[0, h:h + 1, :] = caug[h, :, HEAD_DIM:2 * HEAD_DIM].T[0:1, :]
        seq_lane = lax.broadcasted_iota(jnp.int32, m_scr.shape, 1) == s_id // tiles_per_seq
        m_all = jnp.where(seq_lane, m_scr[...], mall_scr[...])
        mall_scr[...] = m_all
        mout_ref[...] = m_all[0:HEADS, 0:mout_ref.shape[1]]
        convout_ref[0] = cu[t - (CONV_WIDTH - 1):t, :]


def _prompt_call(x, p, xs, ps, mix_s, wts):
    bsz, seq, _ = x.shape
    t = SEQ_TILE
    assert seq % t == 0 and t % CHUNK == 0 and CHUNK % LANES == 0 and xs.shape[0] <= t and bsz <= LANES
    consts = [wts["w_heads"], wts["w_conv"], wts["w_gt"], wts["b_i"], wts["b_f"], wts["norm_mix"], wts["mh_norm"],
              wts["conv_w"], wts["w_out"], wts["norm_mlp"], wts["w_up"], wts["w_down"], wts["norm_ple"], wts["w_pg"],
              wts["w_pp"], wts["norm_final"]]
    assert len(consts) == N_PROMPT_CONSTS
    in_smem = {3, 4}
    in_hbm = {8, 10, 11, 13, 14}
    assert all(consts[i].dtype == F32 and consts[i].shape[0] % STAGE_ROWS == 0
               and consts[i].shape[1] % STAGE_COLS == 0 for i in in_hbm)
    nj = seq // t
    nt = bsz * nj

    def cur(s):
        c = jnp.minimum(s, nt - 1)
        return c // nj, c % nj

    def lag(s):
        c = jnp.maximum(s - 1, 0)
        return c // nj, c % nj

    step_in_specs = [pl.BlockSpec((1, t, D_MODEL), lambda s: (*cur(s), 0)),
                     pl.BlockSpec((1, t, PLE_DIM), lambda s: (*lag(s), 0))]
    step_out_specs = [pl.BlockSpec((1, t, D_MODEL), lambda s: (*lag(s), 0)),
                      pl.BlockSpec((1, HEADS, HEAD_DIM, HEAD_DIM), lambda s: (cur(s)[0], 0, 0, 0)),
                      pl.BlockSpec((1, HEADS, HEAD_DIM), lambda s: (cur(s)[0], 0, 0)),
                      pl.BlockSpec((1, CONV_WIDTH - 1, CONV_CH), lambda s: (cur(s)[0], 0, 0))]

    hbm = pl.BlockSpec(memory_space=pl.ANY)
    vmem = pl.BlockSpec(memory_space=pltpu.VMEM)
    in_specs = [hbm, hbm, vmem, vmem, vmem]
    in_specs += [pl.BlockSpec(memory_space=pltpu.SMEM) if i in in_smem else hbm if i in in_hbm else vmem
                 for i in range(len(consts))]
    out_shape = (jax.ShapeDtypeStruct((bsz, seq, D_MODEL), F32),
                 jax.ShapeDtypeStruct(xs.shape, F32),
                 jax.ShapeDtypeStruct((bsz, HEADS, HEAD_DIM, HEAD_DIM), F32),
                 jax.ShapeDtypeStruct((bsz, HEADS, HEAD_DIM), F32),
                 jax.ShapeDtypeStruct((HEADS, bsz), F32),
                 jax.ShapeDtypeStruct((bsz, CONV_WIDTH - 1, CONV_CH), F32))
    out_specs = (hbm, vmem, hbm, hbm, vmem, hbm)
    assert len(out_shape) == N_PROMPT_OUTS
    n_stage = 4
    scratch = [pltpu.VMEM((HEADS, HEAD_DIM, 2 * HEAD_DIM), F32),
               pltpu.VMEM((SUBLANES, LANES), F32),
               pltpu.VMEM((t + SUBLANES, CONV_CH), F32),
               pltpu.VMEM((t, D_MODEL), BF16),
               pltpu.VMEM((t, D_MODEL), F32),
               pltpu.VMEM((CHUNK, CHUNK), BF16),
               pltpu.VMEM((t, PLE_DIM), F32),
               pltpu.VMEM((SUBLANES, LANES), F32)]
    scratch += [pltpu.VMEM(consts[i].shape, BF16) for i in sorted(in_hbm)]
    scratch += [pltpu.VMEM((n_stage, STAGE_ROWS, STAGE_COLS), F32),
                pltpu.SemaphoreType.DMA((n_stage + t // STAGE_ROWS,))]
    assert t % STAGE_ROWS == 0 and STAGE_COLS <= D_MODEL
    return pl.pallas_call(
        functools.partial(_prompt_kernel, nj, nt, step_in_specs, step_out_specs),
        in_specs=in_specs,
        out_specs=out_specs,
        out_shape=out_shape,
        scratch_shapes=scratch,
        compiler_params=pltpu.CompilerParams(vmem_limit_bytes=VMEM_LIMIT_BYTES),
        name="prompt_layer",
    )(x, p, xs, ps, mix_s, *consts)


def _sample_state_kernel(zq_ref, zk_ref, zv_ref, zo_ref, zb_ref, zc_ref, zu_ref, g_ref, c0_ref, n0_ref, m0_ref,
                         sc_ref, bi_ref, bf_ref, mh_ref, cw_ref,
                         mix_ref, cnew_ref, nnew_ref, mnew_ref, buf_ref,
                         inter_scr, wv_scr, cd_scr, m0_scr, mnew_scr):
    tb = SAMPLE_TILE
    step = pl.program_id(0)
    seqs = pl.ds(pl.multiple_of(step * tb, tb), tb)

    @pl.when(step == 0)
    def _():
        m0_scr[...] = jnp.zeros(m0_scr.shape, F32)
        m0_scr[0:HEADS, :] = m0_ref[...]
        m0_scr[...] = m0_scr[...].T

    cu = zc_ref[...] * zu_ref[...]
    old0 = sc_ref[0, :, 0, :]
    old1 = sc_ref[0, :, 1, :]
    conv = cw_ref[0] * old0 + cw_ref[1] * old1 + cw_ref[2] * cu
    mix_ref[:, MLSTM_W:D_MODEL] = (zb_ref[...] * conv).astype(BF16)
    buf_ref[0, :, 0, :] = old1
    buf_ref[0, :, 1, :] = cu

    g = g_ref[...]
    ig = g[:, 0:HEADS] + bi_ref[...]
    lf = _log_sigmoid(g[:, HEADS:2 * HEADS] + bf_ref[...])
    m_inter = lf + m0_scr[seqs, 0:HEADS]
    m_new = jnp.maximum(m_inter, ig)
    w_in = jnp.exp(ig - m_new)
    c_dec = jnp.exp(m_inter - m_new)
    e_neg_m = jnp.exp(-m_new)
    lane = lax.broadcasted_iota(jnp.int32, (tb, LANES), 1)
    m_wide = jnp.zeros((tb, LANES), F32)
    for h in range(HEADS):
        m_wide = jnp.where(lane == h, m_new[:, h:h + 1], m_wide)
    mnew_scr[seqs, :] = m_wide

    @pl.when(step == pl.num_programs(0) - 1)
    def _():
        mnew_ref[...] = mnew_scr[...].T[0:HEADS, :]

    vs, scores, qns = [], [], []
    for h in range(HEADS):
        hs = slice(h * HEAD_DIM, (h + 1) * HEAD_DIM)
        q = zq_ref[:, hs] * Q_SCALE
        k = zk_ref[:, hs]
        v = zv_ref[:, hs]
        n0 = n0_ref[0, :, h, :]
        wi = w_in[:, h:h + 1]
        cd = c_dec[:, h:h + 1]
        scores.append(jnp.sum(q * k, axis=1, keepdims=True) * wi)
        qns.append(jnp.sum(q * n0, axis=1, keepdims=True))
        wv_scr[:, hs] = wi * v
        cd_scr[:, hs] = jnp.broadcast_to(cd, (tb, HEAD_DIM))
        nnew_ref[0, :, h, :] = cd * n0 + wi * k
        vs.append(v)

    eye = (lax.broadcasted_iota(jnp.int32, (HEAD_DIM, HEAD_DIM), 0)
           == lax.broadcasted_iota(jnp.int32, (HEAD_DIM, HEAD_DIM), 1))

    for i in range(tb):
        row = slice(i, i + 1)
        for h in range(HEADS):
            hs = slice(h * HEAD_DIM, (h + 1) * HEAD_DIM)
            c0 = c0_ref[0, i, h]
            q_rows = jnp.broadcast_to(zq_ref[row, hs] * Q_SCALE, (SUBLANES, HEAD_DIM)).astype(BF16)
            inter_scr[row, hs] = _dot(q_rows, c0.astype(BF16))[0:1, :]
            k_diag = jnp.where(eye, jnp.broadcast_to(zk_ref[row, hs], (HEAD_DIM, HEAD_DIM)), 0.0).astype(BF16)
            v_rows = jnp.broadcast_to(wv_scr[row, hs], (HEAD_DIM, HEAD_DIM)).astype(BF16)
            cd = jnp.broadcast_to(cd_scr[row, hs], (HEAD_DIM, HEAD_DIM))
            cnew_ref[0, i, h] = cd * c0 + _dot(k_diag, v_rows)

    for h in range(HEADS):
        hs = slice(h * HEAD_DIM, (h + 1) * HEAD_DIM)
        cd = c_dec[:, h:h + 1]
        num = scores[h] * vs[h] + cd * inter_scr[:, hs]
        den = scores[h] + cd * qns[h]
        hh = num / jnp.maximum(jnp.abs(den), e_neg_m[:, h:h + 1])
        hn = hh * lax.rsqrt(jnp.mean(hh * hh, axis=-1, keepdims=True) + EPS)
        mix_ref[:, hs] = (hn * mh_ref[:, hs] * jax.nn.sigmoid(zo_ref[:, hs])).astype(BF16)


def _sample_state_call(c0, n0, m0, sconv, wts):
    zh, zc, g = wts["zh"], wts["zc"], wts["g"]
    nb = zh.shape[0]
    tb = SAMPLE_TILE
    assert nb % tb == 0 and c0.shape[0] == 1 and nb == LANES and m0.shape == (HEADS, nb)

    def zgroup(grp):
        return pl.BlockSpec((tb, MLSTM_W), lambda i, grp=grp: (i, grp))

    def zconv(grp):
        return pl.BlockSpec((tb, CONV_CH), lambda i, grp=grp: (i, grp))

    row_d = pl.BlockSpec((tb, D_MODEL), lambda i: (i, 0))
    state = pl.BlockSpec((1, tb, HEADS, HEAD_DIM, HEAD_DIM), lambda i: (0, i, 0, 0, 0))
    nstate = pl.BlockSpec((1, tb, HEADS, HEAD_DIM), lambda i: (0, i, 0, 0))
    cstate = pl.BlockSpec((1, tb, CONV_WIDTH - 1, CONV_CH), lambda i: (0, i, 0, 0))

    def whole(shape):
        return pl.BlockSpec(shape, lambda i: (0,) * len(shape))

    mix, c_new, n_new, m_new, new_buf = pl.pallas_call(
        _sample_state_kernel,
        grid=(nb // tb,),
        in_specs=[zgroup(0), zgroup(1), zgroup(2), zgroup(3), zconv(0), zconv(1), zconv(2),
                  pl.BlockSpec((tb, LANES), lambda i: (i, 0)),
                  state, nstate, whole((HEADS, nb)), cstate,
                  whole((1, HEADS)), whole((1, HEADS)), whole((1, MLSTM_W)), whole((CONV_WIDTH, 1, CONV_CH))],
        out_specs=(row_d, state, nstate, whole((HEADS, nb)), cstate),
        out_shape=(jax.ShapeDtypeStruct((nb, D_MODEL), BF16),
                   jax.ShapeDtypeStruct(c0.shape, F32),
                   jax.ShapeDtypeStruct(n0.shape, F32),
                   jax.ShapeDtypeStruct((HEADS, nb), F32),
                   jax.ShapeDtypeStruct(sconv.shape, F32)),
        scratch_shapes=[pltpu.VMEM((tb, MLSTM_W), F32)] * 3 + [pltpu.VMEM((nb, LANES), F32)] * 2,
        compiler_params=pltpu.CompilerParams(dimension_semantics=("arbitrary",),
                                             vmem_limit_bytes=VMEM_LIMIT_BYTES),
        name="sample_state",
    )(zh, zh, zh, zh, zc, zc, zc, g, c0, n0, m0, sconv, wts["b_i"], wts["b_f"], wts["mh_norm"], wts["conv_w"])
    return mix, c_new, n_new, m_new, new_buf


def _weight_prep_kernel(win_ref, xs_ref, nmix_ref,
                        heads_ref, conv_ref, gt_ref, zh_ref, zc_ref, g_ref, hb_scr):
    step = pl.program_id(0)
    g0 = 4 * MLSTM_W
    g1 = g0 + 2 * HEADS

    @pl.when(step == 0)
    def _():
        hb = _rms(xs_ref[:, 0, :], nmix_ref[...]).astype(BF16)
        kb = hb_scr.shape[2]
        for i in range(hb_scr.shape[0]):
            hb_scr[i] = hb[:, i * kb:(i + 1) * kb]
        zh_ref[...] = jnp.zeros(zh_ref.shape, F32)
        zc_ref[...] = jnp.zeros(zc_ref.shape, F32)
        g_ref[...] = jnp.zeros(g_ref.shape, F32)

    w_heads = win_ref[0:g0, :].T.astype(BF16)
    w_conv = win_ref[g1:, :].T.astype(BF16)
    heads_ref[...] = w_heads
    conv_ref[...] = w_conv
    gates = win_ref[g0:g1, :]
    rows = gates.shape[1]
    gt_ref[...] = jnp.concatenate([gates, jnp.zeros((GATE_ROWS - 2 * HEADS, rows), F32)], axis=0).astype(BF16)
    w_gc = jnp.concatenate([gates, jnp.zeros((LANES - 2 * HEADS, rows), F32)], axis=0).T.astype(BF16)
    hb_k = hb_scr[step]
    zh_ref[...] += _dot(hb_k, w_heads)
    zc_ref[...] += _dot(hb_k, w_conv)
    g_ref[...] += _dot(hb_k, w_gc)


def _prepare_weights(xs, norm_mix, w_in, b_gate_i, b_gate_f, mh_norm, conv_w, w_out, norm_mlp, w_up, w_down,
                     norm_ple, w_ple_gate, w_ple_proj, norm_final):
    w_in_t = jnp.swapaxes(w_in, 0, 1)
    n_in = w_in.shape[1]
    nb = xs.shape[0]
    outs = ((D_MODEL, 4 * MLSTM_W), (D_MODEL, 3 * CONV_CH), (GATE_ROWS, D_MODEL))
    z_outs = ((nb, 4 * MLSTM_W), (nb, 3 * CONV_CH), (nb, LANES))
    steps = PREP_STEPS

    def rows(shape):
        return pl.BlockSpec((shape[0] // steps, shape[1]), lambda i: (i, 0))

    def whole(shape):
        return pl.BlockSpec(shape, lambda i: (0,) * len(shape))

    out_specs = [rows(o) for o in outs]
    out_specs[2] = pl.BlockSpec((GATE_ROWS, D_MODEL // steps), lambda i: (0, i))
    out_specs += [whole(z) for z in z_outs]
    assert n_in == 4 * MLSTM_W + 2 * HEADS + 3 * CONV_CH
    norm_mix = norm_mix.reshape(1, D_MODEL)
    w_heads, w_conv, w_gt, zh, zc, g = pl.pallas_call(
        _weight_prep_kernel,
        grid=(steps,),
        in_specs=[pl.BlockSpec((n_in, D_MODEL // steps), lambda i: (0, i)), whole(xs.shape), whole(norm_mix.shape)],
        out_specs=out_specs,
        out_shape=[jax.ShapeDtypeStruct(o, BF16) for o in outs] + [jax.ShapeDtypeStruct(z, F32) for z in z_outs],
        scratch_shapes=[pltpu.VMEM((steps, nb, D_MODEL // steps), BF16)],
        compiler_params=pltpu.CompilerParams(dimension_semantics=("arbitrary",),
                                             vmem_limit_bytes=VMEM_LIMIT_BYTES),
        name="weight_prep",
    )(w_in_t, xs, norm_mix)
    return dict(
        w_heads=w_heads,
        w_conv=w_conv,
        w_gt=w_gt,
        zh=zh,
        zc=zc,
        g=g,
        b_i=b_gate_i.reshape(1, HEADS),
        b_f=b_gate_f.reshape(1, HEADS),
        norm_mix=norm_mix,
        mh_norm=mh_norm.reshape(1, MLSTM_W),
        conv_w=jnp.swapaxes(conv_w, 0, 1),
        w_out=w_out,
        norm_mlp=norm_mlp.reshape(1, D_MODEL),
        w_up=w_up,
        w_down=w_down,
        norm_ple=norm_ple.reshape(1, D_MODEL),
        w_pg=w_ple_gate,
        w_pp=w_ple_proj,
        norm_final=norm_final.reshape(1, D_MODEL),
    )


def kernel(x_prompt, x_sample, state_mlstm_C, state_mlstm_n, state_mlstm_m, state_conv, p_prompt, p_sample,
           norm_mix, w_in, b_gate_i, b_gate_f, mh_norm, conv_w, w_out, norm_mlp, w_up, w_down, norm_ple,
           w_ple_gate, w_ple_proj, norm_final):
    assert norm_mix.shape[0] == 1, "single-layer trunk"
    xs = x_sample
    wts = _prepare_weights(xs, norm_mix[0], w_in[0], b_gate_i[0], b_gate_f[0], mh_norm[0], conv_w, w_out[0],
                           norm_mlp[0], w_up[0], w_down[0], norm_ple[0], w_ple_gate[0], w_ple_proj[0], norm_final)

    mix_s, c_new, n_new, m_new, new_buf = _sample_state_call(
        state_mlstm_C, state_mlstm_n, jnp.swapaxes(state_mlstm_m[0], 0, 1), state_conv, wts)

    y_prompt, ys, prompt_c, prompt_n, prompt_m, conv_tail = _prompt_call(
        x_prompt, p_prompt[0], xs, p_sample, mix_s, wts)

    return (y_prompt, ys, prompt_c[None], prompt_n[None], jnp.swapaxes(prompt_m, 0, 1)[None], conv_tail[None],
            c_new, n_new, jnp.swapaxes(m_new, 0, 1)[None], new_buf)
```
